```python
import math
import jax, jax.numpy as jnp
from jax import lax
import numpy as np

D_MODEL = 1024
BATCH = 2
SEQ = 8192
DEPTH = 4
DEC_BATCH = 128
DEC_SEQ = 4
PAST_LEN = 2048
PAGE_SIZE = 128

NSA_HEADS = 8
NSA_KV_HEADS = 2
HEAD_DIM = 64
NSA_WIDTH = NSA_HEADS * HEAD_DIM
NSA_HG = NSA_HEADS // NSA_KV_HEADS
CMP_BLOCK = 32
CMP_HIDDEN = 2 * HEAD_DIM
SEL_BLOCK = 64
TOP_N = 16
WINDOW = 512
Q_BLOCK = 128
FORCE_BONUS = 1.0e4
ROPE_THETA = 10000.0
SSD_HEADS = 16
SSD_HEAD_DIM = 64
SSD_INNER = SSD_HEADS * SSD_HEAD_DIM
SSD_GROUPS = 2
SSD_STATE = 128
SSD_CONV = 4
SSD_CONV_CH = SSD_INNER + 2 * SSD_GROUPS * SSD_STATE
SSD_CHUNK = 64
GLA_HEADS = 4
GLA_DK = 128
GLA_DV = 256
GLA_KEY_WIDTH = GLA_HEADS * GLA_DK
GLA_VAL_WIDTH = GLA_HEADS * GLA_DV
GLA_GATE_RANK = 16
GLA_TAU = 16.0
GLA_CHUNK = 32
N_EVEN = (DEPTH + 1) // 2
N_ODD = DEPTH // 2
EVEN_SPLIT = [NSA_WIDTH, 3 * NSA_HEADS, 6 * NSA_KV_HEADS * HEAD_DIM, NSA_WIDTH,
              SSD_INNER, SSD_CONV_CH, SSD_HEADS]
EVEN_IN = sum(EVEN_SPLIT)
MIX_EVEN = NSA_WIDTH + SSD_INNER
ODD_SPLIT = [GLA_KEY_WIDTH, GLA_KEY_WIDTH, GLA_VAL_WIDTH, GLA_GATE_RANK, GLA_VAL_WIDTH]
ODD_IN = sum(ODD_SPLIT)
F32 = jnp.float32

kernel_name = "nsa_ssd_gla_hybrid_step"


def _split(x, sizes):
    idx = [int(v) for v in np.cumsum(sizes)[:-1]]
    return jnp.split(x, idx, axis=-1)


def rms_norm(x, w, eps=1e-6):
    x32 = x.astype(F32)
    y = x32 * lax.rsqrt(jnp.mean(x32 * x32, axis=-1, keepdims=True) + eps)
    return (y * w.astype(F32)).astype(x.dtype)


def rope(x, pos):
    half = HEAD_DIM // 2
    inv = ROPE_THETA ** (-jnp.arange(half, dtype=F32) / half)
    ang = pos.astype(F32)[:, None] * inv[None, :]
    cos = jnp.cos(ang)[None, :, None, :]
    sin = jnp.sin(ang)[None, :, None, :]
    x32 = x.astype(F32)
    x1, x2 = x32[..., :half], x32[..., half:]
    return jnp.concatenate([x1 * cos - x2 * sin, x1 * sin + x2 * cos], axis=-1).astype(x.dtype)


def masked_softmax(s, mask):
    s = jnp.where(mask, s.astype(F32), -jnp.inf)
    m = jnp.max(s, axis=-1, keepdims=True)
    m = jnp.where(jnp.isfinite(m), m, 0.0)
    p = jnp.exp(s - m)
    d = jnp.sum(p, axis=-1, keepdims=True)
    return p / jnp.where(d > 0, d, 1.0)


def nsa_compress(raw, pe, w1, w2):
    b, l, g, dh = raw.shape
    n = l // CMP_BLOCK
    blk = raw[:, :n * CMP_BLOCK].reshape(b, n, CMP_BLOCK, g, dh) + pe[None, None, :, None, :].astype(raw.dtype)
    flat = blk.transpose(0, 1, 3, 2, 4).reshape(b, n, g, CMP_BLOCK * dh)
    return jax.nn.silu(flat @ w1) @ w2


_gather_blocks = jax.vmap(jax.vmap(lambda kb, ix: kb[ix]))


def nsa_attend_block(q, gates, qpos, kc, vc, ks_blk, vs_blk, kw, vw, wpos):
    b, qb = q.shape[:2]
    scale = HEAD_DIM ** -0.5
    qg = q.reshape(b, qb, NSA_KV_HEADS, NSA_HG, HEAD_DIM)
    nc = kc.shape[1]
    cmp_end = (jnp.arange(nc) + 1) * CMP_BLOCK - 1
    mask_c = cmp_end[None, :] <= qpos[:, None]
    p_c = masked_softmax(jnp.einsum('bqghd,bngd->bghqn', qg, kc) * scale, mask_c)
    o_c = jnp.einsum('bghqn,bngd->bqghd', p_c.astype(vc.dtype), vc)
    ns = ks_blk.shape[2]
    imp = jnp.sum(p_c, axis=2)
    imp = jnp.pad(imp, ((0, 0), (0, 0), (0, 0), (0, 2 * ns - nc)))
    imp = imp.reshape(b, NSA_KV_HEADS, qb, ns, 2).sum(-1)
    blk = jnp.arange(ns)
    cur = qpos // SEL_BLOCK
    forced = (blk[None, :] == 0) | (blk[None, :] == cur[:, None]) | (blk[None, :] == cur[:, None] - 1)
    causal_blk = (blk * SEL_BLOCK)[None, :] <= qpos[:, None]
    imp = jnp.where(forced, imp + FORCE_BONUS, imp)
    imp = jnp.where(causal_blk, imp, -jnp.inf)
    n_top = min(TOP_N, ns)
    _, idx = lax.top_k(imp, n_top)
    kg = _gather_blocks(ks_blk, idx).reshape(b, NSA_KV_HEADS, qb, n_top * SEL_BLOCK, HEAD_DIM)
    vg = _gather_blocks(vs_blk, idx).reshape(b, NSA_KV_HEADS, qb, n_top * SEL_BLOCK, HEAD_DIM)
    key_pos = (idx[..., None] * SEL_BLOCK + jnp.arange(SEL_BLOCK)).reshape(b, NSA_KV_HEADS, qb, n_top * SEL_BLOCK)
    mask_s = (key_pos <= qpos[None, None, :, None])[:, :, None]
    p_s = masked_softmax(jnp.einsum('bqghd,bgqkd->bghqk', qg, kg) * scale, mask_s)
    o_s = jnp.einsum('bghqk,bgqkd->bqghd', p_s.astype(vg.dtype), vg)
    dist = qpos[:, None] - wpos[None, :]
    mask_w = (dist >= 0) & (dist < WINDOW) & (wpos >= 0)[None, :]
    p_w = masked_softmax(jnp.einsum('bqghd,blgd->bghql', qg, kw) * scale, mask_w)
    o_w = jnp.einsum('bghql,blgd->bqghd', p_w.astype(vw.dtype), vw)
    g = gates.reshape(b, qb, 3, NSA_KV_HEADS, NSA_HG, 1)
    o = g[:, :, 0] * o_c + g[:, :, 1] * o_s + g[:, :, 2] * o_w
    return o.reshape(b, qb, NSA_WIDTH).astype(q.dtype)


def nsa_mixer(q_raw, g_raw, kv_raw, pos, pe, w1, w2, past_kv, past_win):
    b, s, _ = q_raw.shape
    q = rope(q_raw.reshape(b, s, NSA_HEADS, HEAD_DIM), pos)
    gates = jax.nn.sigmoid(g_raw.astype(F32)).reshape(b, s, 3, NSA_HEADS)
    kv = kv_raw.reshape(b, s, 6, NSA_KV_HEADS, HEAD_DIM)
    k_cmp = rope(kv[:, :, 0], pos)
    k_slc = rope(kv[:, :, 2], pos)
    k_win = rope(kv[:, :, 4], pos)
    new_kv = jnp.stack([k_cmp, kv[:, :, 1], k_slc, kv[:, :, 3]], axis=2)
    new_win = jnp.stack([k_win, kv[:, :, 5]], axis=2)
    if past_kv is None:
        full_kv, win_kv = new_kv, new_win
        win_keep = min(WINDOW, s)
    else:
        full_kv = jnp.concatenate([past_kv.astype(new_kv.dtype), new_kv], axis=1)
        win_kv = jnp.concatenate([past_win.astype(new_win.dtype), new_win], axis=1)
        win_keep = past_win.shape[1]
    L = full_kv.shape[1]
    kc = nsa_compress(full_kv[:, :, 0], pe[0], w1[0], w2[0])
    vc = nsa_compress(full_kv[:, :, 1], pe[1], w1[1], w2[1])
    ns = -(-L // SEL_BLOCK)
    sel = jnp.pad(full_kv[:, :, 2:4], ((0, 0), (0, ns * SEL_BLOCK - L), (0, 0), (0, 0), (0, 0)))
    sel = sel.reshape(b, ns, SEL_BLOCK, 2, NSA_KV_HEADS, HEAD_DIM).transpose(3, 0, 4, 1, 2, 5)
    ks_blk, vs_blk = sel[0], sel[1]
    if past_kv is None:
        nb = s // Q_BLOCK
        win_pad = jnp.pad(win_kv, ((0, 0), (WINDOW, 0), (0, 0), (0, 0), (0, 0)))

        def one_block(args):
            qb_, gb_, q0 = args
            qpos = q0 + jnp.arange(Q_BLOCK, dtype=jnp.int32)
            wkv = lax.dynamic_slice_in_dim(win_pad, q0, WINDOW + Q_BLOCK, axis=1)
            wpos = q0 - WINDOW + jnp.arange(WINDOW + Q_BLOCK, dtype=jnp.int32)
            return nsa_attend_block(qb_, gb_, qpos, kc, vc, ks_blk, vs_blk, wkv[:, :, 0], wkv[:, :, 1], wpos)

        qs = q.reshape(b, nb, Q_BLOCK, NSA_HEADS, HEAD_DIM).swapaxes(0, 1)
        gs = gates.reshape(b, nb, Q_BLOCK, 3, NSA_HEADS).swapaxes(0, 1)
        out = lax.map(one_block, (qs, gs, jnp.arange(nb, dtype=jnp.int32) * Q_BLOCK))
        out = out.swapaxes(0, 1).reshape(b, s, NSA_WIDTH)
    else:
        past_len = past_kv.shape[1]
        wpos = past_len - win_keep + jnp.arange(win_keep + s, dtype=jnp.int32)
        out = nsa_attend_block(q, gates, pos, kc, vc, ks_blk, vs_blk, win_kv[:, :, 0], win_kv[:, :, 1], wpos)
    return out, new_kv, win_kv[:, -win_keep:]


def _chunked(t, n, c, pad):
    t = jnp.pad(t.astype(F32), [(0, 0), (0, pad)] + [(0, 0)] * (t.ndim - 2))
    return t.reshape((t.shape[0], n, c) + t.shape[2:]).swapaxes(0, 1)


def ssd_scan(x, dt, a, bm, cm, h0):
    b, s = x.shape[:2]
    c = min(SSD_CHUNK, s)
    n = -(-s // c)
    pad = n * c - s
    hg = SSD_HEADS // SSD_GROUPS
    xs = (_chunked(x, n, c, pad), _chunked(dt, n, c, pad), _chunked(bm, n, c, pad), _chunked(cm, n, c, pad))
    tri = jnp.tril(jnp.ones((c, c), bool))[None, :, :, None]

    def step(h, inp):
        xc, dtc, bc, cc = inp
        cum = jnp.cumsum(dtc * a, axis=1)
        xdt = xc * dtc[..., None]
        diff = cum[:, :, None, :] - cum[:, None, :, :]
        lmat = jnp.where(tri, jnp.exp(jnp.where(tri, diff, 0.0)), 0.0)
        cb = jnp.repeat(jnp.einsum('btgn,bsgn->btsg', cc, bc), hg, axis=-1)
        ch = jnp.repeat(cc, hg, axis=2)
        bh = jnp.repeat(bc, hg, axis=2)
        y = jnp.einsum('btsh,bshp->bthp', cb * lmat, xdt) \
            + jnp.einsum('bthn,bhpn->bthp', ch, h) * jnp.exp(cum)[..., None]
        decay_out = jnp.exp(cum[:, -1:, :] - cum)
        h = h * jnp.exp(cum[:, -1])[:, :, None, None] \
            + jnp.einsum('bshn,bshp->bhpn', bh * decay_out[..., None], xdt)
        return h, y

    h, ys = lax.scan(step, h0.astype(F32), xs)
    y = ys.swapaxes(0, 1).reshape(b, n * c, SSD_HEADS, SSD_HEAD_DIM)[:, :s]
    return y, h


def gla_scan(q, k, v, lg, s0):
    b, s = q.shape[:2]
    c = min(GLA_CHUNK, s)
    n = -(-s // c)
    pad = n * c - s
    xs = (_chunked(q, n, c, pad), _chunked(k, n, c, pad), _chunked(v, n, c, pad), _chunked(lg, n, c, pad))
    tri = jnp.tril(jnp.ones((c, c), bool))[None, None]

    def step(st, inp):
        qc, kc, vc, gc = inp
        bcum = jnp.cumsum(gc, axis=1)
        qg = qc * jnp.exp(bcum)
        kg = kc * jnp.exp(-bcum)
        att = jnp.where(tri, jnp.einsum('bthd,bshd->bhts', qg, kg), 0.0)
        o = jnp.einsum('bhts,bshv->bthv', att, vc) + jnp.einsum('bthd,bhdv->bthv', qg, st)
        blast = bcum[:, -1]
        st = st * jnp.exp(blast)[..., None] \
            + jnp.einsum('bshd,bshv->bhdv', kc * jnp.exp(blast[:, None] - bcum), vc)
        return st, o

    st, os_ = lax.scan(step, s0.astype(F32), xs)
    o = os_.swapaxes(0, 1).reshape(b, n * c, GLA_HEADS, GLA_DV)[:, :s]
    return o, st


def even_layer(h, pos, e, W, past):
    b, s, _ = h.shape
    proj = h @ W["w_in_even"][e]
    q_a, g_a, kv_a, z_a, z_b, xbc, dt_raw = _split(proj, EVEN_SPLIT)
    if past is None:
        past_kv = past_win = None
        prev = jnp.zeros((b, SSD_CONV - 1, SSD_CONV_CH), h.dtype)
        h0 = jnp.zeros((b, SSD_HEADS, SSD_HEAD_DIM, SSD_STATE), F32)
    else:
        past_kv, past_win, prev, h0 = past
    o_a, new_kv, new_win = nsa_mixer(q_a, g_a, kv_a, pos, W["nsa_cmp_pe"][e], W["nsa_cmp_w1"][e],
                                     W["nsa_cmp_w2"][e], past_kv, past_win)
    o_a = o_a * jax.nn.silu(z_a)
    xp = jnp.concatenate([prev.astype(xbc.dtype), xbc], axis=1)
    cw = W["ssd_conv_w"][e]
    conv = W["ssd_conv_b"][e]
    for i in range(SSD_CONV):
        conv = conv + xp[:, i:i + s] * cw[i]
    new_conv = xp[:, -(SSD_CONV - 1):]
    xs, bm, cm = _split(jax.nn.silu(conv), [SSD_INNER, SSD_GROUPS * SSD_STATE, SSD_GROUPS * SSD_STATE])
    xs = xs.reshape(b, s, SSD_HEADS, SSD_HEAD_DIM)
    dt = jax.nn.softplus(dt_raw.astype(F32) + W["ssd_dt_bias"][e].astype(F32))
    a = -jnp.exp(W["ssd_a_log"][e].astype(F32))
    y, new_ssd = ssd_scan(xs, dt, a, bm.reshape(b, s, SSD_GROUPS, SSD_STATE),
                          cm.reshape(b, s, SSD_GROUPS, SSD_STATE), h0)
    y = y + xs.astype(F32) * W["ssd_d"][e].astype(F32)[:, None]
    y = y.reshape(b, s, SSD_INNER) * jax.nn.silu(z_b.astype(F32))
    y = rms_norm(y, W["ssd_norm_w"][e]).astype(h.dtype)
    out = jnp.concatenate([o_a.astype(h.dtype), y], axis=-1) @ W["w_out_even"][e]
    return out, (new_kv, new_win, new_conv, new_ssd.astype(h.dtype))


def odd_layer(h, o_idx, W, past):
    b, s, _ = h.shape
    proj = h @ W["w_in_odd"][o_idx]
    q, k, v, glr, r = _split(proj, ODD_SPLIT)
    q = q.reshape(b, s, GLA_HEADS, GLA_DK) * (GLA_DK ** -0.5)
    k = k.reshape(b, s, GLA_HEADS, GLA_DK)
    v = v.reshape(b, s, GLA_HEADS, GLA_DV)
    lg = jax.nn.log_sigmoid((glr @ W["gla_w_gate2"][o_idx] + W["gla_b_gate"][o_idx]).astype(F32)) / GLA_TAU
    lg = lg.reshape(b, s, GLA_HEADS, GLA_DK)
    s0 = jnp.zeros((b, GLA_HEADS, GLA_DK, GLA_DV), F32) if past is None else past
    o, st = gla_scan(q, k, v, lg, s0)
    o = rms_norm(o, W["gla_norm_w"][o_idx]).reshape(b, s, GLA_VAL_WIDTH)
    o = (o * jax.nn.silu(r.astype(F32))).astype(h.dtype)
    return o @ W["w_out_odd"][o_idx], st.astype(h.dtype)


def apply_layer(x, c, pos, l, W, past):
    mod = jax.nn.silu(c) @ W["w_mod"][l] + W["b_mod"][l]
    shift, scale, gate = jnp.split(mod[:, None, :], 3, axis=-1)
    h = rms_norm(x, W["norm_w"][l]) * (1 + scale) + shift
    if l % 2 == 0:
        out, st = even_layer(h, pos, l // 2, W, past)
    else:
        out, st = odd_layer(h, l // 2, W, past)
    return x + gate * out, st


def setup_inputs(seed: int = 0) -> dict:
    key = jax.random.key(seed)
    kit = iter(list(jax.random.split(key, 40)))

    def nrm(shape, scale):
        return jax.random.normal(next(kit), shape, F32) * scale

    n_pages = PAST_LEN // PAGE_SIZE
    n_used = DEC_BATCH * n_pages
    n_pool = n_used + -(-n_used // 4)
    win_keep = min(WINDOW, PAST_LEN)
    d = D_MODEL
    inp = {}
    inp["x_prompt"] = nrm((BATCH, SEQ, d), 1.0)
    inp["x_sample"] = nrm((DEC_BATCH, DEC_SEQ, d), 1.0)
    inp["c_prompt"] = nrm((BATCH, d), 1.0)
    inp["c_sample"] = nrm((DEC_BATCH, d), 1.0)
    inp["cache_nsa"] = nrm((N_EVEN, n_pool, PAGE_SIZE, 4, NSA_KV_HEADS, HEAD_DIM), 1.0)
    inp["cache_nsa_win"] = nrm((N_EVEN, DEC_BATCH, win_keep, 2, NSA_KV_HEADS, HEAD_DIM), 1.0)
    inp["state_ssd_conv"] = nrm((N_EVEN, DEC_BATCH, SSD_CONV - 1, SSD_CONV_CH), 1.0)
    inp["state_ssd"] = nrm((N_EVEN, DEC_BATCH, SSD_HEADS, SSD_HEAD_DIM, SSD_STATE), 0.1)
    inp["state_gla"] = nrm((N_ODD, DEC_BATCH, GLA_HEADS, GLA_DK, GLA_DV), 0.1)
    inp["page_table"] = jax.random.permutation(next(kit), n_pool)[:n_used].reshape(DEC_BATCH, n_pages).astype(jnp.int32)
    inp["norm_w"] = 1.0 + nrm((DEPTH, d), 0.02)
    inp["w_mod"] = nrm((DEPTH, d, 3 * d), 0.5 * d ** -0.5)
    inp["b_mod"] = nrm((DEPTH, 3 * d), 0.02)
    inp["w_in_even"] = nrm((N_EVEN, d, EVEN_IN), d ** -0.5)
    inp["w_out_even"] = nrm((N_EVEN, MIX_EVEN, d), MIX_EVEN ** -0.5)
    inp["nsa_cmp_pe"] = nrm((N_EVEN, 2, CMP_BLOCK, HEAD_DIM), 0.1)
    inp["nsa_cmp_w1"] = nrm((N_EVEN, 2, CMP_BLOCK * HEAD_DIM, CMP_HIDDEN), (CMP_BLOCK * HEAD_DIM) ** -0.5)
    inp["nsa_cmp_w2"] = nrm((N_EVEN, 2, CMP_HIDDEN, HEAD_DIM), CMP_HIDDEN ** -0.5)
    inp["ssd_conv_w"] = nrm((N_EVEN, SSD_CONV, SSD_CONV_CH), SSD_CONV ** -0.5)
    inp["ssd_conv_b"] = nrm((N_EVEN, SSD_CONV_CH), 0.02)
    dt0 = jnp.exp(jax.random.uniform(next(kit), (N_EVEN, SSD_HEADS), F32,
                                     minval=math.log(1e-3), maxval=math.log(1e-1)))
    inp["ssd_dt_bias"] = dt0 + jnp.log(-jnp.expm1(-dt0))
    inp["ssd_a_log"] = jnp.log(jax.random.uniform(next(kit), (N_EVEN, SSD_HEADS), F32, minval=1.0, maxval=16.0))
    inp["ssd_d"] = 1.0 + nrm((N_EVEN, SSD_HEADS), 0.02)
    inp["ssd_norm_w"] = 1.0 + nrm((N_EVEN, SSD_INNER), 0.02)
    inp["w_in_odd"] = nrm((N_ODD, d, ODD_IN), d ** -0.5)
    inp["gla_w_gate2"] = nrm((N_ODD, GLA_GATE_RANK, GLA_KEY_WIDTH), GLA_GATE_RANK ** -0.5)
    inp["gla_b_gate"] = nrm((N_ODD, GLA_KEY_WIDTH), 0.02)
    inp["gla_norm_w"] = 1.0 + nrm((N_ODD, GLA_DV), 0.02)
    inp["w_out_odd"] = nrm((N_ODD, GLA_VAL_WIDTH, d), GLA_VAL_WIDTH ** -0.5)
    inp["final_norm_w"] = 1.0 + nrm((d,), 0.02)
    return inp


def reference(x_prompt, x_sample, c_prompt, c_sample, cache_nsa, cache_nsa_win, state_ssd_conv, state_ssd,
              state_gla, page_table, norm_w, w_mod, b_mod, w_in_even, w_out_even, nsa_cmp_pe, nsa_cmp_w1,
              nsa_cmp_w2, ssd_conv_w, ssd_conv_b, ssd_dt_bias, ssd_a_log, ssd_d, ssd_norm_w, w_in_odd,
              gla_w_gate2, gla_b_gate, gla_norm_w, w_out_odd, final_norm_w):
    W = dict(norm_w=norm_w, w_mod=w_mod, b_mod=b_mod, w_in_even=w_in_even, w_out_even=w_out_even,
             nsa_cmp_pe=nsa_cmp_pe, nsa_cmp_w1=nsa_cmp_w1, nsa_cmp_w2=nsa_cmp_w2, ssd_conv_w=ssd_conv_w,
             ssd_conv_b=ssd_conv_b, ssd_dt_bias=ssd_dt_bias, ssd_a_log=ssd_a_log, ssd_d=ssd_d,
             ssd_norm_w=ssd_norm_w, w_in_odd=w_in_odd, gla_w_gate2=gla_w_gate2, gla_b_gate=gla_b_gate,
             gla_norm_w=gla_norm_w, w_out_odd=w_out_odd)
    dec_b = x_sample.shape[0]
    past_len = page_table.shape[1] * PAGE_SIZE
    pos_p = jnp.arange(x_prompt.shape[1], dtype=jnp.int32)
    pos_s = past_len + jnp.arange(x_sample.shape[1], dtype=jnp.int32)
    hp, hs = x_prompt, x_sample
    kv_p, kv_s, win_p, win_s, cv_p, cv_s, ss_p, ss_s, gl_p, gl_s = ([] for _ in range(10))
    for l in range(DEPTH):
        e = l // 2
        if l % 2 == 0:
            past_kv = cache_nsa[e][page_table].reshape(dec_b, past_len, 4, NSA_KV_HEADS, HEAD_DIM)
            past = (past_kv, cache_nsa_win[e], state_ssd_conv[e], state_ssd[e])
            hp, st_p = apply_layer(hp, c_prompt, pos_p, l, W, None)
            hs, st_s = apply_layer(hs, c_sample, pos_s, l, W, past)
            kv_p.append(st_p[0]); win_p.append(st_p[1]); cv_p.append(st_p[2]); ss_p.append(st_p[3])
            kv_s.append(st_s[0]); win_s.append(st_s[1]); cv_s.append(st_s[2]); ss_s.append(st_s[3])
        else:
            hp, st_p = apply_layer(hp, c_prompt, pos_p, l, W, None)
            hs, st_s = apply_layer(hs, c_sample, pos_s, l, W, state_gla[e])
            gl_p.append(st_p); gl_s.append(st_s)
    y_prompt = rms_norm(hp, final_norm_w)
    y_sample = rms_norm(hs, final_norm_w)
    nsa_rows_prompt = jnp.stack(kv_p)
    nsa_rows_sample = jnp.stack(kv_s)
    nsa_win_prompt = jnp.stack(win_p)
    nsa_win_sample = jnp.stack(win_s)
    ssd_conv_prompt = jnp.stack(cv_p)
    ssd_conv_sample = jnp.stack(cv_s)
    ssd_state_prompt = jnp.stack(ss_p)
    ssd_state_sample = jnp.stack(ss_s)
    gla_state_prompt = jnp.stack(gl_p)
    gla_state_sample = jnp.stack(gl_s)
    return (y_prompt, y_sample, nsa_rows_prompt, nsa_rows_sample, nsa_win_prompt, nsa_win_sample,
            ssd_conv_prompt, ssd_conv_sample, ssd_state_prompt, ssd_state_sample,
            gla_state_prompt, gla_state_sample)
```

```python
import functools
import math

import jax
import jax.numpy as jnp
import numpy as np
from jax import lax
from jax.experimental import pallas as pl
from jax.experimental.pallas import tpu as pltpu

F32 = jnp.float32
BF16 = jnp.bfloat16

PAGE_SIZE = 128
NSA_HEADS = 8
NSA_KV_HEADS = 2
HEAD_DIM = 64
NSA_WIDTH = NSA_HEADS * HEAD_DIM
NSA_HG = NSA_HEADS // NSA_KV_HEADS
CMP_BLOCK = 32
CMP_HIDDEN = 2 * HEAD_DIM
SEL_BLOCK = 64
TOP_N = 16
WINDOW = 512
FORCE_BONUS = 1.0e4
ROPE_THETA = 10000.0
SSD_HEADS = 16
SSD_HEAD_DIM = 64
SSD_INNER = SSD_HEADS * SSD_HEAD_DIM
SSD_GROUPS = 2
SSD_STATE = 128
SSD_CONV = 4
SSD_CONV_CH = SSD_INNER + 2 * SSD_GROUPS * SSD_STATE
GLA_HEADS = 4
GLA_DK = 128
GLA_DV = 256
GLA_KEY_WIDTH = GLA_HEADS * GLA_DK
GLA_VAL_WIDTH = GLA_HEADS * GLA_DV
GLA_GATE_RANK = 16
GLA_TAU = 16.0
GLA_CHUNK = 32
SSD_CHUNK = 64

LANES = 128
SUBLANES = 8
VMEM_LIMIT = 56 * 1024 * 1024

NEG_BIG = -1.0e30
DEC_PAD = 8

E_XBC, E_Q, E_ZB, E_ZA, E_KV, E_GD, E_END = 0, 1536, 2048, 3072, 3584, 4352, 4480
GD_GATE, GD_DT = 0, 24
O_Q, O_K, O_V, O_R, O_GLR, O_END = 0, 512, 1024, 2048, 3072, 3200


def _cparams(sem):
    return pltpu.CompilerParams(dimension_semantics=sem, vmem_limit_bytes=VMEM_LIMIT)


def _dot(a, b):
    return jnp.dot(a, b, preferred_element_type=F32)


def _dot_nt(a, b):
    return lax.dot_general(a, b, (((1,), (1,)), ((), ())), preferred_element_type=F32)


def _dot_tn(a, b):
    return lax.dot_general(a, b, (((0,), (0,)), ((), ())), preferred_element_type=F32)


def _split3(x):
    hi = x.astype(BF16)
    r1 = x - hi.astype(F32)
    mid = r1.astype(BF16)
    lo = (r1 - mid.astype(F32)).astype(BF16)
    return hi, mid, lo


def _dot3(x, w01):
    hi, mid, lo = _split3(x)
    return _dot(hi, w01) + _dot(mid, w01) + _dot(lo, w01)


def _dot3_tn(x, w01):
    hi, mid, lo = _split3(x)
    return _dot_tn(hi, w01) + _dot_tn(mid, w01) + _dot_tn(lo, w01)


def _dot3_wx(w01, x):
    hi, mid, lo = _split3(x)
    return _dot(w01, hi) + _dot(w01, mid) + _dot(w01, lo)


def _silu(x):
    return x * (1.0 / (1.0 + jnp.exp(-x)))


def _sigmoid(x):
    return 1.0 / (1.0 + jnp.exp(-x))


def _softplus(x):
    return jnp.maximum(x, 0.0) + jnp.log1p(jnp.exp(-jnp.abs(x)))


def _log_sigmoid(x):
    return jnp.minimum(x, 0.0) - jnp.log1p(jnp.exp(-jnp.abs(x)))


def _mod_kernel(c_ref, w_ref, b_ref, o_ref):
    a = _silu(c_ref[...]).astype(BF16)
    o_ref[0] = _dot(a, w_ref[0]) + b_ref[0]


def mod_all(c_all, w_mod_bf, b_mod, tn=1024):
    nl, d, n = w_mod_bf.shape
    m = c_all.shape[0]
    return pl.pallas_call(
        _mod_kernel,
        grid=(nl, n // tn),
        in_specs=[
            pl.BlockSpec((m, d), lambda l, j: (0, 0)),
            pl.BlockSpec((1, d, tn), lambda l, j: (l, 0, j)),
            pl.BlockSpec((1, 1, tn), lambda l, j: (l, 0, j)),
        ],
        out_specs=pl.BlockSpec((1, m, tn), lambda l, j: (l, 0, j)),
        out_shape=jax.ShapeDtypeStruct((nl, m, n), F32),
        compiler_params=_cparams(("arbitrary", "arbitrary")),
        name="mod_all",
    )(c_all, w_mod_bf, b_mod.reshape(nl, 1, n))


def _inproj_kernel(x_ref, nw_ref, sc_ref, sh_ref, w_ref, o_ref, *, n_chunk):
    x = x_ref[...]
    ms = jnp.mean(x * x, axis=-1, keepdims=True)
    y = x * lax.rsqrt(ms + 1e-6) * nw_ref[...]
    h = (y * (1.0 + sc_ref[0]) + sh_ref[0]).astype(BF16)
    n = o_ref.shape[1]
    for n0 in range(0, n, n_chunk):
        n1 = min(n0 + n_chunk, n)
        o_ref[:, n0:n1] = _dot(h, w_ref[:, n0:n1])


def inproj(x, nw, scale, shift, w_bf, tm, rows_per_mod):
    t, d = x.shape
    n = w_bf.shape[1]
    r = scale.shape[1]
    mod_map = lambda i: ((i * tm) // rows_per_mod, 0, 0)
    return pl.pallas_call(
        functools.partial(_inproj_kernel, n_chunk=640),
        grid=(t // tm,),
        in_specs=[
            pl.BlockSpec((tm, d), lambda i: (i, 0)),
            pl.BlockSpec((1, d), lambda i: (0, 0)),
            pl.BlockSpec((1, r, d), mod_map),
            pl.BlockSpec((1, r, d), mod_map),
            pl.BlockSpec((d, n), lambda i: (0, 0)),
        ],
        out_specs=pl.BlockSpec((tm, n), lambda i: (i, 0)),
        out_shape=jax.ShapeDtypeStruct((t, n), F32),
        compiler_params=_cparams(("arbitrary",)),
        name="inproj",
    )(x, nw.reshape(1, d), scale, shift, w_bf)


def _outproj_kernel(*refs, n_in, final_norm):
    a_refs = refs[:n_in]
    w_refs = refs[n_in:2 * n_in]
    x_ref, g_ref, fw_ref, o_ref = refs[2 * n_in:]
    acc = _dot(a_refs[0][...].astype(BF16), w_refs[0][...])
    for a_ref, w_ref in zip(a_refs[1:], w_refs[1:]):
        acc = acc + _dot(a_ref[...].astype(BF16), w_ref[...])
    y = x_ref[...] + g_ref[0] * acc
    if final_norm:
        ms = jnp.mean(y * y, axis=-1, keepdims=True)
        y = y * lax.rsqrt(ms + 1e-6) * fw_ref[...]
    o_ref[...] = y


def outproj(a_list, w_list, x, gate, final_w, tm, rows_per_mod, final_norm):
    t, d = x.shape
    r = gate.shape[1]
    n_in = len(a_list)
    in_specs = [pl.BlockSpec((tm, a.shape[1]), lambda i: (i, 0)) for a in a_list]
    in_specs += [pl.BlockSpec(w.shape, lambda i: (0, 0)) for w in w_list]
    in_specs += [
        pl.BlockSpec((tm, d), lambda i: (i, 0)),
        pl.BlockSpec((1, r, d), lambda i: ((i * tm) // rows_per_mod, 0, 0)),
        pl.BlockSpec((1, d), lambda i: (0, 0)),
    ]
    return pl.pallas_call(
        functools.partial(_outproj_kernel, n_in=n_in, final_norm=final_norm),
        grid=(t // tm,),
        in_specs=in_specs,
        out_specs=pl.BlockSpec((tm, d), lambda i: (i, 0)),
        out_shape=jax.ShapeDtypeStruct((t, d), F32),
        compiler_params=_cparams(("arbitrary",)),
        name="outproj",
    )(*a_list, *w_list, x, gate, final_w.reshape(1, d))


def _rope_tile(t, cos, sin_signed, lane):
    fwd = pltpu.roll(t, LANES - HEAD_DIM // 2, 1)
    bwd = pltpu.roll(t, HEAD_DIM // 2, 1)
    partner = jnp.where((lane % HEAD_DIM) < HEAD_DIM // 2, fwd, bwd)
    return t * cos + partner * sin_signed


def _nsa_prep_kernel(q_ref, kv01_ref, kv23_ref, kv45_ref, cos_ref, sin_ref,
                     rows_ref, win_ref, qrot_ref, *hm_refs, head_major):
    cos = cos_ref[...]
    sin = sin_ref[...]
    lane = lax.broadcasted_iota(jnp.int32, cos.shape, 1)
    rope = lambda t: _rope_tile(t, cos, sin, lane)
    kv01, kv23, kv45 = kv01_ref[...], kv23_ref[...], kv45_ref[...]
    k_slc = rope(kv23[:, :LANES])
    v_slc = kv23[:, LANES:]
    k_win = rope(kv45[:, :LANES])
    v_win = kv45[:, LANES:]
    rows_ref[:, 0:LANES] = rope(kv01[:, :LANES])
    rows_ref[:, LANES:2 * LANES] = kv01[:, LANES:]
    rows_ref[:, 2 * LANES:3 * LANES] = k_slc
    rows_ref[:, 3 * LANES:4 * LANES] = v_slc
    win_ref[:, 0:LANES] = k_win
    win_ref[:, LANES:2 * LANES] = v_win
    q = q_ref[...]
    qr = [rope(q[:, j * LANES:(j + 1) * LANES]) for j in range(NSA_WIDTH // LANES)]
    for j, t in enumerate(qr):
        qrot_ref[:, j * LANES:(j + 1) * LANES] = t
    if head_major:
        qh_ref, ks_ref, vs_ref, kw_ref, vw_ref = hm_refs
        scale = HEAD_DIM ** -0.5
        for h in range(NSA_HEADS):
            lo = (h % 2) * HEAD_DIM
            qh_ref[0, h] = (qr[h // 2][:, lo:lo + HEAD_DIM] * scale).astype(BF16)
        for g in range(NSA_KV_HEADS):
            lo = g * HEAD_DIM
            ks_ref[0, g] = k_slc[:, lo:lo + HEAD_DIM].astype(BF16)
            vs_ref[0, g] = v_slc[:, lo:lo + HEAD_DIM].astype(BF16)
            kw_ref[0, g] = k_win[:, lo:lo + HEAD_DIM].astype(BF16)
            vw_ref[0, g] = v_win[:, lo:lo + HEAD_DIM].astype(BF16)


def nsa_prep(proj, cos_tab, sin_tab, tm, seq, head_major):
    t = proj.shape[0]
    tab_blocks = cos_tab.shape[0] // tm
    sb = seq // tm
    out_shape = [jax.ShapeDtypeStruct((t, 4 * LANES), F32),
                 jax.ShapeDtypeStruct((t, 2 * LANES), F32),
                 jax.ShapeDtypeStruct((t, NSA_WIDTH), F32)]
    out_specs = [pl.BlockSpec((tm, 4 * LANES), lambda i: (i, 0)),
                 pl.BlockSpec((tm, 2 * LANES), lambda i: (i, 0)),
                 pl.BlockSpec((tm, NSA_WIDTH), lambda i: (i, 0))]
    if head_major:
        nb = t // seq
        hm_map = lambda i: (i // sb, 0, i % sb, 0)
        out_shape.append(jax.ShapeDtypeStruct((nb, NSA_HEADS, seq, HEAD_DIM), BF16))
        out_specs.append(pl.BlockSpec((1, NSA_HEADS, tm, HEAD_DIM), hm_map))
        for _ in range(4):
            out_shape.append(jax.ShapeDtypeStruct((nb, NSA_KV_HEADS, seq, HEAD_DIM), BF16))
            out_specs.append(pl.BlockSpec((1, NSA_KV_HEADS, tm, HEAD_DIM), hm_map))
    kvb = E_KV // (2 * LANES)
    return pl.pallas_call(
        functools.partial(_nsa_prep_kernel, head_major=head_major),
        grid=(t // tm,),
        in_specs=[
            pl.BlockSpec((tm, NSA_WIDTH), lambda i: (i, E_Q // NSA_WIDTH)),
            pl.BlockSpec((tm, 2 * LANES), lambda i: (i, kvb)),
            pl.BlockSpec((tm, 2 * LANES), lambda i: (i, kvb + 1)),
            pl.BlockSpec((tm, 2 * LANES), lambda i: (i, kvb + 2)),
            pl.BlockSpec((tm, LANES), lambda i: (i % tab_blocks, 0)),
            pl.BlockSpec((tm, LANES), lambda i: (i % tab_blocks, 0)),
        ],
        out_specs=out_specs,
        out_shape=out_shape,
        compiler_params=_cparams(("arbitrary",)),
        name="nsa_prep",
    )(proj, proj, proj, proj, cos_tab, sin_tab)


def _nsa_compress_kernel(x_ref, pe_ref, w1_ref, w2_ref, o_ref):
    r = x_ref.shape[0]
    row_w = 4 * LANES
    acc = jnp.zeros((r, 4 * CMP_HIDDEN), F32)
    for t in range(CMP_BLOCK):
        xt = x_ref[:, t * row_w:t * row_w + 2 * LANES] + pe_ref[t:t + 1, :]
        acc = acc + _dot(xt.astype(BF16), w1_ref[t])
    o_ref[...] = _dot(_silu(acc).astype(BF16), w2_ref[...])


def nsa_compress(xblk, pe_tab, w1_blk, w2_blk, r):
    nb, kdim = xblk.shape
    return pl.pallas_call(
        _nsa_compress_kernel,
        grid=(nb // r,),
        in_specs=[
            pl.BlockSpec((r, kdim), lambda i: (i, 0)),
            pl.BlockSpec(pe_tab.shape, lambda i: (0, 0)),
            pl.BlockSpec(w1_blk.shape, lambda i: (0, 0, 0)),
            pl.BlockSpec(w2_blk.shape, lambda i: (0, 0)),
        ],
        out_specs=pl.BlockSpec((r, 4 * HEAD_DIM), lambda i: (i, 0)),
        out_shape=jax.ShapeDtypeStruct((nb, 4 * HEAD_DIM), F32),
        compiler_params=_cparams(("arbitrary",)),
        name="nsa_compress",
    )(xblk, pe_tab, w1_blk, w2_blk)


def _page_gather_kernel(pt_ref, pool_ref, o_ref):
    b = pl.program_id(0)
    for j in range(o_ref.shape[1]):
        o_ref[0, j:j + 1, :] = pool_ref[pl.ds(pt_ref[b, j], 1), :]


def page_gather(pool, page_table):
    nb, npg = page_table.shape
    w = pool.shape[1]
    return pl.pallas_call(
        _page_gather_kernel,
        grid_spec=pltpu.PrefetchScalarGridSpec(
            num_scalar_prefetch=1,
            grid=(nb,),
            in_specs=[pl.BlockSpec(pool.shape, lambda b, pt: (0, 0))],
            out_specs=pl.BlockSpec((1, npg, w), lambda b, pt: (b, 0, 0)),
        ),
        out_shape=jax.ShapeDtypeStruct((nb, npg, w), F32),
        compiler_params=_cparams(("arbitrary",)),
        name="page_gather",
    )(page_table, pool)


def _masked_softmax3(s, mask):
    s = jnp.where(mask[None], s, -jnp.inf)
    m = jnp.max(s, axis=-1, keepdims=True)
    m = jnp.where(m > -jnp.inf, m, 0.0)
    p = jnp.exp(s - m)
    d = jnp.sum(p, axis=-1, keepdims=True)
    return p / jnp.where(d > 0, d, 1.0)


def _cmp_branch_and_select(q2, kc, vc, qpos, nq):
    s = _dot_nt(q2, kc).reshape(NSA_HG, nq, 2 * LANES)
    col = lax.broadcasted_iota(jnp.int32, (1, 2 * LANES), 1)
    cblk = 2 * (col % LANES) + col // LANES
    cmp_end = (cblk + 1) * CMP_BLOCK - 1
    p = _masked_softmax3(s, cmp_end <= qpos)
    o_c = _dot(p.reshape(NSA_HG * nq, 2 * LANES).astype(BF16), vc).reshape(NSA_HG, nq, HEAD_DIM)
    imp = jnp.sum(p, axis=0)
    imp = imp[:, :LANES] + imp[:, LANES:]
    j = lax.broadcasted_iota(jnp.int32, (1, LANES), 1)
    cur = qpos // SEL_BLOCK
    forced = jnp.where(j == 0, 1, jnp.where(j == cur, 1, jnp.where(j == cur - 1, 1, 0))) > 0
    causal = j <= cur
    imp = jnp.where(forced, imp + FORCE_BONUS, imp)
    imp = jnp.where(causal, imp, -jnp.inf)

    def pick(_, c):
        imp_c, sel_c = c
        idx = jnp.argmax(imp_c, axis=-1, keepdims=True).astype(jnp.int32)
        hit = j == idx
        return jnp.where(hit, -jnp.inf, imp_c), jnp.where(hit, 1.0, sel_c)

    _, sel = lax.fori_loop(0, TOP_N, pick, (imp, jnp.zeros((nq, LANES), F32)))
    return o_c, jnp.where(causal, sel, 0.0)


def _flash_init(nq):
    return (jnp.full((NSA_HG, nq, 1), NEG_BIG, F32), jnp.zeros((NSA_HG, nq, 1), F32),
            jnp.zeros((NSA_HG, nq, HEAD_DIM), F32))


def _flash_update(carry, s, mask, v):
    m, l, acc = carry
    s = jnp.where(mask[None], s, NEG_BIG)
    m_new = jnp.maximum(m, jnp.max(s, axis=-1, keepdims=True))
    alpha = jnp.exp(m - m_new)
    p = jnp.where(mask[None], jnp.exp(s - m_new), 0.0)
    l = alpha * l + jnp.sum(p, axis=-1, keepdims=True)
    h, q, n = p.shape
    pv = _dot(p.reshape(h * q, n).astype(v.dtype), v).reshape(h, q, HEAD_DIM)
    return m_new, l, alpha * acc + pv


def _flash_finish(carry):
    _, l, acc = carry
    return acc / jnp.where(l > 0, l, 1.0)


def _gate_column(sig, lane, col):
    return jnp.sum(jnp.where(lane == col, sig, 0.0), axis=-1, keepdims=True)


def _nsa_attn_kernel(q_ref, kc_ref, vc_ref, ks_ref, vs_ref, kw_ref, vw_ref, gd_ref, za_ref, e_ref,
                     o_ref, *, tq, tk, twk):
    g = pl.program_id(1)
    q0 = pl.program_id(2) * tq
    q2 = q_ref[0].reshape(NSA_HG * tq, HEAD_DIM)
    qpos = q0 + lax.broadcasted_iota(jnp.int32, (tq, 1), 0)
    o_c, sel = _cmp_branch_and_select(q2, kc_ref[0, 0], vc_ref[0, 0], qpos, tq)
    selb = sel.astype(BF16)

    def sel_body(t, carry):
        k0 = pl.multiple_of(t * tk, tk)
        k = ks_ref[0, 0, pl.ds(k0, tk), :]
        v = vs_ref[0, 0, pl.ds(k0, tk), :]
        s = _dot_nt(q2, k).reshape(NSA_HG, tq, tk)
        blk_on = _dot(selb, e_ref[:, pl.ds(k0, tk)])
        kpos = k0 + lax.broadcasted_iota(jnp.int32, (1, tk), 1)
        mask = jnp.where(kpos <= qpos, blk_on, 0.0) > 0.5
        return _flash_update(carry, s, mask, v)

    o_s = _flash_finish(lax.fori_loop(0, (q0 + tq + tk - 1) // tk, sel_body, _flash_init(tq)))

    def win_body(t, carry):
        k0 = pl.multiple_of(t * twk, twk)
        k = kw_ref[0, 0, pl.ds(k0, twk), :]
        v = vw_ref[0, 0, pl.ds(k0, twk), :]
        s = _dot_nt(q2, k).reshape(NSA_HG, tq, twk)
        dist = qpos - (k0 + lax.broadcasted_iota(jnp.int32, (1, twk), 1))
        mask = jnp.where(dist >= 0, dist, WINDOW) < WINDOW
        return _flash_update(carry, s, mask, v)

    w_lo = jnp.maximum(q0 - WINDOW, 0) // twk
    w_hi = (q0 + tq + twk - 1) // twk
    o_w = _flash_finish(lax.fori_loop(w_lo, w_hi, win_body, _flash_init(tq)))

    sig = _sigmoid(gd_ref[...])
    lane = lax.broadcasted_iota(jnp.int32, sig.shape, 1)
    outs = []
    for h in range(NSA_HG):
        head = g * NSA_HG + h
        g_c = _gate_column(sig, lane, GD_GATE + head)
        g_s = _gate_column(sig, lane, GD_GATE + NSA_HEADS + head)
        g_w = _gate_column(sig, lane, GD_GATE + 2 * NSA_HEADS + head)
        outs.append(g_c * o_c[h] + g_s * o_s[h] + g_w * o_w[h])
    o_ref[...] = jnp.concatenate(outs, axis=-1) * _silu(za_ref[...])


def nsa_attn_prompt(proj, qh, kc, vc, ks, vs, kw, vw, e_mat, tq=128, tk=512, twk=128):
    nb, _, seq, _ = qh.shape
    nq = seq // tq
    gw = NSA_HG * HEAD_DIM
    kv_spec = pl.BlockSpec((1, 1, seq, HEAD_DIM), lambda b, g, i: (b, g, 0, 0))
    c_spec = pl.BlockSpec((1, 1, 2 * LANES, HEAD_DIM), lambda b, g, i: (b, g, 0, 0))
    return pl.pallas_call(
        functools.partial(_nsa_attn_kernel, tq=tq, tk=tk, twk=twk),
        grid=(nb, NSA_KV_HEADS, nq),
        in_specs=[
            pl.BlockSpec((1, NSA_HG, tq, HEAD_DIM), lambda b, g, i: (b, g, i, 0)),
            c_spec, c_spec, kv_spec, kv_spec, kv_spec, kv_spec,
            pl.BlockSpec((tq, LANES), lambda b, g, i: (b * nq + i, E_GD // LANES)),
            pl.BlockSpec((tq, gw), lambda b, g, i: (b * nq + i, E_ZA // gw + g)),
            pl.BlockSpec(e_mat.shape, lambda b, g, i: (0, 0)),
        ],
        out_specs=pl.BlockSpec((tq, gw), lambda b, g, i: (b * nq + i, g)),
        out_shape=jax.ShapeDtypeStruct((nb * seq, NSA_WIDTH), F32),
        compiler_params=_cparams(("arbitrary", "arbitrary", "arbitrary")),
        name="nsa_attn_prompt",
    )(qh, kc, vc, ks, vs, kw, vw, proj, proj, e_mat)


def _nsa_dec_kernel(pt_ref, q_ref, kc_ref, vc_ref, page_ref, rows_ref, pwin_ref, wnew_ref, gd_ref, za_ref,
                    e_ref, o_ref, sel_sc, m_sc, l_sc, acc_sc, oc_sc, *, past_len, n_pages):
    del pt_ref
    j = pl.program_id(1)
    nq = DEC_PAD
    scale = HEAD_DIM ** -0.5
    qpos = past_len + lax.broadcasted_iota(jnp.int32, (nq, 1), 0)
    q = q_ref[0]

    def q_group(g):
        parts = [q[:, (g * NSA_HG + h) * HEAD_DIM:(g * NSA_HG + h + 1) * HEAD_DIM] for h in range(NSA_HG)]
        return jnp.concatenate(parts, axis=0) * scale

    qg = [q_group(g) for g in range(NSA_KV_HEADS)]

    @pl.when(j == 0)
    def _():
        for g in range(NSA_KV_HEADS):
            o_c, sel = _cmp_branch_and_select(qg[g].astype(BF16), kc_ref[0, g].astype(BF16),
                                              vc_ref[0, g].astype(BF16), qpos, nq)
            oc_sc[g] = o_c
            sel_sc[g] = sel
            m0, l0, a0 = _flash_init(nq)
            m_sc[g] = m0
            l_sc[g] = l0
            acc_sc[g] = a0

    page = page_ref[0]
    k0 = pl.multiple_of(j * PAGE_SIZE, PAGE_SIZE)
    kpos = k0 + lax.broadcasted_iota(jnp.int32, (1, PAGE_SIZE), 1)
    for g in range(NSA_KV_HEADS):
        k = page[:, g * HEAD_DIM:(g + 1) * HEAD_DIM].astype(BF16)
        v = page[:, LANES + g * HEAD_DIM:LANES + (g + 1) * HEAD_DIM].astype(BF16)
        s = _dot_nt(qg[g].astype(BF16), k).reshape(NSA_HG, nq, PAGE_SIZE)
        blk_on = _dot(sel_sc[g].astype(BF16), e_ref[:, pl.ds(k0, PAGE_SIZE)])
        mask = jnp.where(kpos <= qpos, blk_on, 0.0) > 0.5
        m1, l1, a1 = _flash_update((m_sc[g], l_sc[g], acc_sc[g]), s, mask, v)
        m_sc[g] = m1
        l_sc[g] = l1
        acc_sc[g] = a1

    @pl.when(j == n_pages - 1)
    def _():
        rows = rows_ref[0]
        pwin = pwin_ref[0]
        wnew = wnew_ref[0]
        win_keep = pwin.shape[0]
        sig = _sigmoid(gd_ref[0])
        lane = lax.broadcasted_iota(jnp.int32, sig.shape, 1)
        npos = past_len + lax.broadcasted_iota(jnp.int32, (1, nq), 1)
        wpos = past_len - win_keep + lax.broadcasted_iota(jnp.int32, (1, win_keep), 1)
        outs = []
        for g in range(NSA_KV_HEADS):
            kn = rows[:, 2 * LANES + g * HEAD_DIM:2 * LANES + (g + 1) * HEAD_DIM]
            vn = rows[:, 3 * LANES + g * HEAD_DIM:3 * LANES + (g + 1) * HEAD_DIM]
            s = _dot_nt(qg[g], kn).reshape(NSA_HG, nq, nq)
            lane_s = lax.broadcasted_iota(jnp.int32, (nq, LANES), 1)
            blk_on = _gate_column(sel_sc[g], lane_s, past_len // SEL_BLOCK)
            mask = jnp.where(npos <= qpos, blk_on, 0.0) > 0.5
            o_s = _flash_finish(_flash_update((m_sc[g], l_sc[g], acc_sc[g]), s, mask, vn))
            kwp = pwin[:, g * HEAD_DIM:(g + 1) * HEAD_DIM].astype(BF16)
            vwp = pwin[:, LANES + g * HEAD_DIM:LANES + (g + 1) * HEAD_DIM].astype(BF16)
            s = _dot_nt(qg[g].astype(BF16), kwp).reshape(NSA_HG, nq, win_keep)
            dist = qpos - wpos
            mask = jnp.where(wpos >= 0, jnp.where(dist >= 0, dist, WINDOW), WINDOW) < WINDOW
            carry = _flash_update(_flash_init(nq), s, mask, vwp)
            kwn = wnew[:, g * HEAD_DIM:(g + 1) * HEAD_DIM]
            vwn = wnew[:, LANES + g * HEAD_DIM:LANES + (g + 1) * HEAD_DIM]
            s = _dot_nt(qg[g], kwn).reshape(NSA_HG, nq, nq)
            dist = qpos - npos
            mask = jnp.where(dist >= 0, dist, WINDOW) < WINDOW
            o_w = _flash_finish(_flash_update(carry, s, mask, vwn))
            o_c = oc_sc[g]
            for h in range(NSA_HG):
                head = g * NSA_HG + h
                g_c = _gate_column(sig, lane, GD_GATE + head)
                g_s = _gate_column(sig, lane, GD_GATE + NSA_HEADS + head)
                g_w = _gate_column(sig, lane, GD_GATE + 2 * NSA_HEADS + head)
                outs.append(g_c * o_c[h] + g_s * o_s[h] + g_w * o_w[h])
        o_ref[0] = jnp.concatenate(outs, axis=-1) * _silu(za_ref[0])


def nsa_attn_decode(page_table, qrot, kc, vc, cache, rows, pwin, wnew, proj3, e_mat, past_len):
    nb, npg = page_table.shape
    win_keep = pwin.shape[1]
    per_b = lambda blk: pl.BlockSpec(blk, lambda b, j, pt: (b,) + (0,) * (len(blk) - 1))
    hq = (NSA_KV_HEADS, NSA_HG, DEC_PAD)
    return pl.pallas_call(
        functools.partial(_nsa_dec_kernel, past_len=past_len, n_pages=npg),
        grid_spec=pltpu.PrefetchScalarGridSpec(
            num_scalar_prefetch=1,
            grid=(nb, npg),
            in_specs=[
                per_b((1, DEC_PAD, NSA_WIDTH)),
                per_b((1, NSA_KV_HEADS, 2 * LANES, HEAD_DIM)),
                per_b((1, NSA_KV_HEADS, 2 * LANES, HEAD_DIM)),
                pl.BlockSpec((1, PAGE_SIZE, 2 * LANES), lambda b, j, pt: (pt[b, j], 0, 1)),
                per_b((1, DEC_PAD, 4 * LANES)),
                per_b((1, win_keep, 2 * LANES)),
                per_b((1, DEC_PAD, 2 * LANES)),
                pl.BlockSpec((1, DEC_PAD, LANES), lambda b, j, pt: (b, 0, E_GD // LANES)),
                pl.BlockSpec((1, DEC_PAD, NSA_WIDTH), lambda b, j, pt: (b, 0, E_ZA // NSA_WIDTH)),
                pl.BlockSpec(e_mat.shape, lambda b, j, pt: (0, 0)),
            ],
            out_specs=per_b((1, DEC_PAD, NSA_WIDTH)),
            scratch_shapes=[
                pltpu.VMEM((NSA_KV_HEADS, DEC_PAD, LANES), F32),
                pltpu.VMEM(hq + (1,), F32),
                pltpu.VMEM(hq + (1,), F32),
                pltpu.VMEM(hq + (HEAD_DIM,), F32),
                pltpu.VMEM(hq + (HEAD_DIM,), F32),
            ],
        ),
        out_shape=jax.ShapeDtypeStruct((nb, DEC_PAD, NSA_WIDTH), F32),
        compiler_params=_cparams(("arbitrary", "arbitrary")),
        name="nsa_attn_decode",
    )(page_table, qrot, kc, vc, cache, rows, pwin, wnew, proj3, proj3, e_mat)


def _tri_masks(c):
    row = lax.broadcasted_iota(jnp.int32, (c, c), 0)
    col = lax.broadcasted_iota(jnp.int32, (c, c), 1)
    lower = row >= col
    return lower, jnp.where(lower, 1.0, 0.0).astype(BF16), jnp.where(row <= col, 1.0, 0.0).astype(BF16)


def _ssd_kernel(zb_ref, xbc_ref, gd_ref, prev_ref, h0_ref, cw_ref, cb_ref, dtb_ref, alog_ref, alog16_ref,
                dexp_ref, nw_ref, e16_ref, y_ref, so_ref, h_sc, xbuf, *, chunk, n_valid):
    c = pl.program_id(1)
    hp = SSD_HEADS // SSD_GROUPS * SSD_HEAD_DIM
    halo = SUBLANES

    @pl.when(c == 0)
    def _():
        h_sc[...] = h0_ref[0]
        xbuf[0:halo] = prev_ref[0]

    xbuf[halo:halo + chunk] = xbc_ref[...]
    conv = cb_ref[...]
    for i in range(SSD_CONV):
        lo = halo - (SSD_CONV - 1) + i
        conv = conv + xbuf[lo:lo + chunk] * cw_ref[i:i + 1, :]
    xbuf[0:halo] = xbuf[chunk:chunk + halo]
    act = _silu(conv)
    xs = act[:, :SSD_INNER]
    bm = act[:, SSD_INNER:SSD_INNER + SSD_GROUPS * SSD_STATE]
    cm = act[:, SSD_INNER + SSD_GROUPS * SSD_STATE:]

    dt16 = _softplus(gd_ref[...] + dtb_ref[...])[:, GD_DT:GD_DT + SSD_HEADS]
    if n_valid is not None:
        tok = c * chunk + lax.broadcasted_iota(jnp.int32, (chunk, 1), 0)
        dt16 = jnp.where(tok < n_valid, dt16, 0.0)
    dt = _dot3(dt16, e16_ref[...])
    da = dt * (-jnp.exp(alog_ref[...]))
    da16 = dt16 * (-jnp.exp(alog16_ref[...]))
    lower, tri, tri_t = _tri_masks(chunk)
    cum = _dot3_wx(tri, da)
    cum16 = _dot3_wx(tri, da16)
    cum16_t = _dot3_tn(da16, tri_t)
    ones8 = jnp.ones((chunk, SUBLANES), BF16)
    decay_col = jnp.exp(_dot3_tn(da, ones8)[:, 0:1])
    xdt = xs * dt
    cum_last = cum[chunk - 1:chunk, :]
    xw = (xdt * jnp.exp(cum_last - cum)).astype(BF16)
    xdt_b = xdt.astype(BF16)
    lane = lax.broadcasted_iota(jnp.int32, (chunk, LANES), 1)
    low_half = lane < SSD_HEAD_DIM

    y_intra = [None] * (SSD_INNER // LANES)
    y_inter = []
    for g in range(SSD_GROUPS):
        cg = cm[:, g * SSD_STATE:(g + 1) * SSD_STATE].astype(BF16)
        bg = bm[:, g * SSD_STATE:(g + 1) * SSD_STATE].astype(BF16)
        h_g = h_sc[g * hp:(g + 1) * hp, :]
        cb = _dot_nt(cg, bg)
        y_inter.append(_dot_nt(cg, h_g.astype(BF16)))
        for hh in range(SSD_HEADS // SSD_GROUPS):
            h = g * (SSD_HEADS // SSD_GROUPS) + hh
            diff = cum16[:, h:h + 1] - cum16_t[h:h + 1, :]
            lmat = jnp.where(lower, jnp.exp(jnp.where(lower, diff, 0.0)), 0.0)
            m = (cb * lmat).astype(BF16)
            pair = h // 2
            x_pair = xdt_b[:, pair * LANES:(pair + 1) * LANES]
            keep = low_half if h % 2 == 0 else jnp.logical_not(low_half)
            contrib = _dot(m, jnp.where(keep, x_pair, jnp.zeros_like(x_pair)))
            y_intra[pair] = contrib if y_intra[pair] is None else y_intra[pair] + contrib
        h_sc[g * hp:(g + 1) * hp, :] = (h_g * decay_col[g * hp:(g + 1) * hp, :]
                                        + _dot_tn(xw[:, g * hp:(g + 1) * hp], bg))
    y = jnp.concatenate(y_intra, axis=-1) + jnp.concatenate(y_inter, axis=-1) * jnp.exp(cum)
    y = y + xs * dexp_ref[...]
    y = y * _silu(zb_ref[...])
    ms = jnp.mean(y * y, axis=-1, keepdims=True)
    y_ref[...] = y * lax.rsqrt(ms + 1e-6) * nw_ref[...]

    @pl.when(c == pl.num_programs(1) - 1)
    def _():
        so_ref[0] = h_sc[...]


def ssd_mixer(proj, prev8, h0, conv_w, conv_b, dtb_row, alog_exp, alog16, d_exp, norm_w, e16, nb, seq, chunk,
              n_valid):
    nch = seq // chunk
    row = lambda b, c: b * nch + c
    full = lambda a: pl.BlockSpec(a.shape, lambda b, c: (0,) * a.ndim)
    return pl.pallas_call(
        functools.partial(_ssd_kernel, chunk=chunk, n_valid=n_valid),
        grid=(nb, nch),
        in_specs=[
            pl.BlockSpec((chunk, SSD_INNER), lambda b, c: (row(b, c), E_ZB // SSD_INNER)),
            pl.BlockSpec((chunk, SSD_CONV_CH), lambda b, c: (row(b, c), E_XBC // SSD_CONV_CH)),
            pl.BlockSpec((chunk, LANES), lambda b, c: (row(b, c), E_GD // LANES)),
            pl.BlockSpec((1, SUBLANES, SSD_CONV_CH), lambda b, c: (b, 0, 0)),
            pl.BlockSpec((1, SSD_INNER, SSD_STATE), lambda b, c: (b, 0, 0)),
            full(conv_w), full(conv_b), full(dtb_row), full(alog_exp), full(alog16), full(d_exp), full(norm_w),
            full(e16),
        ],
        out_specs=[
            pl.BlockSpec((chunk, SSD_INNER), lambda b, c: (row(b, c), 0)),
            pl.BlockSpec((1, SSD_INNER, SSD_STATE), lambda b, c: (b, 0, 0)),
        ],
        out_shape=[jax.ShapeDtypeStruct((nb * seq, SSD_INNER), F32),
                   jax.ShapeDtypeStruct((nb, SSD_INNER, SSD_STATE), F32)],
        scratch_shapes=[pltpu.VMEM((SSD_INNER, SSD_STATE), F32),
                        pltpu.VMEM((SUBLANES + chunk, SSD_CONV_CH), F32)],
        compiler_params=_cparams(("arbitrary", "arbitrary")),
        name="ssd_mixer",
    )(proj, proj, proj, prev8, h0, conv_w, conv_b, dtb_row, alog_exp, alog16, d_exp, norm_w, e16)


def _gla_kernel(q_ref, k_ref, v_ref, r_ref, glr_ref, s0_ref, wg_ref, bg_ref, nw_ref, o_ref, so_ref, s_sc,
                *, chunk, n_valid):
    c = pl.program_id(1)

    @pl.when(c == 0)
    def _():
        s_sc[...] = s0_ref[0]

    lg = _log_sigmoid(_dot(glr_ref[...].astype(BF16), wg_ref[...]) + bg_ref[...]) * (1.0 / GLA_TAU)
    k = k_ref[...]
    if n_valid is not None:
        tok = c * chunk + lax.broadcasted_iota(jnp.int32, (chunk, 1), 0)
        lg = jnp.where(tok < n_valid, lg, 0.0)
        k = jnp.where(tok < n_valid, k, 0.0)
    lower, tri, _ = _tri_masks(chunk)
    bcum = _dot3_wx(tri, lg)
    blast = bcum[chunk - 1:chunk, :]
    ones8 = jnp.ones((chunk, SUBLANES), BF16)
    decay_col = jnp.exp(_dot3_tn(lg, ones8)[:, 0:1])
    qg = (q_ref[...] * (GLA_DK ** -0.5) * jnp.exp(bcum)).astype(BF16)
    kg = (k * jnp.exp(-bcum)).astype(BF16)
    kd = (k * jnp.exp(blast - bcum)).astype(BF16)
    v = v_ref[...].astype(BF16)
    r = r_ref[...]
    nw = nw_ref[...]
    for h in range(GLA_HEADS):
        kcols = slice(h * GLA_DK, (h + 1) * GLA_DK)
        vcols = slice(h * GLA_DV, (h + 1) * GLA_DV)
        s_h = s_sc[kcols, :]
        att = jnp.where(lower, _dot_nt(qg[:, kcols], kg[:, kcols]), 0.0)
        o = _dot(att.astype(BF16), v[:, vcols]) + _dot(qg[:, kcols], s_h.astype(BF16))
        s_sc[kcols, :] = s_h * decay_col[kcols, :] + _dot_tn(kd[:, kcols], v[:, vcols])
        ms = jnp.mean(o * o, axis=-1, keepdims=True)
        o = o * lax.rsqrt(ms + 1e-6) * nw
        o_ref[:, vcols] = o * _silu(r[:, vcols])

    @pl.when(c == pl.num_programs(1) - 1)
    def _():
        so_ref[0] = s_sc[...]


def gla_mixer(proj, s0, wg_pad, bg, norm_w, nb, seq, chunk, n_valid):
    nch = seq // chunk
    row = lambda b, c: b * nch + c
    full = lambda a: pl.BlockSpec(a.shape, lambda b, c: (0,) * a.ndim)
    srows = GLA_HEADS * GLA_DK
    return pl.pallas_call(
        functools.partial(_gla_kernel, chunk=chunk, n_valid=n_valid),
        grid=(nb, nch),
        in_specs=[
            pl.BlockSpec((chunk, GLA_KEY_WIDTH), lambda b, c: (row(b, c), O_Q // GLA_KEY_WIDTH)),
            pl.BlockSpec((chunk, GLA_KEY_WIDTH), lambda b, c: (row(b, c), O_K // GLA_KEY_WIDTH)),
            pl.BlockSpec((chunk, GLA_VAL_WIDTH), lambda b, c: (row(b, c), O_V // GLA_VAL_WIDTH)),
            pl.BlockSpec((chunk, GLA_VAL_WIDTH), lambda b, c: (row(b, c), O_R // GLA_VAL_WIDTH)),
            pl.BlockSpec((chunk, LANES), lambda b, c: (row(b, c), O_GLR // LANES)),
            pl.BlockSpec((1, srows, GLA_DV), lambda b, c: (b, 0, 0)),
            full(wg_pad), full(bg), full(norm_w),
        ],
        out_specs=[
            pl.BlockSpec((chunk, GLA_VAL_WIDTH), lambda b, c: (row(b, c), 0)),
            pl.BlockSpec((1, srows, GLA_DV), lambda b, c: (b, 0, 0)),
        ],
        out_shape=[jax.ShapeDtypeStruct((nb * seq, GLA_VAL_WIDTH), F32),
                   jax.ShapeDtypeStruct((nb, srows, GLA_DV), F32)],
        scratch_shapes=[pltpu.VMEM((srows, GLA_DV), F32)],
        compiler_params=_cparams(("arbitrary", "arbitrary")),
        name="gla_mixer",
    )(proj, proj, proj, proj, proj, s0, wg_pad, bg, norm_w)


def _even_weight(w):
    q, g_a, kv, z_a, z_b, xbc, dt = jnp.split(w, list(np.cumsum(
        [NSA_WIDTH, 3 * NSA_HEADS, 6 * NSA_KV_HEADS * HEAD_DIM, NSA_WIDTH, SSD_INNER, SSD_CONV_CH])), axis=-1)
    pad = jnp.zeros((w.shape[0], E_END - E_GD - 3 * NSA_HEADS - SSD_HEADS), w.dtype)
    return jnp.concatenate([xbc, q, z_b, z_a, kv, g_a, dt, pad], axis=-1).astype(BF16)


def _odd_weight(w):
    q, k, v, glr, r = jnp.split(w, list(np.cumsum(
        [GLA_KEY_WIDTH, GLA_KEY_WIDTH, GLA_VAL_WIDTH, GLA_GATE_RANK])), axis=-1)
    pad = jnp.zeros((w.shape[0], O_END - O_GLR - GLA_GATE_RANK), w.dtype)
    return jnp.concatenate([q, k, v, r, glr, pad], axis=-1).astype(BF16)


def _rope_tables(pos):
    half = HEAD_DIM // 2
    inv = ROPE_THETA ** (-jnp.arange(half, dtype=F32) / half)
    ang = pos.astype(F32)[:, None] * inv[None, :]
    cos, sin = jnp.cos(ang), jnp.sin(ang)
    reps = LANES // HEAD_DIM
    return jnp.tile(jnp.concatenate([cos, cos], -1), (1, reps)), jnp.tile(jnp.concatenate([-sin, sin], -1), (1, reps))


def _compress_weights(pe, w1, w2):
    pe_tab = jnp.concatenate([pe[0], pe[0], pe[1], pe[1]], axis=-1)
    w1r = w1.reshape(2, CMP_BLOCK, HEAD_DIM, CMP_HIDDEN)
    w1_blk = jnp.zeros((CMP_BLOCK, 4 * HEAD_DIM, 4 * CMP_HIDDEN), F32)
    w2_blk = jnp.zeros((4 * CMP_HIDDEN, 4 * HEAD_DIM), F32)
    for part in range(4):
        src = part // 2
        w1_blk = w1_blk.at[:, part * HEAD_DIM:(part + 1) * HEAD_DIM,
                           part * CMP_HIDDEN:(part + 1) * CMP_HIDDEN].set(w1r[src])
        w2_blk = w2_blk.at[part * CMP_HIDDEN:(part + 1) * CMP_HIDDEN,
                           part * HEAD_DIM:(part + 1) * HEAD_DIM].set(w2[src])
    return pe_tab, w1_blk.astype(BF16), w2_blk.astype(BF16)


def _cmp_layout(cmp, nb, nc, dtype):
    c = cmp.reshape(nb, nc, 2, NSA_KV_HEADS, HEAD_DIM).transpose(2, 0, 3, 1, 4)
    halves = []
    for par in range(2):
        h = c[:, :, :, par::2]
        halves.append(jnp.pad(h, ((0, 0), (0, 0), (0, 0), (0, LANES - h.shape[3]), (0, 0))))
    c = jnp.concatenate(halves, axis=3).astype(dtype)
    return c[0], c[1]


def _sel_expand(n_keys):
    blk = jnp.arange(n_keys, dtype=jnp.int32) // SEL_BLOCK
    return (blk[None, :] == jnp.arange(LANES, dtype=jnp.int32)[:, None]).astype(BF16)


def kernel(x_prompt, x_sample, c_prompt, c_sample, cache_nsa, cache_nsa_win, state_ssd_conv, state_ssd, state_gla, page_table, norm_w, w_mod, b_mod, w_in_even, w_out_even, nsa_cmp_pe, nsa_cmp_w1, nsa_cmp_w2, ssd_conv_w, ssd_conv_b, ssd_dt_bias, ssd_a_log, ssd_d, ssd_norm_w, w_in_odd, gla_w_gate2, gla_b_gate, gla_norm_w, w_out_odd, final_norm_w):
    bp, sp, d = x_prompt.shape
    bs, ss, _ = x_sample.shape
    depth = norm_w.shape[0]
    n_pool = cache_nsa.shape[1]
    npg = page_table.shape[1]
    past_len = npg * PAGE_SIZE
    win_keep = cache_nsa_win.shape[2]
    assert sp % 512 == 0 and sp // SEL_BLOCK <= LANES and sp // CMP_BLOCK <= 2 * LANES
    assert ss <= DEC_PAD and ss < CMP_BLOCK and past_len % SEL_BLOCK == 0 and ss >= SSD_CONV - 1
    assert past_len // SEL_BLOCK + 1 <= LANES and (bs * DEC_PAD) % 256 == 0 and (n_pool * 4) % 128 == 0
    tp, td = bp * sp, bs * DEC_PAD
    tm_d = 256

    c_all = jnp.concatenate([c_prompt, c_sample], axis=0)
    c_all = jnp.pad(c_all, ((0, -c_all.shape[0] % SUBLANES), (0, 0)))
    mod = mod_all(c_all, w_mod.astype(BF16), b_mod)

    def mods(l):
        shift, scale, gate = jnp.split(mod[l], 3, axis=-1)
        mp = [m[:bp].reshape(bp, 1, d) for m in (shift, scale, gate)]
        ms = [jnp.repeat(m[bp:bp + bs], DEC_PAD, axis=0).reshape(td // tm_d, tm_d, d) for m in (shift, scale, gate)]
        return mp, ms

    xp = x_prompt.reshape(tp, d)
    xs = jnp.pad(x_sample, ((0, 0), (0, DEC_PAD - ss), (0, 0))).reshape(td, d)

    cos_p, sin_p = _rope_tables(jnp.arange(sp, dtype=jnp.int32))
    cos_s, sin_s = _rope_tables(past_len + jnp.arange(DEC_PAD, dtype=jnp.int32))
    cos_s, sin_s = jnp.tile(cos_s, (tm_d // DEC_PAD, 1)), jnp.tile(sin_s, (tm_d // DEC_PAD, 1))
    e_mat_p = _sel_expand(sp)
    e_mat_s = _sel_expand(past_len)
    e16 = (jnp.arange(SSD_INNER, dtype=jnp.int32)[None, :] // SSD_HEAD_DIM
           == jnp.arange(SSD_HEADS, dtype=jnp.int32)[:, None]).astype(BF16)

    outs = {k: [] for k in ("kv_p", "kv_s", "win_p", "win_s", "cv_p", "cv_s", "ss_p", "ss_s", "gl_p", "gl_s")}
    for l in range(depth):
        e = l // 2
        (shift_p, scale_p, gate_p), (shift_s, scale_s, gate_s) = mods(l)
        last = l == depth - 1
        if l % 2 == 0:
            w_in = _even_weight(w_in_even[e])
            w_out = w_out_even[e].astype(BF16)
            w_out_a, w_out_b = w_out[:NSA_WIDTH], w_out[NSA_WIDTH:]
            pe_tab, w1_blk, w2_blk = _compress_weights(nsa_cmp_pe[e], nsa_cmp_w1[e], nsa_cmp_w2[e])
            conv_w, conv_b = ssd_conv_w[e], ssd_conv_b[e].reshape(1, SSD_CONV_CH)
            dtb_row = jnp.zeros((1, LANES), F32).at[0, GD_DT:GD_DT + SSD_HEADS].set(ssd_dt_bias[e])
            alog16 = ssd_a_log[e].reshape(1, SSD_HEADS)
            alog_exp = jnp.repeat(ssd_a_log[e], SSD_HEAD_DIM).reshape(1, SSD_INNER)
            d_exp = jnp.repeat(ssd_d[e], SSD_HEAD_DIM).reshape(1, SSD_INNER)
            ssd_nw = ssd_norm_w[e].reshape(1, SSD_INNER)
            ssd_args = (conv_w, conv_b, dtb_row, alog_exp, alog16, d_exp, ssd_nw, e16)

            proj = inproj(xp, norm_w[l], scale_p, shift_p, w_in, 256, sp)
            rows, win, _, qh, ks, vs, kw, vw = nsa_prep(proj, cos_p, sin_p, 256, sp, True)
            nc = sp // CMP_BLOCK
            cmp = nsa_compress(rows.reshape(tp // CMP_BLOCK, CMP_BLOCK * 4 * LANES), pe_tab, w1_blk, w2_blk,
                               min(128, tp // CMP_BLOCK))
            kc, vc = _cmp_layout(cmp, bp, nc, BF16)
            o_a = nsa_attn_prompt(proj, qh, kc, vc, ks, vs, kw, vw, e_mat_p)
            y, h_new = ssd_mixer(proj, jnp.zeros((bp, SUBLANES, SSD_CONV_CH), F32),
                                 jnp.zeros((bp, SSD_INNER, SSD_STATE), F32), *ssd_args, bp, sp, SSD_CHUNK, None)
            xp = outproj([o_a, y], [w_out_a, w_out_b], xp, gate_p, final_norm_w, 512, sp, last)
            outs["kv_p"].append(rows.reshape(bp, sp, 4, NSA_KV_HEADS, HEAD_DIM))
            outs["win_p"].append(win.reshape(bp, sp, 2, NSA_KV_HEADS, HEAD_DIM)[:, -min(WINDOW, sp):])
            outs["cv_p"].append(proj[:, E_XBC:E_XBC + SSD_CONV_CH].reshape(bp, sp, SSD_CONV_CH)[:, -(SSD_CONV - 1):])
            outs["ss_p"].append(h_new.reshape(bp, SSD_HEADS, SSD_HEAD_DIM, SSD_STATE))

            proj = inproj(xs, norm_w[l], scale_s, shift_s, w_in, tm_d, tm_d)
            rows, win, qrot = nsa_prep(proj, cos_s, sin_s, tm_d, DEC_PAD, False)
            cache_e = cache_nsa[e]
            pool_blocks = n_pool * PAGE_SIZE // CMP_BLOCK
            cmp_pool = nsa_compress(cache_e.reshape(pool_blocks, CMP_BLOCK * 4 * LANES), pe_tab, w1_blk, w2_blk, 128)
            per_page = PAGE_SIZE // CMP_BLOCK
            cmp_s = page_gather(cmp_pool.reshape(n_pool, per_page * 4 * HEAD_DIM), page_table)
            kc, vc = _cmp_layout(cmp_s.reshape(bs * npg * per_page, 4 * HEAD_DIM), bs, npg * per_page, F32)
            o_a = nsa_attn_decode(page_table, qrot.reshape(bs, DEC_PAD, NSA_WIDTH), kc, vc,
                                  cache_e.reshape(n_pool, PAGE_SIZE, 4 * LANES), rows.reshape(bs, DEC_PAD, 4 * LANES),
                                  cache_nsa_win[e].reshape(bs, win_keep, 2 * LANES),
                                  win.reshape(bs, DEC_PAD, 2 * LANES), proj.reshape(bs, DEC_PAD, E_END),
                                  e_mat_s, past_len).reshape(td, NSA_WIDTH)
            prev8 = jnp.pad(state_ssd_conv[e], ((0, 0), (SUBLANES - (SSD_CONV - 1), 0), (0, 0)))
            y, h_new = ssd_mixer(proj, prev8, state_ssd[e].reshape(bs, SSD_INNER, SSD_STATE), *ssd_args,
                                 bs, DEC_PAD, DEC_PAD, ss)
            xs = outproj([o_a, y], [w_out_a, w_out_b], xs, gate_s, final_norm_w, tm_d, tm_d, last)
            outs["kv_s"].append(rows.reshape(bs, DEC_PAD, 4, NSA_KV_HEADS, HEAD_DIM)[:, :ss])
            new_win = win.reshape(bs, DEC_PAD, 2, NSA_KV_HEADS, HEAD_DIM)[:, :ss]
            outs["win_s"].append(jnp.concatenate([cache_nsa_win[e], new_win], axis=1)[:, -win_keep:])
            new_xbc = proj[:, E_XBC:E_XBC + SSD_CONV_CH].reshape(bs, DEC_PAD, SSD_CONV_CH)[:, :ss]
            outs["cv_s"].append(jnp.concatenate([state_ssd_conv[e], new_xbc], axis=1)[:, -(SSD_CONV - 1):])
            outs["ss_s"].append(h_new.reshape(bs, SSD_HEADS, SSD_HEAD_DIM, SSD_STATE))
        else:
            w_in = _odd_weight(w_in_odd[e])
            w_out = w_out_odd[e].astype(BF16)
            wg_pad = jnp.pad(gla_w_gate2[e], ((0, LANES - GLA_GATE_RANK), (0, 0))).astype(BF16)
            bg = gla_b_gate[e].reshape(1, GLA_KEY_WIDTH)
            gnw = gla_norm_w[e].reshape(1, GLA_DV)
            srows = GLA_HEADS * GLA_DK
            proj = inproj(xp, norm_w[l], scale_p, shift_p, w_in, 256, sp)
            o, st = gla_mixer(proj, jnp.zeros((bp, srows, GLA_DV), F32), wg_pad, bg, gnw, bp, sp, GLA_CHUNK, None)
            xp = outproj([o], [w_out], xp, gate_p, final_norm_w, 512, sp, last)
            outs["gl_p"].append(st.reshape(bp, GLA_HEADS, GLA_DK, GLA_DV))
            proj = inproj(xs, norm_w[l], scale_s, shift_s, w_in, tm_d, tm_d)
            o, st = gla_mixer(proj, state_gla[e].reshape(bs, srows, GLA_DV), wg_pad, bg, gnw, bs, DEC_PAD, DEC_PAD, ss)
            xs = outproj([o], [w_out], xs, gate_s, final_norm_w, tm_d, tm_d, last)
            outs["gl_s"].append(st.reshape(bs, GLA_HEADS, GLA_DK, GLA_DV))

    y_prompt = xp.reshape(bp, sp, d)
    y_sample = xs.reshape(bs, DEC_PAD, d)[:, :ss]
    st = {k: jnp.stack(v) for k, v in outs.items()}
    return (y_prompt, y_sample, st["kv_p"], st["kv_s"], st["win_p"], st["win_s"], st["cv_p"], st["cv_s"],
            st["ss_p"], st["ss_s"], st["gl_p"], st["gl_s"])
```

```python
import functools
import math

import jax
import jax.numpy as jnp
import numpy as np
from jax import lax
from jax.experimental import pallas as pl
from jax.experimental.pallas import tpu as pltpu

F32 = jnp.float32
BF16 = jnp.bfloat16

PAGE_SIZE = 128
NSA_HEADS = 8
NSA_KV_HEADS = 2
HEAD_DIM = 64
NSA_WIDTH = NSA_HEADS * HEAD_DIM
NSA_HG = NSA_HEADS // NSA_KV_HEADS
CMP_BLOCK = 32
CMP_HIDDEN = 2 * HEAD_DIM
SEL_BLOCK = 64
TOP_N = 16
WINDOW = 512
FORCE_BONUS = 1.0e4
ROPE_THETA = 10000.0
SSD_HEADS = 16
SSD_HEAD_DIM = 64
SSD_INNER = SSD_HEADS * SSD_HEAD_DIM
SSD_GROUPS = 2
SSD_STATE = 128
SSD_CONV = 4
SSD_CONV_CH = SSD_INNER + 2 * SSD_GROUPS * SSD_STATE
GLA_HEADS = 4
GLA_DK = 128
GLA_DV = 256
GLA_KEY_WIDTH = GLA_HEADS * GLA_DK
GLA_VAL_WIDTH = GLA_HEADS * GLA_DV
GLA_GATE_RANK = 16
GLA_TAU = 16.0
GLA_CHUNK = 32
SSD_CHUNK = 64

LANES = 128
SUBLANES = 8
VMEM_LIMIT = 56 * 1024 * 1024

NEG_BIG = -1.0e30
DEC_PAD = 8

E_XBC, E_Q, E_ZB, E_ZA, E_KV, E_GD, E_END = 0, 1536, 2048, 3072, 3584, 4352, 4480
GD_GATE, GD_DT = 0, 24
O_Q, O_K, O_V, O_R, O_GLR, O_END = 0, 512, 1024, 2048, 3072, 3200


def _cparams(sem):
    return pltpu.CompilerParams(dimension_semantics=sem, vmem_limit_bytes=VMEM_LIMIT)


def _dot(a, b):
    return jnp.dot(a, b, preferred_element_type=F32)


def _dot_nt(a, b):
    return lax.dot_general(a, b, (((1,), (1,)), ((), ())), preferred_element_type=F32)


def _dot_tn(a, b):
    return lax.dot_general(a, b, (((0,), (0,)), ((), ())), preferred_element_type=F32)


def _split3(x):
    hi = x.astype(BF16)
    r1 = x - hi.astype(F32)
    mid = r1.astype(BF16)
    lo = (r1 - mid.astype(F32)).astype(BF16)
    return hi, mid, lo


def _dot3(x, w01):
    hi, mid, lo = _split3(x)
    return _dot(hi, w01) + _dot(mid, w01) + _dot(lo, w01)


def _dot3_tn(x, w01):
    hi, mid, lo = _split3(x)
    return _dot_tn(hi, w01) + _dot_tn(mid, w01) + _dot_tn(lo, w01)


def _dot3_wx(w01, x):
    hi, mid, lo = _split3(x)
    return _dot(w01, hi) + _dot(w01, mid) + _dot(w01, lo)


def _silu(x):
    return x * (1.0 / (1.0 + jnp.exp(-x)))


def _sigmoid(x):
    return 1.0 / (1.0 + jnp.exp(-x))


def _softplus(x):
    return jnp.maximum(x, 0.0) + jnp.log1p(jnp.exp(-jnp.abs(x)))


def _log_sigmoid(x):
    return jnp.minimum(x, 0.0) - jnp.log1p(jnp.exp(-jnp.abs(x)))


def _mod_kernel(c_ref, w_ref, b_ref, o_ref):
    a = _silu(c_ref[...]).astype(BF16)
    o_ref[0] = _dot(a, w_ref[0]) + b_ref[0]


def mod_all(c_all, w_mod_bf, b_mod, tn=1024):
    nl, d, n = w_mod_bf.shape
    m = c_all.shape[0]
    return pl.pallas_call(
        _mod_kernel,
        grid=(nl, n // tn),
        in_specs=[
            pl.BlockSpec((m, d), lambda l, j: (0, 0)),
            pl.BlockSpec((1, d, tn), lambda l, j: (l, 0, j)),
            pl.BlockSpec((1, 1, tn), lambda l, j: (l, 0, j)),
        ],
        out_specs=pl.BlockSpec((1, m, tn), lambda l, j: (l, 0, j)),
        out_shape=jax.ShapeDtypeStruct((nl, m, n), F32),
        compiler_params=_cparams(("arbitrary", "arbitrary")),
        name="mod_all",
    )(c_all, w_mod_bf, b_mod.reshape(nl, 1, n))


def _inproj_kernel(x_ref, nw_ref, sc_ref, sh_ref, w_ref, o_ref, *, n_chunk):
    x = x_ref[...]
    ms = jnp.mean(x * x, axis=-1, keepdims=True)
    y = x * lax.rsqrt(ms + 1e-6) * nw_ref[...]
    h = (y * (1.0 + sc_ref[0]) + sh_ref[0]).astype(BF16)
    n = o_ref.shape[1]
    for n0 in range(0, n, n_chunk):
        n1 = min(n0 + n_chunk, n)
        o_ref[:, n0:n1] = _dot(h, w_ref[:, n0:n1])


def inproj(x, nw, scale, shift, w_bf, tm, rows_per_mod):
    t, d = x.shape
    n = w_bf.shape[1]
    r = scale.shape[1]
    mod_map = lambda i: ((i * tm) // rows_per_mod, 0, 0)
    return pl.pallas_call(
        functools.partial(_inproj_kernel, n_chunk=640),
        grid=(t // tm,),
        in_specs=[
            pl.BlockSpec((tm, d), lambda i: (i, 0)),
            pl.BlockSpec((1, d), lambda i: (0, 0)),
            pl.BlockSpec((1, r, d), mod_map),
            pl.BlockSpec((1, r, d), mod_map),
            pl.BlockSpec((d, n), lambda i: (0, 0)),
        ],
        out_specs=pl.BlockSpec((tm, n), lambda i: (i, 0)),
        out_shape=jax.ShapeDtypeStruct((t, n), F32),
        compiler_params=_cparams(("arbitrary",)),
        name="inproj",
    )(x, nw.reshape(1, d), scale, shift, w_bf)


def _outproj_kernel(*refs, n_in, final_norm):
    a_refs = refs[:n_in]
    w_refs = refs[n_in:2 * n_in]
    x_ref, g_ref, fw_ref, o_ref = refs[2 * n_in:]
    acc = _dot(a_refs[0][...].astype(BF16), w_refs[0][...])
    for a_ref, w_ref in zip(a_refs[1:], w_refs[1:]):
        acc = acc + _dot(a_ref[...].astype(BF16), w_ref[...])
    y = x_ref[...] + g_ref[0] * acc
    if final_norm:
        ms = jnp.mean(y * y, axis=-1, keepdims=True)
        y = y * lax.rsqrt(ms + 1e-6) * fw_ref[...]
    o_ref[...] = y


def outproj(a_list, w_list, x, gate, final_w, tm, rows_per_mod, final_norm):
    t, d = x.shape
    r = gate.shape[1]
    n_in = len(a_list)
    in_specs = [pl.BlockSpec((tm, a.shape[1]), lambda i: (i, 0)) for a in a_list]
    in_specs += [pl.BlockSpec(w.shape, lambda i: (0, 0)) for w in w_list]
    in_specs += [
        pl.BlockSpec((tm, d), lambda i: (i, 0)),
        pl.BlockSpec((1, r, d), lambda i: ((i * tm) // rows_per_mod, 0, 0)),
        pl.BlockSpec((1, d), lambda i: (0, 0)),
    ]
    return pl.pallas_call(
        functools.partial(_outproj_kernel, n_in=n_in, final_norm=final_norm),
        grid=(t // tm,),
        in_specs=in_specs,
        out_specs=pl.BlockSpec((tm, d), lambda i: (i, 0)),
        out_shape=jax.ShapeDtypeStruct((t, d), F32),
        compiler_params=_cparams(("arbitrary",)),
        name="outproj",
    )(*a_list, *w_list, x, gate, final_w.reshape(1, d))


def _rope_tile(t, cos, sin_signed, lane):
    fwd = pltpu.roll(t, LANES - HEAD_DIM // 2, 1)
    bwd = pltpu.roll(t, HEAD_DIM // 2, 1)
    partner = jnp.where((lane % HEAD_DIM) < HEAD_DIM // 2, fwd, bwd)
    return t * cos + partner * sin_signed


def _nsa_prep_kernel(q_ref, kv01_ref, kv23_ref, kv45_ref, cos_ref, sin_ref,
                     rows_ref, win_ref, qrot_ref, *hm_refs, head_major):
    cos = cos_ref[...]
    sin = sin_ref[...]
    lane = lax.broadcasted_iota(jnp.int32, cos.shape, 1)
    rope = lambda t: _rope_tile(t, cos, sin, lane)
    kv01, kv23, kv45 = kv01_ref[...], kv23_ref[...], kv45_ref[...]
    k_slc = rope(kv23[:, :LANES])
    v_slc = kv23[:, LANES:]
    k_win = rope(kv45[:, :LANES])
    v_win = kv45[:, LANES:]
    rows_ref[:, 0:LANES] = rope(kv01[:, :LANES])
    rows_ref[:, LANES:2 * LANES] = kv01[:, LANES:]
    rows_ref[:, 2 * LANES:3 * LANES] = k_slc
    rows_ref[:, 3 * LANES:4 * LANES] = v_slc
    win_ref[:, 0:LANES] = k_win
    win_ref[:, LANES:2 * LANES] = v_win
    q = q_ref[...]
    qr = [rope(q[:, j * LANES:(j + 1) * LANES]) for j in range(NSA_WIDTH // LANES)]
    for j, t in enumerate(qr):
        qrot_ref[:, j * LANES:(j + 1) * LANES] = t
    if head_major:
        qh_ref, ks_ref, vs_ref, kw_ref, vw_ref = hm_refs
        scale = HEAD_DIM ** -0.5
        for h in range(NSA_HEADS):
            lo = (h % 2) * HEAD_DIM
            qh_ref[0, h] = (qr[h // 2][:, lo:lo + HEAD_DIM] * scale).astype(BF16)
        for g in range(NSA_KV_HEADS):
            lo = g * HEAD_DIM
            ks_ref[0, g] = k_slc[:, lo:lo + HEAD_DIM].astype(BF16)
            vs_ref[0, g] = v_slc[:, lo:lo + HEAD_DIM].astype(BF16)
            kw_ref[0, g] = k_win[:, lo:lo + HEAD_DIM].astype(BF16)
            vw_ref[0, g] = v_win[:, lo:lo + HEAD_DIM].astype(BF16)


def nsa_prep(proj, cos_tab, sin_tab, tm, seq, head_major):
    t = proj.shape[0]
    tab_blocks = cos_tab.shape[0] // tm
    sb = seq // tm
    out_shape = [jax.ShapeDtypeStruct((t, 4 * LANES), F32),
                 jax.ShapeDtypeStruct((t, 2 * LANES), F32),
                 jax.ShapeDtypeStruct((t, NSA_WIDTH), F32)]
    out_specs = [pl.BlockSpec((tm, 4 * LANES), lambda i: (i, 0)),
                 pl.BlockSpec((tm, 2 * LANES), lambda i: (i, 0)),
                 pl.BlockSpec((tm, NSA_WIDTH), lambda i: (i, 0))]
    if head_major:
        nb = t // seq
        hm_map = lambda i: (i // sb, 0, i % sb, 0)
        out_shape.append(jax.ShapeDtypeStruct((nb, NSA_HEADS, seq, HEAD_DIM), BF16))
        out_specs.append(pl.BlockSpec((1, NSA_HEADS, tm, HEAD_DIM), hm_map))
        for _ in range(4):
            out_shape.append(jax.ShapeDtypeStruct((nb, NSA_KV_HEADS, seq, HEAD_DIM), BF16))
            out_specs.append(pl.BlockSpec((1, NSA_KV_HEADS, tm, HEAD_DIM), hm_map))
    kvb = E_KV // (2 * LANES)
    return pl.pallas_call(
        functools.partial(_nsa_prep_kernel, head_major=head_major),
        grid=(t // tm,),
        in_specs=[
            pl.BlockSpec((tm, NSA_WIDTH), lambda i: (i, E_Q // NSA_WIDTH)),
            pl.BlockSpec((tm, 2 * LANES), lambda i: (i, kvb)),
            pl.BlockSpec((tm, 2 * LANES), lambda i: (i, kvb + 1)),
            pl.BlockSpec((tm, 2 * LANES), lambda i: (i, kvb + 2)),
            pl.BlockSpec((tm, LANES), lambda i: (i % tab_blocks, 0)),
            pl.BlockSpec((tm, LANES), lambda i: (i % tab_blocks, 0)),
        ],
        out_specs=out_specs,
        out_shape=out_shape,
        compiler_params=_cparams(("arbitrary",)),
        name="nsa_prep",
    )(proj, proj, proj, proj, cos_tab, sin_tab)


def _nsa_compress_kernel(x_ref, pe_ref, w1_ref, w2_ref, o_ref):
    r = x_ref.shape[0]
    row_w = 4 * LANES
    acc = jnp.zeros((r, 4 * CMP_HIDDEN), F32)
    for t in range(CMP_BLOCK):
        xt = x_ref[:, t * row_w:t * row_w + 2 * LANES] + pe_ref[t:t + 1, :]
        acc = acc + _dot(xt.astype(BF16), w1_ref[t])
    o_ref[...] = _dot(_silu(acc).astype(BF16), w2_ref[...])


def nsa_compress(xblk, pe_tab, w1_blk, w2_blk, r):
    nb, kdim = xblk.shape
    return pl.pallas_call(
        _nsa_compress_kernel,
        grid=(nb // r,),
        in_specs=[
            pl.BlockSpec((r, kdim), lambda i: (i, 0)),
            pl.BlockSpec(pe_tab.shape, lambda i: (0, 0)),
            pl.BlockSpec(w1_blk.shape, lambda i: (0, 0, 0)),
            pl.BlockSpec(w2_blk.shape, lambda i: (0, 0)),
        ],
        out_specs=pl.BlockSpec((r, 4 * HEAD_DIM), lambda i: (i, 0)),
        out_shape=jax.ShapeDtypeStruct((nb, 4 * HEAD_DIM), F32),
        compiler_params=_cparams(("arbitrary",)),
        name="nsa_compress",
    )(xblk, pe_tab, w1_blk, w2_blk)


def _nsa_compress_cache_kernel(x_ref, pe_ref, w1_ref, w2_ref, o_ref, xbuf):
    n_page = x_ref.shape[1]
    gd = NSA_KV_HEADS * HEAD_DIM
    per_page = PAGE_SIZE // CMP_BLOCK

    def to_token_major(p, carry):
        for c in range(2):
            xt = x_ref[0, p, c].reshape(gd, PAGE_SIZE)
            xbuf[c, pl.ds(pl.multiple_of(p * PAGE_SIZE, PAGE_SIZE), PAGE_SIZE), :] = xt.T + pe_ref[c]
        return carry

    lax.fori_loop(0, n_page, to_token_major, 0)
    rows = n_page * per_page
    acc = [jnp.zeros((rows, NSA_KV_HEADS * CMP_HIDDEN), F32) for _ in range(2)]
    for t in range(CMP_BLOCK):
        for c in range(2):
            a = xbuf[c, pl.ds(t, rows, stride=CMP_BLOCK), :].astype(BF16)
            acc[c] = acc[c] + _dot(a, w1_ref[c, t])
    for c in range(2):
        o_ref[:, c * gd:(c + 1) * gd] = _dot(_silu(acc[c]).astype(BF16), w2_ref[c])


def nsa_compress_cache(cache_t, layer, pe_tok, w1_bd, w2_bd, n_page):
    n_pool = cache_t.shape[1]
    per_page = PAGE_SIZE // CMP_BLOCK
    gd = NSA_KV_HEADS * HEAD_DIM
    full = lambda a: pl.BlockSpec(a.shape, lambda i: (0,) * a.ndim)
    return pl.pallas_call(
        _nsa_compress_cache_kernel,
        grid=(n_pool // n_page,),
        in_specs=[
            pl.BlockSpec((1, n_page, 2, NSA_KV_HEADS, HEAD_DIM, PAGE_SIZE), lambda i: (layer, i, 0, 0, 0, 0)),
            full(pe_tok), full(w1_bd), full(w2_bd),
        ],
        out_specs=pl.BlockSpec((n_page * per_page, 2 * gd), lambda i: (i, 0)),
        out_shape=jax.ShapeDtypeStruct((n_pool * per_page, 2 * gd), F32),
        scratch_shapes=[pltpu.VMEM((2, n_page * PAGE_SIZE, gd), F32)],
        compiler_params=_cparams(("arbitrary",)),
        name="nsa_compress_cache",
    )(cache_t, pe_tok, w1_bd, w2_bd)


def _page_gather_kernel(pt_ref, pool_ref, o_ref):
    b = pl.program_id(0)
    for j in range(o_ref.shape[1]):
        o_ref[0, j:j + 1, :] = pool_ref[pl.ds(pt_ref[b, j], 1), :]


def page_gather(pool, page_table):
    nb, npg = page_table.shape
    w = pool.shape[1]
    return pl.pallas_call(
        _page_gather_kernel,
        grid_spec=pltpu.PrefetchScalarGridSpec(
            num_scalar_prefetch=1,
            grid=(nb,),
            in_specs=[pl.BlockSpec(pool.shape, lambda b, pt: (0, 0))],
            out_specs=pl.BlockSpec((1, npg, w), lambda b, pt: (b, 0, 0)),
        ),
        out_shape=jax.ShapeDtypeStruct((nb, npg, w), F32),
        compiler_params=_cparams(("arbitrary",)),
        name="page_gather",
    )(page_table, pool)


def _masked_softmax3(s, mask):
    s = jnp.where(mask[None], s, -jnp.inf)
    m = jnp.max(s, axis=-1, keepdims=True)
    m = jnp.where(m > -jnp.inf, m, 0.0)
    p = jnp.exp(s - m)
    d = jnp.sum(p, axis=-1, keepdims=True)
    return p / jnp.where(d > 0, d, 1.0)


def _cmp_branch_and_select(q2, kc, vc, qpos, nq):
    s = _dot_nt(q2, kc).reshape(NSA_HG, nq, 2 * LANES)
    col = lax.broadcasted_iota(jnp.int32, (1, 2 * LANES), 1)
    cblk = 2 * (col % LANES) + col // LANES
    cmp_end = (cblk + 1) * CMP_BLOCK - 1
    p = _masked_softmax3(s, cmp_end <= qpos)
    o_c = _dot(p.reshape(NSA_HG * nq, 2 * LANES).astype(BF16), vc).reshape(NSA_HG, nq, HEAD_DIM)
    imp = jnp.sum(p, axis=0)
    imp = imp[:, :LANES] + imp[:, LANES:]
    j = lax.broadcasted_iota(jnp.int32, (1, LANES), 1)
    cur = qpos // SEL_BLOCK
    forced = jnp.where(j == 0, 1, jnp.where(j == cur, 1, jnp.where(j == cur - 1, 1, 0))) > 0
    causal = j <= cur
    imp = jnp.where(forced, imp + FORCE_BONUS, imp)
    imp = jnp.where(causal, imp, -jnp.inf)

    def pick(_, c):
        imp_c, sel_c = c
        idx = jnp.argmax(imp_c, axis=-1, keepdims=True).astype(jnp.int32)
        hit = j == idx
        return jnp.where(hit, -jnp.inf, imp_c), jnp.where(hit, 1.0, sel_c)

    _, sel = lax.fori_loop(0, TOP_N, pick, (imp, jnp.zeros((nq, LANES), F32)))
    return o_c, jnp.where(causal, sel, 0.0)


def _flash_init(nq):
    return (jnp.full((NSA_HG, nq, 1), NEG_BIG, F32), jnp.zeros((NSA_HG, nq, 1), F32),
            jnp.zeros((NSA_HG, nq, HEAD_DIM), F32))


def _flash_update(carry, s, mask, v, v_transposed=False):
    m, l, acc = carry
    s = jnp.where(mask[None], s, NEG_BIG)
    m_new = jnp.maximum(m, jnp.max(s, axis=-1, keepdims=True))
    alpha = jnp.exp(m - m_new)
    p = jnp.where(mask[None], jnp.exp(s - m_new), 0.0)
    l = alpha * l + jnp.sum(p, axis=-1, keepdims=True)
    h, q, n = p.shape
    p2 = p.reshape(h * q, n).astype(v.dtype)
    pv = (_dot_nt(p2, v) if v_transposed else _dot(p2, v)).reshape(h, q, HEAD_DIM)
    return m_new, l, alpha * acc + pv


def _flash_finish(carry):
    _, l, acc = carry
    return acc / jnp.where(l > 0, l, 1.0)


def _gate_column(sig, lane, col):
    return jnp.sum(jnp.where(lane == col, sig, 0.0), axis=-1, keepdims=True)


def _nsa_attn_kernel(q_ref, kc_ref, vc_ref, ks_ref, vs_ref, kw_ref, vw_ref, gd_ref, za_ref, e_ref,
                     o_ref, *, tq, tk, twk):
    g = pl.program_id(1)
    q0 = pl.program_id(2) * tq
    q2 = q_ref[0].reshape(NSA_HG * tq, HEAD_DIM)
    qpos = q0 + lax.broadcasted_iota(jnp.int32, (tq, 1), 0)
    o_c, sel = _cmp_branch_and_select(q2, kc_ref[0, 0], vc_ref[0, 0], qpos, tq)
    selb = sel.astype(BF16)

    def sel_body(t, carry):
        k0 = pl.multiple_of(t * tk, tk)
        k = ks_ref[0, 0, pl.ds(k0, tk), :]
        v = vs_ref[0, 0, pl.ds(k0, tk), :]
        s = _dot_nt(q2, k).reshape(NSA_HG, tq, tk)
        blk_on = _dot(selb, e_ref[:, pl.ds(k0, tk)])
        kpos = k0 + lax.broadcasted_iota(jnp.int32, (1, tk), 1)
        mask = jnp.where(kpos <= qpos, blk_on, 0.0) > 0.5
        return _flash_update(carry, s, mask, v)

    o_s = _flash_finish(lax.fori_loop(0, (q0 + tq + tk - 1) // tk, sel_body, _flash_init(tq)))

    def win_body(t, carry):
        k0 = pl.multiple_of(t * twk, twk)
        k = kw_ref[0, 0, pl.ds(k0, twk), :]
        v = vw_ref[0, 0, pl.ds(k0, twk), :]
        s = _dot_nt(q2, k).reshape(NSA_HG, tq, twk)
        dist = qpos - (k0 + lax.broadcasted_iota(jnp.int32, (1, twk), 1))
        mask = jnp.where(dist >= 0, dist, WINDOW) < WINDOW
        return _flash_update(carry, s, mask, v)

    w_lo = jnp.maximum(q0 - WINDOW, 0) // twk
    w_hi = (q0 + tq + twk - 1) // twk
    o_w = _flash_finish(lax.fori_loop(w_lo, w_hi, win_body, _flash_init(tq)))

    sig = _sigmoid(gd_ref[...])
    lane = lax.broadcasted_iota(jnp.int32, sig.shape, 1)
    outs = []
    for h in range(NSA_HG):
        head = g * NSA_HG + h
        g_c = _gate_column(sig, lane, GD_GATE + head)
        g_s = _gate_column(sig, lane, GD_GATE + NSA_HEADS + head)
        g_w = _gate_column(sig, lane, GD_GATE + 2 * NSA_HEADS + head)
        outs.append(g_c * o_c[h] + g_s * o_s[h] + g_w * o_w[h])
    o_ref[...] = jnp.concatenate(outs, axis=-1) * _silu(za_ref[...])


def nsa_attn_prompt(proj, qh, kc, vc, ks, vs, kw, vw, e_mat, tq=128, tk=512, twk=128):
    nb, _, seq, _ = qh.shape
    nq = seq // tq
    gw = NSA_HG * HEAD_DIM
    kv_spec = pl.BlockSpec((1, 1, seq, HEAD_DIM), lambda b, g, i: (b, g, 0, 0))
    c_spec = pl.BlockSpec((1, 1, 2 * LANES, HEAD_DIM), lambda b, g, i: (b, g, 0, 0))
    return pl.pallas_call(
        functools.partial(_nsa_attn_kernel, tq=tq, tk=tk, twk=twk),
        grid=(nb, NSA_KV_HEADS, nq),
        in_specs=[
            pl.BlockSpec((1, NSA_HG, tq, HEAD_DIM), lambda b, g, i: (b, g, i, 0)),
            c_spec, c_spec, kv_spec, kv_spec, kv_spec, kv_spec,
            pl.BlockSpec((tq, LANES), lambda b, g, i: (b * nq + i, E_GD // LANES)),
            pl.BlockSpec((tq, gw), lambda b, g, i: (b * nq + i, E_ZA // gw + g)),
            pl.BlockSpec(e_mat.shape, lambda b, g, i: (0, 0)),
        ],
        out_specs=pl.BlockSpec((tq, gw), lambda b, g, i: (b * nq + i, g)),
        out_shape=jax.ShapeDtypeStruct((nb * seq, NSA_WIDTH), F32),
        compiler_params=_cparams(("arbitrary", "arbitrary", "arbitrary")),
        name="nsa_attn_prompt",
    )(qh, kc, vc, ks, vs, kw, vw, proj, proj, e_mat)


def _nsa_dec_kernel(pt_ref, q_ref, kc_ref, vc_ref, *rest, past_len, n_pages):
    del pt_ref
    page_refs = rest[:n_pages]
    rows_ref, pwin_ref, wnew_ref, gd_ref, za_ref, e_ref, o_ref = rest[n_pages:]
    nq = DEC_PAD
    scale = HEAD_DIM ** -0.5
    qpos = past_len + lax.broadcasted_iota(jnp.int32, (nq, 1), 0)
    q = q_ref[0]

    def q_group(g):
        parts = [q[:, (g * NSA_HG + h) * HEAD_DIM:(g * NSA_HG + h + 1) * HEAD_DIM] for h in range(NSA_HG)]
        return jnp.concatenate(parts, axis=0) * scale

    qg = [q_group(g) for g in range(NSA_KV_HEADS)]

    rows = rows_ref[0]
    wnew = wnew_ref[0]
    win_keep = pwin_ref.shape[-1]
    sig = _sigmoid(gd_ref[0])
    lane = lax.broadcasted_iota(jnp.int32, sig.shape, 1)
    kpos = lax.broadcasted_iota(jnp.int32, (1, past_len), 1)
    npos = past_len + lax.broadcasted_iota(jnp.int32, (1, nq), 1)
    wpos = past_len - win_keep + lax.broadcasted_iota(jnp.int32, (1, win_keep), 1)
    outs = []
    for g in range(NSA_KV_HEADS):
        qb = qg[g].astype(BF16)
        o_c, sel = _cmp_branch_and_select(qb, kc_ref[0, g].astype(BF16), vc_ref[0, g].astype(BF16), qpos, nq)
        kt = jnp.concatenate([r[0, 0, g] for r in page_refs], axis=1).astype(BF16)
        vt = jnp.concatenate([r[0, 1, g] for r in page_refs], axis=1).astype(BF16)
        s = _dot(qb, kt).reshape(NSA_HG, nq, past_len)
        blk_on = _dot(sel.astype(BF16), e_ref[...])
        mask = jnp.where(kpos <= qpos, blk_on, 0.0) > 0.5
        carry = _flash_update(_flash_init(nq), s, mask, vt, v_transposed=True)
        kn = rows[:, 2 * LANES + g * HEAD_DIM:2 * LANES + (g + 1) * HEAD_DIM]
        vn = rows[:, 3 * LANES + g * HEAD_DIM:3 * LANES + (g + 1) * HEAD_DIM]
        s = _dot_nt(qg[g], kn).reshape(NSA_HG, nq, nq)
        lane_s = lax.broadcasted_iota(jnp.int32, (nq, LANES), 1)
        blk_on = _gate_column(sel, lane_s, past_len // SEL_BLOCK)
        mask = jnp.where(npos <= qpos, blk_on, 0.0) > 0.5
        o_s = _flash_finish(_flash_update(carry, s, mask, vn))
        s = _dot(qb, pwin_ref[0, 0, g].astype(BF16)).reshape(NSA_HG, nq, win_keep)
        dist = qpos - wpos
        mask = jnp.where(wpos >= 0, jnp.where(dist >= 0, dist, WINDOW), WINDOW) < WINDOW
        carry = _flash_update(_flash_init(nq), s, mask, pwin_ref[0, 1, g].astype(BF16), v_transposed=True)
        kwn = wnew[:, g * HEAD_DIM:(g + 1) * HEAD_DIM]
        vwn = wnew[:, LANES + g * HEAD_DIM:LANES + (g + 1) * HEAD_DIM]
        s = _dot_nt(qg[g], kwn).reshape(NSA_HG, nq, nq)
        dist = qpos - npos
        mask = jnp.where(dist >= 0, dist, WINDOW) < WINDOW
        o_w = _flash_finish(_flash_update(carry, s, mask, vwn))
        for h in range(NSA_HG):
            head = g * NSA_HG + h
            g_c = _gate_column(sig, lane, GD_GATE + head)
            g_s = _gate_column(sig, lane, GD_GATE + NSA_HEADS + head)
            g_w = _gate_column(sig, lane, GD_GATE + 2 * NSA_HEADS + head)
            outs.append(g_c * o_c[h] + g_s * o_s[h] + g_w * o_w[h])
    o_ref[0] = jnp.concatenate(outs, axis=-1) * _silu(za_ref[0])


def nsa_attn_decode(page_table, layer, qrot, kc, vc, cache_t, rows, pwin_t, wnew, proj3, e_mat, past_len):
    nb, npg = page_table.shape
    win_keep = pwin_t.shape[-1]
    per_b = lambda blk: pl.BlockSpec(blk, lambda b, pt: (b,) + (0,) * (len(blk) - 1))
    page_blk = (None, 1, 2, NSA_KV_HEADS, HEAD_DIM, PAGE_SIZE)
    page_specs = [pl.BlockSpec(page_blk, lambda b, pt, j=j: (layer, pt[b, j], 1, 0, 0, 0)) for j in range(npg)]
    return pl.pallas_call(
        functools.partial(_nsa_dec_kernel, past_len=past_len, n_pages=npg),
        grid_spec=pltpu.PrefetchScalarGridSpec(
            num_scalar_prefetch=1,
            grid=(nb,),
            in_specs=[
                per_b((1, DEC_PAD, NSA_WIDTH)),
                per_b((1, NSA_KV_HEADS, 2 * LANES, HEAD_DIM)),
                per_b((1, NSA_KV_HEADS, 2 * LANES, HEAD_DIM)),
                *page_specs,
                per_b((1, DEC_PAD, 4 * LANES)),
                pl.BlockSpec((None, 1, 2, NSA_KV_HEADS, HEAD_DIM, win_keep), lambda b, pt: (layer, b, 0, 0, 0, 0)),
                per_b((1, DEC_PAD, 2 * LANES)),
                pl.BlockSpec((1, DEC_PAD, LANES), lambda b, pt: (b, 0, E_GD // LANES)),
                pl.BlockSpec((1, DEC_PAD, NSA_WIDTH), lambda b, pt: (b, 0, E_ZA // NSA_WIDTH)),
                pl.BlockSpec(e_mat.shape, lambda b, pt: (0, 0)),
            ],
            out_specs=per_b((1, DEC_PAD, NSA_WIDTH)),
        ),
        out_shape=jax.ShapeDtypeStruct((nb, DEC_PAD, NSA_WIDTH), F32),
        compiler_params=_cparams(("arbitrary",)),
        name="nsa_attn_decode",
    )(page_table, qrot, kc, vc, *([cache_t] * npg), rows, pwin_t, wnew, proj3, proj3, e_mat)


def _tri_masks(c):
    row = lax.broadcasted_iota(jnp.int32, (c, c), 0)
    col = lax.broadcasted_iota(jnp.int32, (c, c), 1)
    lower = row >= col
    return lower, jnp.where(lower, 1.0, 0.0).astype(BF16), jnp.where(row <= col, 1.0, 0.0).astype(BF16)


def _ssd_kernel(zb_ref, xbc_ref, gd_ref, prev_ref, h0_ref, cw_ref, cb_ref, dtb_ref, alog_ref, alog16_ref,
                dexp_ref, nw_ref, e16_ref, y_ref, so_ref, h_sc, xbuf, *, chunk, n_valid):
    c = pl.program_id(1)
    hp = SSD_HEADS // SSD_GROUPS * SSD_HEAD_DIM
    halo = SUBLANES

    @pl.when(c == 0)
    def _():
        h_sc[...] = h0_ref[0]
        xbuf[0:halo] = prev_ref[0]

    xbuf[halo:halo + chunk] = xbc_ref[...]
    conv = cb_ref[...]
    for i in range(SSD_CONV):
        lo = halo - (SSD_CONV - 1) + i
        conv = conv + xbuf[lo:lo + chunk] * cw_ref[i:i + 1, :]
    xbuf[0:halo] = xbuf[chunk:chunk + halo]
    act = _silu(conv)
    xs = act[:, :SSD_INNER]
    bm = act[:, SSD_INNER:SSD_INNER + SSD_GROUPS * SSD_STATE]
    cm = act[:, SSD_INNER + SSD_GROUPS * SSD_STATE:]

    dt16 = _softplus(gd_ref[...] + dtb_ref[...])[:, GD_DT:GD_DT + SSD_HEADS]
    if n_valid is not None:
        tok = c * chunk + lax.broadcasted_iota(jnp.int32, (chunk, 1), 0)
        dt16 = jnp.where(tok < n_valid, dt16, 0.0)
    dt = _dot3(dt16, e16_ref[...])
    da = dt * (-jnp.exp(alog_ref[...]))
    da16 = dt16 * (-jnp.exp(alog16_ref[...]))
    lower, tri, tri_t = _tri_masks(chunk)
    cum = _dot3_wx(tri, da)
    cum16 = _dot3_wx(tri, da16)
    cum16_t = _dot3_tn(da16, tri_t)
    ones8 = jnp.ones((chunk, SUBLANES), BF16)
    decay_col = jnp.exp(_dot3_tn(da, ones8)[:, 0:1])
    xdt = xs * dt
    cum_last = cum[chunk - 1:chunk, :]
    xw = (xdt * jnp.exp(cum_last - cum)).astype(BF16)
    xdt_b = xdt.astype(BF16)
    lane = lax.broadcasted_iota(jnp.int32, (chunk, LANES), 1)
    low_half = lane < SSD_HEAD_DIM

    y_intra = [None] * (SSD_INNER // LANES)
    y_inter = []
    for g in range(SSD_GROUPS):
        cg = cm[:, g * SSD_STATE:(g + 1) * SSD_STATE].astype(BF16)
        bg = bm[:, g * SSD_STATE:(g + 1) * SSD_STATE].astype(BF16)
        h_g = h_sc[g * hp:(g + 1) * hp, :]
        cb = _dot_nt(cg, bg)
        y_inter.append(_dot_nt(cg, h_g.astype(BF16)))
        for hh in range(SSD_HEADS // SSD_GROUPS):
            h = g * (SSD_HEADS // SSD_GROUPS) + hh
            diff = cum16[:, h:h + 1] - cum16_t[h:h + 1, :]
            lmat = jnp.where(lower, jnp.exp(jnp.where(lower, diff, 0.0)), 0.0)
            m = (cb * lmat).astype(BF16)
            pair = h // 2
            x_pair = xdt_b[:, pair * LANES:(pair + 1) * LANES]
            keep = low_half if h % 2 == 0 else jnp.logical_not(low_half)
            contrib = _dot(m, jnp.where(keep, x_pair, jnp.zeros_like(x_pair)))
            y_intra[pair] = contrib if y_intra[pair] is None else y_intra[pair] + contrib
        h_sc[g * hp:(g + 1) * hp, :] = (h_g * decay_col[g * hp:(g + 1) * hp, :]
                                        + _dot_tn(xw[:, g * hp:(g + 1) * hp], bg))
    y = jnp.concatenate(y_intra, axis=-1) + jnp.concatenate(y_inter, axis=-1) * jnp.exp(cum)
    y = y + xs * dexp_ref[...]
    y = y * _silu(zb_ref[...])
    ms = jnp.mean(y * y, axis=-1, keepdims=True)
    y_ref[...] = y * lax.rsqrt(ms + 1e-6) * nw_ref[...]

    @pl.when(c == pl.num_programs(1) - 1)
    def _():
        so_ref[0] = h_sc[...]


def ssd_mixer(proj, prev8, h0, conv_w, conv_b, dtb_row, alog_exp, alog16, d_exp, norm_w, e16, nb, seq, chunk,
              n_valid):
    nch = seq // chunk
    row = lambda b, c: b * nch + c
    full = lambda a: pl.BlockSpec(a.shape, lambda b, c: (0,) * a.ndim)
    return pl.pallas_call(
        functools.partial(_ssd_kernel, chunk=chunk, n_valid=n_valid),
        grid=(nb, nch),
        in_specs=[
            pl.BlockSpec((chunk, SSD_INNER), lambda b, c: (row(b, c), E_ZB // SSD_INNER)),
            pl.BlockSpec((chunk, SSD_CONV_CH), lambda b, c: (row(b, c), E_XBC // SSD_CONV_CH)),
            pl.BlockSpec((chunk, LANES), lambda b, c: (row(b, c), E_GD // LANES)),
            pl.BlockSpec((1, SUBLANES, SSD_CONV_CH), lambda b, c: (b, 0, 0)),
            pl.BlockSpec((1, SSD_INNER, SSD_STATE), lambda b, c: (b, 0, 0)),
            full(conv_w), full(conv_b), full(dtb_row), full(alog_exp), full(alog16), full(d_exp), full(norm_w),
            full(e16),
        ],
        out_specs=[
            pl.BlockSpec((chunk, SSD_INNER), lambda b, c: (row(b, c), 0)),
            pl.BlockSpec((1, SSD_INNER, SSD_STATE), lambda b, c: (b, 0, 0)),
        ],
        out_shape=[jax.ShapeDtypeStruct((nb * seq, SSD_INNER), F32),
                   jax.ShapeDtypeStruct((nb, SSD_INNER, SSD_STATE), F32)],
        scratch_shapes=[pltpu.VMEM((SSD_INNER, SSD_STATE), F32),
                        pltpu.VMEM((SUBLANES + chunk, SSD_CONV_CH), F32)],
        compiler_params=_cparams(("arbitrary", "arbitrary")),
        name="ssd_mixer",
    )(proj, proj, proj, prev8, h0, conv_w, conv_b, dtb_row, alog_exp, alog16, d_exp, norm_w, e16)


def _gla_kernel(q_ref, k_ref, v_ref, r_ref, glr_ref, s0_ref, wg_ref, bg_ref, nw_ref, o_ref, so_ref, s_sc,
                *, chunk, n_valid):
    c = pl.program_id(1)

    @pl.when(c == 0)
    def _():
        s_sc[...] = s0_ref[0]

    lg = _log_sigmoid(_dot(glr_ref[...].astype(BF16), wg_ref[...]) + bg_ref[...]) * (1.0 / GLA_TAU)
    k = k_ref[...]
    if n_valid is not None:
        tok = c * chunk + lax.broadcasted_iota(jnp.int32, (chunk, 1), 0)
        lg = jnp.where(tok < n_valid, lg, 0.0)
        k = jnp.where(tok < n_valid, k, 0.0)
    lower, tri, _ = _tri_masks(chunk)
    bcum = _dot3_wx(tri, lg)
    blast = bcum[chunk - 1:chunk, :]
    ones8 = jnp.ones((chunk, SUBLANES), BF16)
    decay_col = jnp.exp(_dot3_tn(lg, ones8)[:, 0:1])
    qg = (q_ref[...] * (GLA_DK ** -0.5) * jnp.exp(bcum)).astype(BF16)
    kg = (k * jnp.exp(-bcum)).astype(BF16)
    kd = (k * jnp.exp(blast - bcum)).astype(BF16)
    v = v_ref[...].astype(BF16)
    r = r_ref[...]
    nw = nw_ref[...]
    for h in range(GLA_HEADS):
        kcols = slice(h * GLA_DK, (h + 1) * GLA_DK)
        vcols = slice(h * GLA_DV, (h + 1) * GLA_DV)
        s_h = s_sc[kcols, :]
        att = jnp.where(lower, _dot_nt(qg[:, kcols], kg[:, kcols]), 0.0)
        o = _dot(att.astype(BF16), v[:, vcols]) + _dot(qg[:, kcols], s_h.astype(BF16))
        s_sc[kcols, :] = s_h * decay_col[kcols, :] + _dot_tn(kd[:, kcols], v[:, vcols])
        ms = jnp.mean(o * o, axis=-1, keepdims=True)
        o = o * lax.rsqrt(ms + 1e-6) * nw
        o_ref[:, vcols] = o * _silu(r[:, vcols])

    @pl.when(c == pl.num_programs(1) - 1)
    def _():
        so_ref[0] = s_sc[...]


def gla_mixer(proj, s0, wg_pad, bg, norm_w, nb, seq, chunk, n_valid):
    nch = seq // chunk
    row = lambda b, c: b * nch + c
    full = lambda a: pl.BlockSpec(a.shape, lambda b, c: (0,) * a.ndim)
    srows = GLA_HEADS * GLA_DK
    return pl.pallas_call(
        functools.partial(_gla_kernel, chunk=chunk, n_valid=n_valid),
        grid=(nb, nch),
        in_specs=[
            pl.BlockSpec((chunk, GLA_KEY_WIDTH), lambda b, c: (row(b, c), O_Q // GLA_KEY_WIDTH)),
            pl.BlockSpec((chunk, GLA_KEY_WIDTH), lambda b, c: (row(b, c), O_K // GLA_KEY_WIDTH)),
            pl.BlockSpec((chunk, GLA_VAL_WIDTH), lambda b, c: (row(b, c), O_V // GLA_VAL_WIDTH)),
            pl.BlockSpec((chunk, GLA_VAL_WIDTH), lambda b, c: (row(b, c), O_R // GLA_VAL_WIDTH)),
            pl.BlockSpec((chunk, LANES), lambda b, c: (row(b, c), O_GLR // LANES)),
            pl.BlockSpec((1, srows, GLA_DV), lambda b, c: (b, 0, 0)),
            full(wg_pad), full(bg), full(norm_w),
        ],
        out_specs=[
            pl.BlockSpec((chunk, GLA_VAL_WIDTH), lambda b, c: (row(b, c), 0)),
            pl.BlockSpec((1, srows, GLA_DV), lambda b, c: (b, 0, 0)),
        ],
        out_shape=[jax.ShapeDtypeStruct((nb * seq, GLA_VAL_WIDTH), F32),
                   jax.ShapeDtypeStruct((nb, srows, GLA_DV), F32)],
        scratch_shapes=[pltpu.VMEM((srows, GLA_DV), F32)],
        compiler_params=_cparams(("arbitrary", "arbitrary")),
        name="gla_mixer",
    )(proj, proj, proj, proj, proj, s0, wg_pad, bg, norm_w)


def _even_weight(w):
    q, g_a, kv, z_a, z_b, xbc, dt = jnp.split(w, list(np.cumsum(
        [NSA_WIDTH, 3 * NSA_HEADS, 6 * NSA_KV_HEADS * HEAD_DIM, NSA_WIDTH, SSD_INNER, SSD_CONV_CH])), axis=-1)
    pad = jnp.zeros((w.shape[0], E_END - E_GD - 3 * NSA_HEADS - SSD_HEADS), w.dtype)
    return jnp.concatenate([xbc, q, z_b, z_a, kv, g_a, dt, pad], axis=-1).astype(BF16)


def _odd_weight(w):
    q, k, v, glr, r = jnp.split(w, list(np.cumsum(
        [GLA_KEY_WIDTH, GLA_KEY_WIDTH, GLA_VAL_WIDTH, GLA_GATE_RANK])), axis=-1)
    pad = jnp.zeros((w.shape[0], O_END - O_GLR - GLA_GATE_RANK), w.dtype)
    return jnp.concatenate([q, k, v, r, glr, pad], axis=-1).astype(BF16)


def _rope_tables(pos):
    half = HEAD_DIM // 2
    inv = ROPE_THETA ** (-jnp.arange(half, dtype=F32) / half)
    ang = pos.astype(F32)[:, None] * inv[None, :]
    cos, sin = jnp.cos(ang), jnp.sin(ang)
    reps = LANES // HEAD_DIM
    return jnp.tile(jnp.concatenate([cos, cos], -1), (1, reps)), jnp.tile(jnp.concatenate([-sin, sin], -1), (1, reps))


def _compress_weights(pe, w1, w2):
    pe_tab = jnp.concatenate([pe[0], pe[0], pe[1], pe[1]], axis=-1)
    w1r = w1.reshape(2, CMP_BLOCK, HEAD_DIM, CMP_HIDDEN)
    w1_blk = jnp.zeros((CMP_BLOCK, 4 * HEAD_DIM, 4 * CMP_HIDDEN), F32)
    w2_blk = jnp.zeros((4 * CMP_HIDDEN, 4 * HEAD_DIM), F32)
    for part in range(4):
        src = part // 2
        w1_blk = w1_blk.at[:, part * HEAD_DIM:(part + 1) * HEAD_DIM,
                           part * CMP_HIDDEN:(part + 1) * CMP_HIDDEN].set(w1r[src])
        w2_blk = w2_blk.at[part * CMP_HIDDEN:(part + 1) * CMP_HIDDEN,
                           part * HEAD_DIM:(part + 1) * HEAD_DIM].set(w2[src])
    return pe_tab, w1_blk.astype(BF16), w2_blk.astype(BF16)


def _compress_weights_cache(pe, w1, w2):
    w1r = w1.reshape(2, CMP_BLOCK, HEAD_DIM, CMP_HIDDEN)
    w1_bd = jnp.zeros((2, CMP_BLOCK, NSA_KV_HEADS * HEAD_DIM, NSA_KV_HEADS * CMP_HIDDEN), F32)
    w2_bd = jnp.zeros((2, NSA_KV_HEADS * CMP_HIDDEN, NSA_KV_HEADS * HEAD_DIM), F32)
    for g in range(NSA_KV_HEADS):
        w1_bd = w1_bd.at[:, :, g * HEAD_DIM:(g + 1) * HEAD_DIM, g * CMP_HIDDEN:(g + 1) * CMP_HIDDEN].set(w1r)
        w2_bd = w2_bd.at[:, g * CMP_HIDDEN:(g + 1) * CMP_HIDDEN, g * HEAD_DIM:(g + 1) * HEAD_DIM].set(w2)
    pe_tok = jnp.tile(jnp.concatenate([pe] * NSA_KV_HEADS, axis=-1), (1, PAGE_SIZE // CMP_BLOCK, 1))
    return pe_tok, w1_bd.astype(BF16), w2_bd.astype(BF16)


def _cmp_layout(cmp, nb, nc, dtype):
    c = cmp.reshape(nb, nc, 2, NSA_KV_HEADS, HEAD_DIM).transpose(2, 0, 3, 1, 4)
    halves = []
    for par in range(2):
        h = c[:, :, :, par::2]
        halves.append(jnp.pad(h, ((0, 0), (0, 0), (0, 0), (0, LANES - h.shape[3]), (0, 0))))
    c = jnp.concatenate(halves, axis=3).astype(dtype)
    return c[0], c[1]


def _sel_expand(n_keys):
    blk = jnp.arange(n_keys, dtype=jnp.int32) // SEL_BLOCK
    return (blk[None, :] == jnp.arange(LANES, dtype=jnp.int32)[:, None]).astype(BF16)


def kernel(x_prompt, x_sample, c_prompt, c_sample, cache_nsa, cache_nsa_win, state_ssd_conv, state_ssd, state_gla, page_table, norm_w, w_mod, b_mod, w_in_even, w_out_even, nsa_cmp_pe, nsa_cmp_w1, nsa_cmp_w2, ssd_conv_w, ssd_conv_b, ssd_dt_bias, ssd_a_log, ssd_d, ssd_norm_w, w_in_odd, gla_w_gate2, gla_b_gate, gla_norm_w, w_out_odd, final_norm_w):
    bp, sp, d = x_prompt.shape
    bs, ss, _ = x_sample.shape
    depth = norm_w.shape[0]
    n_pool = cache_nsa.shape[1]
    npg = page_table.shape[1]
    past_len = npg * PAGE_SIZE
    win_keep = cache_nsa_win.shape[2]
    assert sp % 512 == 0 and sp // SEL_BLOCK <= LANES and sp // CMP_BLOCK <= 2 * LANES
    assert ss <= DEC_PAD and ss < CMP_BLOCK and past_len % SEL_BLOCK == 0 and ss >= SSD_CONV - 1
    assert past_len // SEL_BLOCK + 1 <= LANES and (bs * DEC_PAD) % 256 == 0 and n_pool % 32 == 0
    tp, td = bp * sp, bs * DEC_PAD
    tm_d = 256

    c_all = jnp.concatenate([c_prompt, c_sample], axis=0)
    c_all = jnp.pad(c_all, ((0, -c_all.shape[0] % SUBLANES), (0, 0)))
    mod = mod_all(c_all, w_mod.astype(BF16), b_mod)

    def mods(l):
        shift, scale, gate = jnp.split(mod[l], 3, axis=-1)
        mp = [m[:bp].reshape(bp, 1, d) for m in (shift, scale, gate)]
        ms = [jnp.repeat(m[bp:bp + bs], DEC_PAD, axis=0).reshape(td // tm_d, tm_d, d) for m in (shift, scale, gate)]
        return mp, ms

    xp = x_prompt.reshape(tp, d)
    xs = jnp.pad(x_sample, ((0, 0), (0, DEC_PAD - ss), (0, 0))).reshape(td, d)

    cos_p, sin_p = _rope_tables(jnp.arange(sp, dtype=jnp.int32))
    cos_s, sin_s = _rope_tables(past_len + jnp.arange(DEC_PAD, dtype=jnp.int32))
    cos_s, sin_s = jnp.tile(cos_s, (tm_d // DEC_PAD, 1)), jnp.tile(sin_s, (tm_d // DEC_PAD, 1))
    cache_t = jnp.transpose(cache_nsa, (0, 1, 3, 4, 5, 2))
    win_t = jnp.transpose(cache_nsa_win, (0, 1, 3, 4, 5, 2))
    e_mat_p = _sel_expand(sp)
    e_mat_s = _sel_expand(past_len)
    e16 = (jnp.arange(SSD_INNER, dtype=jnp.int32)[None, :] // SSD_HEAD_DIM
           == jnp.arange(SSD_HEADS, dtype=jnp.int32)[:, None]).astype(BF16)

    outs = {k: [] for k in ("kv_p", "kv_s", "win_p", "win_s", "cv_p", "cv_s", "ss_p", "ss_s", "gl_p", "gl_s")}
    for l in range(depth):
        e = l // 2
        (shift_p, scale_p, gate_p), (shift_s, scale_s, gate_s) = mods(l)
        last = l == depth - 1
        if l % 2 == 0:
            w_in = _even_weight(w_in_even[e])
            w_out = w_out_even[e].astype(BF16)
            w_out_a, w_out_b = w_out[:NSA_WIDTH], w_out[NSA_WIDTH:]
            pe_tab, w1_blk, w2_blk = _compress_weights(nsa_cmp_pe[e], nsa_cmp_w1[e], nsa_cmp_w2[e])
            conv_w, conv_b = ssd_conv_w[e], ssd_conv_b[e].reshape(1, SSD_CONV_CH)
            dtb_row = jnp.zeros((1, LANES), F32).at[0, GD_DT:GD_DT + SSD_HEADS].set(ssd_dt_bias[e])
            alog16 = ssd_a_log[e].reshape(1, SSD_HEADS)
            alog_exp = jnp.repeat(ssd_a_log[e], SSD_HEAD_DIM).reshape(1, SSD_INNER)
            d_exp = jnp.repeat(ssd_d[e], SSD_HEAD_DIM).reshape(1, SSD_INNER)
            ssd_nw = ssd_norm_w[e].reshape(1, SSD_INNER)
            ssd_args = (conv_w, conv_b, dtb_row, alog_exp, alog16, d_exp, ssd_nw, e16)

            proj = inproj(xp, norm_w[l], scale_p, shift_p, w_in, 256, sp)
            rows, win, _, qh, ks, vs, kw, vw = nsa_prep(proj, cos_p, sin_p, 256, sp, True)
            nc = sp // CMP_BLOCK
            cmp = nsa_compress(rows.reshape(tp // CMP_BLOCK, CMP_BLOCK * 4 * LANES), pe_tab, w1_blk, w2_blk,
                               min(128, tp // CMP_BLOCK))
            kc, vc = _cmp_layout(cmp, bp, nc, BF16)
            o_a = nsa_attn_prompt(proj, qh, kc, vc, ks, vs, kw, vw, e_mat_p)
            y, h_new = ssd_mixer(proj, jnp.zeros((bp, SUBLANES, SSD_CONV_CH), F32),
                                 jnp.zeros((bp, SSD_INNER, SSD_STATE), F32), *ssd_args, bp, sp, SSD_CHUNK, None)
            xp = outproj([o_a, y], [w_out_a, w_out_b], xp, gate_p, final_norm_w, 512, sp, last)
            outs["kv_p"].append(rows.reshape(bp, sp, 4, NSA_KV_HEADS, HEAD_DIM))
            outs["win_p"].append(win.reshape(bp, sp, 2, NSA_KV_HEADS, HEAD_DIM)[:, -min(WINDOW, sp):])
            outs["cv_p"].append(proj[:, E_XBC:E_XBC + SSD_CONV_CH].reshape(bp, sp, SSD_CONV_CH)[:, -(SSD_CONV - 1):])
            outs["ss_p"].append(h_new.reshape(bp, SSD_HEADS, SSD_HEAD_DIM, SSD_STATE))

            proj = inproj(xs, norm_w[l], scale_s, shift_s, w_in, tm_d, tm_d)
            rows, win, qrot = nsa_prep(proj, cos_s, sin_s, tm_d, DEC_PAD, False)
            pe_tok, w1_bd, w2_bd = _compress_weights_cache(nsa_cmp_pe[e], nsa_cmp_w1[e], nsa_cmp_w2[e])
            cmp_pool = nsa_compress_cache(cache_t, e, pe_tok, w1_bd, w2_bd, 32)
            per_page = PAGE_SIZE // CMP_BLOCK
            cmp_s = page_gather(cmp_pool.reshape(n_pool, per_page * 4 * HEAD_DIM), page_table)
            kc, vc = _cmp_layout(cmp_s.reshape(bs * npg * per_page, 4 * HEAD_DIM), bs, npg * per_page, F32)
            o_a = nsa_attn_decode(page_table, e, qrot.reshape(bs, DEC_PAD, NSA_WIDTH), kc, vc, cache_t,
                                  rows.reshape(bs, DEC_PAD, 4 * LANES), win_t,
                                  win.reshape(bs, DEC_PAD, 2 * LANES), proj.reshape(bs, DEC_PAD, E_END),
                                  e_mat_s, past_len).reshape(td, NSA_WIDTH)
            prev8 = jnp.pad(state_ssd_conv[e], ((0, 0), (SUBLANES - (SSD_CONV - 1), 0), (0, 0)))
            y, h_new = ssd_mixer(proj, prev8, state_ssd[e].reshape(bs, SSD_INNER, SSD_STATE), *ssd_args,
                                 bs, DEC_PAD, DEC_PAD, ss)
            xs = outproj([o_a, y], [w_out_a, w_out_b], xs, gate_s, final_norm_w, tm_d, tm_d, last)
            outs["kv_s"].append(rows.reshape(bs, DEC_PAD, 4, NSA_KV_HEADS, HEAD_DIM)[:, :ss])
            new_win = win.reshape(bs, DEC_PAD, 2, NSA_KV_HEADS, HEAD_DIM)[:, :ss]
            outs["win_s"].append(jnp.concatenate([cache_nsa_win[e], new_win], axis=1)[:, -win_keep:])
            new_xbc = proj[:, E_XBC:E_XBC + SSD_CONV_CH].reshape(bs, DEC_PAD, SSD_CONV_CH)[:, :ss]
            outs["cv_s"].append(jnp.concatenate([state_ssd_conv[e], new_xbc], axis=1)[:, -(SSD_CONV - 1):])
            outs["ss_s"].append(h_new.reshape(bs, SSD_HEADS, SSD_HEAD_DIM, SSD_STATE))
        else:
            w_in = _odd_weight(w_in_odd[e])
            w_out = w_out_odd[e].astype(BF16)
            wg_pad = jnp.pad(gla_w_gate2[e], ((0, LANES - GLA_GATE_RANK), (0, 0))).astype(BF16)
            bg = gla_b_gate[e].reshape(1, GLA_KEY_WIDTH)
            gnw = gla_norm_w[e].reshape(1, GLA_DV)
            srows = GLA_HEADS * GLA_DK
            proj = inproj(xp, norm_w[l], scale_p, shift_p, w_in, 256, sp)
            o, st = gla_mixer(proj, jnp.zeros((bp, srows, GLA_DV), F32), wg_pad, bg, gnw, bp, sp, GLA_CHUNK, None)
            xp = outproj([o], [w_out], xp, gate_p, final_norm_w, 512, sp, last)
            outs["gl_p"].append(st.reshape(bp, GLA_HEADS, GLA_DK, GLA_DV))
            proj = inproj(xs, norm_w[l], scale_s, shift_s, w_in, tm_d, tm_d)
            o, st = gla_mixer(proj, state_gla[e].reshape(bs, srows, GLA_DV), wg_pad, bg, gnw, bs, DEC_PAD, DEC_PAD, ss)
            xs = outproj([o], [w_out], xs, gate_s, final_norm_w, tm_d, tm_d, last)
            outs["gl_s"].append(st.reshape(bs, GLA_HEADS, GLA_DK, GLA_DV))

    y_prompt = xp.reshape(bp, sp, d)
    y_sample = xs.reshape(bs, DEC_PAD, d)[:, :ss]
    st = {k: jnp.stack(v) for k, v in outs.items()}
    return (y_prompt, y_sample, st["kv_p"], st["kv_s"], st["win_p"], st["win_s"], st["cv_p"], st["cv_s"],
            st["ss_p"], st["ss_s"], st["gl_p"], st["gl_s"])
```

```python
import functools
import math

import jax
import jax.numpy as jnp
import numpy as np
from jax import lax
from jax.experimental import pallas as pl
from jax.experimental.pallas import tpu as pltpu

F32 = jnp.float32
BF16 = jnp.bfloat16

PAGE_SIZE = 128
NSA_HEADS = 8
NSA_KV_HEADS = 2
HEAD_DIM = 64
NSA_WIDTH = NSA_HEADS * HEAD_DIM
NSA_HG = NSA_HEADS // NSA_KV_HEADS
CMP_BLOCK = 32
CMP_HIDDEN = 2 * HEAD_DIM
SEL_BLOCK = 64
TOP_N = 16
WINDOW = 512
FORCE_BONUS = 1.0e4
ROPE_THETA = 10000.0
SSD_HEADS = 16
SSD_HEAD_DIM = 64
SSD_INNER = SSD_HEADS * SSD_HEAD_DIM
SSD_GROUPS = 2
SSD_STATE = 128
SSD_CONV = 4
SSD_CONV_CH = SSD_INNER + 2 * SSD_GROUPS * SSD_STATE
GLA_HEADS = 4
GLA_DK = 128
GLA_DV = 256
GLA_KEY_WIDTH = GLA_HEADS * GLA_DK
GLA_VAL_WIDTH = GLA_HEADS * GLA_DV
GLA_GATE_RANK = 16
GLA_TAU = 16.0
GLA_CHUNK = 32
SSD_CHUNK = 64

LANES = 128
SUBLANES = 8
VMEM_LIMIT = 56 * 1024 * 1024

NEG_BIG = -1.0e30
M_INIT = -1.0e29
DEC_PAD = 8

E_XBC, E_Q, E_ZB, E_ZA, E_KV, E_GD, E_END = 0, 1536, 2048, 3072, 3584, 4352, 4480
GD_GATE, GD_DT = 0, 24
O_Q, O_K, O_V, O_R, O_GLR, O_END = 0, 512, 1024, 2048, 3072, 3200


def _cparams(sem):
    return pltpu.CompilerParams(dimension_semantics=sem, vmem_limit_bytes=VMEM_LIMIT)


def _dot(a, b):
    return jnp.dot(a, b, preferred_element_type=F32)


def _dot_nt(a, b):
    return lax.dot_general(a, b, (((1,), (1,)), ((), ())), preferred_element_type=F32)


def _dot_tn(a, b):
    return lax.dot_general(a, b, (((0,), (0,)), ((), ())), preferred_element_type=F32)


def _split3(x):
    hi = x.astype(BF16)
    r1 = x - hi.astype(F32)
    mid = r1.astype(BF16)
    lo = (r1 - mid.astype(F32)).astype(BF16)
    return hi, mid, lo


def _dot3(x, w01):
    hi, mid, lo = _split3(x)
    return _dot(hi, w01) + _dot(mid, w01) + _dot(lo, w01)


def _dot3_tn(x, w01):
    hi, mid, lo = _split3(x)
    return _dot_tn(hi, w01) + _dot_tn(mid, w01) + _dot_tn(lo, w01)


def _dot3_wx(w01, x):
    hi, mid, lo = _split3(x)
    return _dot(w01, hi) + _dot(w01, mid) + _dot(w01, lo)


def _silu(x):
    return x * (1.0 / (1.0 + jnp.exp(-x)))


def _sigmoid(x):
    return 1.0 / (1.0 + jnp.exp(-x))


def _softplus(x):
    return jnp.maximum(x, 0.0) + jnp.log1p(jnp.exp(-jnp.abs(x)))


def _log_sigmoid(x):
    return jnp.minimum(x, 0.0) - jnp.log1p(jnp.exp(-jnp.abs(x)))


def _mod_kernel(c_ref, w_ref, b_ref, o_ref):
    a = _silu(c_ref[...]).astype(BF16)
    o_ref[0] = _dot(a, w_ref[0]) + b_ref[0]


def mod_all(c_all, w_mod_bf, b_mod, tn=1024):
    nl, d, n = w_mod_bf.shape
    m = c_all.shape[0]
    return pl.pallas_call(
        _mod_kernel,
        grid=(nl, n // tn),
        in_specs=[
            pl.BlockSpec((m, d), lambda l, j: (0, 0)),
            pl.BlockSpec((1, d, tn), lambda l, j: (l, 0, j)),
            pl.BlockSpec((1, 1, tn), lambda l, j: (l, 0, j)),
        ],
        out_specs=pl.BlockSpec((1, m, tn), lambda l, j: (l, 0, j)),
        out_shape=jax.ShapeDtypeStruct((nl, m, n), F32),
        compiler_params=_cparams(("arbitrary", "arbitrary")),
        name="mod_all",
    )(c_all, w_mod_bf, b_mod.reshape(nl, 1, n))


def _inproj_kernel(x_ref, nw_ref, sc_ref, sh_ref, w_ref, o_ref, *, n_chunk):
    x = x_ref[...]
    ms = jnp.mean(x * x, axis=-1, keepdims=True)
    y = x * lax.rsqrt(ms + 1e-6) * nw_ref[...]
    h = (y * (1.0 + sc_ref[0]) + sh_ref[0]).astype(BF16)
    n = o_ref.shape[1]
    for n0 in range(0, n, n_chunk):
        n1 = min(n0 + n_chunk, n)
        o_ref[:, n0:n1] = _dot(h, w_ref[:, n0:n1])


def inproj(x, nw, scale, shift, w_bf, tm, rows_per_mod):
    t, d = x.shape
    n = w_bf.shape[1]
    r = scale.shape[1]
    mod_map = lambda i: ((i * tm) // rows_per_mod, 0, 0)
    return pl.pallas_call(
        functools.partial(_inproj_kernel, n_chunk=640),
        grid=(t // tm,),
        in_specs=[
            pl.BlockSpec((tm, d), lambda i: (i, 0)),
            pl.BlockSpec((1, d), lambda i: (0, 0)),
            pl.BlockSpec((1, r, d), mod_map),
            pl.BlockSpec((1, r, d), mod_map),
            pl.BlockSpec((d, n), lambda i: (0, 0)),
        ],
        out_specs=pl.BlockSpec((tm, n), lambda i: (i, 0)),
        out_shape=jax.ShapeDtypeStruct((t, n), F32),
        compiler_params=_cparams(("arbitrary",)),
        name="inproj",
    )(x, nw.reshape(1, d), scale, shift, w_bf)


def _outproj_kernel(*refs, n_in, final_norm):
    a_refs = refs[:n_in]
    w_refs = refs[n_in:2 * n_in]
    x_ref, g_ref, fw_ref, o_ref = refs[2 * n_in:]
    acc = _dot(a_refs[0][...].astype(BF16), w_refs[0][...])
    for a_ref, w_ref in zip(a_refs[1:], w_refs[1:]):
        acc = acc + _dot(a_ref[...].astype(BF16), w_ref[...])
    y = x_ref[...] + g_ref[0] * acc
    if final_norm:
        ms = jnp.mean(y * y, axis=-1, keepdims=True)
        y = y * lax.rsqrt(ms + 1e-6) * fw_ref[...]
    o_ref[...] = y


def outproj(a_list, w_list, x, gate, final_w, tm, rows_per_mod, final_norm):
    t, d = x.shape
    r = gate.shape[1]
    n_in = len(a_list)
    in_specs = [pl.BlockSpec((tm, a.shape[1]), lambda i: (i, 0)) for a in a_list]
    in_specs += [pl.BlockSpec(w.shape, lambda i: (0, 0)) for w in w_list]
    in_specs += [
        pl.BlockSpec((tm, d), lambda i: (i, 0)),
        pl.BlockSpec((1, r, d), lambda i: ((i * tm) // rows_per_mod, 0, 0)),
        pl.BlockSpec((1, d), lambda i: (0, 0)),
    ]
    return pl.pallas_call(
        functools.partial(_outproj_kernel, n_in=n_in, final_norm=final_norm),
        grid=(t // tm,),
        in_specs=in_specs,
        out_specs=pl.BlockSpec((tm, d), lambda i: (i, 0)),
        out_shape=jax.ShapeDtypeStruct((t, d), F32),
        compiler_params=_cparams(("arbitrary",)),
        name="outproj",
    )(*a_list, *w_list, x, gate, final_w.reshape(1, d))


def _rope_tile(t, cos, sin_signed, lane):
    fwd = pltpu.roll(t, LANES - HEAD_DIM // 2, 1)
    bwd = pltpu.roll(t, HEAD_DIM // 2, 1)
    partner = jnp.where((lane % HEAD_DIM) < HEAD_DIM // 2, fwd, bwd)
    return t * cos + partner * sin_signed


def _nsa_prep_kernel(q_ref, kv01_ref, kv23_ref, kv45_ref, cos_ref, sin_ref,
                     rows_ref, win_ref, qrot_ref, *hm_refs, head_major):
    cos = cos_ref[...]
    sin = sin_ref[...]
    lane = lax.broadcasted_iota(jnp.int32, cos.shape, 1)
    rope = lambda t: _rope_tile(t, cos, sin, lane)
    kv01, kv23, kv45 = kv01_ref[...], kv23_ref[...], kv45_ref[...]
    k_slc = rope(kv23[:, :LANES])
    v_slc = kv23[:, LANES:]
    k_win = rope(kv45[:, :LANES])
    v_win = kv45[:, LANES:]
    rows_ref[:, 0:LANES] = rope(kv01[:, :LANES])
    rows_ref[:, LANES:2 * LANES] = kv01[:, LANES:]
    rows_ref[:, 2 * LANES:3 * LANES] = k_slc
    rows_ref[:, 3 * LANES:4 * LANES] = v_slc
    win_ref[:, 0:LANES] = k_win
    win_ref[:, LANES:2 * LANES] = v_win
    q = q_ref[...]
    qr = [rope(q[:, j * LANES:(j + 1) * LANES]) for j in range(NSA_WIDTH // LANES)]
    for j, t in enumerate(qr):
        qrot_ref[:, j * LANES:(j + 1) * LANES] = t
    if head_major:
        qh_ref, ks_ref, vs_ref, kw_ref, vw_ref = hm_refs
        scale = HEAD_DIM ** -0.5
        for h in range(NSA_HEADS):
            lo = (h % 2) * HEAD_DIM
            qh_ref[0, h] = (qr[h // 2][:, lo:lo + HEAD_DIM] * scale).astype(BF16)
        for g in range(NSA_KV_HEADS):
            lo = g * HEAD_DIM
            ks_ref[0, g] = k_slc[:, lo:lo + HEAD_DIM].astype(BF16)
            vs_ref[0, g] = v_slc[:, lo:lo + HEAD_DIM].astype(BF16)
            kw_ref[0, g] = k_win[:, lo:lo + HEAD_DIM].astype(BF16)
            vw_ref[0, g] = v_win[:, lo:lo + HEAD_DIM].astype(BF16)


def nsa_prep(proj, cos_tab, sin_tab, tm, seq, head_major):
    t = proj.shape[0]
    tab_blocks = cos_tab.shape[0] // tm
    sb = seq // tm
    out_shape = [jax.ShapeDtypeStruct((t, 4 * LANES), F32),
                 jax.ShapeDtypeStruct((t, 2 * LANES), F32),
                 jax.ShapeDtypeStruct((t, NSA_WIDTH), F32)]
    out_specs = [pl.BlockSpec((tm, 4 * LANES), lambda i: (i, 0)),
                 pl.BlockSpec((tm, 2 * LANES), lambda i: (i, 0)),
                 pl.BlockSpec((tm, NSA_WIDTH), lambda i: (i, 0))]
    if head_major:
        nb = t // seq
        hm_map = lambda i: (i // sb, 0, i % sb, 0)
        out_shape.append(jax.ShapeDtypeStruct((nb, NSA_HEADS, seq, HEAD_DIM), BF16))
        out_specs.append(pl.BlockSpec((1, NSA_HEADS, tm, HEAD_DIM), hm_map))
        for _ in range(4):
            out_shape.append(jax.ShapeDtypeStruct((nb, NSA_KV_HEADS, seq, HEAD_DIM), BF16))
            out_specs.append(pl.BlockSpec((1, NSA_KV_HEADS, tm, HEAD_DIM), hm_map))
    kvb = E_KV // (2 * LANES)
    return pl.pallas_call(
        functools.partial(_nsa_prep_kernel, head_major=head_major),
        grid=(t // tm,),
        in_specs=[
            pl.BlockSpec((tm, NSA_WIDTH), lambda i: (i, E_Q // NSA_WIDTH)),
            pl.BlockSpec((tm, 2 * LANES), lambda i: (i, kvb)),
            pl.BlockSpec((tm, 2 * LANES), lambda i: (i, kvb + 1)),
            pl.BlockSpec((tm, 2 * LANES), lambda i: (i, kvb + 2)),
            pl.BlockSpec((tm, LANES), lambda i: (i % tab_blocks, 0)),
            pl.BlockSpec((tm, LANES), lambda i: (i % tab_blocks, 0)),
        ],
        out_specs=out_specs,
        out_shape=out_shape,
        compiler_params=_cparams(("arbitrary",)),
        name="nsa_prep",
    )(proj, proj, proj, proj, cos_tab, sin_tab)


def _nsa_compress_kernel(x_ref, pe_ref, w1_ref, w2_ref, o_ref):
    r = x_ref.shape[0]
    row_w = 4 * LANES
    acc = jnp.zeros((r, 4 * CMP_HIDDEN), F32)
    for t in range(CMP_BLOCK):
        xt = x_ref[:, t * row_w:t * row_w + 2 * LANES] + pe_ref[t:t + 1, :]
        acc = acc + _dot(xt.astype(BF16), w1_ref[t])
    o_ref[...] = _dot(_silu(acc).astype(BF16), w2_ref[...])


def nsa_compress(xblk, pe_tab, w1_blk, w2_blk, r):
    nb, kdim = xblk.shape
    return pl.pallas_call(
        _nsa_compress_kernel,
        grid=(nb // r,),
        in_specs=[
            pl.BlockSpec((r, kdim), lambda i: (i, 0)),
            pl.BlockSpec(pe_tab.shape, lambda i: (0, 0)),
            pl.BlockSpec(w1_blk.shape, lambda i: (0, 0, 0)),
            pl.BlockSpec(w2_blk.shape, lambda i: (0, 0)),
        ],
        out_specs=pl.BlockSpec((r, 4 * HEAD_DIM), lambda i: (i, 0)),
        out_shape=jax.ShapeDtypeStruct((nb, 4 * HEAD_DIM), F32),
        compiler_params=_cparams(("arbitrary",)),
        name="nsa_compress",
    )(xblk, pe_tab, w1_blk, w2_blk)


def _nsa_compress_cache_kernel(x_ref, pe_ref, w1_ref, w2_ref, o_ref, xbuf):
    n_page = x_ref.shape[1]
    gd = NSA_KV_HEADS * HEAD_DIM
    per_page = PAGE_SIZE // CMP_BLOCK

    def to_token_major(p, carry):
        for c in range(2):
            xt = x_ref[0, p, c].reshape(gd, PAGE_SIZE)
            xbuf[c, pl.ds(pl.multiple_of(p * PAGE_SIZE, PAGE_SIZE), PAGE_SIZE), :] = xt.T + pe_ref[c]
        return carry

    lax.fori_loop(0, n_page, to_token_major, 0)
    rows = n_page * per_page
    acc = [jnp.zeros((rows, NSA_KV_HEADS * CMP_HIDDEN), F32) for _ in range(2)]
    for t in range(CMP_BLOCK):
        for c in range(2):
            a = xbuf[c, pl.ds(t, rows, stride=CMP_BLOCK), :].astype(BF16)
            acc[c] = acc[c] + _dot(a, w1_ref[c, t])
    for c in range(2):
        o_ref[:, c * gd:(c + 1) * gd] = _dot(_silu(acc[c]).astype(BF16), w2_ref[c])


def nsa_compress_cache(cache_t, layer, pe_tok, w1_bd, w2_bd, n_page):
    n_pool = cache_t.shape[1]
    per_page = PAGE_SIZE // CMP_BLOCK
    gd = NSA_KV_HEADS * HEAD_DIM
    full = lambda a: pl.BlockSpec(a.shape, lambda i: (0,) * a.ndim)
    return pl.pallas_call(
        _nsa_compress_cache_kernel,
        grid=(n_pool // n_page,),
        in_specs=[
            pl.BlockSpec((1, n_page, 2, NSA_KV_HEADS, HEAD_DIM, PAGE_SIZE), lambda i: (layer, i, 0, 0, 0, 0)),
            full(pe_tok), full(w1_bd), full(w2_bd),
        ],
        out_specs=pl.BlockSpec((n_page * per_page, 2 * gd), lambda i: (i, 0)),
        out_shape=jax.ShapeDtypeStruct((n_pool * per_page, 2 * gd), F32),
        scratch_shapes=[pltpu.VMEM((2, n_page * PAGE_SIZE, gd), F32)],
        compiler_params=_cparams(("arbitrary",)),
        name="nsa_compress_cache",
    )(cache_t, pe_tok, w1_bd, w2_bd)


def _page_gather_kernel(pt_ref, pool_ref, o_ref):
    b = pl.program_id(0)
    for j in range(o_ref.shape[1]):
        o_ref[0, j:j + 1, :] = pool_ref[pl.ds(pt_ref[b, j], 1), :]


def page_gather(pool, page_table):
    nb, npg = page_table.shape
    w = pool.shape[1]
    return pl.pallas_call(
        _page_gather_kernel,
        grid_spec=pltpu.PrefetchScalarGridSpec(
            num_scalar_prefetch=1,
            grid=(nb,),
            in_specs=[pl.BlockSpec(pool.shape, lambda b, pt: (0, 0))],
            out_specs=pl.BlockSpec((1, npg, w), lambda b, pt: (b, 0, 0)),
        ),
        out_shape=jax.ShapeDtypeStruct((nb, npg, w), F32),
        compiler_params=_cparams(("arbitrary",)),
        name="page_gather",
    )(page_table, pool)


def _masked_softmax3(s, mask):
    s = jnp.where(mask[None], s, -jnp.inf)
    m = jnp.max(s, axis=-1, keepdims=True)
    m = jnp.where(m > -jnp.inf, m, 0.0)
    p = jnp.exp(s - m)
    d = jnp.sum(p, axis=-1, keepdims=True)
    return p / jnp.where(d > 0, d, 1.0)


def _cmp_branch_and_select(q2, kc, vc, qpos, qpos_row, nq, imp_sc, n_blk):
    s = _dot_nt(q2, kc).reshape(NSA_HG, nq, 2 * LANES)
    col = lax.broadcasted_iota(jnp.int32, (1, 2 * LANES), 1)
    cblk = 2 * (col % LANES) + col // LANES
    cmp_end = (cblk + 1) * CMP_BLOCK - 1
    p = _masked_softmax3(s, cmp_end <= qpos)
    o_c = _dot(p.reshape(NSA_HG * nq, 2 * LANES).astype(BF16), vc).reshape(NSA_HG, nq, HEAD_DIM)
    imp = jnp.sum(p, axis=0)
    imp = (imp[:, :LANES] + imp[:, LANES:]).T
    j = lax.broadcasted_iota(jnp.int32, (LANES, 1), 0)
    cur = qpos_row // SEL_BLOCK
    forced = jnp.where(j == 0, 1, jnp.where(j == cur, 1, jnp.where(j == cur - 1, 1, 0))) > 0
    causal = j <= cur
    imp = jnp.where(forced, imp + FORCE_BONUS, imp)
    imp = jnp.where(causal, imp, -jnp.inf)
    imp_sc[...] = imp

    def count_ahead(i, cnt):
        ri = imp_sc[pl.ds(i, 1), :]
        wins_ties = jnp.where(j > i, 1.0, 0.0)
        return cnt + jnp.where(ri >= imp, jnp.where(ri > imp, 1.0, wins_ties), 0.0)

    rank = lax.fori_loop(0, n_blk, count_ahead, jnp.zeros((LANES, nq), F32))
    sel = jnp.where(causal, jnp.where(rank < TOP_N, 1.0, 0.0), 0.0)
    return o_c, sel.astype(BF16)


def _flash_init(nq):
    return (jnp.full((NSA_HG, nq, 1), M_INIT, F32), jnp.zeros((NSA_HG, nq, 1), F32),
            jnp.zeros((NSA_HG, nq, HEAD_DIM), F32))


def _mask_bias(allowed):
    return jnp.where(allowed, 0.0, NEG_BIG)


def _flash_update(carry, s, bias, v, v_transposed=False):
    m, l, acc = carry
    s = s + bias[None]
    m_new = jnp.maximum(m, jnp.max(s, axis=-1, keepdims=True))
    alpha = jnp.exp(m - m_new)
    p = jnp.exp(s - m_new)
    l = alpha * l + jnp.sum(p, axis=-1, keepdims=True)
    h, q, n = p.shape
    p2 = p.reshape(h * q, n).astype(v.dtype)
    pv = (_dot_nt(p2, v) if v_transposed else _dot(p2, v)).reshape(h, q, HEAD_DIM)
    return m_new, l, alpha * acc + pv


def _flash_finish(carry):
    _, l, acc = carry
    return acc / jnp.where(l > 0, l, 1.0)


def _gate_column(sig, lane, col):
    return jnp.sum(jnp.where(lane == col, sig, 0.0), axis=-1, keepdims=True)


def _nsa_attn_kernel(q_ref, kc_ref, vc_ref, ks_ref, vs_ref, kw_ref, vw_ref, gd_ref, za_ref, e_ref,
                     o_ref, s_sc, p_sc, b_sc, m_sc, l_sc, a_sc, acc_sc, imp_sc, *, tq, tk):
    g = pl.program_id(1)
    q0 = pl.program_id(2) * tq
    q2 = q_ref[0].reshape(NSA_HG * tq, HEAD_DIM)
    qpos = q0 + lax.broadcasted_iota(jnp.int32, (tq, 1), 0)
    qpos_row = q0 + lax.broadcasted_iota(jnp.int32, (1, tq), 1)
    n_blk = (q0 + tq - 1) // SEL_BLOCK + 1
    o_c, sel_t = _cmp_branch_and_select(q2, kc_ref[0, 0], vc_ref[0, 0], qpos, qpos_row, tq, imp_sc, n_blk)
    rb = 2 * SUBLANES

    def reset():
        m_sc[...] = jnp.full(m_sc.shape, M_INIT, F32)
        l_sc[...] = jnp.zeros(l_sc.shape, F32)
        acc_sc[...] = jnp.zeros(acc_sc.shape, F32)

    def softmax_step(k, v, bias, n):
        s_sc[:, :n] = _dot_nt(q2, k)
        b_sc[:, :n] = bias

        def row_block(r, carry):
            r0 = pl.multiple_of(r * rb, rb)
            rows = pl.ds(r0, rb)
            sb = s_sc[rows, :n] + b_sc[pl.ds(pl.multiple_of(lax.rem(r0, tq), rb), rb), :n]
            m_old = m_sc[rows, :]
            m_new = jnp.maximum(m_old, jnp.max(sb, axis=-1, keepdims=True))
            p = jnp.exp(sb - pltpu.repeat(m_new, n // LANES, axis=1))
            alpha = jnp.exp(m_old - m_new)
            l_sc[rows, :] = alpha * l_sc[rows, :] + jnp.sum(p, axis=-1, keepdims=True)
            m_sc[rows, :] = m_new
            a_sc[rows, :] = alpha
            p_sc[rows, :n] = p.astype(BF16)
            return carry

        lax.fori_loop(0, NSA_HG * tq // rb, row_block, 0, unroll=True)
        acc_sc[...] = a_sc[:, :HEAD_DIM] * acc_sc[...] + _dot(p_sc[:, :n], v)

    def finish():
        l = l_sc[:, :HEAD_DIM]
        return acc_sc[...] / jnp.where(l > 0, l, 1.0)

    def sel_tile(t, diagonal):
        k0 = pl.multiple_of(t * tk, tk)
        bias = (_dot_tn(sel_t, e_ref[:, pl.ds(k0, tk)]) - 1.0) * (-NEG_BIG)
        if diagonal:
            kpos = k0 + lax.broadcasted_iota(jnp.int32, (1, tk), 1)
            bias = jnp.where(kpos <= qpos, bias, NEG_BIG)
        softmax_step(ks_ref[0, 0, pl.ds(k0, tk), :], vs_ref[0, 0, pl.ds(k0, tk), :], bias, tk)

    n_below = q0 // tk
    reset()

    def below(t, carry):
        sel_tile(t, False)
        return carry

    lax.fori_loop(0, n_below, below, 0)
    sel_tile(n_below, True)
    o_s = finish()

    wn = WINDOW + tq
    w0 = pl.multiple_of(jnp.maximum(q0 - WINDOW, 0), tq)
    dist = qpos - (w0 + lax.broadcasted_iota(jnp.int32, (1, wn), 1))
    reset()
    softmax_step(kw_ref[0, 0, pl.ds(w0, wn), :], vw_ref[0, 0, pl.ds(w0, wn), :],
                 _mask_bias(jnp.where(dist >= 0, dist, WINDOW) < WINDOW), wn)
    o_w = finish()

    sig = _sigmoid(gd_ref[...])
    lane = lax.broadcasted_iota(jnp.int32, sig.shape, 1)
    outs = []
    for h in range(NSA_HG):
        head = g * NSA_HG + h
        hrows = slice(h * tq, (h + 1) * tq)
        g_c = _gate_column(sig, lane, GD_GATE + head)
        g_s = _gate_column(sig, lane, GD_GATE + NSA_HEADS + head)
        g_w = _gate_column(sig, lane, GD_GATE + 2 * NSA_HEADS + head)
        outs.append(g_c * o_c[h] + g_s * o_s[hrows] + g_w * o_w[hrows])
    o_ref[...] = jnp.concatenate(outs, axis=-1) * _silu(za_ref[...])


def nsa_attn_prompt(proj, qh, kc, vc, ks, vs, kw, vw, e_mat, tq=128, tk=512):
    nb, _, seq, _ = qh.shape
    assert seq >= WINDOW + tq and tk % tq == 0 and tk <= WINDOW + tq
    rows, wn = NSA_HG * tq, WINDOW + tq
    nq = seq // tq
    gw = NSA_HG * HEAD_DIM
    kv_spec = pl.BlockSpec((1, 1, seq, HEAD_DIM), lambda b, g, i: (b, g, 0, 0))
    c_spec = pl.BlockSpec((1, 1, 2 * LANES, HEAD_DIM), lambda b, g, i: (b, g, 0, 0))
    return pl.pallas_call(
        functools.partial(_nsa_attn_kernel, tq=tq, tk=tk),
        grid=(nb, NSA_KV_HEADS, nq),
        in_specs=[
            pl.BlockSpec((1, NSA_HG, tq, HEAD_DIM), lambda b, g, i: (b, g, i, 0)),
            c_spec, c_spec, kv_spec, kv_spec, kv_spec, kv_spec,
            pl.BlockSpec((tq, LANES), lambda b, g, i: (b * nq + i, E_GD // LANES)),
            pl.BlockSpec((tq, gw), lambda b, g, i: (b * nq + i, E_ZA // gw + g)),
            pl.BlockSpec(e_mat.shape, lambda b, g, i: (0, 0)),
        ],
        out_specs=pl.BlockSpec((tq, gw), lambda b, g, i: (b * nq + i, g)),
        out_shape=jax.ShapeDtypeStruct((nb * seq, NSA_WIDTH), F32),
        scratch_shapes=[
            pltpu.VMEM((rows, wn), F32),
            pltpu.VMEM((rows, wn), BF16),
            pltpu.VMEM((tq, wn), F32),
            pltpu.VMEM((rows, LANES), F32),
            pltpu.VMEM((rows, LANES), F32),
            pltpu.VMEM((rows, LANES), F32),
            pltpu.VMEM((rows, HEAD_DIM), F32),
            pltpu.VMEM((LANES, tq), F32),
        ],
        compiler_params=_cparams(("arbitrary", "arbitrary", "arbitrary")),
        name="nsa_attn_prompt",
    )(qh, kc, vc, ks, vs, kw, vw, proj, proj, e_mat)


def _nsa_dec_kernel(pt_ref, q_ref, kc_ref, vc_ref, *rest, past_len, n_pages):
    del pt_ref
    page_refs = rest[:n_pages]
    rows_ref, pwin_ref, wnew_ref, gd_ref, za_ref, e_ref, o_ref, imp_sc = rest[n_pages:]
    nq = DEC_PAD
    scale = HEAD_DIM ** -0.5
    qpos = past_len + lax.broadcasted_iota(jnp.int32, (nq, 1), 0)
    qpos_row = past_len + lax.broadcasted_iota(jnp.int32, (1, nq), 1)
    n_blk = (past_len + nq - 1) // SEL_BLOCK + 1
    q = q_ref[0]

    def q_group(g):
        parts = [q[:, (g * NSA_HG + h) * HEAD_DIM:(g * NSA_HG + h + 1) * HEAD_DIM] for h in range(NSA_HG)]
        return jnp.concatenate(parts, axis=0) * scale

    qg = [q_group(g) for g in range(NSA_KV_HEADS)]

    rows = rows_ref[0]
    wnew = wnew_ref[0]
    win_keep = pwin_ref.shape[-1]
    sig = _sigmoid(gd_ref[0])
    lane = lax.broadcasted_iota(jnp.int32, sig.shape, 1)
    kpos = lax.broadcasted_iota(jnp.int32, (1, past_len), 1)
    npos = past_len + lax.broadcasted_iota(jnp.int32, (1, nq), 1)
    wpos = past_len - win_keep + lax.broadcasted_iota(jnp.int32, (1, win_keep), 1)
    outs = []
    for g in range(NSA_KV_HEADS):
        qb = qg[g].astype(BF16)
        o_c, sel_t = _cmp_branch_and_select(qb, kc_ref[0, g].astype(BF16), vc_ref[0, g].astype(BF16),
                                            qpos, qpos_row, nq, imp_sc, n_blk)
        kt = jnp.concatenate([r[0, 0, g] for r in page_refs], axis=1).astype(BF16)
        vt = jnp.concatenate([r[0, 1, g] for r in page_refs], axis=1).astype(BF16)
        s = _dot(qb, kt).reshape(NSA_HG, nq, past_len)
        blk_on = _dot_tn(sel_t, e_ref[:, :past_len])
        bias = _mask_bias(jnp.where(kpos <= qpos, blk_on, 0.0) > 0.5)
        carry = _flash_update(_flash_init(nq), s, bias, vt, v_transposed=True)
        kn = rows[:, 2 * LANES + g * HEAD_DIM:2 * LANES + (g + 1) * HEAD_DIM]
        vn = rows[:, 3 * LANES + g * HEAD_DIM:3 * LANES + (g + 1) * HEAD_DIM]
        s = _dot_nt(qg[g], kn).reshape(NSA_HG, nq, nq)
        blk_on = _dot_tn(sel_t, e_ref[:, past_len:past_len + LANES])[:, :nq]
        bias = _mask_bias(jnp.where(npos <= qpos, blk_on, 0.0) > 0.5)
        o_s = _flash_finish(_flash_update(carry, s, bias, vn))
        s = _dot(qb, pwin_ref[0, 0, g].astype(BF16)).reshape(NSA_HG, nq, win_keep)
        dist = qpos - wpos
        bias = _mask_bias(jnp.where(wpos >= 0, jnp.where(dist >= 0, dist, WINDOW), WINDOW) < WINDOW)
        carry = _flash_update(_flash_init(nq), s, bias, pwin_ref[0, 1, g].astype(BF16), v_transposed=True)
        kwn = wnew[:, g * HEAD_DIM:(g + 1) * HEAD_DIM]
        vwn = wnew[:, LANES + g * HEAD_DIM:LANES + (g + 1) * HEAD_DIM]
        s = _dot_nt(qg[g], kwn).reshape(NSA_HG, nq, nq)
        dist = qpos - npos
        bias = _mask_bias(jnp.where(dist >= 0, dist, WINDOW) < WINDOW)
        o_w = _flash_finish(_flash_update(carry, s, bias, vwn))
        for h in range(NSA_HG):
            head = g * NSA_HG + h
            g_c = _gate_column(sig, lane, GD_GATE + head)
            g_s = _gate_column(sig, lane, GD_GATE + NSA_HEADS + head)
            g_w = _gate_column(sig, lane, GD_GATE + 2 * NSA_HEADS + head)
            outs.append(g_c * o_c[h] + g_s * o_s[h] + g_w * o_w[h])
    o_ref[0] = jnp.concatenate(outs, axis=-1) * _silu(za_ref[0])


def nsa_attn_decode(page_table, layer, qrot, kc, vc, cache_t, rows, pwin_t, wnew, proj3, e_mat, past_len):
    nb, npg = page_table.shape
    win_keep = pwin_t.shape[-1]
    per_b = lambda blk: pl.BlockSpec(blk, lambda b, pt: (b,) + (0,) * (len(blk) - 1))
    page_blk = (None, 1, 2, NSA_KV_HEADS, HEAD_DIM, PAGE_SIZE)
    page_specs = [pl.BlockSpec(page_blk, lambda b, pt, j=j: (layer, pt[b, j], 1, 0, 0, 0)) for j in range(npg)]
    return pl.pallas_call(
        functools.partial(_nsa_dec_kernel, past_len=past_len, n_pages=npg),
        grid_spec=pltpu.PrefetchScalarGridSpec(
            num_scalar_prefetch=1,
            grid=(nb,),
            in_specs=[
                per_b((1, DEC_PAD, NSA_WIDTH)),
                per_b((1, NSA_KV_HEADS, 2 * LANES, HEAD_DIM)),
                per_b((1, NSA_KV_HEADS, 2 * LANES, HEAD_DIM)),
                *page_specs,
                per_b((1, DEC_PAD, 4 * LANES)),
                pl.BlockSpec((None, 1, 2, NSA_KV_HEADS, HEAD_DIM, win_keep), lambda b, pt: (layer, b, 0, 0, 0, 0)),
                per_b((1, DEC_PAD, 2 * LANES)),
                pl.BlockSpec((1, DEC_PAD, LANES), lambda b, pt: (b, 0, E_GD // LANES)),
                pl.BlockSpec((1, DEC_PAD, NSA_WIDTH), lambda b, pt: (b, 0, E_ZA // NSA_WIDTH)),
                pl.BlockSpec(e_mat.shape, lambda b, pt: (0, 0)),
            ],
            out_specs=per_b((1, DEC_PAD, NSA_WIDTH)),
            scratch_shapes=[pltpu.VMEM((LANES, DEC_PAD), F32)],
        ),
        out_shape=jax.ShapeDtypeStruct((nb, DEC_PAD, NSA_WIDTH), F32),
        compiler_params=_cparams(("arbitrary",)),
        name="nsa_attn_decode",
    )(page_table, qrot, kc, vc, *([cache_t] * npg), rows, pwin_t, wnew, proj3, proj3, e_mat)


def _tri_masks(c):
    row = lax.broadcasted_iota(jnp.int32, (c, c), 0)
    col = lax.broadcasted_iota(jnp.int32, (c, c), 1)
    lower = row >= col
    return lower, jnp.where(lower, 1.0, 0.0).astype(BF16), jnp.where(row <= col, 1.0, 0.0).astype(BF16)


def _ssd_kernel(zb_ref, xbc_ref, gd_ref, prev_ref, h0_ref, cw_ref, cb_ref, dtb_ref, alog_ref, alog16_ref,
                dexp_ref, nw_ref, e16_ref, y_ref, so_ref, h_sc, xbuf, *, chunk, n_valid):
    c = pl.program_id(1)
    hp = SSD_HEADS // SSD_GROUPS * SSD_HEAD_DIM
    halo = SUBLANES

    @pl.when(c == 0)
    def _():
        h_sc[...] = h0_ref[0]
        xbuf[0:halo] = prev_ref[0]

    xbuf[halo:halo + chunk] = xbc_ref[...]
    conv = cb_ref[...]
    for i in range(SSD_CONV):
        lo = halo - (SSD_CONV - 1) + i
        conv = conv + xbuf[lo:lo + chunk] * cw_ref[i:i + 1, :]
    xbuf[0:halo] = xbuf[chunk:chunk + halo]
    act = _silu(conv)
    xs = act[:, :SSD_INNER]
    bm = act[:, SSD_INNER:SSD_INNER + SSD_GROUPS * SSD_STATE]
    cm = act[:, SSD_INNER + SSD_GROUPS * SSD_STATE:]

    dt16 = _softplus(gd_ref[...] + dtb_ref[...])[:, GD_DT:GD_DT + SSD_HEADS]
    if n_valid is not None:
        tok = c * chunk + lax.broadcasted_iota(jnp.int32, (chunk, 1), 0)
        dt16 = jnp.where(tok < n_valid, dt16, 0.0)
    dt = _dot3(dt16, e16_ref[...])
    da = dt * (-jnp.exp(alog_ref[...]))
    da16 = dt16 * (-jnp.exp(alog16_ref[...]))
    lower, tri, tri_t = _tri_masks(chunk)
    cum = _dot3_wx(tri, da)
    cum16 = _dot3_wx(tri, da16)
    cum16_t = _dot3_tn(da16, tri_t)
    ones8 = jnp.ones((chunk, SUBLANES), BF16)
    decay_col = jnp.exp(_dot3_tn(da, ones8)[:, 0:1])
    xdt = xs * dt
    cum_last = cum[chunk - 1:chunk, :]
    xw = (xdt * jnp.exp(cum_last - cum)).astype(BF16)
    xdt_b = xdt.astype(BF16)
    lane = lax.broadcasted_iota(jnp.int32, (chunk, LANES), 1)
    low_half = lane < SSD_HEAD_DIM

    y_intra = [None] * (SSD_INNER // LANES)
    y_inter = []
    for g in range(SSD_GROUPS):
        cg = cm[:, g * SSD_STATE:(g + 1) * SSD_STATE].astype(BF16)
        bg = bm[:, g * SSD_STATE:(g + 1) * SSD_STATE].astype(BF16)
        h_g = h_sc[g * hp:(g + 1) * hp, :]
        cb = _dot_nt(cg, bg)
        y_inter.append(_dot_nt(cg, h_g.astype(BF16)))
        for hh in range(SSD_HEADS // SSD_GROUPS):
            h = g * (SSD_HEADS // SSD_GROUPS) + hh
            diff = cum16[:, h:h + 1] - cum16_t[h:h + 1, :]
            lmat = jnp.where(lower, jnp.exp(jnp.where(lower, diff, 0.0)), 0.0)
            m = (cb * lmat).astype(BF16)
            pair = h // 2
            x_pair = xdt_b[:, pair * LANES:(pair + 1) * LANES]
            keep = low_half if h % 2 == 0 else jnp.logical_not(low_half)
            contrib = _dot(m, jnp.where(keep, x_pair, jnp.zeros_like(x_pair)))
            y_intra[pair] = contrib if y_intra[pair] is None else y_intra[pair] + contrib
        h_sc[g * hp:(g + 1) * hp, :] = (h_g * decay_col[g * hp:(g + 1) * hp, :]
                                        + _dot_tn(xw[:, g * hp:(g + 1) * hp], bg))
    y = jnp.concatenate(y_intra, axis=-1) + jnp.concatenate(y_inter, axis=-1) * jnp.exp(cum)
    y = y + xs * dexp_ref[...]
    y = y * _silu(zb_ref[...])
    ms = jnp.mean(y * y, axis=-1, keepdims=True)
    y_ref[...] = y * lax.rsqrt(ms + 1e-6) * nw_ref[...]

    @pl.when(c == pl.num_programs(1) - 1)
    def _():
        so_ref[0] = h_sc[...]


def ssd_mixer(proj, prev8, h0, conv_w, conv_b, dtb_row, alog_exp, alog16, d_exp, norm_w, e16, nb, seq, chunk,
              n_valid):
    nch = seq // chunk
    row = lambda b, c: b * nch + c
    full = lambda a: pl.BlockSpec(a.shape, lambda b, c: (0,) * a.ndim)
    return pl.pallas_call(
        functools.partial(_ssd_kernel, chunk=chunk, n_valid=n_valid),
        grid=(nb, nch),
        in_specs=[
            pl.BlockSpec((chunk, SSD_INNER), lambda b, c: (row(b, c), E_ZB // SSD_INNER)),
            pl.BlockSpec((chunk, SSD_CONV_CH), lambda b, c: (row(b, c), E_XBC // SSD_CONV_CH)),
            pl.BlockSpec((chunk, LANES), lambda b, c: (row(b, c), E_GD // LANES)),
            pl.BlockSpec((1, SUBLANES, SSD_CONV_CH), lambda b, c: (b, 0, 0)),
            pl.BlockSpec((1, SSD_INNER, SSD_STATE), lambda b, c: (b, 0, 0)),
            full(conv_w), full(conv_b), full(dtb_row), full(alog_exp), full(alog16), full(d_exp), full(norm_w),
            full(e16),
        ],
        out_specs=[
            pl.BlockSpec((chunk, SSD_INNER), lambda b, c: (row(b, c), 0)),
            pl.BlockSpec((1, SSD_INNER, SSD_STATE), lambda b, c: (b, 0, 0)),
        ],
        out_shape=[jax.ShapeDtypeStruct((nb * seq, SSD_INNER), F32),
                   jax.ShapeDtypeStruct((nb, SSD_INNER, SSD_STATE), F32)],
        scratch_shapes=[pltpu.VMEM((SSD_INNER, SSD_STATE), F32),
                        pltpu.VMEM((SUBLANES + chunk, SSD_CONV_CH), F32)],
        compiler_params=_cparams(("arbitrary", "arbitrary")),
        name="ssd_mixer",
    )(proj, proj, proj, prev8, h0, conv_w, conv_b, dtb_row, alog_exp, alog16, d_exp, norm_w, e16)


def _gla_kernel(q_ref, k_ref, v_ref, r_ref, glr_ref, s0_ref, wg_ref, bg_ref, nw_ref, o_ref, so_ref, s_sc,
                *, chunk, n_sub, n_valid):
    c = pl.program_id(1)
    n_seq = q_ref.shape[0]
    span = n_sub * chunk

    @pl.when(c == 0)
    def _():
        s_sc[...] = s0_ref[...]

    row = lax.broadcasted_iota(jnp.int32, (span, span), 0)
    col = lax.broadcasted_iota(jnp.int32, (span, span), 1)
    lower = jnp.where(row // chunk == col // chunk, row - col, -1) >= 0
    tri = jnp.where(lower, 1.0, 0.0).astype(BF16)
    crow = lax.broadcasted_iota(jnp.int32, (span, n_sub * SUBLANES), 0) // chunk
    ccol = lax.broadcasted_iota(jnp.int32, (span, n_sub * SUBLANES), 1) // SUBLANES
    chunk_ones = jnp.where(crow == ccol, 1.0, 0.0).astype(BF16)
    nw = nw_ref[...]
    for b in range(n_seq):
        lg = _log_sigmoid(_dot(glr_ref[b].astype(BF16), wg_ref[...]) + bg_ref[...]) * (1.0 / GLA_TAU)
        k = k_ref[b]
        if n_valid is not None:
            tok = c * span + lax.broadcasted_iota(jnp.int32, (span, 1), 0)
            lg = jnp.where(tok < n_valid, lg, 0.0)
            k = jnp.where(tok < n_valid, k, 0.0)
        bcum = _dot3_wx(tri, lg)
        total = _dot3_tn(lg, chunk_ones)
        qg = (q_ref[b] * (GLA_DK ** -0.5) * jnp.exp(bcum)).astype(BF16)
        kg = (k * jnp.exp(-bcum)).astype(BF16)
        kd = jnp.concatenate(
            [k[u * chunk:(u + 1) * chunk] * jnp.exp(bcum[(u + 1) * chunk - 1:(u + 1) * chunk]
                                                    - bcum[u * chunk:(u + 1) * chunk]) for u in range(n_sub)],
            axis=0).astype(BF16)
        v = v_ref[b].astype(BF16)
        r = r_ref[b]
        for h in range(GLA_HEADS):
            kcols = slice(h * GLA_DK, (h + 1) * GLA_DK)
            vcols = slice(h * GLA_DV, (h + 1) * GLA_DV)
            att = jnp.where(lower, _dot_nt(qg[:, kcols], kg[:, kcols]), 0.0)
            o = _dot(att.astype(BF16), v[:, vcols])
            s_h = s_sc[b, kcols, :]
            carried = []
            for u in range(n_sub):
                rows = slice(u * chunk, (u + 1) * chunk)
                carried.append(_dot(qg[rows, kcols], s_h.astype(BF16)))
                decay = jnp.exp(total[kcols, u * SUBLANES:u * SUBLANES + 1])
                s_h = s_h * decay + _dot_tn(kd[rows, kcols], v[rows, vcols])
            s_sc[b, kcols, :] = s_h
            o = o + jnp.concatenate(carried, axis=0)
            ms = jnp.mean(o * o, axis=-1, keepdims=True)
            o = o * lax.rsqrt(ms + 1e-6) * nw
            o_ref[b, :, vcols] = o * _silu(r[:, vcols])

    @pl.when(c == pl.num_programs(1) - 1)
    def _():
        so_ref[...] = s_sc[...]


def gla_mixer(proj, s0, wg_pad, bg, norm_w, nb, seq, chunk, n_sub, n_seq, n_valid):
    span = n_sub * chunk
    full = lambda a: pl.BlockSpec(a.shape, lambda b, c: (0,) * a.ndim)
    srows = GLA_HEADS * GLA_DK
    proj3 = proj.reshape(nb, seq, O_END)
    col_spec = lambda width, off: pl.BlockSpec((n_seq, span, width), lambda b, c: (b, c, off // width))
    st_spec = pl.BlockSpec((n_seq, srows, GLA_DV), lambda b, c: (b, 0, 0))
    o, st = pl.pallas_call(
        functools.partial(_gla_kernel, chunk=chunk, n_sub=n_sub, n_valid=n_valid),
        grid=(nb // n_seq, seq // span),
        in_specs=[
            col_spec(GLA_KEY_WIDTH, O_Q), col_spec(GLA_KEY_WIDTH, O_K), col_spec(GLA_VAL_WIDTH, O_V),
            col_spec(GLA_VAL_WIDTH, O_R), col_spec(LANES, O_GLR), st_spec,
            full(wg_pad), full(bg), full(norm_w),
        ],
        out_specs=[pl.BlockSpec((n_seq, span, GLA_VAL_WIDTH), lambda b, c: (b, c, 0)), st_spec],
        out_shape=[jax.ShapeDtypeStruct((nb, seq, GLA_VAL_WIDTH), F32),
                   jax.ShapeDtypeStruct((nb, srows, GLA_DV), F32)],
        scratch_shapes=[pltpu.VMEM((n_seq, srows, GLA_DV), F32)],
        compiler_params=_cparams(("arbitrary", "arbitrary")),
        name="gla_mixer",
    )(proj3, proj3, proj3, proj3, proj3, s0, wg_pad, bg, norm_w)
    return o.reshape(nb * seq, GLA_VAL_WIDTH), st


def _even_weight(w):
    q, g_a, kv, z_a, z_b, xbc, dt = jnp.split(w, list(np.cumsum(
        [NSA_WIDTH, 3 * NSA_HEADS, 6 * NSA_KV_HEADS * HEAD_DIM, NSA_WIDTH, SSD_INNER, SSD_CONV_CH])), axis=-1)
    pad = jnp.zeros((w.shape[0], E_END - E_GD - 3 * NSA_HEADS - SSD_HEADS), w.dtype)
    return jnp.concatenate([xbc, q, z_b, z_a, kv, g_a, dt, pad], axis=-1).astype(BF16)


def _odd_weight(w):
    q, k, v, glr, r = jnp.split(w, list(np.cumsum(
        [GLA_KEY_WIDTH, GLA_KEY_WIDTH, GLA_VAL_WIDTH, GLA_GATE_RANK])), axis=-1)
    pad = jnp.zeros((w.shape[0], O_END - O_GLR - GLA_GATE_RANK), w.dtype)
    return jnp.concatenate([q, k, v, r, glr, pad], axis=-1).astype(BF16)


def _rope_tables(pos):
    half = HEAD_DIM // 2
    inv = ROPE_THETA ** (-jnp.arange(half, dtype=F32) / half)
    ang = pos.astype(F32)[:, None] * inv[None, :]
    cos, sin = jnp.cos(ang), jnp.sin(ang)
    reps = LANES // HEAD_DIM
    return jnp.tile(jnp.concatenate([cos, cos], -1), (1, reps)), jnp.tile(jnp.concatenate([-sin, sin], -1), (1, reps))


def _compress_weights(pe, w1, w2):
    pe_tab = jnp.concatenate([pe[0], pe[0], pe[1], pe[1]], axis=-1)
    w1r = w1.reshape(2, CMP_BLOCK, HEAD_DIM, CMP_HIDDEN)
    w1_blk = jnp.zeros((CMP_BLOCK, 4 * HEAD_DIM, 4 * CMP_HIDDEN), F32)
    w2_blk = jnp.zeros((4 * CMP_HIDDEN, 4 * HEAD_DIM), F32)
    for part in range(4):
        src = part // 2
        w1_blk = w1_blk.at[:, part * HEAD_DIM:(part + 1) * HEAD_DIM,
                           part * CMP_HIDDEN:(part + 1) * CMP_HIDDEN].set(w1r[src])
        w2_blk = w2_blk.at[part * CMP_HIDDEN:(part + 1) * CMP_HIDDEN,
                           part * HEAD_DIM:(part + 1) * HEAD_DIM].set(w2[src])
    return pe_tab, w1_blk.astype(BF16), w2_blk.astype(BF16)


def _compress_weights_cache(pe, w1, w2):
    w1r = w1.reshape(2, CMP_BLOCK, HEAD_DIM, CMP_HIDDEN)
    w1_bd = jnp.zeros((2, CMP_BLOCK, NSA_KV_HEADS * HEAD_DIM, NSA_KV_HEADS * CMP_HIDDEN), F32)
    w2_bd = jnp.zeros((2, NSA_KV_HEADS * CMP_HIDDEN, NSA_KV_HEADS * HEAD_DIM), F32)
    for g in range(NSA_KV_HEADS):
        w1_bd = w1_bd.at[:, :, g * HEAD_DIM:(g + 1) * HEAD_DIM, g * CMP_HIDDEN:(g + 1) * CMP_HIDDEN].set(w1r)
        w2_bd = w2_bd.at[:, g * CMP_HIDDEN:(g + 1) * CMP_HIDDEN, g * HEAD_DIM:(g + 1) * HEAD_DIM].set(w2)
    pe_tok = jnp.tile(jnp.concatenate([pe] * NSA_KV_HEADS, axis=-1), (1, PAGE_SIZE // CMP_BLOCK, 1))
    return pe_tok, w1_bd.astype(BF16), w2_bd.astype(BF16)


def _cmp_layout(cmp, nb, nc, dtype):
    c = cmp.reshape(nb, nc, 2, NSA_KV_HEADS, HEAD_DIM).transpose(2, 0, 3, 1, 4)
    halves = []
    for par in range(2):
        h = c[:, :, :, par::2]
        halves.append(jnp.pad(h, ((0, 0), (0, 0), (0, 0), (0, LANES - h.shape[3]), (0, 0))))
    c = jnp.concatenate(halves, axis=3).astype(dtype)
    return c[0], c[1]


def _sel_expand(n_keys):
    blk = jnp.arange(n_keys, dtype=jnp.int32) // SEL_BLOCK
    return (blk[None, :] == jnp.arange(LANES, dtype=jnp.int32)[:, None]).astype(BF16)


def kernel(x_prompt, x_sample, c_prompt, c_sample, cache_nsa, cache_nsa_win, state_ssd_conv, state_ssd, state_gla, page_table, norm_w, w_mod, b_mod, w_in_even, w_out_even, nsa_cmp_pe, nsa_cmp_w1, nsa_cmp_w2, ssd_conv_w, ssd_conv_b, ssd_dt_bias, ssd_a_log, ssd_d, ssd_norm_w, w_in_odd, gla_w_gate2, gla_b_gate, gla_norm_w, w_out_odd, final_norm_w):
    bp, sp, d = x_prompt.shape
    bs, ss, _ = x_sample.shape
    depth = norm_w.shape[0]
    n_pool = cache_nsa.shape[1]
    npg = page_table.shape[1]
    past_len = npg * PAGE_SIZE
    win_keep = cache_nsa_win.shape[2]
    assert sp % 512 == 0 and sp // SEL_BLOCK <= LANES and sp // CMP_BLOCK <= 2 * LANES
    assert ss <= DEC_PAD and ss < CMP_BLOCK and past_len % SEL_BLOCK == 0 and ss >= SSD_CONV - 1
    assert past_len // SEL_BLOCK + 1 <= LANES and (bs * DEC_PAD) % 256 == 0 and n_pool % 32 == 0
    tp, td = bp * sp, bs * DEC_PAD
    tm_d = 256

    c_all = jnp.concatenate([c_prompt, c_sample], axis=0)
    c_all = jnp.pad(c_all, ((0, -c_all.shape[0] % SUBLANES), (0, 0)))
    mod = mod_all(c_all, w_mod.astype(BF16), b_mod)

    def mods(l):
        shift, scale, gate = jnp.split(mod[l], 3, axis=-1)
        mp = [m[:bp].reshape(bp, 1, d) for m in (shift, scale, gate)]
        ms = [jnp.repeat(m[bp:bp + bs], DEC_PAD, axis=0).reshape(td // tm_d, tm_d, d) for m in (shift, scale, gate)]
        return mp, ms

    xp = x_prompt.reshape(tp, d)
    xs = jnp.pad(x_sample, ((0, 0), (0, DEC_PAD - ss), (0, 0))).reshape(td, d)

    cos_p, sin_p = _rope_tables(jnp.arange(sp, dtype=jnp.int32))
    cos_s, sin_s = _rope_tables(past_len + jnp.arange(DEC_PAD, dtype=jnp.int32))
    cos_s, sin_s = jnp.tile(cos_s, (tm_d // DEC_PAD, 1)), jnp.tile(sin_s, (tm_d // DEC_PAD, 1))
    cache_t = jnp.transpose(cache_nsa, (0, 1, 3, 4, 5, 2))
    win_t = jnp.transpose(cache_nsa_win, (0, 1, 3, 4, 5, 2))
    e_mat_p = _sel_expand(sp)
    e_mat_s = _sel_expand(past_len + LANES)
    e16 = (jnp.arange(SSD_INNER, dtype=jnp.int32)[None, :] // SSD_HEAD_DIM
           == jnp.arange(SSD_HEADS, dtype=jnp.int32)[:, None]).astype(BF16)

    outs = {k: [] for k in ("kv_p", "kv_s", "win_p", "win_s", "cv_p", "cv_s", "ss_p", "ss_s", "gl_p", "gl_s")}
    for l in range(depth):
        e = l // 2
        (shift_p, scale_p, gate_p), (shift_s, scale_s, gate_s) = mods(l)
        last = l == depth - 1
        if l % 2 == 0:
            w_in = _even_weight(w_in_even[e])
            w_out = w_out_even[e].astype(BF16)
            w_out_a, w_out_b = w_out[:NSA_WIDTH], w_out[NSA_WIDTH:]
            pe_tab, w1_blk, w2_blk = _compress_weights(nsa_cmp_pe[e], nsa_cmp_w1[e], nsa_cmp_w2[e])
            conv_w, conv_b = ssd_conv_w[e], ssd_conv_b[e].reshape(1, SSD_CONV_CH)
            dtb_row = jnp.zeros((1, LANES), F32).at[0, GD_DT:GD_DT + SSD_HEADS].set(ssd_dt_bias[e])
            alog16 = ssd_a_log[e].reshape(1, SSD_HEADS)
            alog_exp = jnp.repeat(ssd_a_log[e], SSD_HEAD_DIM).reshape(1, SSD_INNER)
            d_exp = jnp.repeat(ssd_d[e], SSD_HEAD_DIM).reshape(1, SSD_INNER)
            ssd_nw = ssd_norm_w[e].reshape(1, SSD_INNER)
            ssd_args = (conv_w, conv_b, dtb_row, alog_exp, alog16, d_exp, ssd_nw, e16)

            proj = inproj(xp, norm_w[l], scale_p, shift_p, w_in, 256, sp)
            rows, win, _, qh, ks, vs, kw, vw = nsa_prep(proj, cos_p, sin_p, 256, sp, True)
            nc = sp // CMP_BLOCK
            cmp = nsa_compress(rows.reshape(tp // CMP_BLOCK, CMP_BLOCK * 4 * LANES), pe_tab, w1_blk, w2_blk,
                               min(128, tp // CMP_BLOCK))
            kc, vc = _cmp_layout(cmp, bp, nc, BF16)
            o_a = nsa_attn_prompt(proj, qh, kc, vc, ks, vs, kw, vw, e_mat_p)
            y, h_new = ssd_mixer(proj, jnp.zeros((bp, SUBLANES, SSD_CONV_CH), F32),
                                 jnp.zeros((bp, SSD_INNER, SSD_STATE), F32), *ssd_args, bp, sp, SSD_CHUNK, None)
            xp = outproj([o_a, y], [w_out_a, w_out_b], xp, gate_p, final_norm_w, 512, sp, last)
            outs["kv_p"].append(rows.reshape(bp, sp, 4, NSA_KV_HEADS, HEAD_DIM))
            outs["win_p"].append(win.reshape(bp, sp, 2, NSA_KV_HEADS, HEAD_DIM)[:, -min(WINDOW, sp):])
            outs["cv_p"].append(proj[:, E_XBC:E_XBC + SSD_CONV_CH].reshape(bp, sp, SSD_CONV_CH)[:, -(SSD_CONV - 1):])
            outs["ss_p"].append(h_new.reshape(bp, SSD_HEADS, SSD_HEAD_DIM, SSD_STATE))

            proj = inproj(xs, norm_w[l], scale_s, shift_s, w_in, tm_d, tm_d)
            rows, win, qrot = nsa_prep(proj, cos_s, sin_s, tm_d, DEC_PAD, False)
            pe_tok, w1_bd, w2_bd = _compress_weights_cache(nsa_cmp_pe[e], nsa_cmp_w1[e], nsa_cmp_w2[e])
            cmp_pool = nsa_compress_cache(cache_t, e, pe_tok, w1_bd, w2_bd, 32)
            per_page = PAGE_SIZE // CMP_BLOCK
            cmp_s = page_gather(cmp_pool.reshape(n_pool, per_page * 4 * HEAD_DIM), page_table)
            kc, vc = _cmp_layout(cmp_s.reshape(bs * npg * per_page, 4 * HEAD_DIM), bs, npg * per_page, F32)
            o_a = nsa_attn_decode(page_table, e, qrot.reshape(bs, DEC_PAD, NSA_WIDTH), kc, vc, cache_t,
                                  rows.reshape(bs, DEC_PAD, 4 * LANES), win_t,
                                  win.reshape(bs, DEC_PAD, 2 * LANES), proj.reshape(bs, DEC_PAD, E_END),
                                  e_mat_s, past_len).reshape(td, NSA_WIDTH)
            prev8 = jnp.pad(state_ssd_conv[e], ((0, 0), (SUBLANES - (SSD_CONV - 1), 0), (0, 0)))
            y, h_new = ssd_mixer(proj, prev8, state_ssd[e].reshape(bs, SSD_INNER, SSD_STATE), *ssd_args,
                                 bs, DEC_PAD, DEC_PAD, ss)
            xs = outproj([o_a, y], [w_out_a, w_out_b], xs, gate_s, final_norm_w, tm_d, tm_d, last)
            outs["kv_s"].append(rows.reshape(bs, DEC_PAD, 4, NSA_KV_HEADS, HEAD_DIM)[:, :ss])
            new_win = win.reshape(bs, DEC_PAD, 2, NSA_KV_HEADS, HEAD_DIM)[:, :ss]
            outs["win_s"].append(jnp.concatenate([cache_nsa_win[e], new_win], axis=1)[:, -win_keep:])
            new_xbc = proj[:, E_XBC:E_XBC + SSD_CONV_CH].reshape(bs, DEC_PAD, SSD_CONV_CH)[:, :ss]
            outs["cv_s"].append(jnp.concatenate([state_ssd_conv[e], new_xbc], axis=1)[:, -(SSD_CONV - 1):])
            outs["ss_s"].append(h_new.reshape(bs, SSD_HEADS, SSD_HEAD_DIM, SSD_STATE))
        else:
            w_in = _odd_weight(w_in_odd[e])
            w_out = w_out_odd[e].astype(BF16)
            wg_pad = jnp.pad(gla_w_gate2[e], ((0, LANES - GLA_GATE_RANK), (0, 0))).astype(BF16)
            bg = gla_b_gate[e].reshape(1, GLA_KEY_WIDTH)
            gnw = gla_norm_w[e].reshape(1, GLA_DV)
            srows = GLA_HEADS * GLA_DK
            proj = inproj(xp, norm_w[l], scale_p, shift_p, w_in, 256, sp)
            o, st = gla_mixer(proj, jnp.zeros((bp, srows, GLA_DV), F32), wg_pad, bg, gnw, bp, sp, GLA_CHUNK, 4,
                              2 if bp % 2 == 0 else 1, None)
            xp = outproj([o], [w_out], xp, gate_p, final_norm_w, 512, sp, last)
            outs["gl_p"].append(st.reshape(bp, GLA_HEADS, GLA_DK, GLA_DV))
            proj = inproj(xs, norm_w[l], scale_s, shift_s, w_in, tm_d, tm_d)
            o, st = gla_mixer(proj, state_gla[e].reshape(bs, srows, GLA_DV), wg_pad, bg, gnw, bs, DEC_PAD, DEC_PAD, 1,
                              4 if bs % 4 == 0 else 1, ss)
            xs = outproj([o], [w_out], xs, gate_s, final_norm_w, tm_d, tm_d, last)
            outs["gl_s"].append(st.reshape(bs, GLA_HEADS, GLA_DK, GLA_DV))

    y_prompt = xp.reshape(bp, sp, d)
    y_sample = xs.reshape(bs, DEC_PAD, d)[:, :ss]
    st = {k: jnp.stack(v) for k, v in outs.items()}
    return (y_prompt, y_sample, st["kv_p"], st["kv_s"], st["win_p"], st["win_s"], st["cv_p"], st["cv_s"],
            st["ss_p"], st["ss_s"], st["gl_p"], st["gl_s"])
```

```python
import functools
import math

import jax
import jax.numpy as jnp
import numpy as np
from jax import lax
from jax.experimental import pallas as pl
from jax.experimental.pallas import tpu as pltpu

F32 = jnp.float32
BF16 = jnp.bfloat16

PAGE_SIZE = 128
NSA_HEADS = 8
NSA_KV_HEADS = 2
HEAD_DIM = 64
NSA_WIDTH = NSA_HEADS * HEAD_DIM
NSA_HG = NSA_HEADS // NSA_KV_HEADS
CMP_BLOCK = 32
CMP_HIDDEN = 2 * HEAD_DIM
SEL_BLOCK = 64
TOP_N = 16
WINDOW = 512
FORCE_BONUS = 1.0e4
ROPE_THETA = 10000.0
SSD_HEADS = 16
SSD_HEAD_DIM = 64
SSD_INNER = SSD_HEADS * SSD_HEAD_DIM
SSD_GROUPS = 2
SSD_STATE = 128
SSD_CONV = 4
SSD_CONV_CH = SSD_INNER + 2 * SSD_GROUPS * SSD_STATE
GLA_HEADS = 4
GLA_DK = 128
GLA_DV = 256
GLA_KEY_WIDTH = GLA_HEADS * GLA_DK
GLA_VAL_WIDTH = GLA_HEADS * GLA_DV
GLA_GATE_RANK = 16
GLA_TAU = 16.0
GLA_CHUNK = 32
SSD_CHUNK = 64

LANES = 128
SUBLANES = 8
VMEM_LIMIT = 56 * 1024 * 1024

NEG_BIG = -1.0e30
M_INIT = -1.0e29
DEC_PAD = 8

E_XBC, E_Q, E_ZB, E_ZA, E_KV, E_GD, E_END = 0, 1536, 2048, 3072, 3584, 4352, 4480
GD_GATE, GD_DT = 0, 24
O_Q, O_K, O_V, O_R, O_GLR, O_END = 0, 512, 1024, 2048, 3072, 3200


def _cparams(sem):
    return pltpu.CompilerParams(dimension_semantics=sem, vmem_limit_bytes=VMEM_LIMIT)


def _dot(a, b):
    return jnp.dot(a, b, preferred_element_type=F32)


def _dot_nt(a, b):
    return lax.dot_general(a, b, (((1,), (1,)), ((), ())), preferred_element_type=F32)


def _dot_tn(a, b):
    return lax.dot_general(a, b, (((0,), (0,)), ((), ())), preferred_element_type=F32)


def _split3(x):
    hi = x.astype(BF16)
    r1 = x - hi.astype(F32)
    mid = r1.astype(BF16)
    lo = (r1 - mid.astype(F32)).astype(BF16)
    return hi, mid, lo


def _dot3(x, w01):
    hi, mid, lo = _split3(x)
    return _dot(hi, w01) + _dot(mid, w01) + _dot(lo, w01)


def _dot3_tn(x, w01):
    hi, mid, lo = _split3(x)
    return _dot_tn(hi, w01) + _dot_tn(mid, w01) + _dot_tn(lo, w01)


def _dot3_wx(w01, x):
    hi, mid, lo = _split3(x)
    return _dot(w01, hi) + _dot(w01, mid) + _dot(w01, lo)


def _silu(x):
    return x * (1.0 / (1.0 + jnp.exp(-x)))


def _sigmoid(x):
    return 1.0 / (1.0 + jnp.exp(-x))


def _softplus(x):
    return jnp.maximum(x, 0.0) + jnp.log1p(jnp.exp(-jnp.abs(x)))


def _log_sigmoid(x):
    return jnp.minimum(x, 0.0) - jnp.log1p(jnp.exp(-jnp.abs(x)))


def _mod_kernel(c_ref, w_ref, b_ref, o_ref):
    a = _silu(c_ref[...]).astype(BF16)
    o_ref[0] = _dot(a, w_ref[0]) + b_ref[0]


def mod_all(c_all, w_mod_bf, b_mod, tn=1024):
    nl, d, n = w_mod_bf.shape
    m = c_all.shape[0]
    return pl.pallas_call(
        _mod_kernel,
        grid=(nl, n // tn),
        in_specs=[
            pl.BlockSpec((m, d), lambda l, j: (0, 0)),
            pl.BlockSpec((1, d, tn), lambda l, j: (l, 0, j)),
            pl.BlockSpec((1, 1, tn), lambda l, j: (l, 0, j)),
        ],
        out_specs=pl.BlockSpec((1, m, tn), lambda l, j: (l, 0, j)),
        out_shape=jax.ShapeDtypeStruct((nl, m, n), F32),
        compiler_params=_cparams(("arbitrary", "arbitrary")),
        name="mod_all",
    )(c_all, w_mod_bf, b_mod.reshape(nl, 1, n))


def _inproj_kernel(x_ref, nw_ref, sc_ref, sh_ref, w_ref, o_ref, *, n_chunk):
    x = x_ref[...]
    ms = jnp.mean(x * x, axis=-1, keepdims=True)
    y = x * lax.rsqrt(ms + 1e-6) * nw_ref[...]
    h = (y * (1.0 + sc_ref[0]) + sh_ref[0]).astype(BF16)
    n = o_ref.shape[1]
    for n0 in range(0, n, n_chunk):
        n1 = min(n0 + n_chunk, n)
        o_ref[:, n0:n1] = _dot(h, w_ref[:, n0:n1])


def inproj(x, nw, scale, shift, w_bf, tm, rows_per_mod):
    t, d = x.shape
    n = w_bf.shape[1]
    r = scale.shape[1]
    mod_map = lambda i: ((i * tm) // rows_per_mod, 0, 0)
    return pl.pallas_call(
        functools.partial(_inproj_kernel, n_chunk=640),
        grid=(t // tm,),
        in_specs=[
            pl.BlockSpec((tm, d), lambda i: (i, 0)),
            pl.BlockSpec((1, d), lambda i: (0, 0)),
            pl.BlockSpec((1, r, d), mod_map),
            pl.BlockSpec((1, r, d), mod_map),
            pl.BlockSpec((d, n), lambda i: (0, 0)),
        ],
        out_specs=pl.BlockSpec((tm, n), lambda i: (i, 0)),
        out_shape=jax.ShapeDtypeStruct((t, n), F32),
        compiler_params=_cparams(("arbitrary",)),
        name="inproj",
    )(x, nw.reshape(1, d), scale, shift, w_bf)


def _outproj_kernel(*refs, n_in, final_norm):
    a_refs = refs[:n_in]
    w_refs = refs[n_in:2 * n_in]
    x_ref, g_ref, fw_ref, o_ref = refs[2 * n_in:]
    acc = _dot(a_refs[0][...].astype(BF16), w_refs[0][...])
    for a_ref, w_ref in zip(a_refs[1:], w_refs[1:]):
        acc = acc + _dot(a_ref[...].astype(BF16), w_ref[...])
    y = x_ref[...] + g_ref[0] * acc
    if final_norm:
        ms = jnp.mean(y * y, axis=-1, keepdims=True)
        y = y * lax.rsqrt(ms + 1e-6) * fw_ref[...]
    o_ref[...] = y


def outproj(a_list, w_list, x, gate, final_w, tm, rows_per_mod, final_norm):
    t, d = x.shape
    r = gate.shape[1]
    n_in = len(a_list)
    in_specs = [pl.BlockSpec((tm, a.shape[1]), lambda i: (i, 0)) for a in a_list]
    in_specs += [pl.BlockSpec(w.shape, lambda i: (0, 0)) for w in w_list]
    in_specs += [
        pl.BlockSpec((tm, d), lambda i: (i, 0)),
        pl.BlockSpec((1, r, d), lambda i: ((i * tm) // rows_per_mod, 0, 0)),
        pl.BlockSpec((1, d), lambda i: (0, 0)),
    ]
    return pl.pallas_call(
        functools.partial(_outproj_kernel, n_in=n_in, final_norm=final_norm),
        grid=(t // tm,),
        in_specs=in_specs,
        out_specs=pl.BlockSpec((tm, d), lambda i: (i, 0)),
        out_shape=jax.ShapeDtypeStruct((t, d), F32),
        compiler_params=_cparams(("arbitrary",)),
        name="outproj",
    )(*a_list, *w_list, x, gate, final_w.reshape(1, d))


def _rope_tile(t, cos, sin_signed, lane):
    fwd = pltpu.roll(t, LANES - HEAD_DIM // 2, 1)
    bwd = pltpu.roll(t, HEAD_DIM // 2, 1)
    partner = jnp.where((lane % HEAD_DIM) < HEAD_DIM // 2, fwd, bwd)
    return t * cos + partner * sin_signed


def _nsa_prep_kernel(q_ref, kv01_ref, kv23_ref, kv45_ref, cos_ref, sin_ref,
                     rows_ref, win_ref, qrot_ref, *hm_refs, head_major):
    cos = cos_ref[...]
    sin = sin_ref[...]
    lane = lax.broadcasted_iota(jnp.int32, cos.shape, 1)
    rope = lambda t: _rope_tile(t, cos, sin, lane)
    kv01, kv23, kv45 = kv01_ref[...], kv23_ref[...], kv45_ref[...]
    k_slc = rope(kv23[:, :LANES])
    v_slc = kv23[:, LANES:]
    k_win = rope(kv45[:, :LANES])
    v_win = kv45[:, LANES:]
    rows_ref[:, 0:LANES] = rope(kv01[:, :LANES])
    rows_ref[:, LANES:2 * LANES] = kv01[:, LANES:]
    rows_ref[:, 2 * LANES:3 * LANES] = k_slc
    rows_ref[:, 3 * LANES:4 * LANES] = v_slc
    win_ref[:, 0:LANES] = k_win
    win_ref[:, LANES:2 * LANES] = v_win
    q = q_ref[...]
    qr = [rope(q[:, j * LANES:(j + 1) * LANES]) for j in range(NSA_WIDTH // LANES)]
    for j, t in enumerate(qr):
        qrot_ref[:, j * LANES:(j + 1) * LANES] = t
    if head_major:
        qh_ref, ks_ref, vs_ref, kw_ref, vw_ref = hm_refs
        scale = HEAD_DIM ** -0.5
        for h in range(NSA_HEADS):
            lo = (h % 2) * HEAD_DIM
            qh_ref[0, h] = (qr[h // 2][:, lo:lo + HEAD_DIM] * scale).astype(BF16)
        for g in range(NSA_KV_HEADS):
            lo = g * HEAD_DIM
            ks_ref[0, g] = k_slc[:, lo:lo + HEAD_DIM].astype(BF16)
            vs_ref[0, g] = v_slc[:, lo:lo + HEAD_DIM].astype(BF16)
            kw_ref[0, g] = k_win[:, lo:lo + HEAD_DIM].astype(BF16)
            vw_ref[0, g] = v_win[:, lo:lo + HEAD_DIM].astype(BF16)


def nsa_prep(proj, cos_tab, sin_tab, tm, seq, head_major):
    t = proj.shape[0]
    tab_blocks = cos_tab.shape[0] // tm
    sb = seq // tm
    out_shape = [jax.ShapeDtypeStruct((t, 4 * LANES), F32),
                 jax.ShapeDtypeStruct((t, 2 * LANES), F32),
                 jax.ShapeDtypeStruct((t, NSA_WIDTH), F32)]
    out_specs = [pl.BlockSpec((tm, 4 * LANES), lambda i: (i, 0)),
                 pl.BlockSpec((tm, 2 * LANES), lambda i: (i, 0)),
                 pl.BlockSpec((tm, NSA_WIDTH), lambda i: (i, 0))]
    if head_major:
        nb = t // seq
        hm_map = lambda i: (i // sb, 0, i % sb, 0)
        out_shape.append(jax.ShapeDtypeStruct((nb, NSA_HEADS, seq, HEAD_DIM), BF16))
        out_specs.append(pl.BlockSpec((1, NSA_HEADS, tm, HEAD_DIM), hm_map))
        for _ in range(4):
            out_shape.append(jax.ShapeDtypeStruct((nb, NSA_KV_HEADS, seq, HEAD_DIM), BF16))
            out_specs.append(pl.BlockSpec((1, NSA_KV_HEADS, tm, HEAD_DIM), hm_map))
    kvb = E_KV // (2 * LANES)
    return pl.pallas_call(
        functools.partial(_nsa_prep_kernel, head_major=head_major),
        grid=(t // tm,),
        in_specs=[
            pl.BlockSpec((tm, NSA_WIDTH), lambda i: (i, E_Q // NSA_WIDTH)),
            pl.BlockSpec((tm, 2 * LANES), lambda i: (i, kvb)),
            pl.BlockSpec((tm, 2 * LANES), lambda i: (i, kvb + 1)),
            pl.BlockSpec((tm, 2 * LANES), lambda i: (i, kvb + 2)),
            pl.BlockSpec((tm, LANES), lambda i: (i % tab_blocks, 0)),
            pl.BlockSpec((tm, LANES), lambda i: (i % tab_blocks, 0)),
        ],
        out_specs=out_specs,
        out_shape=out_shape,
        compiler_params=_cparams(("arbitrary",)),
        name="nsa_prep",
    )(proj, proj, proj, proj, cos_tab, sin_tab)


def _nsa_compress_kernel(x_ref, pe_ref, w1_ref, w2_ref, o_ref):
    r = x_ref.shape[0]
    row_w = 4 * LANES
    acc = jnp.zeros((r, 4 * CMP_HIDDEN), F32)
    for t in range(CMP_BLOCK):
        xt = x_ref[:, t * row_w:t * row_w + 2 * LANES] + pe_ref[t:t + 1, :]
        acc = acc + _dot(xt.astype(BF16), w1_ref[t])
    o_ref[...] = _dot(_silu(acc).astype(BF16), w2_ref[...])


def nsa_compress(xblk, pe_tab, w1_blk, w2_blk, r):
    nb, kdim = xblk.shape
    return pl.pallas_call(
        _nsa_compress_kernel,
        grid=(nb // r,),
        in_specs=[
            pl.BlockSpec((r, kdim), lambda i: (i, 0)),
            pl.BlockSpec(pe_tab.shape, lambda i: (0, 0)),
            pl.BlockSpec(w1_blk.shape, lambda i: (0, 0, 0)),
            pl.BlockSpec(w2_blk.shape, lambda i: (0, 0)),
        ],
        out_specs=pl.BlockSpec((r, 4 * HEAD_DIM), lambda i: (i, 0)),
        out_shape=jax.ShapeDtypeStruct((nb, 4 * HEAD_DIM), F32),
        compiler_params=_cparams(("arbitrary",)),
        name="nsa_compress",
    )(xblk, pe_tab, w1_blk, w2_blk)


def _nsa_compress_cache_kernel(x_ref, pe_ref, w1_ref, w2_ref, o_ref, xbuf):
    n_page = x_ref.shape[1]
    gd = NSA_KV_HEADS * HEAD_DIM
    per_page = PAGE_SIZE // CMP_BLOCK

    def to_token_major(p, carry):
        for c in range(2):
            xt = x_ref[0, p, c].reshape(gd, PAGE_SIZE)
            xbuf[c, pl.ds(pl.multiple_of(p * PAGE_SIZE, PAGE_SIZE), PAGE_SIZE), :] = xt.T + pe_ref[c]
        return carry

    lax.fori_loop(0, n_page, to_token_major, 0)
    rows = n_page * per_page
    acc = [jnp.zeros((rows, NSA_KV_HEADS * CMP_HIDDEN), F32) for _ in range(2)]
    for t in range(CMP_BLOCK):
        for c in range(2):
            a = xbuf[c, pl.ds(t, rows, stride=CMP_BLOCK), :].astype(BF16)
            acc[c] = acc[c] + _dot(a, w1_ref[c, t])
    for c in range(2):
        o_ref[:, c * gd:(c + 1) * gd] = _dot(_silu(acc[c]).astype(BF16), w2_ref[c])


def nsa_compress_cache(cache_t, layer, pe_tok, w1_bd, w2_bd, n_page):
    n_pool = cache_t.shape[1]
    per_page = PAGE_SIZE // CMP_BLOCK
    gd = NSA_KV_HEADS * HEAD_DIM
    full = lambda a: pl.BlockSpec(a.shape, lambda i: (0,) * a.ndim)
    return pl.pallas_call(
        _nsa_compress_cache_kernel,
        grid=(n_pool // n_page,),
        in_specs=[
            pl.BlockSpec((1, n_page, 2, NSA_KV_HEADS, HEAD_DIM, PAGE_SIZE), lambda i: (layer, i, 0, 0, 0, 0)),
            full(pe_tok), full(w1_bd), full(w2_bd),
        ],
        out_specs=pl.BlockSpec((n_page * per_page, 2 * gd), lambda i: (i, 0)),
        out_shape=jax.ShapeDtypeStruct((n_pool * per_page, 2 * gd), F32),
        scratch_shapes=[pltpu.VMEM((2, n_page * PAGE_SIZE, gd), F32)],
        compiler_params=_cparams(("arbitrary",)),
        name="nsa_compress_cache",
    )(cache_t, pe_tok, w1_bd, w2_bd)


def _page_gather_kernel(pt_ref, pool_ref, o_ref):
    b = pl.program_id(0)
    for j in range(o_ref.shape[1]):
        o_ref[0, j:j + 1, :] = pool_ref[pl.ds(pt_ref[b, j], 1), :]


def page_gather(pool, page_table):
    nb, npg = page_table.shape
    w = pool.shape[1]
    return pl.pallas_call(
        _page_gather_kernel,
        grid_spec=pltpu.PrefetchScalarGridSpec(
            num_scalar_prefetch=1,
            grid=(nb,),
            in_specs=[pl.BlockSpec(pool.shape, lambda b, pt: (0, 0))],
            out_specs=pl.BlockSpec((1, npg, w), lambda b, pt: (b, 0, 0)),
        ),
        out_shape=jax.ShapeDtypeStruct((nb, npg, w), F32),
        compiler_params=_cparams(("arbitrary",)),
        name="page_gather",
    )(page_table, pool)


def _masked_softmax3(s, mask):
    s = jnp.where(mask[None], s, -jnp.inf)
    m = jnp.max(s, axis=-1, keepdims=True)
    m = jnp.where(m > -jnp.inf, m, 0.0)
    p = jnp.exp(s - m)
    d = jnp.sum(p, axis=-1, keepdims=True)
    return p / jnp.where(d > 0, d, 1.0)


def _drain(gen):
    try:
        while True:
            next(gen)
    except StopIteration as stop:
        return stop.value


def _round_robin(gens):
    results = [None] * len(gens)
    live = list(range(len(gens)))
    while live:
        for i in list(live):
            try:
                next(gens[i])
            except StopIteration as stop:
                results[i] = stop.value
                live.remove(i)
        yield
    return results


def _run_interleaved(gens):
    return _drain(_round_robin(gens))


def _cmp_branch_and_select(*args, **kwargs):
    return _drain(_cmp_branch_and_select_stages(*args, **kwargs))


def _cmp_branch_and_select_stages(q2, kc, vc, qpos, qpos_row, nq, imp_sc, n_blk, pair_t=None):
    nc = kc.shape[0]
    s = _dot_nt(q2, kc).reshape(NSA_HG, nq, nc)
    yield
    col = lax.broadcasted_iota(jnp.int32, (1, nc), 1)
    cblk = col if pair_t is not None else 2 * (col % LANES) + col // LANES
    cmp_end = (cblk + 1) * CMP_BLOCK - 1
    p = _masked_softmax3(s, cmp_end <= qpos)
    yield
    o_c = _dot(p.reshape(NSA_HG * nq, nc).astype(BF16), vc).reshape(NSA_HG, nq, HEAD_DIM)
    imp = jnp.sum(p, axis=0)
    if pair_t is not None:
        hi, mid, lo = _split3(imp)
        imp = _dot_nt(pair_t, hi) + _dot_nt(pair_t, mid) + _dot_nt(pair_t, lo)
    else:
        imp = (imp[:, :LANES] + imp[:, LANES:]).T
    yield
    j = lax.broadcasted_iota(jnp.int32, (LANES, 1), 0)
    cur = qpos_row // SEL_BLOCK
    forced = jnp.where(j == 0, 1, jnp.where(j == cur, 1, jnp.where(j == cur - 1, 1, 0))) > 0
    causal = j <= cur
    imp = jnp.where(forced, imp + FORCE_BONUS, imp)
    imp = jnp.where(causal, imp, -jnp.inf)
    def count_ahead(i, ri, cnt):
        wins_ties = jnp.where(j > i, 1.0, 0.0)
        return cnt + jnp.where(ri >= imp, jnp.where(ri > imp, 1.0, wins_ties), 0.0)

    rank = jnp.zeros((LANES, nq), F32)
    if isinstance(n_blk, int):
        for i in range(n_blk):
            rank = count_ahead(i, imp[i:i + 1, :], rank)
    else:
        imp_sc[...] = imp
        rank = lax.fori_loop(0, n_blk, lambda i, c: count_ahead(i, imp_sc[pl.ds(i, 1), :], c), rank)
    sel = jnp.where(causal, jnp.where(rank < TOP_N, 1.0, 0.0), 0.0)
    return o_c, sel.T.astype(BF16)


def _flash_init(nq):
    return (jnp.full((NSA_HG, nq, 1), M_INIT, F32), jnp.zeros((NSA_HG, nq, 1), F32),
            jnp.zeros((NSA_HG, nq, HEAD_DIM), F32))


def _mask_bias(allowed):
    return jnp.where(allowed, 0.0, NEG_BIG)


def _flash_update(carry, s, bias, v, v_transposed=False):
    return _drain(_flash_update_stages(carry, s, bias, v, v_transposed))


def _flash_update_stages(carry, s, bias, v, v_transposed=False):
    m, l, acc = carry
    s = s + bias[None]
    m_new = jnp.maximum(m, jnp.max(s, axis=-1, keepdims=True))
    alpha = jnp.exp(m - m_new)
    p = jnp.exp(s - m_new)
    l = alpha * l + jnp.sum(p, axis=-1, keepdims=True)
    h, q, n = p.shape
    p2 = p.reshape(h * q, n).astype(v.dtype)
    yield
    pv = (_dot_nt(p2, v) if v_transposed else _dot(p2, v)).reshape(h, q, HEAD_DIM)
    return m_new, l, alpha * acc + pv


def _flash_finish(carry):
    _, l, acc = carry
    return acc / jnp.where(l > 0, l, 1.0)


def _gate_column(sig, lane, col):
    return jnp.sum(jnp.where(lane == col, sig, 0.0), axis=-1, keepdims=True)


def _nsa_attn_kernel(q_ref, kc_ref, vc_ref, ks_ref, vs_ref, kw_ref, vw_ref, gd_ref, za_ref, e_ref,
                     o_ref, s_sc, p_sc, b_sc, m_sc, l_sc, a_sc, acc_sc, imp_sc, *, tq, tk):
    g = pl.program_id(1)
    q0 = pl.program_id(2) * tq
    q2 = q_ref[0].reshape(NSA_HG * tq, HEAD_DIM)
    qpos = q0 + lax.broadcasted_iota(jnp.int32, (tq, 1), 0)
    qpos_row = q0 + lax.broadcasted_iota(jnp.int32, (1, tq), 1)
    n_blk = (q0 + tq - 1) // SEL_BLOCK + 1
    o_c, sel = _cmp_branch_and_select(q2, kc_ref[0, 0], vc_ref[0, 0], qpos, qpos_row, tq, imp_sc, n_blk)
    rb = 2 * SUBLANES

    def reset():
        m_sc[...] = jnp.full(m_sc.shape, M_INIT, F32)
        l_sc[...] = jnp.zeros(l_sc.shape, F32)
        acc_sc[...] = jnp.zeros(acc_sc.shape, F32)

    def softmax_step(k, v, bias, n):
        s_sc[:, :n] = _dot_nt(q2, k)
        b_sc[:, :n] = bias

        def row_block(r, carry):
            r0 = pl.multiple_of(r * rb, rb)
            rows = pl.ds(r0, rb)
            sb = s_sc[rows, :n] + b_sc[pl.ds(pl.multiple_of(lax.rem(r0, tq), rb), rb), :n]
            m_old = m_sc[rows, :]
            m_new = jnp.maximum(m_old, jnp.max(sb, axis=-1, keepdims=True))
            p = jnp.exp(sb - jnp.concatenate([m_new] * (n // LANES), axis=1))
            alpha = jnp.exp(m_old - m_new)
            l_sc[rows, :] = alpha * l_sc[rows, :] + jnp.sum(p, axis=-1, keepdims=True)
            m_sc[rows, :] = m_new
            a_sc[rows, :] = alpha
            p_sc[rows, :n] = p.astype(BF16)
            return carry

        lax.fori_loop(0, NSA_HG * tq // rb, row_block, 0, unroll=True)
        acc_sc[...] = a_sc[:, :HEAD_DIM] * acc_sc[...] + _dot(p_sc[:, :n], v)

    def finish():
        l = l_sc[:, :HEAD_DIM]
        return acc_sc[...] / jnp.where(l > 0, l, 1.0)

    def sel_tile(t, diagonal):
        k0 = pl.multiple_of(t * tk, tk)
        bias = (_dot(sel, e_ref[:, pl.ds(k0, tk)]) - 1.0) * (-NEG_BIG)
        if diagonal:
            kpos = k0 + lax.broadcasted_iota(jnp.int32, (1, tk), 1)
            bias = jnp.where(kpos <= qpos, bias, NEG_BIG)
        softmax_step(ks_ref[0, 0, pl.ds(k0, tk), :], vs_ref[0, 0, pl.ds(k0, tk), :], bias, tk)

    n_below = q0 // tk
    reset()

    def below(t, carry):
        sel_tile(t, False)
        return carry

    lax.fori_loop(0, n_below, below, 0)
    sel_tile(n_below, True)
    o_s = finish()

    wn = WINDOW + tq
    w0 = pl.multiple_of(jnp.maximum(q0 - WINDOW, 0), tq)
    dist = qpos - (w0 + lax.broadcasted_iota(jnp.int32, (1, wn), 1))
    reset()
    softmax_step(kw_ref[0, 0, pl.ds(w0, wn), :], vw_ref[0, 0, pl.ds(w0, wn), :],
                 _mask_bias(jnp.where(dist >= 0, dist, WINDOW) < WINDOW), wn)
    o_w = finish()

    sig = _sigmoid(gd_ref[...])
    lane = lax.broadcasted_iota(jnp.int32, sig.shape, 1)
    outs = []
    for h in range(NSA_HG):
        head = g * NSA_HG + h
        hrows = slice(h * tq, (h + 1) * tq)
        g_c = _gate_column(sig, lane, GD_GATE + head)
        g_s = _gate_column(sig, lane, GD_GATE + NSA_HEADS + head)
        g_w = _gate_column(sig, lane, GD_GATE + 2 * NSA_HEADS + head)
        outs.append(g_c * o_c[h] + g_s * o_s[hrows] + g_w * o_w[hrows])
    o_ref[...] = jnp.concatenate(outs, axis=-1) * _silu(za_ref[...])


def nsa_attn_prompt(proj, qh, kc, vc, ks, vs, kw, vw, e_mat, tq=128, tk=512):
    nb, _, seq, _ = qh.shape
    assert seq >= WINDOW + tq and tk % tq == 0 and tk <= WINDOW + tq
    rows, wn = NSA_HG * tq, WINDOW + tq
    nq = seq // tq
    gw = NSA_HG * HEAD_DIM
    kv_spec = pl.BlockSpec((1, 1, seq, HEAD_DIM), lambda b, g, i: (b, g, 0, 0))
    c_spec = pl.BlockSpec((1, 1, 2 * LANES, HEAD_DIM), lambda b, g, i: (b, g, 0, 0))
    return pl.pallas_call(
        functools.partial(_nsa_attn_kernel, tq=tq, tk=tk),
        grid=(nb, NSA_KV_HEADS, nq),
        in_specs=[
            pl.BlockSpec((1, NSA_HG, tq, HEAD_DIM), lambda b, g, i: (b, g, i, 0)),
            c_spec, c_spec, kv_spec, kv_spec, kv_spec, kv_spec,
            pl.BlockSpec((tq, LANES), lambda b, g, i: (b * nq + i, E_GD // LANES)),
            pl.BlockSpec((tq, gw), lambda b, g, i: (b * nq + i, E_ZA // gw + g)),
            pl.BlockSpec(e_mat.shape, lambda b, g, i: (0, 0)),
        ],
        out_specs=pl.BlockSpec((tq, gw), lambda b, g, i: (b * nq + i, g)),
        out_shape=jax.ShapeDtypeStruct((nb * seq, NSA_WIDTH), F32),
        scratch_shapes=[
            pltpu.VMEM((rows, wn), F32),
            pltpu.VMEM((rows, wn), BF16),
            pltpu.VMEM((tq, wn), F32),
            pltpu.VMEM((rows, LANES), F32),
            pltpu.VMEM((rows, LANES), F32),
            pltpu.VMEM((rows, LANES), F32),
            pltpu.VMEM((rows, HEAD_DIM), F32),
            pltpu.VMEM((LANES, tq), F32),
        ],
        compiler_params=_cparams(("arbitrary", "arbitrary", "arbitrary")),
        name="nsa_attn_prompt",
    )(qh, kc, vc, ks, vs, kw, vw, proj, proj, e_mat)


def _nsa_dec_kernel(pt_ref, q_ref, cmp_ref, *rest, past_len, n_pages):
    del pt_ref
    n_samp = q_ref.shape[0]
    tail = rest[n_samp * n_pages:]
    o_ref = tail[-2]
    outs = _run_interleaved([
        _nsa_dec_sample(si, q_ref, cmp_ref, rest[si * n_pages:(si + 1) * n_pages], *tail, past_len=past_len)
        for si in range(n_samp)])
    for si in range(n_samp):
        o_ref[si] = outs[si]


def _nsa_dec_sample(si, q_ref, cmp_ref, page_refs, rows_ref, pwin_ref, wnew_ref, gd_ref, za_ref, e_ref, pair_ref,
                    o_ref, imp_sc, *, past_len):
    del o_ref
    nq = DEC_PAD
    scale = HEAD_DIM ** -0.5
    qpos = past_len + lax.broadcasted_iota(jnp.int32, (nq, 1), 0)
    qpos_row = past_len + lax.broadcasted_iota(jnp.int32, (1, nq), 1)
    n_blk = (past_len + nq - 1) // SEL_BLOCK + 1
    q = q_ref[si]
    cmp = cmp_ref[si]

    def q_group(g):
        parts = [q[:, (g * NSA_HG + h) * HEAD_DIM:(g * NSA_HG + h + 1) * HEAD_DIM] for h in range(NSA_HG)]
        return jnp.concatenate(parts, axis=0) * scale

    qg = [q_group(g) for g in range(NSA_KV_HEADS)]

    rows = rows_ref[si]
    wnew = wnew_ref[si]
    win_keep = pwin_ref.shape[-1]
    sig = _sigmoid(gd_ref[si])
    lane = lax.broadcasted_iota(jnp.int32, sig.shape, 1)
    kpos = lax.broadcasted_iota(jnp.int32, (1, past_len), 1)
    npos = past_len + lax.broadcasted_iota(jnp.int32, (1, nq), 1)
    wpos = past_len - win_keep + lax.broadcasted_iota(jnp.int32, (1, win_keep), 1)
    outs = []
    for g in range(NSA_KV_HEADS):
        qb = qg[g].astype(BF16)
        kc = cmp[:, g * HEAD_DIM:(g + 1) * HEAD_DIM].astype(BF16)
        vc = cmp[:, LANES + g * HEAD_DIM:LANES + (g + 1) * HEAD_DIM].astype(BF16)
        o_c, sel = yield from _cmp_branch_and_select_stages(
            qb, kc, vc, qpos, qpos_row, nq, imp_sc.at[si * NSA_KV_HEADS + g], n_blk, pair_ref[...])
        yield
        kt = jnp.concatenate([r[0, 0, g] for r in page_refs], axis=1).astype(BF16)
        vt = jnp.concatenate([r[0, 1, g] for r in page_refs], axis=1).astype(BF16)
        s = _dot(qb, kt).reshape(NSA_HG, nq, past_len)
        blk_on = _dot(sel, e_ref[:, :past_len])
        yield
        bias = _mask_bias(jnp.where(kpos <= qpos, blk_on, 0.0) > 0.5)
        carry = yield from _flash_update_stages(_flash_init(nq), s, bias, vt, v_transposed=True)
        yield
        kn = rows[:, 2 * LANES + g * HEAD_DIM:2 * LANES + (g + 1) * HEAD_DIM]
        vn = rows[:, 3 * LANES + g * HEAD_DIM:3 * LANES + (g + 1) * HEAD_DIM]
        s = _dot_nt(qg[g], kn).reshape(NSA_HG, nq, nq)
        blk_on = _dot(sel, e_ref[:, past_len:past_len + LANES])[:, :nq]
        yield
        bias = _mask_bias(jnp.where(npos <= qpos, blk_on, 0.0) > 0.5)
        o_s = _flash_finish((yield from _flash_update_stages(carry, s, bias, vn)))
        yield
        s = _dot(qb, pwin_ref[si, 0, g].astype(BF16)).reshape(NSA_HG, nq, win_keep)
        yield
        dist = qpos - wpos
        bias = _mask_bias(jnp.where(wpos >= 0, jnp.where(dist >= 0, dist, WINDOW), WINDOW) < WINDOW)
        carry = yield from _flash_update_stages(_flash_init(nq), s, bias, pwin_ref[si, 1, g].astype(BF16),
                                                v_transposed=True)
        yield
        kwn = wnew[:, g * HEAD_DIM:(g + 1) * HEAD_DIM]
        vwn = wnew[:, LANES + g * HEAD_DIM:LANES + (g + 1) * HEAD_DIM]
        s = _dot_nt(qg[g], kwn).reshape(NSA_HG, nq, nq)
        yield
        dist = qpos - npos
        bias = _mask_bias(jnp.where(dist >= 0, dist, WINDOW) < WINDOW)
        o_w = _flash_finish((yield from _flash_update_stages(carry, s, bias, vwn)))
        for h in range(NSA_HG):
            head = g * NSA_HG + h
            g_c = _gate_column(sig, lane, GD_GATE + head)
            g_s = _gate_column(sig, lane, GD_GATE + NSA_HEADS + head)
            g_w = _gate_column(sig, lane, GD_GATE + 2 * NSA_HEADS + head)
            outs.append(g_c * o_c[h] + g_s * o_s[h] + g_w * o_w[h])
        yield
    return jnp.concatenate(outs, axis=-1) * _silu(za_ref[si])


def nsa_attn_decode(page_table, layer, qrot, cmp, cache_t, rows, pwin_t, wnew, proj3, e_mat, past_len, n_samp):
    nb, npg = page_table.shape
    win_keep = pwin_t.shape[-1]
    ncb = cmp.shape[1]
    per_b = lambda blk: pl.BlockSpec(blk, lambda b, pt: (b,) + (0,) * (len(blk) - 1))
    page_blk = (None, 1, 2, NSA_KV_HEADS, HEAD_DIM, PAGE_SIZE)
    page_specs = [pl.BlockSpec(page_blk, lambda b, pt, si=si, j=j: (layer, pt[b * n_samp + si, j], 1, 0, 0, 0))
                  for si in range(n_samp) for j in range(npg)]
    pair_t = (jnp.arange(ncb, dtype=jnp.int32)[None, :] // 2
              == jnp.arange(LANES, dtype=jnp.int32)[:, None]).astype(BF16)
    return pl.pallas_call(
        functools.partial(_nsa_dec_kernel, past_len=past_len, n_pages=npg),
        grid_spec=pltpu.PrefetchScalarGridSpec(
            num_scalar_prefetch=1,
            grid=(nb // n_samp,),
            in_specs=[
                per_b((n_samp, DEC_PAD, NSA_WIDTH)),
                per_b((n_samp, ncb, 4 * HEAD_DIM)),
                *page_specs,
                per_b((n_samp, DEC_PAD, 4 * LANES)),
                pl.BlockSpec((None, n_samp, 2, NSA_KV_HEADS, HEAD_DIM, win_keep),
                             lambda b, pt: (layer, b, 0, 0, 0, 0)),
                per_b((n_samp, DEC_PAD, 2 * LANES)),
                pl.BlockSpec((n_samp, DEC_PAD, LANES), lambda b, pt: (b, 0, E_GD // LANES)),
                pl.BlockSpec((n_samp, DEC_PAD, NSA_WIDTH), lambda b, pt: (b, 0, E_ZA // NSA_WIDTH)),
                pl.BlockSpec(e_mat.shape, lambda b, pt: (0, 0)),
                pl.BlockSpec(pair_t.shape, lambda b, pt: (0, 0)),
            ],
            out_specs=per_b((n_samp, DEC_PAD, NSA_WIDTH)),
            scratch_shapes=[pltpu.VMEM((n_samp * NSA_KV_HEADS, LANES, DEC_PAD), F32)],
        ),
        out_shape=jax.ShapeDtypeStruct((nb, DEC_PAD, NSA_WIDTH), F32),
        compiler_params=_cparams(("arbitrary",)),
        name="nsa_attn_decode",
    )(page_table, qrot, cmp, *([cache_t] * (n_samp * npg)), rows, pwin_t, wnew, proj3, proj3, e_mat, pair_t)


def _tri_masks(c):
    row = lax.broadcasted_iota(jnp.int32, (c, c), 0)
    col = lax.broadcasted_iota(jnp.int32, (c, c), 1)
    lower = row >= col
    return lower, jnp.where(lower, 1.0, 0.0).astype(BF16), jnp.where(row <= col, 1.0, 0.0).astype(BF16)


def _ssd_kernel(zb_ref, xbc_ref, gd_ref, prev_ref, h0_ref, cw_ref, cb_ref, dtb_ref, alog_ref, alog16_ref,
                dexp_ref, nw_ref, e16_ref, y_ref, so_ref, h_sc, xbuf, *, chunk, n_valid):
    c = pl.program_id(1)
    n_seq = zb_ref.shape[0]
    hp = SSD_HEADS // SSD_GROUPS * SSD_HEAD_DIM
    halo = SUBLANES

    @pl.when(c == 0)
    def _():
        h_sc[...] = h0_ref[...]
        xbuf[:, 0:halo] = prev_ref[...]

    lower, tri, tri_t = _tri_masks(chunk)
    ones8 = jnp.ones((chunk, SUBLANES), BF16)
    lane = lax.broadcasted_iota(jnp.int32, (chunk, LANES), 1)
    low_half = lane < SSD_HEAD_DIM

    def seq_stages(b):
        xbuf[b, halo:halo + chunk] = xbc_ref[b]
        conv = cb_ref[...]
        for i in range(SSD_CONV):
            lo = halo - (SSD_CONV - 1) + i
            conv = conv + xbuf[b, lo:lo + chunk] * cw_ref[i:i + 1, :]
        xbuf[b, 0:halo] = xbuf[b, chunk:chunk + halo]
        act = _silu(conv)
        xs = act[:, :SSD_INNER]
        bm = act[:, SSD_INNER:SSD_INNER + SSD_GROUPS * SSD_STATE]
        cm = act[:, SSD_INNER + SSD_GROUPS * SSD_STATE:]

        dt16 = _softplus(gd_ref[b] + dtb_ref[...])[:, GD_DT:GD_DT + SSD_HEADS]
        if n_valid is not None:
            tok = c * chunk + lax.broadcasted_iota(jnp.int32, (chunk, 1), 0)
            dt16 = jnp.where(tok < n_valid, dt16, 0.0)
        yield
        dt = _dot3(dt16, e16_ref[...])
        da = dt * (-jnp.exp(alog_ref[...]))
        da16 = dt16 * (-jnp.exp(alog16_ref[...]))
        yield
        cum = _dot3_wx(tri, da)
        yield
        cum16 = _dot3_wx(tri, da16)
        cum16_t = _dot3_tn(da16, tri_t)
        yield
        decay_col = jnp.exp(_dot3_tn(da, ones8)[:, 0:1])
        xdt = xs * dt
        cum_last = cum[chunk - 1:chunk, :]
        xw = (xdt * jnp.exp(cum_last - cum)).astype(BF16)
        xdt_b = xdt.astype(BF16)

        y_intra = [None] * (SSD_INNER // LANES)
        y_inter = []
        for g in range(SSD_GROUPS):
            cg = cm[:, g * SSD_STATE:(g + 1) * SSD_STATE].astype(BF16)
            bg = bm[:, g * SSD_STATE:(g + 1) * SSD_STATE].astype(BF16)
            h_g = h_sc[b, g * hp:(g + 1) * hp, :]
            yield
            cb = _dot_nt(cg, bg)
            y_inter.append(_dot_nt(cg, h_g.astype(BF16)))
            for hh in range(SSD_HEADS // SSD_GROUPS):
                h = g * (SSD_HEADS // SSD_GROUPS) + hh
                diff = cum16[:, h:h + 1] - cum16_t[h:h + 1, :]
                lmat = jnp.where(lower, jnp.exp(jnp.where(lower, diff, 0.0)), 0.0)
                m = (cb * lmat).astype(BF16)
                pair = h // 2
                x_pair = xdt_b[:, pair * LANES:(pair + 1) * LANES]
                keep = low_half if h % 2 == 0 else jnp.logical_not(low_half)
                yield
                contrib = _dot(m, jnp.where(keep, x_pair, jnp.zeros_like(x_pair)))
                y_intra[pair] = contrib if y_intra[pair] is None else y_intra[pair] + contrib
            yield
            h_sc[b, g * hp:(g + 1) * hp, :] = (h_g * decay_col[g * hp:(g + 1) * hp, :]
                                               + _dot_tn(xw[:, g * hp:(g + 1) * hp], bg))
        y = jnp.concatenate(y_intra, axis=-1) + jnp.concatenate(y_inter, axis=-1) * jnp.exp(cum)
        y = y + xs * dexp_ref[...]
        y = y * _silu(zb_ref[b])
        ms = jnp.mean(y * y, axis=-1, keepdims=True)
        y_ref[b] = y * lax.rsqrt(ms + 1e-6) * nw_ref[...]

    _run_interleaved([seq_stages(b) for b in range(n_seq)])

    @pl.when(c == pl.num_programs(1) - 1)
    def _():
        so_ref[...] = h_sc[...]


def ssd_mixer(proj, prev8, h0, layer, conv_w, conv_b, dtb_row, alog_exp, alog16, d_exp, norm_w, e16, nb, seq, chunk,
              n_seq, n_valid):
    full = lambda a: pl.BlockSpec(a.shape, lambda b, c: (0,) * a.ndim)
    proj3 = proj.reshape(nb, seq, E_END)
    col_spec = lambda width, off: pl.BlockSpec((n_seq, chunk, width), lambda b, c: (b, c, off // width))
    st_spec = pl.BlockSpec((n_seq, SSD_INNER, SSD_STATE), lambda b, c: (b, 0, 0))
    y, st = pl.pallas_call(
        functools.partial(_ssd_kernel, chunk=chunk, n_valid=n_valid),
        grid=(nb // n_seq, seq // chunk),
        in_specs=[
            col_spec(SSD_INNER, E_ZB), col_spec(SSD_CONV_CH, E_XBC), col_spec(LANES, E_GD),
            pl.BlockSpec((n_seq, SUBLANES, SSD_CONV_CH), lambda b, c: (b, 0, 0)),
            pl.BlockSpec((None, n_seq, SSD_INNER, SSD_STATE), lambda b, c: (layer, b, 0, 0)),
            full(conv_w), full(conv_b), full(dtb_row), full(alog_exp), full(alog16), full(d_exp), full(norm_w),
            full(e16),
        ],
        out_specs=[pl.BlockSpec((n_seq, chunk, SSD_INNER), lambda b, c: (b, c, 0)), st_spec],
        out_shape=[jax.ShapeDtypeStruct((nb, seq, SSD_INNER), F32),
                   jax.ShapeDtypeStruct((nb, SSD_INNER, SSD_STATE), F32)],
        scratch_shapes=[pltpu.VMEM((n_seq, SSD_INNER, SSD_STATE), F32),
                        pltpu.VMEM((n_seq, SUBLANES + chunk, SSD_CONV_CH), F32)],
        compiler_params=_cparams(("arbitrary", "arbitrary")),
        name="ssd_mixer",
    )(proj3, proj3, proj3, prev8, h0, conv_w, conv_b, dtb_row, alog_exp, alog16, d_exp, norm_w, e16)
    return y.reshape(nb * seq, SSD_INNER), st


def _gla_kernel(q_ref, k_ref, v_ref, r_ref, glr_ref, s0_ref, wg_ref, bg_ref, nw_ref, o_ref, so_ref, s_sc,
                *, chunk, n_sub, n_valid):
    c = pl.program_id(1)
    n_seq = q_ref.shape[0]
    span = n_sub * chunk

    @pl.when(c == 0)
    def _():
        s_sc[...] = s0_ref[...]

    row = lax.broadcasted_iota(jnp.int32, (span, span), 0)
    col = lax.broadcasted_iota(jnp.int32, (span, span), 1)
    lower = jnp.where(row // chunk == col // chunk, row - col, -1) >= 0
    tri = jnp.where(lower, 1.0, 0.0).astype(BF16)
    crow = lax.broadcasted_iota(jnp.int32, (span, n_sub * SUBLANES), 0) // chunk
    ccol = lax.broadcasted_iota(jnp.int32, (span, n_sub * SUBLANES), 1) // SUBLANES
    chunk_ones = jnp.where(crow == ccol, 1.0, 0.0).astype(BF16)
    nw = nw_ref[...]

    def head_stages(b, h, qg, kg, kd, v, r, total):
        kcols = slice(h * GLA_DK, (h + 1) * GLA_DK)
        vcols = slice(h * GLA_DV, (h + 1) * GLA_DV)
        att = jnp.where(lower, _dot_nt(qg[:, kcols], kg[:, kcols]), 0.0)
        yield
        o = _dot(att.astype(BF16), v[:, vcols])
        s_h = s_sc[b, kcols, :]
        carried = []
        for u in range(n_sub):
            rows = slice(u * chunk, (u + 1) * chunk)
            yield
            carried.append(_dot(qg[rows, kcols], s_h.astype(BF16)))
            decay = jnp.exp(total[kcols, u * SUBLANES:u * SUBLANES + 1])
            s_h = s_h * decay + _dot_tn(kd[rows, kcols], v[rows, vcols])
        s_sc[b, kcols, :] = s_h
        yield
        o = o + jnp.concatenate(carried, axis=0)
        ms = jnp.mean(o * o, axis=-1, keepdims=True)
        o = o * lax.rsqrt(ms + 1e-6) * nw
        o_ref[b, :, vcols] = o * _silu(r[:, vcols])

    def seq_stages(b):
        lg = _log_sigmoid(_dot(glr_ref[b].astype(BF16), wg_ref[...]) + bg_ref[...]) * (1.0 / GLA_TAU)
        k = k_ref[b]
        if n_valid is not None:
            tok = c * span + lax.broadcasted_iota(jnp.int32, (span, 1), 0)
            lg = jnp.where(tok < n_valid, lg, 0.0)
            k = jnp.where(tok < n_valid, k, 0.0)
        yield
        bcum = _dot3_wx(tri, lg)
        yield
        total = _dot3_tn(lg, chunk_ones)
        qg = (q_ref[b] * (GLA_DK ** -0.5) * jnp.exp(bcum)).astype(BF16)
        kg = (k * jnp.exp(-bcum)).astype(BF16)
        kd = jnp.concatenate(
            [k[u * chunk:(u + 1) * chunk] * jnp.exp(bcum[(u + 1) * chunk - 1:(u + 1) * chunk]
                                                    - bcum[u * chunk:(u + 1) * chunk]) for u in range(n_sub)],
            axis=0).astype(BF16)
        v = v_ref[b].astype(BF16)
        r = r_ref[b]
        yield
        yield from _round_robin([head_stages(b, h, qg, kg, kd, v, r, total) for h in range(GLA_HEADS)])

    _run_interleaved([seq_stages(b) for b in range(n_seq)])

    @pl.when(c == pl.num_programs(1) - 1)
    def _():
        so_ref[...] = s_sc[...]


def gla_mixer(proj, s0, layer, wg_pad, bg, norm_w, nb, seq, chunk, n_sub, n_seq, n_valid):
    span = n_sub * chunk
    full = lambda a: pl.BlockSpec(a.shape, lambda b, c: (0,) * a.ndim)
    srows = GLA_HEADS * GLA_DK
    proj3 = proj.reshape(nb, seq, O_END)
    col_spec = lambda width, off: pl.BlockSpec((n_seq, span, width), lambda b, c: (b, c, off // width))
    st_spec = pl.BlockSpec((n_seq, srows, GLA_DV), lambda b, c: (b, 0, 0))
    o, st = pl.pallas_call(
        functools.partial(_gla_kernel, chunk=chunk, n_sub=n_sub, n_valid=n_valid),
        grid=(nb // n_seq, seq // span),
        in_specs=[
            col_spec(GLA_KEY_WIDTH, O_Q), col_spec(GLA_KEY_WIDTH, O_K), col_spec(GLA_VAL_WIDTH, O_V),
            col_spec(GLA_VAL_WIDTH, O_R), col_spec(LANES, O_GLR),
            pl.BlockSpec((None, n_seq, srows, GLA_DV), lambda b, c: (layer, b, 0, 0)),
            full(wg_pad), full(bg), full(norm_w),
        ],
        out_specs=[pl.BlockSpec((n_seq, span, GLA_VAL_WIDTH), lambda b, c: (b, c, 0)), st_spec],
        out_shape=[jax.ShapeDtypeStruct((nb, seq, GLA_VAL_WIDTH), F32),
                   jax.ShapeDtypeStruct((nb, srows, GLA_DV), F32)],
        scratch_shapes=[pltpu.VMEM((n_seq, srows, GLA_DV), F32)],
        compiler_params=_cparams(("arbitrary", "arbitrary")),
        name="gla_mixer",
    )(proj3, proj3, proj3, proj3, proj3, s0, wg_pad, bg, norm_w)
    return o.reshape(nb * seq, GLA_VAL_WIDTH), st


def _even_weight(w):
    q, g_a, kv, z_a, z_b, xbc, dt = jnp.split(w, list(np.cumsum(
        [NSA_WIDTH, 3 * NSA_HEADS, 6 * NSA_KV_HEADS * HEAD_DIM, NSA_WIDTH, SSD_INNER, SSD_CONV_CH])), axis=-1)
    pad = jnp.zeros((w.shape[0], E_END - E_GD - 3 * NSA_HEADS - SSD_HEADS), w.dtype)
    return jnp.concatenate([xbc, q, z_b, z_a, kv, g_a, dt, pad], axis=-1).astype(BF16)


def _odd_weight(w):
    q, k, v, glr, r = jnp.split(w, list(np.cumsum(
        [GLA_KEY_WIDTH, GLA_KEY_WIDTH, GLA_VAL_WIDTH, GLA_GATE_RANK])), axis=-1)
    pad = jnp.zeros((w.shape[0], O_END - O_GLR - GLA_GATE_RANK), w.dtype)
    return jnp.concatenate([q, k, v, r, glr, pad], axis=-1).astype(BF16)


def _rope_tables(pos):
    half = HEAD_DIM // 2
    inv = ROPE_THETA ** (-jnp.arange(half, dtype=F32) / half)
    ang = pos.astype(F32)[:, None] * inv[None, :]
    cos, sin = jnp.cos(ang), jnp.sin(ang)
    reps = LANES // HEAD_DIM
    return jnp.tile(jnp.concatenate([cos, cos], -1), (1, reps)), jnp.tile(jnp.concatenate([-sin, sin], -1), (1, reps))


def _compress_weights(pe, w1, w2):
    pe_tab = jnp.concatenate([pe[0], pe[0], pe[1], pe[1]], axis=-1)
    w1r = w1.reshape(2, CMP_BLOCK, HEAD_DIM, CMP_HIDDEN)
    w1_blk = jnp.zeros((CMP_BLOCK, 4 * HEAD_DIM, 4 * CMP_HIDDEN), F32)
    w2_blk = jnp.zeros((4 * CMP_HIDDEN, 4 * HEAD_DIM), F32)
    for part in range(4):
        src = part // 2
        w1_blk = w1_blk.at[:, part * HEAD_DIM:(part + 1) * HEAD_DIM,
                           part * CMP_HIDDEN:(part + 1) * CMP_HIDDEN].set(w1r[src])
        w2_blk = w2_blk.at[part * CMP_HIDDEN:(part + 1) * CMP_HIDDEN,
                           part * HEAD_DIM:(part + 1) * HEAD_DIM].set(w2[src])
    return pe_tab, w1_blk.astype(BF16), w2_blk.astype(BF16)


def _compress_weights_cache(pe, w1, w2):
    w1r = w1.reshape(2, CMP_BLOCK, HEAD_DIM, CMP_HIDDEN)
    w1_bd = jnp.zeros((2, CMP_BLOCK, NSA_KV_HEADS * HEAD_DIM, NSA_KV_HEADS * CMP_HIDDEN), F32)
    w2_bd = jnp.zeros((2, NSA_KV_HEADS * CMP_HIDDEN, NSA_KV_HEADS * HEAD_DIM), F32)
    for g in range(NSA_KV_HEADS):
        w1_bd = w1_bd.at[:, :, g * HEAD_DIM:(g + 1) * HEAD_DIM, g * CMP_HIDDEN:(g + 1) * CMP_HIDDEN].set(w1r)
        w2_bd = w2_bd.at[:, g * CMP_HIDDEN:(g + 1) * CMP_HIDDEN, g * HEAD_DIM:(g + 1) * HEAD_DIM].set(w2)
    pe_tok = jnp.tile(jnp.concatenate([pe] * NSA_KV_HEADS, axis=-1), (1, PAGE_SIZE // CMP_BLOCK, 1))
    return pe_tok, w1_bd.astype(BF16), w2_bd.astype(BF16)


def _cmp_layout(cmp, nb, nc, dtype):
    c = cmp.reshape(nb, nc, 2, NSA_KV_HEADS, HEAD_DIM).transpose(2, 0, 3, 1, 4)
    halves = []
    for par in range(2):
        h = c[:, :, :, par::2]
        halves.append(jnp.pad(h, ((0, 0), (0, 0), (0, 0), (0, LANES - h.shape[3]), (0, 0))))
    c = jnp.concatenate(halves, axis=3).astype(dtype)
    return c[0], c[1]


def _sel_expand(n_keys):
    blk = jnp.arange(n_keys, dtype=jnp.int32) // SEL_BLOCK
    return (blk[None, :] == jnp.arange(LANES, dtype=jnp.int32)[:, None]).astype(BF16)


def kernel(x_prompt, x_sample, c_prompt, c_sample, cache_nsa, cache_nsa_win, state_ssd_conv, state_ssd, state_gla, page_table, norm_w, w_mod, b_mod, w_in_even, w_out_even, nsa_cmp_pe, nsa_cmp_w1, nsa_cmp_w2, ssd_conv_w, ssd_conv_b, ssd_dt_bias, ssd_a_log, ssd_d, ssd_norm_w, w_in_odd, gla_w_gate2, gla_b_gate, gla_norm_w, w_out_odd, final_norm_w):
    bp, sp, d = x_prompt.shape
    bs, ss, _ = x_sample.shape
    depth = norm_w.shape[0]
    n_pool = cache_nsa.shape[1]
    npg = page_table.shape[1]
    past_len = npg * PAGE_SIZE
    win_keep = cache_nsa_win.shape[2]
    assert sp % 512 == 0 and sp // SEL_BLOCK <= LANES and sp // CMP_BLOCK <= 2 * LANES
    assert ss <= DEC_PAD and ss < CMP_BLOCK and past_len % SEL_BLOCK == 0 and ss >= SSD_CONV - 1
    assert past_len // SEL_BLOCK + 1 <= LANES and (bs * DEC_PAD) % 256 == 0 and n_pool % 32 == 0
    tp, td = bp * sp, bs * DEC_PAD
    tm_d = 256

    c_all = jnp.concatenate([c_prompt, c_sample], axis=0)
    c_all = jnp.pad(c_all, ((0, -c_all.shape[0] % SUBLANES), (0, 0)))
    mod = mod_all(c_all, w_mod.astype(BF16), b_mod)

    def mods(l):
        shift, scale, gate = jnp.split(mod[l], 3, axis=-1)
        mp = [m[:bp].reshape(bp, 1, d) for m in (shift, scale, gate)]
        ms = [jnp.repeat(m[bp:bp + bs], DEC_PAD, axis=0).reshape(td // tm_d, tm_d, d) for m in (shift, scale, gate)]
        return mp, ms

    xp = x_prompt.reshape(tp, d)
    xs = jnp.pad(x_sample, ((0, 0), (0, DEC_PAD - ss), (0, 0))).reshape(td, d)

    cos_p, sin_p = _rope_tables(jnp.arange(sp, dtype=jnp.int32))
    cos_s, sin_s = _rope_tables(past_len + jnp.arange(DEC_PAD, dtype=jnp.int32))
    cos_s, sin_s = jnp.tile(cos_s, (tm_d // DEC_PAD, 1)), jnp.tile(sin_s, (tm_d // DEC_PAD, 1))
    cache_t = jnp.transpose(cache_nsa, (0, 1, 3, 4, 5, 2))
    win_t = jnp.transpose(cache_nsa_win, (0, 1, 3, 4, 5, 2))
    ssd_h0_s = state_ssd.reshape(state_ssd.shape[0], bs, SSD_INNER, SSD_STATE)
    gla_s0_s = state_gla.reshape(state_gla.shape[0], bs, GLA_HEADS * GLA_DK, GLA_DV)
    e_mat_p = _sel_expand(sp)
    e_mat_s = _sel_expand(past_len + LANES)
    e16 = (jnp.arange(SSD_INNER, dtype=jnp.int32)[None, :] // SSD_HEAD_DIM
           == jnp.arange(SSD_HEADS, dtype=jnp.int32)[:, None]).astype(BF16)

    outs = {k: [] for k in ("kv_p", "kv_s", "win_p", "win_s", "cv_p", "cv_s", "ss_p", "ss_s", "gl_p", "gl_s")}
    for l in range(depth):
        e = l // 2
        (shift_p, scale_p, gate_p), (shift_s, scale_s, gate_s) = mods(l)
        last = l == depth - 1
        if l % 2 == 0:
            w_in = _even_weight(w_in_even[e])
            w_out = w_out_even[e].astype(BF16)
            w_out_a, w_out_b = w_out[:NSA_WIDTH], w_out[NSA_WIDTH:]
            pe_tab, w1_blk, w2_blk = _compress_weights(nsa_cmp_pe[e], nsa_cmp_w1[e], nsa_cmp_w2[e])
            conv_w, conv_b = ssd_conv_w[e], ssd_conv_b[e].reshape(1, SSD_CONV_CH)
            dtb_row = jnp.zeros((1, LANES), F32).at[0, GD_DT:GD_DT + SSD_HEADS].set(ssd_dt_bias[e])
            alog16 = ssd_a_log[e].reshape(1, SSD_HEADS)
            alog_exp = jnp.repeat(ssd_a_log[e], SSD_HEAD_DIM).reshape(1, SSD_INNER)
            d_exp = jnp.repeat(ssd_d[e], SSD_HEAD_DIM).reshape(1, SSD_INNER)
            ssd_nw = ssd_norm_w[e].reshape(1, SSD_INNER)
            ssd_args = (conv_w, conv_b, dtb_row, alog_exp, alog16, d_exp, ssd_nw, e16)

            proj = inproj(xp, norm_w[l], scale_p, shift_p, w_in, 256, sp)
            rows, win, _, qh, ks, vs, kw, vw = nsa_prep(proj, cos_p, sin_p, 256, sp, True)
            nc = sp // CMP_BLOCK
            cmp = nsa_compress(rows.reshape(tp // CMP_BLOCK, CMP_BLOCK * 4 * LANES), pe_tab, w1_blk, w2_blk,
                               min(128, tp // CMP_BLOCK))
            kc, vc = _cmp_layout(cmp, bp, nc, BF16)
            o_a = nsa_attn_prompt(proj, qh, kc, vc, ks, vs, kw, vw, e_mat_p)
            y, h_new = ssd_mixer(proj, jnp.zeros((bp, SUBLANES, SSD_CONV_CH), F32),
                                 jnp.zeros((1, bp, SSD_INNER, SSD_STATE), F32), 0, *ssd_args, bp, sp, SSD_CHUNK,
                                 2 if bp % 2 == 0 else 1, None)
            xp = outproj([o_a, y], [w_out_a, w_out_b], xp, gate_p, final_norm_w, 512, sp, last)
            outs["kv_p"].append(rows.reshape(bp, sp, 4, NSA_KV_HEADS, HEAD_DIM))
            outs["win_p"].append(win.reshape(bp, sp, 2, NSA_KV_HEADS, HEAD_DIM)[:, -min(WINDOW, sp):])
            outs["cv_p"].append(proj.reshape(bp, sp, E_END)[:, -(SSD_CONV - 1):, E_XBC:E_XBC + SSD_CONV_CH])
            outs["ss_p"].append(h_new.reshape(bp, SSD_HEADS, SSD_HEAD_DIM, SSD_STATE))

            proj = inproj(xs, norm_w[l], scale_s, shift_s, w_in, tm_d, tm_d)
            rows, win, qrot = nsa_prep(proj, cos_s, sin_s, tm_d, DEC_PAD, False)
            pe_tok, w1_bd, w2_bd = _compress_weights_cache(nsa_cmp_pe[e], nsa_cmp_w1[e], nsa_cmp_w2[e])
            cmp_pool = nsa_compress_cache(cache_t, e, pe_tok, w1_bd, w2_bd, 32)
            per_page = PAGE_SIZE // CMP_BLOCK
            cmp_s = page_gather(cmp_pool.reshape(n_pool, per_page * 4 * HEAD_DIM), page_table)
            o_a = nsa_attn_decode(page_table, e, qrot.reshape(bs, DEC_PAD, NSA_WIDTH),
                                  cmp_s.reshape(bs, npg * per_page, 4 * HEAD_DIM), cache_t,
                                  rows.reshape(bs, DEC_PAD, 4 * LANES), win_t,
                                  win.reshape(bs, DEC_PAD, 2 * LANES), proj.reshape(bs, DEC_PAD, E_END),
                                  e_mat_s, past_len, 4 if bs % 4 == 0 else 1).reshape(td, NSA_WIDTH)
            prev8 = jnp.pad(state_ssd_conv[e], ((0, 0), (SUBLANES - (SSD_CONV - 1), 0), (0, 0)))
            y, h_new = ssd_mixer(proj, prev8, ssd_h0_s, e, *ssd_args, bs, DEC_PAD, DEC_PAD,
                                 4 if bs % 4 == 0 else 1, ss)
            xs = outproj([o_a, y], [w_out_a, w_out_b], xs, gate_s, final_norm_w, tm_d, tm_d, last)
            outs["kv_s"].append(rows.reshape(bs, DEC_PAD, 4, NSA_KV_HEADS, HEAD_DIM)[:, :ss])
            new_win = win.reshape(bs, DEC_PAD, 2, NSA_KV_HEADS, HEAD_DIM)[:, :ss]
            outs["win_s"].append(jnp.concatenate([cache_nsa_win[e], new_win], axis=1)[:, -win_keep:])
            new_xbc = proj.reshape(bs, DEC_PAD, E_END)[:, :ss, E_XBC:E_XBC + SSD_CONV_CH]
            outs["cv_s"].append(jnp.concatenate([state_ssd_conv[e], new_xbc], axis=1)[:, -(SSD_CONV - 1):])
            outs["ss_s"].append(h_new.reshape(bs, SSD_HEADS, SSD_HEAD_DIM, SSD_STATE))
        else:
            w_in = _odd_weight(w_in_odd[e])
            w_out = w_out_odd[e].astype(BF16)
            wg_pad = jnp.pad(gla_w_gate2[e], ((0, LANES - GLA_GATE_RANK), (0, 0))).astype(BF16)
            bg = gla_b_gate[e].reshape(1, GLA_KEY_WIDTH)
            gnw = gla_norm_w[e].reshape(1, GLA_DV)
            srows = GLA_HEADS * GLA_DK
            proj = inproj(xp, norm_w[l], scale_p, shift_p, w_in, 256, sp)
            o, st = gla_mixer(proj, jnp.zeros((1, bp, srows, GLA_DV), F32), 0, wg_pad, bg, gnw, bp, sp, GLA_CHUNK, 4,
                              2 if bp % 2 == 0 else 1, None)
            xp = outproj([o], [w_out], xp, gate_p, final_norm_w, 512, sp, last)
            outs["gl_p"].append(st.reshape(bp, GLA_HEADS, GLA_DK, GLA_DV))
            proj = inproj(xs, norm_w[l], scale_s, shift_s, w_in, tm_d, tm_d)
            o, st = gla_mixer(proj, gla_s0_s, e, wg_pad, bg, gnw, bs, DEC_PAD, DEC_PAD, 1,
                              4 if bs % 4 == 0 else 1, ss)
            xs = outproj([o], [w_out], xs, gate_s, final_norm_w, tm_d, tm_d, last)
            outs["gl_s"].append(st.reshape(bs, GLA_HEADS, GLA_DK, GLA_DV))

    y_prompt = xp.reshape(bp, sp, d)
    y_sample = xs.reshape(bs, DEC_PAD, d)[:, :ss]
    st = {k: jnp.stack(v) for k, v in outs.items()}
    return (y_prompt, y_sample, st["kv_p"], st["kv_s"], st["win_p"], st["win_s"], st["cv_p"], st["cv_s"],
            st["ss_p"], st["ss_s"], st["gl_p"], st["gl_s"])
```

```python
import functools
import math

import jax
import jax.numpy as jnp
import numpy as np
from jax import lax
from jax.experimental import pallas as pl
from jax.experimental.pallas import tpu as pltpu

F32 = jnp.float32
BF16 = jnp.bfloat16

PAGE_SIZE = 128
NSA_HEADS = 8
NSA_KV_HEADS = 2
HEAD_DIM = 64
NSA_WIDTH = NSA_HEADS * HEAD_DIM
NSA_HG = NSA_HEADS // NSA_KV_HEADS
CMP_BLOCK = 32
CMP_HIDDEN = 2 * HEAD_DIM
SEL_BLOCK = 64
TOP_N = 16
WINDOW = 512
FORCE_BONUS = 1.0e4
ROPE_THETA = 10000.0
SSD_HEADS = 16
SSD_HEAD_DIM = 64
SSD_INNER = SSD_HEADS * SSD_HEAD_DIM
SSD_GROUPS = 2
SSD_STATE = 128
SSD_CONV = 4
SSD_CONV_CH = SSD_INNER + 2 * SSD_GROUPS * SSD_STATE
GLA_HEADS = 4
GLA_DK = 128
GLA_DV = 256
GLA_KEY_WIDTH = GLA_HEADS * GLA_DK
GLA_VAL_WIDTH = GLA_HEADS * GLA_DV
GLA_GATE_RANK = 16
GLA_TAU = 16.0
GLA_CHUNK = 32
SSD_CHUNK = 64

LANES = 128
SUBLANES = 8
VMEM_LIMIT = 56 * 1024 * 1024

NEG_BIG = -1.0e30
M_INIT = -1.0e29
DEC_PAD = 8

E_XBC, E_Q, E_ZB, E_ZA, E_KV, E_GD, E_END = 0, 1536, 2048, 3072, 3584, 4352, 4480
GD_GATE, GD_DT = 0, 24
O_Q, O_K, O_V, O_R, O_GLR, O_END = 0, 512, 1024, 2048, 3072, 3200


def _cparams(sem):
    return pltpu.CompilerParams(dimension_semantics=sem, vmem_limit_bytes=VMEM_LIMIT)


def _dot(a, b):
    return jnp.dot(a, b, preferred_element_type=F32)


def _dot_nt(a, b):
    return lax.dot_general(a, b, (((1,), (1,)), ((), ())), preferred_element_type=F32)


def _dot_tn(a, b):
    return lax.dot_general(a, b, (((0,), (0,)), ((), ())), preferred_element_type=F32)


def _split3(x):
    hi = x.astype(BF16)
    r1 = x - hi.astype(F32)
    mid = r1.astype(BF16)
    lo = (r1 - mid.astype(F32)).astype(BF16)
    return hi, mid, lo


def _dot3(x, w01):
    hi, mid, lo = _split3(x)
    return _dot(hi, w01) + _dot(mid, w01) + _dot(lo, w01)


def _dot3_tn(x, w01):
    hi, mid, lo = _split3(x)
    return _dot_tn(hi, w01) + _dot_tn(mid, w01) + _dot_tn(lo, w01)


def _dot3_wx(w01, x):
    hi, mid, lo = _split3(x)
    return _dot(w01, hi) + _dot(w01, mid) + _dot(w01, lo)


def _silu(x):
    return x * (1.0 / (1.0 + jnp.exp(-x)))


def _sigmoid(x):
    return 1.0 / (1.0 + jnp.exp(-x))


def _softplus(x):
    return jnp.maximum(x, 0.0) + jnp.log1p(jnp.exp(-jnp.abs(x)))


def _log_sigmoid(x):
    return jnp.minimum(x, 0.0) - jnp.log1p(jnp.exp(-jnp.abs(x)))


def _mod_kernel(c_ref, w_ref, b_ref, o_ref):
    a = _silu(c_ref[...]).astype(BF16)
    o_ref[0] = _dot(a, w_ref[0]) + b_ref[0]


def mod_all(c_all, w_mod_bf, b_mod, tn=1024):
    nl, d, n = w_mod_bf.shape
    m = c_all.shape[0]
    return pl.pallas_call(
        _mod_kernel,
        grid=(nl, n // tn),
        in_specs=[
            pl.BlockSpec((m, d), lambda l, j: (0, 0)),
            pl.BlockSpec((1, d, tn), lambda l, j: (l, 0, j)),
            pl.BlockSpec((1, 1, tn), lambda l, j: (l, 0, j)),
        ],
        out_specs=pl.BlockSpec((1, m, tn), lambda l, j: (l, 0, j)),
        out_shape=jax.ShapeDtypeStruct((nl, m, n), F32),
        compiler_params=_cparams(("arbitrary", "arbitrary")),
        name="mod_all",
    )(c_all, w_mod_bf, b_mod.reshape(nl, 1, n))


def _inproj_kernel(x_ref, nw_ref, sc_ref, sh_ref, w_ref, o_ref, *, n_chunk):
    x = x_ref[...]
    ms = jnp.mean(x * x, axis=-1, keepdims=True)
    y = x * lax.rsqrt(ms + 1e-6) * nw_ref[...]
    h = (y * (1.0 + sc_ref[0]) + sh_ref[0]).astype(BF16)
    n = o_ref.shape[1]
    for n0 in range(0, n, n_chunk):
        n1 = min(n0 + n_chunk, n)
        o_ref[:, n0:n1] = _dot(h, w_ref[:, n0:n1])


def inproj(x, nw, scale, shift, w_bf, tm, rows_per_mod):
    t, d = x.shape
    n = w_bf.shape[1]
    r = scale.shape[1]
    mod_map = lambda i: ((i * tm) // rows_per_mod, 0, 0)
    return pl.pallas_call(
        functools.partial(_inproj_kernel, n_chunk=640),
        grid=(t // tm,),
        in_specs=[
            pl.BlockSpec((tm, d), lambda i: (i, 0)),
            pl.BlockSpec((1, d), lambda i: (0, 0)),
            pl.BlockSpec((1, r, d), mod_map),
            pl.BlockSpec((1, r, d), mod_map),
            pl.BlockSpec((d, n), lambda i: (0, 0)),
        ],
        out_specs=pl.BlockSpec((tm, n), lambda i: (i, 0)),
        out_shape=jax.ShapeDtypeStruct((t, n), F32),
        compiler_params=_cparams(("arbitrary",)),
        name="inproj",
    )(x, nw.reshape(1, d), scale, shift, w_bf)


def _outproj_kernel(*refs, n_in, final_norm):
    a_refs = refs[:n_in]
    w_refs = refs[n_in:2 * n_in]
    x_ref, g_ref, fw_ref, o_ref = refs[2 * n_in:]
    acc = _dot(a_refs[0][...].astype(BF16), w_refs[0][...])
    for a_ref, w_ref in zip(a_refs[1:], w_refs[1:]):
        acc = acc + _dot(a_ref[...].astype(BF16), w_ref[...])
    y = x_ref[...] + g_ref[0] * acc
    if final_norm:
        ms = jnp.mean(y * y, axis=-1, keepdims=True)
        y = y * lax.rsqrt(ms + 1e-6) * fw_ref[...]
    o_ref[...] = y


def outproj(a_list, w_list, x, gate, final_w, tm, rows_per_mod, final_norm):
    t, d = x.shape
    r = gate.shape[1]
    n_in = len(a_list)
    in_specs = [pl.BlockSpec((tm, a.shape[1]), lambda i: (i, 0)) for a in a_list]
    in_specs += [pl.BlockSpec(w.shape, lambda i: (0, 0)) for w in w_list]
    in_specs += [
        pl.BlockSpec((tm, d), lambda i: (i, 0)),
        pl.BlockSpec((1, r, d), lambda i: ((i * tm) // rows_per_mod, 0, 0)),
        pl.BlockSpec((1, d), lambda i: (0, 0)),
    ]
    return pl.pallas_call(
        functools.partial(_outproj_kernel, n_in=n_in, final_norm=final_norm),
        grid=(t // tm,),
        in_specs=in_specs,
        out_specs=pl.BlockSpec((tm, d), lambda i: (i, 0)),
        out_shape=jax.ShapeDtypeStruct((t, d), F32),
        compiler_params=_cparams(("arbitrary",)),
        name="outproj",
    )(*a_list, *w_list, x, gate, final_w.reshape(1, d))


def _rope_tile(t, cos, sin_signed, lane):
    fwd = pltpu.roll(t, LANES - HEAD_DIM // 2, 1)
    bwd = pltpu.roll(t, HEAD_DIM // 2, 1)
    partner = jnp.where((lane % HEAD_DIM) < HEAD_DIM // 2, fwd, bwd)
    return t * cos + partner * sin_signed


def _nsa_prep_kernel(q_ref, kv01_ref, kv23_ref, kv45_ref, cos_ref, sin_ref,
                     rows_ref, win_ref, qrot_ref, *hm_refs, head_major):
    cos = cos_ref[...]
    sin = sin_ref[...]
    lane = lax.broadcasted_iota(jnp.int32, cos.shape, 1)
    rope = lambda t: _rope_tile(t, cos, sin, lane)
    kv01, kv23, kv45 = kv01_ref[...], kv23_ref[...], kv45_ref[...]
    k_slc = rope(kv23[:, :LANES])
    v_slc = kv23[:, LANES:]
    k_win = rope(kv45[:, :LANES])
    v_win = kv45[:, LANES:]
    rows_ref[:, 0:LANES] = rope(kv01[:, :LANES])
    rows_ref[:, LANES:2 * LANES] = kv01[:, LANES:]
    rows_ref[:, 2 * LANES:3 * LANES] = k_slc
    rows_ref[:, 3 * LANES:4 * LANES] = v_slc
    win_ref[:, 0:LANES] = k_win
    win_ref[:, LANES:2 * LANES] = v_win
    q = q_ref[...]
    qr = [rope(q[:, j * LANES:(j + 1) * LANES]) for j in range(NSA_WIDTH // LANES)]
    for j, t in enumerate(qr):
        qrot_ref[:, j * LANES:(j + 1) * LANES] = t
    if head_major:
        qh_ref, ks_ref, vs_ref, kw_ref, vw_ref = hm_refs
        scale = HEAD_DIM ** -0.5
        for h in range(NSA_HEADS):
            lo = (h % 2) * HEAD_DIM
            qh_ref[0, h] = (qr[h // 2][:, lo:lo + HEAD_DIM] * scale).astype(BF16)
        for g in range(NSA_KV_HEADS):
            lo = g * HEAD_DIM
            ks_ref[0, g] = k_slc[:, lo:lo + HEAD_DIM].astype(BF16)
            vs_ref[0, g] = v_slc[:, lo:lo + HEAD_DIM].astype(BF16)
            kw_ref[0, g] = k_win[:, lo:lo + HEAD_DIM].astype(BF16)
            vw_ref[0, g] = v_win[:, lo:lo + HEAD_DIM].astype(BF16)


def nsa_prep(proj, cos_tab, sin_tab, tm, seq, head_major):
    t = proj.shape[0]
    tab_blocks = cos_tab.shape[0] // tm
    sb = seq // tm
    out_shape = [jax.ShapeDtypeStruct((t, 4 * LANES), F32),
                 jax.ShapeDtypeStruct((t, 2 * LANES), F32),
                 jax.ShapeDtypeStruct((t, NSA_WIDTH), F32)]
    out_specs = [pl.BlockSpec((tm, 4 * LANES), lambda i: (i, 0)),
                 pl.BlockSpec((tm, 2 * LANES), lambda i: (i, 0)),
                 pl.BlockSpec((tm, NSA_WIDTH), lambda i: (i, 0))]
    if head_major:
        nb = t // seq
        hm_map = lambda i: (i // sb, 0, i % sb, 0)
        out_shape.append(jax.ShapeDtypeStruct((nb, NSA_HEADS, seq, HEAD_DIM), BF16))
        out_specs.append(pl.BlockSpec((1, NSA_HEADS, tm, HEAD_DIM), hm_map))
        for _ in range(4):
            out_shape.append(jax.ShapeDtypeStruct((nb, NSA_KV_HEADS, seq, HEAD_DIM), BF16))
            out_specs.append(pl.BlockSpec((1, NSA_KV_HEADS, tm, HEAD_DIM), hm_map))
    kvb = E_KV // (2 * LANES)
    return pl.pallas_call(
        functools.partial(_nsa_prep_kernel, head_major=head_major),
        grid=(t // tm,),
        in_specs=[
            pl.BlockSpec((tm, NSA_WIDTH), lambda i: (i, E_Q // NSA_WIDTH)),
            pl.BlockSpec((tm, 2 * LANES), lambda i: (i, kvb)),
            pl.BlockSpec((tm, 2 * LANES), lambda i: (i, kvb + 1)),
            pl.BlockSpec((tm, 2 * LANES), lambda i: (i, kvb + 2)),
            pl.BlockSpec((tm, LANES), lambda i: (i % tab_blocks, 0)),
            pl.BlockSpec((tm, LANES), lambda i: (i % tab_blocks, 0)),
        ],
        out_specs=out_specs,
        out_shape=out_shape,
        compiler_params=_cparams(("arbitrary",)),
        name="nsa_prep",
    )(proj, proj, proj, proj, cos_tab, sin_tab)


def _nsa_compress_kernel(x_ref, pe_ref, w1_ref, w2_ref, o_ref):
    r = x_ref.shape[0]
    row_w = 4 * LANES
    acc = jnp.zeros((r, 4 * CMP_HIDDEN), F32)
    for t in range(CMP_BLOCK):
        xt = x_ref[:, t * row_w:t * row_w + 2 * LANES] + pe_ref[t:t + 1, :]
        acc = acc + _dot(xt.astype(BF16), w1_ref[t])
    o_ref[...] = _dot(_silu(acc).astype(BF16), w2_ref[...])


def nsa_compress(xblk, pe_tab, w1_blk, w2_blk, r):
    nb, kdim = xblk.shape
    return pl.pallas_call(
        _nsa_compress_kernel,
        grid=(nb // r,),
        in_specs=[
            pl.BlockSpec((r, kdim), lambda i: (i, 0)),
            pl.BlockSpec(pe_tab.shape, lambda i: (0, 0)),
            pl.BlockSpec(w1_blk.shape, lambda i: (0, 0, 0)),
            pl.BlockSpec(w2_blk.shape, lambda i: (0, 0)),
        ],
        out_specs=pl.BlockSpec((r, 4 * HEAD_DIM), lambda i: (i, 0)),
        out_shape=jax.ShapeDtypeStruct((nb, 4 * HEAD_DIM), F32),
        compiler_params=_cparams(("arbitrary",)),
        name="nsa_compress",
    )(xblk, pe_tab, w1_blk, w2_blk)


def _nsa_compress_cache_kernel(pt_ref, *refs, n_page):
    del pt_ref
    page_refs = refs[:n_page]
    pe_ref, w1_ref, w2_ref, o_ref, xbuf = refs[n_page:]
    gd = NSA_KV_HEADS * HEAD_DIM
    per_page = PAGE_SIZE // CMP_BLOCK
    for p, x_ref in enumerate(page_refs):
        for c in range(2):
            xbuf[c, p * PAGE_SIZE:(p + 1) * PAGE_SIZE, :] = x_ref[0, c].reshape(gd, PAGE_SIZE).T + pe_ref[c]
    rows = n_page * per_page
    acc = [jnp.zeros((rows, NSA_KV_HEADS * CMP_HIDDEN), F32) for _ in range(2)]
    for t in range(CMP_BLOCK):
        for c in range(2):
            a = xbuf[c, pl.ds(t, rows, stride=CMP_BLOCK), :].astype(BF16)
            acc[c] = acc[c] + _dot(a, w1_ref[c, t])
    for c in range(2):
        o_ref[:, c * gd:(c + 1) * gd] = _dot(_silu(acc[c]).astype(BF16), w2_ref[c])


def nsa_compress_cache(cache_t, layer, pages, pe_tok, w1_bd, w2_bd, n_page):
    per_page = PAGE_SIZE // CMP_BLOCK
    gd = NSA_KV_HEADS * HEAD_DIM
    full = lambda a: pl.BlockSpec(a.shape, lambda i, pt: (0,) * a.ndim)
    page_blk = (None, 1, 2, NSA_KV_HEADS, HEAD_DIM, PAGE_SIZE)
    page_specs = [pl.BlockSpec(page_blk, lambda i, pt, j=j: (layer, pt[i * n_page + j], 0, 0, 0, 0))
                  for j in range(n_page)]
    return pl.pallas_call(
        functools.partial(_nsa_compress_cache_kernel, n_page=n_page),
        grid_spec=pltpu.PrefetchScalarGridSpec(
            num_scalar_prefetch=1,
            grid=(pages.shape[0] // n_page,),
            in_specs=[*page_specs, full(pe_tok), full(w1_bd), full(w2_bd)],
            out_specs=pl.BlockSpec((n_page * per_page, 2 * gd), lambda i, pt: (i, 0)),
            scratch_shapes=[pltpu.VMEM((2, n_page * PAGE_SIZE, gd), F32)],
        ),
        out_shape=jax.ShapeDtypeStruct((pages.shape[0] * per_page, 2 * gd), F32),
        compiler_params=_cparams(("arbitrary",)),
        name="nsa_compress_cache",
    )(pages, *([cache_t] * n_page), pe_tok, w1_bd, w2_bd)


def _masked_softmax3(s, mask):
    s = jnp.where(mask[None], s, -jnp.inf)
    m = jnp.max(s, axis=-1, keepdims=True)
    m = jnp.where(m > -jnp.inf, m, 0.0)
    p = jnp.exp(s - m)
    d = jnp.sum(p, axis=-1, keepdims=True)
    return p / jnp.where(d > 0, d, 1.0)


def _drain(gen):
    try:
        while True:
            next(gen)
    except StopIteration as stop:
        return stop.value


def _round_robin(gens):
    results = [None] * len(gens)
    live = list(range(len(gens)))
    while live:
        for i in list(live):
            try:
                next(gens[i])
            except StopIteration as stop:
                results[i] = stop.value
                live.remove(i)
        yield
    return results


def _run_interleaved(gens):
    return _drain(_round_robin(gens))


def _cmp_branch_and_select(*args, **kwargs):
    return _drain(_cmp_branch_and_select_stages(*args, **kwargs))


def _cmp_branch_and_select_stages(q2, kc, vc, qpos, qpos_row, nq, imp_sc, n_blk, pair_t=None):
    nc = kc.shape[0]
    s = _dot_nt(q2, kc).reshape(NSA_HG, nq, nc)
    yield
    col = lax.broadcasted_iota(jnp.int32, (1, nc), 1)
    cblk = col if pair_t is not None else 2 * (col % LANES) + col // LANES
    cmp_end = (cblk + 1) * CMP_BLOCK - 1
    p = _masked_softmax3(s, cmp_end <= qpos)
    yield
    o_c = _dot(p.reshape(NSA_HG * nq, nc).astype(BF16), vc).reshape(NSA_HG, nq, HEAD_DIM)
    imp = jnp.sum(p, axis=0)
    if pair_t is not None:
        hi, mid, lo = _split3(imp)
        imp = _dot_nt(pair_t, hi) + _dot_nt(pair_t, mid) + _dot_nt(pair_t, lo)
    else:
        imp = (imp[:, :LANES] + imp[:, LANES:]).T
    yield
    j = lax.broadcasted_iota(jnp.int32, (LANES, 1), 0)
    cur = qpos_row // SEL_BLOCK
    forced = jnp.where(j == 0, 1, jnp.where(j == cur, 1, jnp.where(j == cur - 1, 1, 0))) > 0
    causal = j <= cur
    imp = jnp.where(forced, imp + FORCE_BONUS, imp)
    imp = jnp.where(causal, imp, -jnp.inf)
    def count_ahead(i, ri, cnt):
        wins_ties = jnp.where(j > i, 1.0, 0.0)
        return cnt + jnp.where(ri >= imp, jnp.where(ri > imp, 1.0, wins_ties), 0.0)

    rank = jnp.zeros((LANES, nq), F32)
    if isinstance(n_blk, int):
        for i in range(n_blk):
            rank = count_ahead(i, imp[i:i + 1, :], rank)
    else:
        imp_sc[...] = imp
        rank = lax.fori_loop(0, n_blk, lambda i, c: count_ahead(i, imp_sc[pl.ds(i, 1), :], c), rank)
    sel = jnp.where(causal, jnp.where(rank < TOP_N, 1.0, 0.0), 0.0)
    return o_c, sel.T.astype(BF16)


def _flash_init(nq):
    return (jnp.full((NSA_HG, nq, 1), M_INIT, F32), jnp.zeros((NSA_HG, nq, 1), F32),
            jnp.zeros((NSA_HG, nq, HEAD_DIM), F32))


def _mask_bias(allowed):
    return jnp.where(allowed, 0.0, NEG_BIG)


def _flash_update(carry, s, bias, v, v_transposed=False):
    return _drain(_flash_update_stages(carry, s, bias, v, v_transposed))


def _flash_update_stages(carry, s, bias, v, v_transposed=False):
    m, l, acc = carry
    s = s + bias[None]
    m_new = jnp.maximum(m, jnp.max(s, axis=-1, keepdims=True))
    alpha = jnp.exp(m - m_new)
    p = jnp.exp(s - m_new)
    l = alpha * l + jnp.sum(p, axis=-1, keepdims=True)
    h, q, n = p.shape
    p2 = p.reshape(h * q, n).astype(v.dtype)
    yield
    pv = (_dot_nt(p2, v) if v_transposed else _dot(p2, v)).reshape(h, q, HEAD_DIM)
    return m_new, l, alpha * acc + pv


def _flash_finish(carry):
    _, l, acc = carry
    return acc / jnp.where(l > 0, l, 1.0)


def _gate_column(sig, lane, col):
    return jnp.sum(jnp.where(lane == col, sig, 0.0), axis=-1, keepdims=True)


def _softmax_rows(s_ref, b_ref, p_ref, m_ref, l_ref, a_ref, n, tq):
    rb = 2 * SUBLANES
    for r0 in range(0, s_ref.shape[0], rb):
        rows = slice(r0, r0 + rb)
        sb = s_ref[rows, :n] + b_ref[r0 % tq:r0 % tq + rb, :n]
        m_old = m_ref[rows, :]
        m_new = jnp.maximum(m_old, jnp.max(sb, axis=-1, keepdims=True))
        p = jnp.exp(sb - jnp.concatenate([m_new] * (n // LANES), axis=1))
        alpha = jnp.exp(m_old - m_new)
        l_ref[rows, :] = alpha * l_ref[rows, :] + jnp.sum(p, axis=-1, keepdims=True)
        m_ref[rows, :] = m_new
        a_ref[rows, :] = alpha
        p_ref[rows, :n] = p.astype(BF16)


def _nsa_attn_kernel(q_ref, kc_ref, vc_ref, ks_ref, vs_ref, kw_ref, vw_ref, gd_ref, za_ref, e_ref,
                     o_ref, s_sc, p_sc, b_sc, m_sc, l_sc, a_sc, acc_sc, imp_sc, *, tq, tk):
    g = pl.program_id(1)
    q0 = pl.program_id(2) * tq
    q2 = q_ref[0].reshape(NSA_HG * tq, HEAD_DIM)
    qpos = q0 + lax.broadcasted_iota(jnp.int32, (tq, 1), 0)
    qpos_row = q0 + lax.broadcasted_iota(jnp.int32, (1, tq), 1)
    n_blk = (q0 + tq - 1) // SEL_BLOCK + 1
    o_c, sel = _cmp_branch_and_select(q2, kc_ref[0, 0], vc_ref[0, 0], qpos, qpos_row, tq, imp_sc, n_blk)

    def reset():
        m_sc[...] = jnp.full(m_sc.shape, M_INIT, F32)
        l_sc[...] = jnp.zeros(l_sc.shape, F32)
        acc_sc[...] = jnp.zeros(acc_sc.shape, F32)

    def softmax_step(k, v, bias, n):
        s_sc[:, :n] = _dot_nt(q2, k)
        b_sc[:, :n] = bias
        _softmax_rows(s_sc, b_sc, p_sc, m_sc, l_sc, a_sc, n, tq)
        acc_sc[...] = a_sc[:, :HEAD_DIM] * acc_sc[...] + _dot(p_sc[:, :n], v)

    def finish():
        l = l_sc[:, :HEAD_DIM]
        return acc_sc[...] / jnp.where(l > 0, l, 1.0)

    def sel_tile(t, diagonal):
        k0 = pl.multiple_of(t * tk, tk)
        bias = (_dot(sel, e_ref[:, pl.ds(k0, tk)]) - 1.0) * (-NEG_BIG)
        if diagonal:
            kpos = k0 + lax.broadcasted_iota(jnp.int32, (1, tk), 1)
            bias = jnp.where(kpos <= qpos, bias, NEG_BIG)
        softmax_step(ks_ref[0, 0, pl.ds(k0, tk), :], vs_ref[0, 0, pl.ds(k0, tk), :], bias, tk)

    n_below = q0 // tk
    reset()

    def below(t, carry):
        sel_tile(t, False)
        return carry

    lax.fori_loop(0, n_below, below, 0)
    sel_tile(n_below, True)
    o_s = finish()

    wn = WINDOW + tq
    w0 = pl.multiple_of(jnp.maximum(q0 - WINDOW, 0), tq)
    dist = qpos - (w0 + lax.broadcasted_iota(jnp.int32, (1, wn), 1))
    reset()
    softmax_step(kw_ref[0, 0, pl.ds(w0, wn), :], vw_ref[0, 0, pl.ds(w0, wn), :],
                 _mask_bias(jnp.where(dist >= 0, dist, WINDOW) < WINDOW), wn)
    o_w = finish()


    sig = _sigmoid(gd_ref[...])
    lane = lax.broadcasted_iota(jnp.int32, sig.shape, 1)
    outs = []
    for h in range(NSA_HG):
        head = g * NSA_HG + h
        hrows = slice(h * tq, (h + 1) * tq)
        g_c = _gate_column(sig, lane, GD_GATE + head)
        g_s = _gate_column(sig, lane, GD_GATE + NSA_HEADS + head)
        g_w = _gate_column(sig, lane, GD_GATE + 2 * NSA_HEADS + head)
        outs.append(g_c * o_c[h] + g_s * o_s[hrows] + g_w * o_w[hrows])
    o_ref[...] = jnp.concatenate(outs, axis=-1) * _silu(za_ref[...])


def nsa_attn_prompt(proj, qh, kc, vc, ks, vs, kw, vw, e_mat, tq=128, tk=512):
    nb, _, seq, _ = qh.shape
    assert seq >= WINDOW + tq and tk % tq == 0 and tk <= WINDOW + tq
    rows, wn = NSA_HG * tq, WINDOW + tq
    nq = seq // tq
    gw = NSA_HG * HEAD_DIM
    kv_spec = pl.BlockSpec((1, 1, seq, HEAD_DIM), lambda b, g, i: (b, g, 0, 0))
    c_spec = pl.BlockSpec((1, 1, 2 * LANES, HEAD_DIM), lambda b, g, i: (b, g, 0, 0))
    return pl.pallas_call(
        functools.partial(_nsa_attn_kernel, tq=tq, tk=tk),
        grid=(nb, NSA_KV_HEADS, nq),
        in_specs=[
            pl.BlockSpec((1, NSA_HG, tq, HEAD_DIM), lambda b, g, i: (b, g, i, 0)),
            c_spec, c_spec, kv_spec, kv_spec, kv_spec, kv_spec,
            pl.BlockSpec((tq, LANES), lambda b, g, i: (b * nq + i, E_GD // LANES)),
            pl.BlockSpec((tq, gw), lambda b, g, i: (b * nq + i, E_ZA // gw + g)),
            pl.BlockSpec(e_mat.shape, lambda b, g, i: (0, 0)),
        ],
        out_specs=pl.BlockSpec((tq, gw), lambda b, g, i: (b * nq + i, g)),
        out_shape=jax.ShapeDtypeStruct((nb * seq, NSA_WIDTH), F32),
        scratch_shapes=[
            pltpu.VMEM((rows, wn), F32),
            pltpu.VMEM((rows, wn), BF16),
            pltpu.VMEM((tq, wn), F32),
            pltpu.VMEM((rows, LANES), F32),
            pltpu.VMEM((rows, LANES), F32),
            pltpu.VMEM((rows, LANES), F32),
            pltpu.VMEM((rows, HEAD_DIM), F32),
            pltpu.VMEM((LANES, tq), F32),
        ],
        compiler_params=_cparams(("arbitrary", "arbitrary", "arbitrary")),
        name="nsa_attn_prompt",
    )(qh, kc, vc, ks, vs, kw, vw, proj, proj, e_mat)


def _nsa_dec_kernel(pt_ref, q_ref, cmp_ref, *rest, past_len, n_pages):
    del pt_ref
    n_samp = q_ref.shape[0]
    tail = rest[n_samp * n_pages:]
    o_ref = tail[-2]
    outs = _run_interleaved([
        _nsa_dec_sample(si, q_ref, cmp_ref, rest[si * n_pages:(si + 1) * n_pages], *tail, past_len=past_len)
        for si in range(n_samp)])
    for si in range(n_samp):
        o_ref[si] = outs[si]


def _nsa_dec_sample(si, q_ref, cmp_ref, page_refs, rows_ref, pwin_ref, wnew_ref, gd_ref, za_ref, e_ref, pair_ref,
                    o_ref, imp_sc, *, past_len):
    del o_ref
    nq = DEC_PAD
    scale = HEAD_DIM ** -0.5
    qpos = past_len + lax.broadcasted_iota(jnp.int32, (nq, 1), 0)
    qpos_row = past_len + lax.broadcasted_iota(jnp.int32, (1, nq), 1)
    n_blk = (past_len + nq - 1) // SEL_BLOCK + 1
    q = q_ref[si]
    cmp = cmp_ref[si]

    def q_group(g):
        parts = [q[:, (g * NSA_HG + h) * HEAD_DIM:(g * NSA_HG + h + 1) * HEAD_DIM] for h in range(NSA_HG)]
        return jnp.concatenate(parts, axis=0) * scale

    qg = [q_group(g) for g in range(NSA_KV_HEADS)]

    rows = rows_ref[si]
    wnew = wnew_ref[si]
    win_keep = pwin_ref.shape[-1]
    sig = _sigmoid(gd_ref[si])
    lane = lax.broadcasted_iota(jnp.int32, sig.shape, 1)
    kpos = lax.broadcasted_iota(jnp.int32, (1, past_len), 1)
    npos = past_len + lax.broadcasted_iota(jnp.int32, (1, nq), 1)
    wpos = past_len - win_keep + lax.broadcasted_iota(jnp.int32, (1, win_keep), 1)
    outs = []
    for g in range(NSA_KV_HEADS):
        qb = qg[g].astype(BF16)
        kc = cmp[:, g * HEAD_DIM:(g + 1) * HEAD_DIM].astype(BF16)
        vc = cmp[:, LANES + g * HEAD_DIM:LANES + (g + 1) * HEAD_DIM].astype(BF16)
        o_c, sel = yield from _cmp_branch_and_select_stages(
            qb, kc, vc, qpos, qpos_row, nq, imp_sc.at[si * NSA_KV_HEADS + g], n_blk, pair_ref[...])
        yield
        kt = jnp.concatenate([r[0, 0, g] for r in page_refs], axis=1).astype(BF16)
        vt = jnp.concatenate([r[0, 1, g] for r in page_refs], axis=1).astype(BF16)
        s = _dot(qb, kt).reshape(NSA_HG, nq, past_len)
        blk_on = _dot(sel, e_ref[:, :past_len])
        yield
        bias = _mask_bias(jnp.where(kpos <= qpos, blk_on, 0.0) > 0.5)
        carry = yield from _flash_update_stages(_flash_init(nq), s, bias, vt, v_transposed=True)
        yield
        kn = rows[:, 2 * LANES + g * HEAD_DIM:2 * LANES + (g + 1) * HEAD_DIM]
        vn = rows[:, 3 * LANES + g * HEAD_DIM:3 * LANES + (g + 1) * HEAD_DIM]
        s = _dot_nt(qg[g], kn).reshape(NSA_HG, nq, nq)
        blk_on = _dot(sel, e_ref[:, past_len:past_len + LANES])[:, :nq]
        yield
        bias = _mask_bias(jnp.where(npos <= qpos, blk_on, 0.0) > 0.5)
        o_s = _flash_finish((yield from _flash_update_stages(carry, s, bias, vn)))
        yield
        s = _dot(qb, pwin_ref[si, 0, g].astype(BF16)).reshape(NSA_HG, nq, win_keep)
        yield
        dist = qpos - wpos
        bias = _mask_bias(jnp.where(wpos >= 0, jnp.where(dist >= 0, dist, WINDOW), WINDOW) < WINDOW)
        carry = yield from _flash_update_stages(_flash_init(nq), s, bias, pwin_ref[si, 1, g].astype(BF16),
                                                v_transposed=True)
        yield
        kwn = wnew[:, g * HEAD_DIM:(g + 1) * HEAD_DIM]
        vwn = wnew[:, LANES + g * HEAD_DIM:LANES + (g + 1) * HEAD_DIM]
        s = _dot_nt(qg[g], kwn).reshape(NSA_HG, nq, nq)
        yield
        dist = qpos - npos
        bias = _mask_bias(jnp.where(dist >= 0, dist, WINDOW) < WINDOW)
        o_w = _flash_finish((yield from _flash_update_stages(carry, s, bias, vwn)))
        for h in range(NSA_HG):
            head = g * NSA_HG + h
            g_c = _gate_column(sig, lane, GD_GATE + head)
            g_s = _gate_column(sig, lane, GD_GATE + NSA_HEADS + head)
            g_w = _gate_column(sig, lane, GD_GATE + 2 * NSA_HEADS + head)
            outs.append(g_c * o_c[h] + g_s * o_s[h] + g_w * o_w[h])
        yield
    return jnp.concatenate(outs, axis=-1) * _silu(za_ref[si])


def nsa_attn_decode(page_table, layer, qrot, cmp, cache_t, rows, pwin_t, wnew, proj3, e_mat, past_len, n_samp):
    nb, npg = page_table.shape
    win_keep = pwin_t.shape[-1]
    ncb = cmp.shape[1]
    per_b = lambda blk: pl.BlockSpec(blk, lambda b, pt: (b,) + (0,) * (len(blk) - 1))
    page_blk = (None, 1, 2, NSA_KV_HEADS, HEAD_DIM, PAGE_SIZE)
    page_specs = [pl.BlockSpec(page_blk, lambda b, pt, si=si, j=j: (layer, pt[b * n_samp + si, j], 1, 0, 0, 0))
                  for si in range(n_samp) for j in range(npg)]
    pair_t = (jnp.arange(ncb, dtype=jnp.int32)[None, :] // 2
              == jnp.arange(LANES, dtype=jnp.int32)[:, None]).astype(BF16)
    return pl.pallas_call(
        functools.partial(_nsa_dec_kernel, past_len=past_len, n_pages=npg),
        grid_spec=pltpu.PrefetchScalarGridSpec(
            num_scalar_prefetch=1,
            grid=(nb // n_samp,),
            in_specs=[
                per_b((n_samp, DEC_PAD, NSA_WIDTH)),
                per_b((n_samp, ncb, 4 * HEAD_DIM)),
                *page_specs,
                per_b((n_samp, DEC_PAD, 4 * LANES)),
                pl.BlockSpec((None, n_samp, 2, NSA_KV_HEADS, HEAD_DIM, win_keep),
                             lambda b, pt: (layer, b, 0, 0, 0, 0)),
                per_b((n_samp, DEC_PAD, 2 * LANES)),
                pl.BlockSpec((n_samp, DEC_PAD, LANES), lambda b, pt: (b, 0, E_GD // LANES)),
                pl.BlockSpec((n_samp, DEC_PAD, NSA_WIDTH), lambda b, pt: (b, 0, E_ZA // NSA_WIDTH)),
                pl.BlockSpec(e_mat.shape, lambda b, pt: (0, 0)),
                pl.BlockSpec(pair_t.shape, lambda b, pt: (0, 0)),
            ],
            out_specs=per_b((n_samp, DEC_PAD, NSA_WIDTH)),
            scratch_shapes=[pltpu.VMEM((n_samp * NSA_KV_HEADS, LANES, DEC_PAD), F32)],
        ),
        out_shape=jax.ShapeDtypeStruct((nb, DEC_PAD, NSA_WIDTH), F32),
        compiler_params=_cparams(("arbitrary",)),
        name="nsa_attn_decode",
    )(page_table, qrot, cmp, *([cache_t] * (n_samp * npg)), rows, pwin_t, wnew, proj3, proj3, e_mat, pair_t)


def _tri_masks(c):
    row = lax.broadcasted_iota(jnp.int32, (c, c), 0)
    col = lax.broadcasted_iota(jnp.int32, (c, c), 1)
    lower = row >= col
    return lower, jnp.where(lower, 1.0, 0.0).astype(BF16), jnp.where(row <= col, 1.0, 0.0).astype(BF16)


def _ssd_kernel(zb_ref, xbc_ref, gd_ref, prev_ref, h0_ref, cw_ref, cb_ref, dtb_ref, alog_ref, alog16_ref,
                dexp_ref, nw_ref, e16_ref, *rest, chunk, n_valid):
    y_ref, so_ref, h_sc, xbuf = rest[-4:]
    c = pl.program_id(1)
    n_seq = zb_ref.shape[0]
    hp = SSD_HEADS // SSD_GROUPS * SSD_HEAD_DIM
    halo = SUBLANES

    @pl.when(c == 0)
    def _():
        h_sc[...] = h0_ref[...]
        xbuf[:, 0:halo] = prev_ref[...]

    lower, tri, tri_t = _tri_masks(chunk)
    ones8 = jnp.ones((chunk, SUBLANES), BF16)
    lane = lax.broadcasted_iota(jnp.int32, (chunk, LANES), 1)
    low_half = lane < SSD_HEAD_DIM

    def seq_stages(b):
        xbuf[b, halo:halo + chunk] = xbc_ref[b]
        conv = cb_ref[...]
        for i in range(SSD_CONV):
            lo = halo - (SSD_CONV - 1) + i
            conv = conv + xbuf[b, lo:lo + chunk] * cw_ref[i:i + 1, :]
        xbuf[b, 0:halo] = xbuf[b, chunk:chunk + halo]
        act = _silu(conv)
        xs = act[:, :SSD_INNER]
        bm = act[:, SSD_INNER:SSD_INNER + SSD_GROUPS * SSD_STATE]
        cm = act[:, SSD_INNER + SSD_GROUPS * SSD_STATE:]

        dt16 = _softplus(gd_ref[b] + dtb_ref[...])[:, GD_DT:GD_DT + SSD_HEADS]
        if n_valid is not None:
            tok = c * chunk + lax.broadcasted_iota(jnp.int32, (chunk, 1), 0)
            dt16 = jnp.where(tok < n_valid, dt16, 0.0)
        yield
        dt = _dot3(dt16, e16_ref[...])
        da = dt * (-jnp.exp(alog_ref[...]))
        da16 = dt16 * (-jnp.exp(alog16_ref[...]))
        yield
        cum = _dot3_wx(tri, da)
        yield
        cum16 = _dot3_wx(tri, da16)
        cum16_t = _dot3_tn(da16, tri_t)
        yield
        decay_col = jnp.exp(_dot3_tn(da, ones8)[:, 0:1])
        xdt = xs * dt
        cum_last = cum[chunk - 1:chunk, :]
        xw = (xdt * jnp.exp(cum_last - cum)).astype(BF16)
        xdt_b = xdt.astype(BF16)

        y_intra = [None] * (SSD_INNER // LANES)
        y_inter = []
        for g in range(SSD_GROUPS):
            cg = cm[:, g * SSD_STATE:(g + 1) * SSD_STATE].astype(BF16)
            bg = bm[:, g * SSD_STATE:(g + 1) * SSD_STATE].astype(BF16)
            h_g = h_sc[b, g * hp:(g + 1) * hp, :]
            yield
            cb = _dot_nt(cg, bg)
            y_inter.append(_dot_nt(cg, h_g.astype(BF16)))
            for hh in range(SSD_HEADS // SSD_GROUPS):
                h = g * (SSD_HEADS // SSD_GROUPS) + hh
                diff = cum16[:, h:h + 1] - cum16_t[h:h + 1, :]
                lmat = jnp.where(lower, jnp.exp(jnp.where(lower, diff, 0.0)), 0.0)
                m = (cb * lmat).astype(BF16)
                pair = h // 2
                x_pair = xdt_b[:, pair * LANES:(pair + 1) * LANES]
                keep = low_half if h % 2 == 0 else jnp.logical_not(low_half)
                yield
                contrib = _dot(m, jnp.where(keep, x_pair, jnp.zeros_like(x_pair)))
                y_intra[pair] = contrib if y_intra[pair] is None else y_intra[pair] + contrib
            yield
            h_sc[b, g * hp:(g + 1) * hp, :] = (h_g * decay_col[g * hp:(g + 1) * hp, :]
                                               + _dot_tn(xw[:, g * hp:(g + 1) * hp], bg))
        y = jnp.concatenate(y_intra, axis=-1) + jnp.concatenate(y_inter, axis=-1) * jnp.exp(cum)
        y = y + xs * dexp_ref[...]
        y = y * _silu(zb_ref[b])
        ms = jnp.mean(y * y, axis=-1, keepdims=True)
        y_ref[b] = y * lax.rsqrt(ms + 1e-6) * nw_ref[...]

    _run_interleaved([seq_stages(b) for b in range(n_seq)])

    @pl.when(c == pl.num_programs(1) - 1)
    def _():
        so_ref[...] = h_sc[...]


def _stacked_state_io(stack, n_layers, blk, out_layer, n_in):
    shape = jax.ShapeDtypeStruct((n_layers,) + blk[0], F32)
    spec = pl.BlockSpec((None,) + blk[1], lambda b, c: (out_layer, b, 0, 0))
    if stack is None:
        return spec, shape, [], [], {}
    return spec, shape, [stack], [pl.BlockSpec(memory_space=pl.ANY)], {n_in: 1}


def ssd_mixer(proj, prev8, h0, layer, conv_w, conv_b, dtb_row, alog_exp, alog16, d_exp, norm_w, e16, nb, seq, chunk,
              n_seq, n_valid, stack, out_layer, n_layers):
    full = lambda a: pl.BlockSpec(a.shape, lambda b, c: (0,) * a.ndim)
    proj3 = proj.reshape(nb, seq, E_END)
    col_spec = lambda width, off: pl.BlockSpec((n_seq, chunk, width), lambda b, c: (b, c, off // width))
    in_specs = [
        col_spec(SSD_INNER, E_ZB), col_spec(SSD_CONV_CH, E_XBC), col_spec(LANES, E_GD),
        pl.BlockSpec((n_seq, SUBLANES, SSD_CONV_CH), lambda b, c: (b, 0, 0)),
        pl.BlockSpec((None, n_seq, SSD_INNER, SSD_STATE), lambda b, c: (layer, b, 0, 0)),
        full(conv_w), full(conv_b), full(dtb_row), full(alog_exp), full(alog16), full(d_exp), full(norm_w),
        full(e16),
    ]
    st_spec, st_shape, extra, extra_specs, aliases = _stacked_state_io(
        stack, n_layers, ((nb, SSD_INNER, SSD_STATE), (n_seq, SSD_INNER, SSD_STATE)), out_layer, len(in_specs))
    y, st = pl.pallas_call(
        functools.partial(_ssd_kernel, chunk=chunk, n_valid=n_valid),
        grid=(nb // n_seq, seq // chunk),
        in_specs=in_specs + extra_specs,
        out_specs=[pl.BlockSpec((n_seq, chunk, SSD_INNER), lambda b, c: (b, c, 0)), st_spec],
        out_shape=[jax.ShapeDtypeStruct((nb, seq, SSD_INNER), F32), st_shape],
        scratch_shapes=[pltpu.VMEM((n_seq, SSD_INNER, SSD_STATE), F32),
                        pltpu.VMEM((n_seq, SUBLANES + chunk, SSD_CONV_CH), F32)],
        input_output_aliases=aliases,
        compiler_params=_cparams(("arbitrary", "arbitrary")),
        name="ssd_mixer",
    )(proj3, proj3, proj3, prev8, h0, conv_w, conv_b, dtb_row, alog_exp, alog16, d_exp, norm_w, e16, *extra)
    return y.reshape(nb * seq, SSD_INNER), st


def _gla_kernel(q_ref, k_ref, v_ref, r_ref, glr_ref, s0_ref, wg_ref, bg_ref, nw_ref, *rest,
                chunk, n_sub, n_valid):
    o_ref, so_ref, s_sc = rest[-3:]
    c = pl.program_id(1)
    n_seq = q_ref.shape[0]
    span = n_sub * chunk

    @pl.when(c == 0)
    def _():
        s_sc[...] = s0_ref[...]

    row = lax.broadcasted_iota(jnp.int32, (span, span), 0)
    col = lax.broadcasted_iota(jnp.int32, (span, span), 1)
    lower = jnp.where(row // chunk == col // chunk, row - col, -1) >= 0
    tri = jnp.where(lower, 1.0, 0.0).astype(BF16)
    crow = lax.broadcasted_iota(jnp.int32, (span, n_sub * SUBLANES), 0) // chunk
    ccol = lax.broadcasted_iota(jnp.int32, (span, n_sub * SUBLANES), 1) // SUBLANES
    chunk_ones = jnp.where(crow == ccol, 1.0, 0.0).astype(BF16)
    nw = nw_ref[...]

    def head_stages(b, h, qg, kg, kd, v, r, total):
        kcols = slice(h * GLA_DK, (h + 1) * GLA_DK)
        vcols = slice(h * GLA_DV, (h + 1) * GLA_DV)
        att = jnp.where(lower, _dot_nt(qg[:, kcols], kg[:, kcols]), 0.0)
        yield
        o = _dot(att.astype(BF16), v[:, vcols])
        s_h = s_sc[b, kcols, :]
        carried = []
        for u in range(n_sub):
            rows = slice(u * chunk, (u + 1) * chunk)
            yield
            carried.append(_dot(qg[rows, kcols], s_h.astype(BF16)))
            decay = jnp.exp(total[kcols, u * SUBLANES:u * SUBLANES + 1])
            s_h = s_h * decay + _dot_tn(kd[rows, kcols], v[rows, vcols])
        s_sc[b, kcols, :] = s_h
        yield
        o = o + jnp.concatenate(carried, axis=0)
        ms = jnp.mean(o * o, axis=-1, keepdims=True)
        o = o * lax.rsqrt(ms + 1e-6) * nw
        o_ref[b, :, vcols] = o * _silu(r[:, vcols])

    def seq_stages(b):
        lg = _log_sigmoid(_dot(glr_ref[b].astype(BF16), wg_ref[...]) + bg_ref[...]) * (1.0 / GLA_TAU)
        k = k_ref[b]
        if n_valid is not None:
            tok = c * span + lax.broadcasted_iota(jnp.int32, (span, 1), 0)
            lg = jnp.where(tok < n_valid, lg, 0.0)
            k = jnp.where(tok < n_valid, k, 0.0)
        yield
        bcum = _dot3_wx(tri, lg)
        yield
        total = _dot3_tn(lg, chunk_ones)
        qg = (q_ref[b] * (GLA_DK ** -0.5) * jnp.exp(bcum)).astype(BF16)
        kg = (k * jnp.exp(-bcum)).astype(BF16)
        kd = jnp.concatenate(
            [k[u * chunk:(u + 1) * chunk] * jnp.exp(bcum[(u + 1) * chunk - 1:(u + 1) * chunk]
                                                    - bcum[u * chunk:(u + 1) * chunk]) for u in range(n_sub)],
            axis=0).astype(BF16)
        v = v_ref[b].astype(BF16)
        r = r_ref[b]
        yield
        yield from _round_robin([head_stages(b, h, qg, kg, kd, v, r, total) for h in range(GLA_HEADS)])

    _run_interleaved([seq_stages(b) for b in range(n_seq)])

    @pl.when(c == pl.num_programs(1) - 1)
    def _():
        so_ref[...] = s_sc[...]


def gla_mixer(proj, s0, layer, wg_pad, bg, norm_w, nb, seq, chunk, n_sub, n_seq, n_valid, stack, out_layer,
              n_layers):
    span = n_sub * chunk
    full = lambda a: pl.BlockSpec(a.shape, lambda b, c: (0,) * a.ndim)
    srows = GLA_HEADS * GLA_DK
    proj3 = proj.reshape(nb, seq, O_END)
    col_spec = lambda width, off: pl.BlockSpec((n_seq, span, width), lambda b, c: (b, c, off // width))
    in_specs = [
        col_spec(GLA_KEY_WIDTH, O_Q), col_spec(GLA_KEY_WIDTH, O_K), col_spec(GLA_VAL_WIDTH, O_V),
        col_spec(GLA_VAL_WIDTH, O_R), col_spec(LANES, O_GLR),
        pl.BlockSpec((None, n_seq, srows, GLA_DV), lambda b, c: (layer, b, 0, 0)),
        full(wg_pad), full(bg), full(norm_w),
    ]
    st_spec, st_shape, extra, extra_specs, aliases = _stacked_state_io(
        stack, n_layers, ((nb, srows, GLA_DV), (n_seq, srows, GLA_DV)), out_layer, len(in_specs))
    o, st = pl.pallas_call(
        functools.partial(_gla_kernel, chunk=chunk, n_sub=n_sub, n_valid=n_valid),
        grid=(nb // n_seq, seq // span),
        in_specs=in_specs + extra_specs,
        out_specs=[pl.BlockSpec((n_seq, span, GLA_VAL_WIDTH), lambda b, c: (b, c, 0)), st_spec],
        out_shape=[jax.ShapeDtypeStruct((nb, seq, GLA_VAL_WIDTH), F32), st_shape],
        scratch_shapes=[pltpu.VMEM((n_seq, srows, GLA_DV), F32)],
        input_output_aliases=aliases,
        compiler_params=_cparams(("arbitrary", "arbitrary")),
        name="gla_mixer",
    )(proj3, proj3, proj3, proj3, proj3, s0, wg_pad, bg, norm_w, *extra)
    return o.reshape(nb * seq, GLA_VAL_WIDTH), st


def _even_weight(w):
    q, g_a, kv, z_a, z_b, xbc, dt = jnp.split(w, list(np.cumsum(
        [NSA_WIDTH, 3 * NSA_HEADS, 6 * NSA_KV_HEADS * HEAD_DIM, NSA_WIDTH, SSD_INNER, SSD_CONV_CH])), axis=-1)
    pad = jnp.zeros((w.shape[0], E_END - E_GD - 3 * NSA_HEADS - SSD_HEADS), w.dtype)
    return jnp.concatenate([xbc, q, z_b, z_a, kv, g_a, dt, pad], axis=-1).astype(BF16)


def _odd_weight(w):
    q, k, v, glr, r = jnp.split(w, list(np.cumsum(
        [GLA_KEY_WIDTH, GLA_KEY_WIDTH, GLA_VAL_WIDTH, GLA_GATE_RANK])), axis=-1)
    pad = jnp.zeros((w.shape[0], O_END - O_GLR - GLA_GATE_RANK), w.dtype)
    return jnp.concatenate([q, k, v, r, glr, pad], axis=-1).astype(BF16)


def _rope_tables(pos):
    half = HEAD_DIM // 2
    inv = ROPE_THETA ** (-jnp.arange(half, dtype=F32) / half)
    ang = pos.astype(F32)[:, None] * inv[None, :]
    cos, sin = jnp.cos(ang), jnp.sin(ang)
    reps = LANES // HEAD_DIM
    return jnp.tile(jnp.concatenate([cos, cos], -1), (1, reps)), jnp.tile(jnp.concatenate([-sin, sin], -1), (1, reps))


def _compress_weights(pe, w1, w2):
    pe_tab = jnp.concatenate([pe[0], pe[0], pe[1], pe[1]], axis=-1)
    w1r = w1.reshape(2, CMP_BLOCK, HEAD_DIM, CMP_HIDDEN)
    w1_blk = jnp.zeros((CMP_BLOCK, 4 * HEAD_DIM, 4 * CMP_HIDDEN), F32)
    w2_blk = jnp.zeros((4 * CMP_HIDDEN, 4 * HEAD_DIM), F32)
    for part in range(4):
        src = part // 2
        w1_blk = w1_blk.at[:, part * HEAD_DIM:(part + 1) * HEAD_DIM,
                           part * CMP_HIDDEN:(part + 1) * CMP_HIDDEN].set(w1r[src])
        w2_blk = w2_blk.at[part * CMP_HIDDEN:(part + 1) * CMP_HIDDEN,
                           part * HEAD_DIM:(part + 1) * HEAD_DIM].set(w2[src])
    return pe_tab, w1_blk.astype(BF16), w2_blk.astype(BF16)


def _compress_weights_cache(pe, w1, w2):
    w1r = w1.reshape(2, CMP_BLOCK, HEAD_DIM, CMP_HIDDEN)
    w1_bd = jnp.zeros((2, CMP_BLOCK, NSA_KV_HEADS * HEAD_DIM, NSA_KV_HEADS * CMP_HIDDEN), F32)
    w2_bd = jnp.zeros((2, NSA_KV_HEADS * CMP_HIDDEN, NSA_KV_HEADS * HEAD_DIM), F32)
    for g in range(NSA_KV_HEADS):
        w1_bd = w1_bd.at[:, :, g * HEAD_DIM:(g + 1) * HEAD_DIM, g * CMP_HIDDEN:(g + 1) * CMP_HIDDEN].set(w1r)
        w2_bd = w2_bd.at[:, g * CMP_HIDDEN:(g + 1) * CMP_HIDDEN, g * HEAD_DIM:(g + 1) * HEAD_DIM].set(w2)
    pe_tok = jnp.tile(jnp.concatenate([pe] * NSA_KV_HEADS, axis=-1), (1, PAGE_SIZE // CMP_BLOCK, 1))
    return pe_tok, w1_bd.astype(BF16), w2_bd.astype(BF16)


def _cmp_layout(cmp, nb, nc, dtype):
    c = cmp.reshape(nb, nc, 2, NSA_KV_HEADS, HEAD_DIM).transpose(2, 0, 3, 1, 4)
    halves = []
    for par in range(2):
        h = c[:, :, :, par::2]
        halves.append(jnp.pad(h, ((0, 0), (0, 0), (0, 0), (0, LANES - h.shape[3]), (0, 0))))
    c = jnp.concatenate(halves, axis=3).astype(dtype)
    return c[0], c[1]


def _sel_expand(n_keys):
    blk = jnp.arange(n_keys, dtype=jnp.int32) // SEL_BLOCK
    return (blk[None, :] == jnp.arange(LANES, dtype=jnp.int32)[:, None]).astype(BF16)


def kernel(x_prompt, x_sample, c_prompt, c_sample, cache_nsa, cache_nsa_win, state_ssd_conv, state_ssd, state_gla, page_table, norm_w, w_mod, b_mod, w_in_even, w_out_even, nsa_cmp_pe, nsa_cmp_w1, nsa_cmp_w2, ssd_conv_w, ssd_conv_b, ssd_dt_bias, ssd_a_log, ssd_d, ssd_norm_w, w_in_odd, gla_w_gate2, gla_b_gate, gla_norm_w, w_out_odd, final_norm_w):
    bp, sp, d = x_prompt.shape
    bs, ss, _ = x_sample.shape
    depth = norm_w.shape[0]
    npg = page_table.shape[1]
    past_len = npg * PAGE_SIZE
    win_keep = cache_nsa_win.shape[2]
    assert sp % 512 == 0 and sp // SEL_BLOCK <= LANES and sp // CMP_BLOCK <= 2 * LANES
    assert ss <= DEC_PAD and ss < CMP_BLOCK and past_len % SEL_BLOCK == 0 and ss >= SSD_CONV - 1
    assert past_len // SEL_BLOCK + 1 <= LANES and (bs * DEC_PAD) % 256 == 0 and (bs * npg) % 32 == 0
    tp, td = bp * sp, bs * DEC_PAD
    tm_d = 256

    c_all = jnp.concatenate([c_prompt, c_sample], axis=0)
    c_all = jnp.pad(c_all, ((0, -c_all.shape[0] % SUBLANES), (0, 0)))
    mod = mod_all(c_all, w_mod.astype(BF16), b_mod)

    def mods(l):
        shift, scale, gate = jnp.split(mod[l], 3, axis=-1)
        mp = [m[:bp].reshape(bp, 1, d) for m in (shift, scale, gate)]
        ms = [jnp.repeat(m[bp:bp + bs], DEC_PAD, axis=0).reshape(td // tm_d, tm_d, d) for m in (shift, scale, gate)]
        return mp, ms

    xp = x_prompt.reshape(tp, d)
    xs = jnp.pad(x_sample, ((0, 0), (0, DEC_PAD - ss), (0, 0))).reshape(td, d)

    cos_p, sin_p = _rope_tables(jnp.arange(sp, dtype=jnp.int32))
    cos_s, sin_s = _rope_tables(past_len + jnp.arange(DEC_PAD, dtype=jnp.int32))
    cos_s, sin_s = jnp.tile(cos_s, (tm_d // DEC_PAD, 1)), jnp.tile(sin_s, (tm_d // DEC_PAD, 1))
    cache_t = jnp.transpose(cache_nsa, (0, 1, 3, 4, 5, 2))
    win_t = jnp.transpose(cache_nsa_win, (0, 1, 3, 4, 5, 2))
    ssd_h0_s = state_ssd.reshape(state_ssd.shape[0], bs, SSD_INNER, SSD_STATE)
    gla_s0_s = state_gla.reshape(state_gla.shape[0], bs, GLA_HEADS * GLA_DK, GLA_DV)
    e_mat_p = _sel_expand(sp)
    e_mat_s = _sel_expand(past_len + LANES)
    e16 = (jnp.arange(SSD_INNER, dtype=jnp.int32)[None, :] // SSD_HEAD_DIM
           == jnp.arange(SSD_HEADS, dtype=jnp.int32)[:, None]).astype(BF16)

    outs = {k: [] for k in ("kv_p", "kv_s", "win_p", "win_s", "cv_p", "cv_s")}
    stacks = dict.fromkeys(("ss_p", "ss_s", "gl_p", "gl_s"))
    n_even, n_odd = (depth + 1) // 2, depth // 2
    for l in range(depth):
        e = l // 2
        (shift_p, scale_p, gate_p), (shift_s, scale_s, gate_s) = mods(l)
        last = l == depth - 1
        if l % 2 == 0:
            w_in = _even_weight(w_in_even[e])
            w_out = w_out_even[e].astype(BF16)
            w_out_a, w_out_b = w_out[:NSA_WIDTH], w_out[NSA_WIDTH:]
            pe_tab, w1_blk, w2_blk = _compress_weights(nsa_cmp_pe[e], nsa_cmp_w1[e], nsa_cmp_w2[e])
            conv_w, conv_b = ssd_conv_w[e], ssd_conv_b[e].reshape(1, SSD_CONV_CH)
            dtb_row = jnp.zeros((1, LANES), F32).at[0, GD_DT:GD_DT + SSD_HEADS].set(ssd_dt_bias[e])
            alog16 = ssd_a_log[e].reshape(1, SSD_HEADS)
            alog_exp = jnp.repeat(ssd_a_log[e], SSD_HEAD_DIM).reshape(1, SSD_INNER)
            d_exp = jnp.repeat(ssd_d[e], SSD_HEAD_DIM).reshape(1, SSD_INNER)
            ssd_nw = ssd_norm_w[e].reshape(1, SSD_INNER)
            ssd_args = (conv_w, conv_b, dtb_row, alog_exp, alog16, d_exp, ssd_nw, e16)

            proj = inproj(xp, norm_w[l], scale_p, shift_p, w_in, 256, sp)
            rows, win, _, qh, ks, vs, kw, vw = nsa_prep(proj, cos_p, sin_p, 256, sp, True)
            nc = sp // CMP_BLOCK
            cmp = nsa_compress(rows.reshape(tp // CMP_BLOCK, CMP_BLOCK * 4 * LANES), pe_tab, w1_blk, w2_blk,
                               min(128, tp // CMP_BLOCK))
            kc, vc = _cmp_layout(cmp, bp, nc, BF16)
            o_a = nsa_attn_prompt(proj, qh, kc, vc, ks, vs, kw, vw, e_mat_p)
            y, stacks["ss_p"] = ssd_mixer(proj, jnp.zeros((bp, SUBLANES, SSD_CONV_CH), F32),
                                          jnp.zeros((1, bp, SSD_INNER, SSD_STATE), F32), 0, *ssd_args, bp, sp,
                                          SSD_CHUNK, 2 if bp % 2 == 0 else 1, None, stacks["ss_p"], e, n_even)
            xp = outproj([o_a, y], [w_out_a, w_out_b], xp, gate_p, final_norm_w, 512, sp, last)
            outs["kv_p"].append(rows.reshape(bp, sp, 4, NSA_KV_HEADS, HEAD_DIM))
            outs["win_p"].append(win.reshape(bp, sp, 2, NSA_KV_HEADS, HEAD_DIM)[:, -min(WINDOW, sp):])
            outs["cv_p"].append(proj.reshape(bp, sp, E_END)[:, -(SSD_CONV - 1):, E_XBC:E_XBC + SSD_CONV_CH])

            proj = inproj(xs, norm_w[l], scale_s, shift_s, w_in, tm_d, tm_d)
            rows, win, qrot = nsa_prep(proj, cos_s, sin_s, tm_d, DEC_PAD, False)
            pe_tok, w1_bd, w2_bd = _compress_weights_cache(nsa_cmp_pe[e], nsa_cmp_w1[e], nsa_cmp_w2[e])
            cmp_s = nsa_compress_cache(cache_t, e, page_table.reshape(-1), pe_tok, w1_bd, w2_bd, 32)
            per_page = PAGE_SIZE // CMP_BLOCK
            o_a = nsa_attn_decode(page_table, e, qrot.reshape(bs, DEC_PAD, NSA_WIDTH),
                                  cmp_s.reshape(bs, npg * per_page, 4 * HEAD_DIM), cache_t,
                                  rows.reshape(bs, DEC_PAD, 4 * LANES), win_t,
                                  win.reshape(bs, DEC_PAD, 2 * LANES), proj.reshape(bs, DEC_PAD, E_END),
                                  e_mat_s, past_len, 4 if bs % 4 == 0 else 1).reshape(td, NSA_WIDTH)
            prev8 = jnp.pad(state_ssd_conv[e], ((0, 0), (SUBLANES - (SSD_CONV - 1), 0), (0, 0)))
            y, stacks["ss_s"] = ssd_mixer(proj, prev8, ssd_h0_s, e, *ssd_args, bs, DEC_PAD, DEC_PAD,
                                          4 if bs % 4 == 0 else 1, ss, stacks["ss_s"], e, n_even)
            xs = outproj([o_a, y], [w_out_a, w_out_b], xs, gate_s, final_norm_w, tm_d, tm_d, last)
            outs["kv_s"].append(rows.reshape(bs, DEC_PAD, 4, NSA_KV_HEADS, HEAD_DIM)[:, :ss])
            new_win = win.reshape(bs, DEC_PAD, 2, NSA_KV_HEADS, HEAD_DIM)[:, :ss]
            outs["win_s"].append(jnp.concatenate([cache_nsa_win[e], new_win], axis=1)[:, -win_keep:])
            new_xbc = proj.reshape(bs, DEC_PAD, E_END)[:, :ss, E_XBC:E_XBC + SSD_CONV_CH]
            outs["cv_s"].append(jnp.concatenate([state_ssd_conv[e], new_xbc], axis=1)[:, -(SSD_CONV - 1):])
        else:
            w_in = _odd_weight(w_in_odd[e])
            w_out = w_out_odd[e].astype(BF16)
            wg_pad = jnp.pad(gla_w_gate2[e], ((0, LANES - GLA_GATE_RANK), (0, 0))).astype(BF16)
            bg = gla_b_gate[e].reshape(1, GLA_KEY_WIDTH)
            gnw = gla_norm_w[e].reshape(1, GLA_DV)
            srows = GLA_HEADS * GLA_DK
            proj = inproj(xp, norm_w[l], scale_p, shift_p, w_in, 256, sp)
            o, stacks["gl_p"] = gla_mixer(proj, jnp.zeros((1, bp, srows, GLA_DV), F32), 0, wg_pad, bg, gnw, bp, sp,
                                          GLA_CHUNK, 4, 2 if bp % 2 == 0 else 1, None, stacks["gl_p"], e, n_odd)
            xp = outproj([o], [w_out], xp, gate_p, final_norm_w, 512, sp, last)
            proj = inproj(xs, norm_w[l], scale_s, shift_s, w_in, tm_d, tm_d)
            o, stacks["gl_s"] = gla_mixer(proj, gla_s0_s, e, wg_pad, bg, gnw, bs, DEC_PAD, DEC_PAD, 1,
                                          4 if bs % 4 == 0 else 1, ss, stacks["gl_s"], e, n_odd)
            xs = outproj([o], [w_out], xs, gate_s, final_norm_w, tm_d, tm_d, last)

    y_prompt = xp.reshape(bp, sp, d)
    y_sample = xs.reshape(bs, DEC_PAD, d)[:, :ss]
    st = {k: jnp.stack(v) for k, v in outs.items()}
    st["ss_p"] = stacks["ss_p"].reshape(n_even, bp, SSD_HEADS, SSD_HEAD_DIM, SSD_STATE)
    st["ss_s"] = stacks["ss_s"].reshape(n_even, bs, SSD_HEADS, SSD_HEAD_DIM, SSD_STATE)
    st["gl_p"] = stacks["gl_p"].reshape(n_odd, bp, GLA_HEADS, GLA_DK, GLA_DV)
    st["gl_s"] = stacks["gl_s"].reshape(n_odd, bs, GLA_HEADS, GLA_DK, GLA_DV)
    return (y_prompt, y_sample, st["kv_p"], st["kv_s"], st["win_p"], st["win_s"], st["cv_p"], st["cv_s"],
            st["ss_p"], st["ss_s"], st["gl_p"], st["gl_s"])
```

```python
import functools

import jax
import jax.numpy as jnp
import numpy as np
from jax import lax
from jax.experimental import pallas as pl
from jax.experimental.pallas import tpu as pltpu

F32 = jnp.float32
BF16 = jnp.bfloat16

PAGE_SIZE = 128
NSA_HEADS = 8
NSA_KV_HEADS = 2
HEAD_DIM = 64
NSA_WIDTH = NSA_HEADS * HEAD_DIM
NSA_HG = NSA_HEADS // NSA_KV_HEADS
CMP_BLOCK = 32
CMP_HIDDEN = 2 * HEAD_DIM
SEL_BLOCK = 64
TOP_N = 16
WINDOW = 512
FORCE_BONUS = 1.0e4
ROPE_THETA = 10000.0
SSD_HEADS = 16
SSD_HEAD_DIM = 64
SSD_INNER = SSD_HEADS * SSD_HEAD_DIM
SSD_GROUPS = 2
SSD_STATE = 128
SSD_CONV = 4
SSD_CONV_CH = SSD_INNER + 2 * SSD_GROUPS * SSD_STATE
GLA_HEADS = 4
GLA_DK = 128
GLA_DV = 256
GLA_KEY_WIDTH = GLA_HEADS * GLA_DK
GLA_VAL_WIDTH = GLA_HEADS * GLA_DV
GLA_GATE_RANK = 16
GLA_TAU = 16.0
GLA_CHUNK = 32
SSD_CHUNK = 64

LANES = 128
SUBLANES = 8
VMEM_LIMIT = 56 * 1024 * 1024

TM_IN = 256
TM_OUT = 512
CMP_ROWS = 128
CMP_PAGES = 32
SEQ_PER_STEP = 2
SAMPLES_PER_STEP = 4
GLA_SUB = 4

NEG_BIG = -1.0e30
M_INIT = -1.0e29
DEC_PAD = 8

E_XBC, E_Q, E_ZB, E_ZA, E_KV, E_GD, E_END = 0, 1536, 2048, 3072, 3584, 4352, 4480
GD_GATE, GD_DT = 0, 24
O_Q, O_K, O_V, O_R, O_GLR, O_END = 0, 512, 1024, 2048, 3072, 3200


def _cparams(sem):
    return pltpu.CompilerParams(dimension_semantics=sem, vmem_limit_bytes=VMEM_LIMIT)


def _dot(a, b):
    return jnp.dot(a, b, preferred_element_type=F32)


def _dot_nt(a, b):
    return lax.dot_general(a, b, (((1,), (1,)), ((), ())), preferred_element_type=F32)


def _dot_tn(a, b):
    return lax.dot_general(a, b, (((0,), (0,)), ((), ())), preferred_element_type=F32)


def _split3(x):
    hi = x.astype(BF16)
    r1 = x - hi.astype(F32)
    mid = r1.astype(BF16)
    lo = (r1 - mid.astype(F32)).astype(BF16)
    return hi, mid, lo


def _dot3(x, w01):
    hi, mid, lo = _split3(x)
    return _dot(hi, w01) + _dot(mid, w01) + _dot(lo, w01)


def _dot3_tn(x, w01):
    hi, mid, lo = _split3(x)
    return _dot_tn(hi, w01) + _dot_tn(mid, w01) + _dot_tn(lo, w01)


def _dot3_wx(w01, x):
    hi, mid, lo = _split3(x)
    return _dot(w01, hi) + _dot(w01, mid) + _dot(w01, lo)


def _silu(x):
    return x * (1.0 / (1.0 + jnp.exp(-x)))


def _sigmoid(x):
    return 1.0 / (1.0 + jnp.exp(-x))


def _softplus(x):
    return jnp.maximum(x, 0.0) + jnp.log1p(jnp.exp(-jnp.abs(x)))


def _log_sigmoid(x):
    return jnp.minimum(x, 0.0) - jnp.log1p(jnp.exp(-jnp.abs(x)))


def _mod_kernel(c_ref, w_ref, b_ref, o_ref):
    a = _silu(c_ref[...]).astype(BF16)
    o_ref[0] = _dot(a, w_ref[0]) + b_ref[0]


def mod_all(c_all, w_mod_bf, b_mod, tn=1024):
    nl, d, n = w_mod_bf.shape
    m = c_all.shape[0]
    return pl.pallas_call(
        _mod_kernel,
        grid=(nl, n // tn),
        in_specs=[
            pl.BlockSpec((m, d), lambda l, j: (0, 0)),
            pl.BlockSpec((1, d, tn), lambda l, j: (l, 0, j)),
            pl.BlockSpec((1, 1, tn), lambda l, j: (l, 0, j)),
        ],
        out_specs=pl.BlockSpec((1, m, tn), lambda l, j: (l, 0, j)),
        out_shape=jax.ShapeDtypeStruct((nl, m, n), F32),
        compiler_params=_cparams(("arbitrary", "arbitrary")),
        name="mod_all",
    )(c_all, w_mod_bf, b_mod.reshape(nl, 1, n))


def _inproj_kernel(x_ref, nw_ref, sc_ref, sh_ref, w_ref, o_ref, *, n_chunk):
    x = x_ref[...]
    ms = jnp.mean(x * x, axis=-1, keepdims=True)
    y = x * lax.rsqrt(ms + 1e-6) * nw_ref[...]
    h = (y * (1.0 + sc_ref[0]) + sh_ref[0]).astype(BF16)
    n = o_ref.shape[1]
    for n0 in range(0, n, n_chunk):
        n1 = min(n0 + n_chunk, n)
        o_ref[:, n0:n1] = _dot(h, w_ref[:, n0:n1])


def inproj(x, nw, scale, shift, w_bf, tm, rows_per_mod):
    t, d = x.shape
    n = w_bf.shape[1]
    r = scale.shape[1]
    mod_map = lambda i: ((i * tm) // rows_per_mod, 0, 0)
    return pl.pallas_call(
        functools.partial(_inproj_kernel, n_chunk=640),
        grid=(t // tm,),
        in_specs=[
            pl.BlockSpec((tm, d), lambda i: (i, 0)),
            pl.BlockSpec((1, d), lambda i: (0, 0)),
            pl.BlockSpec((1, r, d), mod_map),
            pl.BlockSpec((1, r, d), mod_map),
            pl.BlockSpec((d, n), lambda i: (0, 0)),
        ],
        out_specs=pl.BlockSpec((tm, n), lambda i: (i, 0)),
        out_shape=jax.ShapeDtypeStruct((t, n), F32),
        compiler_params=_cparams(("arbitrary",)),
        name="inproj",
    )(x, nw.reshape(1, d), scale, shift, w_bf)


def _outproj_kernel(*refs, n_in, final_norm):
    a_refs = refs[:n_in]
    w_refs = refs[n_in:2 * n_in]
    x_ref, g_ref, fw_ref, o_ref = refs[2 * n_in:]
    acc = _dot(a_refs[0][...].astype(BF16), w_refs[0][...])
    for a_ref, w_ref in zip(a_refs[1:], w_refs[1:]):
        acc = acc + _dot(a_ref[...].astype(BF16), w_ref[...])
    y = x_ref[...] + g_ref[0] * acc
    if final_norm:
        ms = jnp.mean(y * y, axis=-1, keepdims=True)
        y = y * lax.rsqrt(ms + 1e-6) * fw_ref[...]
    o_ref[...] = y


def outproj(a_list, w_list, x, gate, final_w, tm, rows_per_mod, final_norm):
    t, d = x.shape
    r = gate.shape[1]
    n_in = len(a_list)
    in_specs = [pl.BlockSpec((tm, a.shape[1]), lambda i: (i, 0)) for a in a_list]
    in_specs += [pl.BlockSpec(w.shape, lambda i: (0, 0)) for w in w_list]
    in_specs += [
        pl.BlockSpec((tm, d), lambda i: (i, 0)),
        pl.BlockSpec((1, r, d), lambda i: ((i * tm) // rows_per_mod, 0, 0)),
        pl.BlockSpec((1, d), lambda i: (0, 0)),
    ]
    return pl.pallas_call(
        functools.partial(_outproj_kernel, n_in=n_in, final_norm=final_norm),
        grid=(t // tm,),
        in_specs=in_specs,
        out_specs=pl.BlockSpec((tm, d), lambda i: (i, 0)),
        out_shape=jax.ShapeDtypeStruct((t, d), F32),
        compiler_params=_cparams(("arbitrary",)),
        name="outproj",
    )(*a_list, *w_list, x, gate, final_w.reshape(1, d))


def _rope_tile(t, cos, sin_signed, lane):
    fwd = pltpu.roll(t, LANES - HEAD_DIM // 2, 1)
    bwd = pltpu.roll(t, HEAD_DIM // 2, 1)
    partner = jnp.where((lane % HEAD_DIM) < HEAD_DIM // 2, fwd, bwd)
    return t * cos + partner * sin_signed


def _nsa_prep_kernel(q_ref, kv01_ref, kv23_ref, kv45_ref, cos_ref, sin_ref,
                     rows_ref, win_ref, qrot_ref, *hm_refs, head_major):
    cos = cos_ref[...]
    sin = sin_ref[...]
    lane = lax.broadcasted_iota(jnp.int32, cos.shape, 1)
    rope = lambda t: _rope_tile(t, cos, sin, lane)
    kv01, kv23, kv45 = kv01_ref[...], kv23_ref[...], kv45_ref[...]
    k_slc = rope(kv23[:, :LANES])
    v_slc = kv23[:, LANES:]
    k_win = rope(kv45[:, :LANES])
    v_win = kv45[:, LANES:]
    rows_ref[:, 0:LANES] = rope(kv01[:, :LANES])
    rows_ref[:, LANES:2 * LANES] = kv01[:, LANES:]
    rows_ref[:, 2 * LANES:3 * LANES] = k_slc
    rows_ref[:, 3 * LANES:4 * LANES] = v_slc
    win_ref[:, 0:LANES] = k_win
    win_ref[:, LANES:2 * LANES] = v_win
    q = q_ref[...]
    qr = [rope(q[:, j * LANES:(j + 1) * LANES]) for j in range(NSA_WIDTH // LANES)]
    for j, t in enumerate(qr):
        qrot_ref[:, j * LANES:(j + 1) * LANES] = t
    if head_major:
        qh_ref, ks_ref, vs_ref, kw_ref, vw_ref = hm_refs
        scale = HEAD_DIM ** -0.5
        for h in range(NSA_HEADS):
            lo = (h % 2) * HEAD_DIM
            qh_ref[0, h] = (qr[h // 2][:, lo:lo + HEAD_DIM] * scale).astype(BF16)
        for g in range(NSA_KV_HEADS):
            lo = g * HEAD_DIM
            ks_ref[0, g] = k_slc[:, lo:lo + HEAD_DIM].astype(BF16)
            vs_ref[0, g] = v_slc[:, lo:lo + HEAD_DIM].astype(BF16)
            kw_ref[0, g] = k_win[:, lo:lo + HEAD_DIM].astype(BF16)
            vw_ref[0, g] = v_win[:, lo:lo + HEAD_DIM].astype(BF16)


def nsa_prep(proj, cos_tab, sin_tab, tm, seq, head_major):
    t = proj.shape[0]
    tab_blocks = cos_tab.shape[0] // tm
    sb = seq // tm
    out_shape = [jax.ShapeDtypeStruct((t, 4 * LANES), F32),
                 jax.ShapeDtypeStruct((t, 2 * LANES), F32),
                 jax.ShapeDtypeStruct((t, NSA_WIDTH), F32)]
    out_specs = [pl.BlockSpec((tm, 4 * LANES), lambda i: (i, 0)),
                 pl.BlockSpec((tm, 2 * LANES), lambda i: (i, 0)),
                 pl.BlockSpec((tm, NSA_WIDTH), lambda i: (i, 0))]
    if head_major:
        nb = t // seq
        hm_map = lambda i: (i // sb, 0, i % sb, 0)
        out_shape.append(jax.ShapeDtypeStruct((nb, NSA_HEADS, seq, HEAD_DIM), BF16))
        out_specs.append(pl.BlockSpec((1, NSA_HEADS, tm, HEAD_DIM), hm_map))
        for _ in range(4):
            out_shape.append(jax.ShapeDtypeStruct((nb, NSA_KV_HEADS, seq, HEAD_DIM), BF16))
            out_specs.append(pl.BlockSpec((1, NSA_KV_HEADS, tm, HEAD_DIM), hm_map))
    kvb = E_KV // (2 * LANES)
    return pl.pallas_call(
        functools.partial(_nsa_prep_kernel, head_major=head_major),
        grid=(t // tm,),
        in_specs=[
            pl.BlockSpec((tm, NSA_WIDTH), lambda i: (i, E_Q // NSA_WIDTH)),
            pl.BlockSpec((tm, 2 * LANES), lambda i: (i, kvb)),
            pl.BlockSpec((tm, 2 * LANES), lambda i: (i, kvb + 1)),
            pl.BlockSpec((tm, 2 * LANES), lambda i: (i, kvb + 2)),
            pl.BlockSpec((tm, LANES), lambda i: (i % tab_blocks, 0)),
            pl.BlockSpec((tm, LANES), lambda i: (i % tab_blocks, 0)),
        ],
        out_specs=out_specs,
        out_shape=out_shape,
        compiler_params=_cparams(("arbitrary",)),
        name="nsa_prep",
    )(proj, proj, proj, proj, cos_tab, sin_tab)


def _nsa_compress_kernel(x_ref, pe_ref, w1_ref, w2_ref, o_ref):
    r = x_ref.shape[0]
    row_w = 4 * LANES
    acc = jnp.zeros((r, 4 * CMP_HIDDEN), F32)
    for t in range(CMP_BLOCK):
        xt = x_ref[:, t * row_w:t * row_w + 2 * LANES] + pe_ref[t:t + 1, :]
        acc = acc + _dot(xt.astype(BF16), w1_ref[t])
    o_ref[...] = _dot(_silu(acc).astype(BF16), w2_ref[...])


def nsa_compress(xblk, pe_tab, w1_blk, w2_blk, r):
    nb, kdim = xblk.shape
    return pl.pallas_call(
        _nsa_compress_kernel,
        grid=(nb // r,),
        in_specs=[
            pl.BlockSpec((r, kdim), lambda i: (i, 0)),
            pl.BlockSpec(pe_tab.shape, lambda i: (0, 0)),
            pl.BlockSpec(w1_blk.shape, lambda i: (0, 0, 0)),
            pl.BlockSpec(w2_blk.shape, lambda i: (0, 0)),
        ],
        out_specs=pl.BlockSpec((r, 4 * HEAD_DIM), lambda i: (i, 0)),
        out_shape=jax.ShapeDtypeStruct((nb, 4 * HEAD_DIM), F32),
        compiler_params=_cparams(("arbitrary",)),
        name="nsa_compress",
    )(xblk, pe_tab, w1_blk, w2_blk)


def _nsa_compress_cache_kernel(pt_ref, *refs, n_page):
    del pt_ref
    page_refs = refs[:n_page]
    pe_ref, w1_ref, w2_ref, o_ref, xbuf = refs[n_page:]
    gd = NSA_KV_HEADS * HEAD_DIM
    per_page = PAGE_SIZE // CMP_BLOCK
    for p, x_ref in enumerate(page_refs):
        for c in range(2):
            xbuf[c, p * PAGE_SIZE:(p + 1) * PAGE_SIZE, :] = x_ref[0, c].reshape(gd, PAGE_SIZE).T + pe_ref[c]
    rows = n_page * per_page
    acc = [jnp.zeros((rows, NSA_KV_HEADS * CMP_HIDDEN), F32) for _ in range(2)]
    for t in range(CMP_BLOCK):
        for c in range(2):
            a = xbuf[c, pl.ds(t, rows, stride=CMP_BLOCK), :].astype(BF16)
            acc[c] = acc[c] + _dot(a, w1_ref[c, t])
    for c in range(2):
        o_ref[:, c * gd:(c + 1) * gd] = _dot(_silu(acc[c]).astype(BF16), w2_ref[c])


def nsa_compress_cache(cache_t, layer, pages, pe_tok, w1_bd, w2_bd, n_page):
    per_page = PAGE_SIZE // CMP_BLOCK
    gd = NSA_KV_HEADS * HEAD_DIM
    full = lambda a: pl.BlockSpec(a.shape, lambda i, pt: (0,) * a.ndim)
    page_blk = (None, 1, 2, NSA_KV_HEADS, HEAD_DIM, PAGE_SIZE)
    page_specs = [pl.BlockSpec(page_blk, lambda i, pt, j=j: (layer, pt[i * n_page + j], 0, 0, 0, 0))
                  for j in range(n_page)]
    return pl.pallas_call(
        functools.partial(_nsa_compress_cache_kernel, n_page=n_page),
        grid_spec=pltpu.PrefetchScalarGridSpec(
            num_scalar_prefetch=1,
            grid=(pages.shape[0] // n_page,),
            in_specs=[*page_specs, full(pe_tok), full(w1_bd), full(w2_bd)],
            out_specs=pl.BlockSpec((n_page * per_page, 2 * gd), lambda i, pt: (i, 0)),
            scratch_shapes=[pltpu.VMEM((2, n_page * PAGE_SIZE, gd), F32)],
        ),
        out_shape=jax.ShapeDtypeStruct((pages.shape[0] * per_page, 2 * gd), F32),
        compiler_params=_cparams(("arbitrary",)),
        name="nsa_compress_cache",
    )(pages, *([cache_t] * n_page), pe_tok, w1_bd, w2_bd)


def _masked_softmax3(s, mask):
    s = jnp.where(mask[None], s, -jnp.inf)
    m = jnp.max(s, axis=-1, keepdims=True)
    m = jnp.where(m > -jnp.inf, m, 0.0)
    p = jnp.exp(s - m)
    d = jnp.sum(p, axis=-1, keepdims=True)
    return p / jnp.where(d > 0, d, 1.0)


def _drain(gen):
    try:
        while True:
            next(gen)
    except StopIteration as stop:
        return stop.value


def _round_robin(gens):
    results = [None] * len(gens)
    live = list(range(len(gens)))
    while live:
        for i in list(live):
            try:
                next(gens[i])
            except StopIteration as stop:
                results[i] = stop.value
                live.remove(i)
        yield
    return results


def _run_interleaved(gens):
    return _drain(_round_robin(gens))


def _cmp_branch_and_select(*args, **kwargs):
    return _drain(_cmp_branch_and_select_stages(*args, **kwargs))


def _cmp_branch_and_select_stages(q2, kc, vc, qpos, qpos_row, nq, imp_sc, n_blk, pair_t=None):
    nc = kc.shape[0]
    s = _dot_nt(q2, kc).reshape(NSA_HG, nq, nc)
    yield
    col = lax.broadcasted_iota(jnp.int32, (1, nc), 1)
    cblk = col if pair_t is not None else 2 * (col % LANES) + col // LANES
    cmp_end = (cblk + 1) * CMP_BLOCK - 1
    p = _masked_softmax3(s, cmp_end <= qpos)
    yield
    o_c = _dot(p.reshape(NSA_HG * nq, nc).astype(BF16), vc).reshape(NSA_HG, nq, HEAD_DIM)
    imp = jnp.sum(p, axis=0)
    if pair_t is not None:
        hi, mid, lo = _split3(imp)
        imp = _dot_nt(pair_t, hi) + _dot_nt(pair_t, mid) + _dot_nt(pair_t, lo)
    else:
        imp = (imp[:, :LANES] + imp[:, LANES:]).T
    yield
    j = lax.broadcasted_iota(jnp.int32, (LANES, 1), 0)
    cur = qpos_row // SEL_BLOCK
    forced = jnp.where(j == 0, 1, jnp.where(j == cur, 1, jnp.where(j == cur - 1, 1, 0))) > 0
    causal = j <= cur
    imp = jnp.where(forced, imp + FORCE_BONUS, imp)
    imp = jnp.where(causal, imp, -jnp.inf)
    def count_ahead(i, ri, cnt):
        wins_ties = jnp.where(j > i, 1.0, 0.0)
        return cnt + jnp.where(ri >= imp, jnp.where(ri > imp, 1.0, wins_ties), 0.0)

    rank = jnp.zeros((LANES, nq), F32)
    if isinstance(n_blk, int):
        for i in range(n_blk):
            rank = count_ahead(i, imp[i:i + 1, :], rank)
    else:
        imp_sc[...] = imp
        rank = lax.fori_loop(0, n_blk, lambda i, c: count_ahead(i, imp_sc[pl.ds(i, 1), :], c), rank)
    sel = jnp.where(causal, jnp.where(rank < TOP_N, 1.0, 0.0), 0.0)
    return o_c, sel.T.astype(BF16)


def _flash_init(nq):
    return (jnp.full((NSA_HG, nq, 1), M_INIT, F32), jnp.zeros((NSA_HG, nq, 1), F32),
            jnp.zeros((NSA_HG, nq, HEAD_DIM), F32))


def _mask_bias(allowed):
    return jnp.where(allowed, 0.0, NEG_BIG)


def _flash_update_stages(carry, s, bias, v, v_transposed=False):
    m, l, acc = carry
    s = s + bias[None]
    m_new = jnp.maximum(m, jnp.max(s, axis=-1, keepdims=True))
    alpha = jnp.exp(m - m_new)
    p = jnp.exp(s - m_new)
    l = alpha * l + jnp.sum(p, axis=-1, keepdims=True)
    h, q, n = p.shape
    p2 = p.reshape(h * q, n).astype(v.dtype)
    yield
    pv = (_dot_nt(p2, v) if v_transposed else _dot(p2, v)).reshape(h, q, HEAD_DIM)
    return m_new, l, alpha * acc + pv


def _flash_finish(carry):
    _, l, acc = carry
    return acc / jnp.where(l > 0, l, 1.0)


def _gate_column(sig, lane, col):
    return jnp.sum(jnp.where(lane == col, sig, 0.0), axis=-1, keepdims=True)


def _softmax_rows(s_ref, b_ref, p_ref, m_ref, l_ref, a_ref, n, tq):
    rb = 2 * SUBLANES
    for r0 in range(0, s_ref.shape[0], rb):
        rows = slice(r0, r0 + rb)
        sb = s_ref[rows, :n] + b_ref[r0 % tq:r0 % tq + rb, :n]
        m_old = m_ref[rows, :]
        m_new = jnp.maximum(m_old, jnp.max(sb, axis=-1, keepdims=True))
        p = jnp.exp(sb - jnp.concatenate([m_new] * (n // LANES), axis=1))
        alpha = jnp.exp(m_old - m_new)
        l_ref[rows, :] = alpha * l_ref[rows, :] + jnp.sum(p, axis=-1, keepdims=True)
        m_ref[rows, :] = m_new
        a_ref[rows, :] = alpha
        p_ref[rows, :n] = p.astype(BF16)


def _nsa_attn_kernel(q_ref, kc_ref, vc_ref, ks_ref, vs_ref, kw_ref, vw_ref, gd_ref, za_ref, e_ref,
                     o_ref, s_sc, p_sc, b_sc, m_sc, l_sc, a_sc, acc_sc, imp_sc, *, tq, tk):
    g = pl.program_id(1)
    q0 = pl.program_id(2) * tq
    q2 = q_ref[0].reshape(NSA_HG * tq, HEAD_DIM)
    qpos = q0 + lax.broadcasted_iota(jnp.int32, (tq, 1), 0)
    qpos_row = q0 + lax.broadcasted_iota(jnp.int32, (1, tq), 1)
    n_blk = (q0 + tq - 1) // SEL_BLOCK + 1
    o_c, sel = _cmp_branch_and_select(q2, kc_ref[0, 0], vc_ref[0, 0], qpos, qpos_row, tq, imp_sc, n_blk)

    def reset():
        m_sc[...] = jnp.full(m_sc.shape, M_INIT, F32)
        l_sc[...] = jnp.zeros(l_sc.shape, F32)
        acc_sc[...] = jnp.zeros(acc_sc.shape, F32)

    def softmax_step(k, v, bias, n):
        b_sc[:, :n] = bias

        def head_stages(h):
            rows = pl.ds(h * tq, tq)
            s_sc[rows, :n] = _dot_nt(q2[h * tq:(h + 1) * tq], k)
            yield
            _softmax_rows(s_sc.at[rows], b_sc, p_sc.at[rows], m_sc.at[rows], l_sc.at[rows], a_sc.at[rows], n, tq)
            yield
            acc_sc[rows, :] = a_sc[rows, :HEAD_DIM] * acc_sc[rows, :] + _dot(p_sc[rows, :n], v)

        _run_interleaved([head_stages(h) for h in range(NSA_HG)])

    def finish():
        l = l_sc[:, :HEAD_DIM]
        return acc_sc[...] / jnp.where(l > 0, l, 1.0)

    def sel_tile(t, diagonal):
        k0 = pl.multiple_of(t * tk, tk)
        bias = (_dot(sel, e_ref[:, pl.ds(k0, tk)]) - 1.0) * (-NEG_BIG)
        if diagonal:
            kpos = k0 + lax.broadcasted_iota(jnp.int32, (1, tk), 1)
            bias = jnp.where(kpos <= qpos, bias, NEG_BIG)
        softmax_step(ks_ref[0, 0, pl.ds(k0, tk), :], vs_ref[0, 0, pl.ds(k0, tk), :], bias, tk)

    n_below = q0 // tk
    reset()

    def below(t, carry):
        sel_tile(t, False)
        return carry

    lax.fori_loop(0, n_below, below, 0)
    sel_tile(n_below, True)
    o_s = finish()

    wn = WINDOW + tq
    w0 = pl.multiple_of(jnp.maximum(q0 - WINDOW, 0), tq)
    dist = qpos - (w0 + lax.broadcasted_iota(jnp.int32, (1, wn), 1))
    reset()
    softmax_step(kw_ref[0, 0, pl.ds(w0, wn), :], vw_ref[0, 0, pl.ds(w0, wn), :],
                 _mask_bias(jnp.where(dist >= 0, dist, WINDOW) < WINDOW), wn)
    o_w = finish()


    sig = _sigmoid(gd_ref[...])
    lane = lax.broadcasted_iota(jnp.int32, sig.shape, 1)
    outs = []
    for h in range(NSA_HG):
        head = g * NSA_HG + h
        hrows = slice(h * tq, (h + 1) * tq)
        g_c = _gate_column(sig, lane, GD_GATE + head)
        g_s = _gate_column(sig, lane, GD_GATE + NSA_HEADS + head)
        g_w = _gate_column(sig, lane, GD_GATE + 2 * NSA_HEADS + head)
        outs.append(g_c * o_c[h] + g_s * o_s[hrows] + g_w * o_w[hrows])
    o_ref[...] = jnp.concatenate(outs, axis=-1) * _silu(za_ref[...])


def nsa_attn_prompt(proj, qh, kc, vc, ks, vs, kw, vw, e_mat, tq=128, tk=512):
    nb, _, seq, _ = qh.shape
    assert seq >= WINDOW + tq and tk % tq == 0 and tk <= WINDOW + tq
    rows, wn = NSA_HG * tq, WINDOW + tq
    nq = seq // tq
    gw = NSA_HG * HEAD_DIM
    kv_spec = pl.BlockSpec((1, 1, seq, HEAD_DIM), lambda b, g, i: (b, g, 0, 0))
    c_spec = pl.BlockSpec((1, 1, 2 * LANES, HEAD_DIM), lambda b, g, i: (b, g, 0, 0))
    return pl.pallas_call(
        functools.partial(_nsa_attn_kernel, tq=tq, tk=tk),
        grid=(nb, NSA_KV_HEADS, nq),
        in_specs=[
            pl.BlockSpec((1, NSA_HG, tq, HEAD_DIM), lambda b, g, i: (b, g, i, 0)),
            c_spec, c_spec, kv_spec, kv_spec, kv_spec, kv_spec,
            pl.BlockSpec((tq, LANES), lambda b, g, i: (b * nq + i, E_GD // LANES)),
            pl.BlockSpec((tq, gw), lambda b, g, i: (b * nq + i, E_ZA // gw + g)),
            pl.BlockSpec(e_mat.shape, lambda b, g, i: (0, 0)),
        ],
        out_specs=pl.BlockSpec((tq, gw), lambda b, g, i: (b * nq + i, g)),
        out_shape=jax.ShapeDtypeStruct((nb * seq, NSA_WIDTH), F32),
        scratch_shapes=[
            pltpu.VMEM((rows, wn), F32),
            pltpu.VMEM((rows, wn), BF16),
            pltpu.VMEM((tq, wn), F32),
            pltpu.VMEM((rows, LANES), F32),
            pltpu.VMEM((rows, LANES), F32),
            pltpu.VMEM((rows, LANES), F32),
            pltpu.VMEM((rows, HEAD_DIM), F32),
            pltpu.VMEM((LANES, tq), F32),
        ],
        compiler_params=_cparams(("arbitrary", "arbitrary", "arbitrary")),
        name="nsa_attn_prompt",
    )(qh, kc, vc, ks, vs, kw, vw, proj, proj, e_mat)


def _nsa_dec_kernel(pt_ref, q_ref, cmp_ref, *rest, past_len, n_pages, n_new):
    del pt_ref
    n_samp = q_ref.shape[0]
    ins = rest[n_samp * n_pages:n_samp * n_pages + 7]
    o_ref, wout_ref, imp_sc = rest[-3:]
    pwin_ref, wnew_ref = ins[1], ins[2]
    outs = _run_interleaved([
        _nsa_dec_sample(si, q_ref, cmp_ref, rest[si * n_pages:(si + 1) * n_pages], *ins, None, imp_sc,
                        past_len=past_len)
        for si in range(n_samp)])
    for si in range(n_samp):
        o_ref[si] = outs[si]
        wnew = wnew_ref[si]
        for c in range(2):
            for g in range(NSA_KV_HEADS):
                new_t = wnew[:, c * LANES + g * HEAD_DIM:c * LANES + (g + 1) * HEAD_DIM].T
                wout_ref[si, c, g] = jnp.concatenate([pwin_ref[si, c, g][:, n_new:], new_t[:, :n_new]], axis=1)


def _nsa_dec_sample(si, q_ref, cmp_ref, page_refs, rows_ref, pwin_ref, wnew_ref, gd_ref, za_ref, e_ref, pair_ref,
                    o_ref, imp_sc, *, past_len):
    del o_ref
    nq = DEC_PAD
    scale = HEAD_DIM ** -0.5
    qpos = past_len + lax.broadcasted_iota(jnp.int32, (nq, 1), 0)
    qpos_row = past_len + lax.broadcasted_iota(jnp.int32, (1, nq), 1)
    n_blk = (past_len + nq - 1) // SEL_BLOCK + 1
    q = q_ref[si]
    cmp = cmp_ref[si]

    def q_group(g):
        parts = [q[:, (g * NSA_HG + h) * HEAD_DIM:(g * NSA_HG + h + 1) * HEAD_DIM] for h in range(NSA_HG)]
        return jnp.concatenate(parts, axis=0) * scale

    qg = [q_group(g) for g in range(NSA_KV_HEADS)]

    rows = rows_ref[si]
    wnew = wnew_ref[si]
    win_keep = pwin_ref.shape[-1]
    sig = _sigmoid(gd_ref[si])
    lane = lax.broadcasted_iota(jnp.int32, sig.shape, 1)
    kpos = lax.broadcasted_iota(jnp.int32, (1, past_len), 1)
    npos = past_len + lax.broadcasted_iota(jnp.int32, (1, nq), 1)
    wpos = past_len - win_keep + lax.broadcasted_iota(jnp.int32, (1, win_keep), 1)
    outs = []
    for g in range(NSA_KV_HEADS):
        qb = qg[g].astype(BF16)
        kc = cmp[:, g * HEAD_DIM:(g + 1) * HEAD_DIM].astype(BF16)
        vc = cmp[:, LANES + g * HEAD_DIM:LANES + (g + 1) * HEAD_DIM].astype(BF16)
        o_c, sel = yield from _cmp_branch_and_select_stages(
            qb, kc, vc, qpos, qpos_row, nq, imp_sc.at[si * NSA_KV_HEADS + g], n_blk, pair_ref[...])
        yield
        kt = jnp.concatenate([r[0, 0, g] for r in page_refs], axis=1).astype(BF16)
        vt = jnp.concatenate([r[0, 1, g] for r in page_refs], axis=1).astype(BF16)
        s = _dot(qb, kt).reshape(NSA_HG, nq, past_len)
        blk_on = _dot(sel, e_ref[:, :past_len])
        yield
        bias = _mask_bias(jnp.where(kpos <= qpos, blk_on, 0.0) > 0.5)
        carry = yield from _flash_update_stages(_flash_init(nq), s, bias, vt, v_transposed=True)
        yield
        kn = rows[:, 2 * LANES + g * HEAD_DIM:2 * LANES + (g + 1) * HEAD_DIM]
        vn = rows[:, 3 * LANES + g * HEAD_DIM:3 * LANES + (g + 1) * HEAD_DIM]
        s = _dot_nt(qg[g], kn).reshape(NSA_HG, nq, nq)
        blk_on = _dot(sel, e_ref[:, past_len:past_len + LANES])[:, :nq]
        yield
        bias = _mask_bias(jnp.where(npos <= qpos, blk_on, 0.0) > 0.5)
        o_s = _flash_finish((yield from _flash_update_stages(carry, s, bias, vn)))
        yield
        s = _dot(qb, pwin_ref[si, 0, g].astype(BF16)).reshape(NSA_HG, nq, win_keep)
        yield
        dist = qpos - wpos
        bias = _mask_bias(jnp.where(wpos >= 0, jnp.where(dist >= 0, dist, WINDOW), WINDOW) < WINDOW)
        carry = yield from _flash_update_stages(_flash_init(nq), s, bias, pwin_ref[si, 1, g].astype(BF16),
                                                v_transposed=True)
        yield
        kwn = wnew[:, g * HEAD_DIM:(g + 1) * HEAD_DIM]
        vwn = wnew[:, LANES + g * HEAD_DIM:LANES + (g + 1) * HEAD_DIM]
        s = _dot_nt(qg[g], kwn).reshape(NSA_HG, nq, nq)
        yield
        dist = qpos - npos
        bias = _mask_bias(jnp.where(dist >= 0, dist, WINDOW) < WINDOW)
        o_w = _flash_finish((yield from _flash_update_stages(carry, s, bias, vwn)))
        for h in range(NSA_HG):
            head = g * NSA_HG + h
            g_c = _gate_column(sig, lane, GD_GATE + head)
            g_s = _gate_column(sig, lane, GD_GATE + NSA_HEADS + head)
            g_w = _gate_column(sig, lane, GD_GATE + 2 * NSA_HEADS + head)
            outs.append(g_c * o_c[h] + g_s * o_s[h] + g_w * o_w[h])
        yield
    return jnp.concatenate(outs, axis=-1) * _silu(za_ref[si])


def nsa_attn_decode(page_table, layer, qrot, cmp, cache_t, rows, pwin_t, wnew, proj3, e_mat, past_len, n_samp,
                    n_new, win_stack):
    nb, npg = page_table.shape
    win_keep = pwin_t.shape[-1]
    ncb = cmp.shape[1]
    per_b = lambda blk: pl.BlockSpec(blk, lambda b, pt: (b,) + (0,) * (len(blk) - 1))
    page_blk = (None, 1, 2, NSA_KV_HEADS, HEAD_DIM, PAGE_SIZE)
    page_specs = [pl.BlockSpec(page_blk, lambda b, pt, si=si, j=j: (layer, pt[b * n_samp + si, j], 1, 0, 0, 0))
                  for si in range(n_samp) for j in range(npg)]
    pair_t = (jnp.arange(ncb, dtype=jnp.int32)[None, :] // 2
              == jnp.arange(LANES, dtype=jnp.int32)[:, None]).astype(BF16)
    win_blk = (None, n_samp, 2, NSA_KV_HEADS, HEAD_DIM, win_keep)
    win_spec = pl.BlockSpec(win_blk, lambda b, pt: (layer, b, 0, 0, 0, 0))
    args = [page_table, qrot, cmp, *([cache_t] * (n_samp * npg)), rows, pwin_t, wnew, proj3, proj3, e_mat, pair_t]
    in_specs = [
        per_b((n_samp, DEC_PAD, NSA_WIDTH)),
        per_b((n_samp, ncb, 4 * HEAD_DIM)),
        *page_specs,
        per_b((n_samp, DEC_PAD, 4 * LANES)),
        win_spec,
        per_b((n_samp, DEC_PAD, 2 * LANES)),
        pl.BlockSpec((n_samp, DEC_PAD, LANES), lambda b, pt: (b, 0, E_GD // LANES)),
        pl.BlockSpec((n_samp, DEC_PAD, NSA_WIDTH), lambda b, pt: (b, 0, E_ZA // NSA_WIDTH)),
        pl.BlockSpec(e_mat.shape, lambda b, pt: (0, 0)),
        pl.BlockSpec(pair_t.shape, lambda b, pt: (0, 0)),
    ]
    aliases = {}
    if win_stack is not None:
        aliases = {len(args): 1}
        args.append(win_stack)
        in_specs.append(pl.BlockSpec(memory_space=pl.ANY))
    return pl.pallas_call(
        functools.partial(_nsa_dec_kernel, past_len=past_len, n_pages=npg, n_new=n_new),
        grid_spec=pltpu.PrefetchScalarGridSpec(
            num_scalar_prefetch=1,
            grid=(nb // n_samp,),
            in_specs=in_specs,
            out_specs=[per_b((n_samp, DEC_PAD, NSA_WIDTH)), win_spec],
            scratch_shapes=[pltpu.VMEM((n_samp * NSA_KV_HEADS, LANES, DEC_PAD), F32)],
        ),
        out_shape=[jax.ShapeDtypeStruct((nb, DEC_PAD, NSA_WIDTH), F32),
                   jax.ShapeDtypeStruct(pwin_t.shape, F32)],
        input_output_aliases=aliases,
        compiler_params=_cparams(("arbitrary",)),
        name="nsa_attn_decode",
    )(*args)


def _tri_masks(c):
    row = lax.broadcasted_iota(jnp.int32, (c, c), 0)
    col = lax.broadcasted_iota(jnp.int32, (c, c), 1)
    lower = row >= col
    return lower, jnp.where(lower, 1.0, 0.0).astype(BF16), jnp.where(row <= col, 1.0, 0.0).astype(BF16)


def _ssd_kernel(zb_ref, xbc_ref, gd_ref, prev_ref, h0_ref, cw_ref, cb_ref, dtb_ref, alog_ref, alog16_ref,
                dexp_ref, nw_ref, e16_ref, *rest, chunk, n_valid):
    y_ref, so_ref, h_sc, xbuf = rest[-4:]
    c = pl.program_id(1)
    n_seq = zb_ref.shape[0]
    hp = SSD_HEADS // SSD_GROUPS * SSD_HEAD_DIM
    halo = SUBLANES

    @pl.when(c == 0)
    def _():
        h_sc[...] = h0_ref[...]
        xbuf[:, 0:halo] = prev_ref[...]

    lower, tri, tri_t = _tri_masks(chunk)
    ones8 = jnp.ones((chunk, SUBLANES), BF16)
    lane = lax.broadcasted_iota(jnp.int32, (chunk, LANES), 1)
    low_half = lane < SSD_HEAD_DIM

    def seq_stages(b):
        xbuf[b, halo:halo + chunk] = xbc_ref[b]
        conv = cb_ref[...]
        for i in range(SSD_CONV):
            lo = halo - (SSD_CONV - 1) + i
            conv = conv + xbuf[b, lo:lo + chunk] * cw_ref[i:i + 1, :]
        xbuf[b, 0:halo] = xbuf[b, chunk:chunk + halo]
        act = _silu(conv)
        xs = act[:, :SSD_INNER]
        bm = act[:, SSD_INNER:SSD_INNER + SSD_GROUPS * SSD_STATE]
        cm = act[:, SSD_INNER + SSD_GROUPS * SSD_STATE:]

        dt16 = _softplus(gd_ref[b] + dtb_ref[...])[:, GD_DT:GD_DT + SSD_HEADS]
        if n_valid is not None:
            tok = c * chunk + lax.broadcasted_iota(jnp.int32, (chunk, 1), 0)
            dt16 = jnp.where(tok < n_valid, dt16, 0.0)
        yield
        dt = _dot3(dt16, e16_ref[...])
        da = dt * (-jnp.exp(alog_ref[...]))
        da16 = dt16 * (-jnp.exp(alog16_ref[...]))
        yield
        cum = _dot3_wx(tri, da)
        yield
        cum16 = _dot3_wx(tri, da16)
        cum16_t = _dot3_tn(da16, tri_t)
        yield
        decay_col = jnp.exp(_dot3_tn(da, ones8)[:, 0:1])
        xdt = xs * dt
        cum_last = cum[chunk - 1:chunk, :]
        xw = (xdt * jnp.exp(cum_last - cum)).astype(BF16)
        xdt_b = xdt.astype(BF16)

        y_intra = [None] * (SSD_INNER // LANES)
        y_inter = []
        for g in range(SSD_GROUPS):
            cg = cm[:, g * SSD_STATE:(g + 1) * SSD_STATE].astype(BF16)
            bg = bm[:, g * SSD_STATE:(g + 1) * SSD_STATE].astype(BF16)
            h_g = h_sc[b, g * hp:(g + 1) * hp, :]
            yield
            cb = _dot_nt(cg, bg)
            y_inter.append(_dot_nt(cg, h_g.astype(BF16)))
            for hh in range(SSD_HEADS // SSD_GROUPS):
                h = g * (SSD_HEADS // SSD_GROUPS) + hh
                diff = cum16[:, h:h + 1] - cum16_t[h:h + 1, :]
                lmat = jnp.where(lower, jnp.exp(jnp.where(lower, diff, 0.0)), 0.0)
                m = (cb * lmat).astype(BF16)
                pair = h // 2
                x_pair = xdt_b[:, pair * LANES:(pair + 1) * LANES]
                keep = low_half if h % 2 == 0 else jnp.logical_not(low_half)
                yield
                contrib = _dot(m, jnp.where(keep, x_pair, jnp.zeros_like(x_pair)))
                y_intra[pair] = contrib if y_intra[pair] is None else y_intra[pair] + contrib
            yield
            h_sc[b, g * hp:(g + 1) * hp, :] = (h_g * decay_col[g * hp:(g + 1) * hp, :]
                                               + _dot_tn(xw[:, g * hp:(g + 1) * hp], bg))
        y = jnp.concatenate(y_intra, axis=-1) + jnp.concatenate(y_inter, axis=-1) * jnp.exp(cum)
        y = y + xs * dexp_ref[...]
        y = y * _silu(zb_ref[b])
        ms = jnp.mean(y * y, axis=-1, keepdims=True)
        y_ref[b] = y * lax.rsqrt(ms + 1e-6) * nw_ref[...]

    _run_interleaved([seq_stages(b) for b in range(n_seq)])

    @pl.when(c == pl.num_programs(1) - 1)
    def _():
        so_ref[...] = h_sc[...]


def _stacked_state_io(stack, n_layers, blk, out_layer, n_in):
    shape = jax.ShapeDtypeStruct((n_layers,) + blk[0], F32)
    spec = pl.BlockSpec((None,) + blk[1], lambda b, c: (out_layer, b, 0, 0))
    if stack is None:
        return spec, shape, [], [], {}
    return spec, shape, [stack], [pl.BlockSpec(memory_space=pl.ANY)], {n_in: 1}


def ssd_mixer(proj, prev8, h0, layer, conv_w, conv_b, dtb_row, alog_exp, alog16, d_exp, norm_w, e16, nb, seq, chunk,
              n_seq, n_valid, stack, out_layer, n_layers):
    full = lambda a: pl.BlockSpec(a.shape, lambda b, c: (0,) * a.ndim)
    proj3 = proj.reshape(nb, seq, E_END)
    col_spec = lambda width, off: pl.BlockSpec((n_seq, chunk, width), lambda b, c: (b, c, off // width))
    in_specs = [
        col_spec(SSD_INNER, E_ZB), col_spec(SSD_CONV_CH, E_XBC), col_spec(LANES, E_GD),
        pl.BlockSpec((n_seq, SUBLANES, SSD_CONV_CH), lambda b, c: (b, 0, 0)),
        pl.BlockSpec((None, n_seq, SSD_INNER, SSD_STATE), lambda b, c: (layer, b, 0, 0)),
        full(conv_w), full(conv_b), full(dtb_row), full(alog_exp), full(alog16), full(d_exp), full(norm_w),
        full(e16),
    ]
    st_spec, st_shape, extra, extra_specs, aliases = _stacked_state_io(
        stack, n_layers, ((nb, SSD_INNER, SSD_STATE), (n_seq, SSD_INNER, SSD_STATE)), out_layer, len(in_specs))
    y, st = pl.pallas_call(
        functools.partial(_ssd_kernel, chunk=chunk, n_valid=n_valid),
        grid=(nb // n_seq, seq // chunk),
        in_specs=in_specs + extra_specs,
        out_specs=[pl.BlockSpec((n_seq, chunk, SSD_INNER), lambda b, c: (b, c, 0)), st_spec],
        out_shape=[jax.ShapeDtypeStruct((nb, seq, SSD_INNER), F32), st_shape],
        scratch_shapes=[pltpu.VMEM((n_seq, SSD_INNER, SSD_STATE), F32),
                        pltpu.VMEM((n_seq, SUBLANES + chunk, SSD_CONV_CH), F32)],
        input_output_aliases=aliases,
        compiler_params=_cparams(("arbitrary", "arbitrary")),
        name="ssd_mixer",
    )(proj3, proj3, proj3, prev8, h0, conv_w, conv_b, dtb_row, alog_exp, alog16, d_exp, norm_w, e16, *extra)
    return y.reshape(nb * seq, SSD_INNER), st


def _gla_kernel(q_ref, k_ref, v_ref, r_ref, glr_ref, s0_ref, wg_ref, bg_ref, nw_ref, *rest,
                chunk, n_sub, n_valid):
    o_ref, so_ref, s_sc = rest[-3:]
    c = pl.program_id(1)
    n_seq = q_ref.shape[0]
    span = n_sub * chunk

    @pl.when(c == 0)
    def _():
        s_sc[...] = s0_ref[...]

    row = lax.broadcasted_iota(jnp.int32, (span, span), 0)
    col = lax.broadcasted_iota(jnp.int32, (span, span), 1)
    lower = jnp.where(row // chunk == col // chunk, row - col, -1) >= 0
    tri = jnp.where(lower, 1.0, 0.0).astype(BF16)
    crow = lax.broadcasted_iota(jnp.int32, (span, n_sub * SUBLANES), 0) // chunk
    ccol = lax.broadcasted_iota(jnp.int32, (span, n_sub * SUBLANES), 1) // SUBLANES
    chunk_ones = jnp.where(crow == ccol, 1.0, 0.0).astype(BF16)
    nw = nw_ref[...]

    def head_stages(b, h, qg, kg, kd, v, r, total):
        kcols = slice(h * GLA_DK, (h + 1) * GLA_DK)
        vcols = slice(h * GLA_DV, (h + 1) * GLA_DV)
        att = jnp.where(lower, _dot_nt(qg[:, kcols], kg[:, kcols]), 0.0)
        yield
        o = _dot(att.astype(BF16), v[:, vcols])
        s_h = s_sc[b, kcols, :]
        carried = []
        for u in range(n_sub):
            rows = slice(u * chunk, (u + 1) * chunk)
            yield
            carried.append(_dot(qg[rows, kcols], s_h.astype(BF16)))
            decay = jnp.exp(total[kcols, u * SUBLANES:u * SUBLANES + 1])
            s_h = s_h * decay + _dot_tn(kd[rows, kcols], v[rows, vcols])
        s_sc[b, kcols, :] = s_h
        yield
        o = o + jnp.concatenate(carried, axis=0)
        ms = jnp.mean(o * o, axis=-1, keepdims=True)
        o = o * lax.rsqrt(ms + 1e-6) * nw
        o_ref[b, :, vcols] = o * _silu(r[:, vcols])

    def seq_stages(b):
        lg = _log_sigmoid(_dot(glr_ref[b].astype(BF16), wg_ref[...]) + bg_ref[...]) * (1.0 / GLA_TAU)
        k = k_ref[b]
        if n_valid is not None:
            tok = c * span + lax.broadcasted_iota(jnp.int32, (span, 1), 0)
            lg = jnp.where(tok < n_valid, lg, 0.0)
            k = jnp.where(tok < n_valid, k, 0.0)
        yield
        bcum = _dot3_wx(tri, lg)
        yield
        total = _dot3_tn(lg, chunk_ones)
        qg = (q_ref[b] * (GLA_DK ** -0.5) * jnp.exp(bcum)).astype(BF16)
        kg = (k * jnp.exp(-bcum)).astype(BF16)
        kd = jnp.concatenate(
            [k[u * chunk:(u + 1) * chunk] * jnp.exp(bcum[(u + 1) * chunk - 1:(u + 1) * chunk]
                                                    - bcum[u * chunk:(u + 1) * chunk]) for u in range(n_sub)],
            axis=0).astype(BF16)
        v = v_ref[b].astype(BF16)
        r = r_ref[b]
        yield
        yield from _round_robin([head_stages(b, h, qg, kg, kd, v, r, total) for h in range(GLA_HEADS)])

    _run_interleaved([seq_stages(b) for b in range(n_seq)])

    @pl.when(c == pl.num_programs(1) - 1)
    def _():
        so_ref[...] = s_sc[...]


def gla_mixer(proj, s0, layer, wg_pad, bg, norm_w, nb, seq, chunk, n_sub, n_seq, n_valid, stack, out_layer,
              n_layers):
    span = n_sub * chunk
    full = lambda a: pl.BlockSpec(a.shape, lambda b, c: (0,) * a.ndim)
    srows = GLA_HEADS * GLA_DK
    proj3 = proj.reshape(nb, seq, O_END)
    col_spec = lambda width, off: pl.BlockSpec((n_seq, span, width), lambda b, c: (b, c, off // width))
    in_specs = [
        col_spec(GLA_KEY_WIDTH, O_Q), col_spec(GLA_KEY_WIDTH, O_K), col_spec(GLA_VAL_WIDTH, O_V),
        col_spec(GLA_VAL_WIDTH, O_R), col_spec(LANES, O_GLR),
        pl.BlockSpec((None, n_seq, srows, GLA_DV), lambda b, c: (layer, b, 0, 0)),
        full(wg_pad), full(bg), full(norm_w),
    ]
    st_spec, st_shape, extra, extra_specs, aliases = _stacked_state_io(
        stack, n_layers, ((nb, srows, GLA_DV), (n_seq, srows, GLA_DV)), out_layer, len(in_specs))
    o, st = pl.pallas_call(
        functools.partial(_gla_kernel, chunk=chunk, n_sub=n_sub, n_valid=n_valid),
        grid=(nb // n_seq, seq // span),
        in_specs=in_specs + extra_specs,
        out_specs=[pl.BlockSpec((n_seq, span, GLA_VAL_WIDTH), lambda b, c: (b, c, 0)), st_spec],
        out_shape=[jax.ShapeDtypeStruct((nb, seq, GLA_VAL_WIDTH), F32), st_shape],
        scratch_shapes=[pltpu.VMEM((n_seq, srows, GLA_DV), F32)],
        input_output_aliases=aliases,
        compiler_params=_cparams(("arbitrary", "arbitrary")),
        name="gla_mixer",
    )(proj3, proj3, proj3, proj3, proj3, s0, wg_pad, bg, norm_w, *extra)
    return o.reshape(nb * seq, GLA_VAL_WIDTH), st


def _even_weight(w):
    q, g_a, kv, z_a, z_b, xbc, dt = jnp.split(w, list(np.cumsum(
        [NSA_WIDTH, 3 * NSA_HEADS, 6 * NSA_KV_HEADS * HEAD_DIM, NSA_WIDTH, SSD_INNER, SSD_CONV_CH])), axis=-1)
    pad = jnp.zeros((w.shape[0], E_END - E_GD - 3 * NSA_HEADS - SSD_HEADS), w.dtype)
    return jnp.concatenate([xbc, q, z_b, z_a, kv, g_a, dt, pad], axis=-1).astype(BF16)


def _odd_weight(w):
    q, k, v, glr, r = jnp.split(w, list(np.cumsum(
        [GLA_KEY_WIDTH, GLA_KEY_WIDTH, GLA_VAL_WIDTH, GLA_GATE_RANK])), axis=-1)
    pad = jnp.zeros((w.shape[0], O_END - O_GLR - GLA_GATE_RANK), w.dtype)
    return jnp.concatenate([q, k, v, r, glr, pad], axis=-1).astype(BF16)


def _rope_tables(pos):
    half = HEAD_DIM // 2
    inv = ROPE_THETA ** (-jnp.arange(half, dtype=F32) / half)
    ang = pos.astype(F32)[:, None] * inv[None, :]
    cos, sin = jnp.cos(ang), jnp.sin(ang)
    reps = LANES // HEAD_DIM
    return jnp.tile(jnp.concatenate([cos, cos], -1), (1, reps)), jnp.tile(jnp.concatenate([-sin, sin], -1), (1, reps))


def _compress_weights(pe, w1, w2):
    pe_tab = jnp.concatenate([pe[0], pe[0], pe[1], pe[1]], axis=-1)
    w1r = w1.reshape(2, CMP_BLOCK, HEAD_DIM, CMP_HIDDEN)
    w1_blk = jnp.zeros((CMP_BLOCK, 4 * HEAD_DIM, 4 * CMP_HIDDEN), F32)
    w2_blk = jnp.zeros((4 * CMP_HIDDEN, 4 * HEAD_DIM), F32)
    for part in range(4):
        src = part // 2
        w1_blk = w1_blk.at[:, part * HEAD_DIM:(part + 1) * HEAD_DIM,
                           part * CMP_HIDDEN:(part + 1) * CMP_HIDDEN].set(w1r[src])
        w2_blk = w2_blk.at[part * CMP_HIDDEN:(part + 1) * CMP_HIDDEN,
                           part * HEAD_DIM:(part + 1) * HEAD_DIM].set(w2[src])
    return pe_tab, w1_blk.astype(BF16), w2_blk.astype(BF16)


def _compress_weights_cache(pe, w1, w2):
    w1r = w1.reshape(2, CMP_BLOCK, HEAD_DIM, CMP_HIDDEN)
    w1_bd = jnp.zeros((2, CMP_BLOCK, NSA_KV_HEADS * HEAD_DIM, NSA_KV_HEADS * CMP_HIDDEN), F32)
    w2_bd = jnp.zeros((2, NSA_KV_HEADS * CMP_HIDDEN, NSA_KV_HEADS * HEAD_DIM), F32)
    for g in range(NSA_KV_HEADS):
        w1_bd = w1_bd.at[:, :, g * HEAD_DIM:(g + 1) * HEAD_DIM, g * CMP_HIDDEN:(g + 1) * CMP_HIDDEN].set(w1r)
        w2_bd = w2_bd.at[:, g * CMP_HIDDEN:(g + 1) * CMP_HIDDEN, g * HEAD_DIM:(g + 1) * HEAD_DIM].set(w2)
    pe_tok = jnp.tile(jnp.concatenate([pe] * NSA_KV_HEADS, axis=-1), (1, PAGE_SIZE // CMP_BLOCK, 1))
    return pe_tok, w1_bd.astype(BF16), w2_bd.astype(BF16)


def _cmp_layout(cmp, nb, nc, dtype):
    c = cmp.reshape(nb, nc, 2, NSA_KV_HEADS, HEAD_DIM).transpose(2, 0, 3, 1, 4)
    halves = []
    for par in range(2):
        h = c[:, :, :, par::2]
        halves.append(jnp.pad(h, ((0, 0), (0, 0), (0, 0), (0, LANES - h.shape[3]), (0, 0))))
    c = jnp.concatenate(halves, axis=3).astype(dtype)
    return c[0], c[1]


def _per_step(n, want):
    return want if n % want == 0 else 1


def _sel_expand(n_keys):
    blk = jnp.arange(n_keys, dtype=jnp.int32) // SEL_BLOCK
    return (blk[None, :] == jnp.arange(LANES, dtype=jnp.int32)[:, None]).astype(BF16)


def kernel(x_prompt, x_sample, c_prompt, c_sample, cache_nsa, cache_nsa_win, state_ssd_conv, state_ssd, state_gla, page_table, norm_w, w_mod, b_mod, w_in_even, w_out_even, nsa_cmp_pe, nsa_cmp_w1, nsa_cmp_w2, ssd_conv_w, ssd_conv_b, ssd_dt_bias, ssd_a_log, ssd_d, ssd_norm_w, w_in_odd, gla_w_gate2, gla_b_gate, gla_norm_w, w_out_odd, final_norm_w):
    bp, sp, d = x_prompt.shape
    bs, ss, _ = x_sample.shape
    depth = norm_w.shape[0]
    npg = page_table.shape[1]
    past_len = npg * PAGE_SIZE
    assert ss <= cache_nsa_win.shape[2]
    assert sp % 512 == 0 and sp // SEL_BLOCK <= LANES and sp // CMP_BLOCK <= 2 * LANES
    assert ss <= DEC_PAD and ss < CMP_BLOCK and past_len % SEL_BLOCK == 0 and ss >= SSD_CONV - 1
    assert past_len // SEL_BLOCK + 1 <= LANES and (bs * DEC_PAD) % TM_IN == 0 and (bs * npg) % CMP_PAGES == 0
    tp, td = bp * sp, bs * DEC_PAD
    tm_d = TM_IN
    seq_ps, samp_ps = _per_step(bp, SEQ_PER_STEP), _per_step(bs, SAMPLES_PER_STEP)

    c_all = jnp.concatenate([c_prompt, c_sample], axis=0)
    c_all = jnp.pad(c_all, ((0, -c_all.shape[0] % SUBLANES), (0, 0)))
    mod = mod_all(c_all, w_mod.astype(BF16), b_mod)

    def mods(l):
        shift, scale, gate = jnp.split(mod[l], 3, axis=-1)
        mp = [m[:bp].reshape(bp, 1, d) for m in (shift, scale, gate)]
        ms = [jnp.repeat(m[bp:bp + bs], DEC_PAD, axis=0).reshape(td // tm_d, tm_d, d) for m in (shift, scale, gate)]
        return mp, ms

    xp = x_prompt.reshape(tp, d)
    xs = jnp.pad(x_sample, ((0, 0), (0, DEC_PAD - ss), (0, 0))).reshape(td, d)

    cos_p, sin_p = _rope_tables(jnp.arange(sp, dtype=jnp.int32))
    cos_s, sin_s = _rope_tables(past_len + jnp.arange(DEC_PAD, dtype=jnp.int32))
    cos_s, sin_s = jnp.tile(cos_s, (tm_d // DEC_PAD, 1)), jnp.tile(sin_s, (tm_d // DEC_PAD, 1))
    cache_t = jnp.transpose(cache_nsa, (0, 1, 3, 4, 5, 2))
    win_t = jnp.transpose(cache_nsa_win, (0, 1, 3, 4, 5, 2))
    ssd_h0_s = state_ssd.reshape(state_ssd.shape[0], bs, SSD_INNER, SSD_STATE)
    gla_s0_s = state_gla.reshape(state_gla.shape[0], bs, GLA_HEADS * GLA_DK, GLA_DV)
    e_mat_p = _sel_expand(sp)
    e_mat_s = _sel_expand(past_len + LANES)
    e16 = (jnp.arange(SSD_INNER, dtype=jnp.int32)[None, :] // SSD_HEAD_DIM
           == jnp.arange(SSD_HEADS, dtype=jnp.int32)[:, None]).astype(BF16)

    outs = {k: [] for k in ("kv_p", "kv_s", "win_p", "cv_p", "cv_s")}
    stacks = dict.fromkeys(("win_s", "ss_p", "ss_s", "gl_p", "gl_s"))
    n_even, n_odd = (depth + 1) // 2, depth // 2
    for l in range(depth):
        e = l // 2
        (shift_p, scale_p, gate_p), (shift_s, scale_s, gate_s) = mods(l)
        last = l == depth - 1
        if l % 2 == 0:
            w_in = _even_weight(w_in_even[e])
            w_out = w_out_even[e].astype(BF16)
            w_out_a, w_out_b = w_out[:NSA_WIDTH], w_out[NSA_WIDTH:]
            pe_tab, w1_blk, w2_blk = _compress_weights(nsa_cmp_pe[e], nsa_cmp_w1[e], nsa_cmp_w2[e])
            conv_w, conv_b = ssd_conv_w[e], ssd_conv_b[e].reshape(1, SSD_CONV_CH)
            dtb_row = jnp.zeros((1, LANES), F32).at[0, GD_DT:GD_DT + SSD_HEADS].set(ssd_dt_bias[e])
            alog16 = ssd_a_log[e].reshape(1, SSD_HEADS)
            alog_exp = jnp.repeat(ssd_a_log[e], SSD_HEAD_DIM).reshape(1, SSD_INNER)
            d_exp = jnp.repeat(ssd_d[e], SSD_HEAD_DIM).reshape(1, SSD_INNER)
            ssd_nw = ssd_norm_w[e].reshape(1, SSD_INNER)
            ssd_args = (conv_w, conv_b, dtb_row, alog_exp, alog16, d_exp, ssd_nw, e16)

            proj = inproj(xp, norm_w[l], scale_p, shift_p, w_in, TM_IN, sp)
            rows, win, _, qh, ks, vs, kw, vw = nsa_prep(proj, cos_p, sin_p, TM_IN, sp, True)
            nc = sp // CMP_BLOCK
            cmp = nsa_compress(rows.reshape(tp // CMP_BLOCK, CMP_BLOCK * 4 * LANES), pe_tab, w1_blk, w2_blk,
                               min(CMP_ROWS, tp // CMP_BLOCK))
            kc, vc = _cmp_layout(cmp, bp, nc, BF16)
            o_a = nsa_attn_prompt(proj, qh, kc, vc, ks, vs, kw, vw, e_mat_p)
            y, stacks["ss_p"] = ssd_mixer(proj, jnp.zeros((bp, SUBLANES, SSD_CONV_CH), F32),
                                          jnp.zeros((1, bp, SSD_INNER, SSD_STATE), F32), 0, *ssd_args, bp, sp,
                                          SSD_CHUNK, seq_ps, None, stacks["ss_p"], e, n_even)
            xp = outproj([o_a, y], [w_out_a, w_out_b], xp, gate_p, final_norm_w, TM_OUT, sp, last)
            outs["kv_p"].append(rows.reshape(bp, sp, 4, NSA_KV_HEADS, HEAD_DIM))
            outs["win_p"].append(win.reshape(bp, sp, 2, NSA_KV_HEADS, HEAD_DIM)[:, -min(WINDOW, sp):])
            outs["cv_p"].append(proj.reshape(bp, sp, E_END)[:, -(SSD_CONV - 1):, E_XBC:E_XBC + SSD_CONV_CH])

            proj = inproj(xs, norm_w[l], scale_s, shift_s, w_in, tm_d, tm_d)
            rows, win, qrot = nsa_prep(proj, cos_s, sin_s, tm_d, DEC_PAD, False)
            pe_tok, w1_bd, w2_bd = _compress_weights_cache(nsa_cmp_pe[e], nsa_cmp_w1[e], nsa_cmp_w2[e])
            cmp_s = nsa_compress_cache(cache_t, e, page_table.reshape(-1), pe_tok, w1_bd, w2_bd, CMP_PAGES)
            per_page = PAGE_SIZE // CMP_BLOCK
            o_a, stacks["win_s"] = nsa_attn_decode(
                page_table, e, qrot.reshape(bs, DEC_PAD, NSA_WIDTH),
                cmp_s.reshape(bs, npg * per_page, 4 * HEAD_DIM), cache_t, rows.reshape(bs, DEC_PAD, 4 * LANES),
                win_t, win.reshape(bs, DEC_PAD, 2 * LANES), proj.reshape(bs, DEC_PAD, E_END), e_mat_s, past_len,
                samp_ps, ss, stacks["win_s"])
            o_a = o_a.reshape(td, NSA_WIDTH)
            prev8 = jnp.pad(state_ssd_conv[e], ((0, 0), (SUBLANES - (SSD_CONV - 1), 0), (0, 0)))
            y, stacks["ss_s"] = ssd_mixer(proj, prev8, ssd_h0_s, e, *ssd_args, bs, DEC_PAD, DEC_PAD,
                                          samp_ps, ss, stacks["ss_s"], e, n_even)
            xs = outproj([o_a, y], [w_out_a, w_out_b], xs, gate_s, final_norm_w, tm_d, tm_d, last)
            outs["kv_s"].append(rows.reshape(bs, DEC_PAD, 4, NSA_KV_HEADS, HEAD_DIM)[:, :ss])
            new_xbc = proj.reshape(bs, DEC_PAD, E_END)[:, :ss, E_XBC:E_XBC + SSD_CONV_CH]
            outs["cv_s"].append(jnp.concatenate([state_ssd_conv[e], new_xbc], axis=1)[:, -(SSD_CONV - 1):])
        else:
            w_in = _odd_weight(w_in_odd[e])
            w_out = w_out_odd[e].astype(BF16)
            wg_pad = jnp.pad(gla_w_gate2[e], ((0, LANES - GLA_GATE_RANK), (0, 0))).astype(BF16)
            bg = gla_b_gate[e].reshape(1, GLA_KEY_WIDTH)
            gnw = gla_norm_w[e].reshape(1, GLA_DV)
            srows = GLA_HEADS * GLA_DK
            proj = inproj(xp, norm_w[l], scale_p, shift_p, w_in, TM_IN, sp)
            o, stacks["gl_p"] = gla_mixer(proj, jnp.zeros((1, bp, srows, GLA_DV), F32), 0, wg_pad, bg, gnw, bp, sp,
                                          GLA_CHUNK, GLA_SUB, seq_ps, None, stacks["gl_p"], e, n_odd)
            xp = outproj([o], [w_out], xp, gate_p, final_norm_w, TM_OUT, sp, last)
            proj = inproj(xs, norm_w[l], scale_s, shift_s, w_in, tm_d, tm_d)
            o, stacks["gl_s"] = gla_mixer(proj, gla_s0_s, e, wg_pad, bg, gnw, bs, DEC_PAD, DEC_PAD, 1,
                                          samp_ps, ss, stacks["gl_s"], e, n_odd)
            xs = outproj([o], [w_out], xs, gate_s, final_norm_w, tm_d, tm_d, last)

    y_prompt = xp.reshape(bp, sp, d)
    y_sample = xs.reshape(bs, DEC_PAD, d)[:, :ss]
    st = {k: jnp.stack(v) for k, v in outs.items()}
    st["win_s"] = jnp.transpose(stacks["win_s"], (0, 1, 5, 2, 3, 4))
    st["ss_p"] = stacks["ss_p"].reshape(n_even, bp, SSD_HEADS, SSD_HEAD_DIM, SSD_STATE)
    st["ss_s"] = stacks["ss_s"].reshape(n_even, bs, SSD_HEADS, SSD_HEAD_DIM, SSD_STATE)
    st["gl_p"] = stacks["gl_p"].reshape(n_odd, bp, GLA_HEADS, GLA_DK, GLA_DV)
    st["gl_s"] = stacks["gl_s"].reshape(n_odd, bs, GLA_HEADS, GLA_DK, GLA_DV)
    return (y_prompt, y_sample, st["kv_p"], st["kv_s"], st["win_p"], st["win_s"], st["cv_p"], st["cv_s"],
            st["ss_p"], st["ss_s"], st["gl_p"], st["gl_s"])
```

```python
import functools

import jax
import jax.numpy as jnp
import numpy as np
from jax import lax
from jax.experimental import pallas as pl
from jax.experimental.pallas import tpu as pltpu

F32 = jnp.float32
BF16 = jnp.bfloat16

PAGE_SIZE = 128
NSA_HEADS = 8
NSA_KV_HEADS = 2
HEAD_DIM = 64
NSA_WIDTH = NSA_HEADS * HEAD_DIM
NSA_HG = NSA_HEADS // NSA_KV_HEADS
CMP_BLOCK = 32
CMP_HIDDEN = 2 * HEAD_DIM
SEL_BLOCK = 64
TOP_N = 16
WINDOW = 512
FORCE_BONUS = 1.0e4
ROPE_THETA = 10000.0
SSD_HEADS = 16
SSD_HEAD_DIM = 64
SSD_INNER = SSD_HEADS * SSD_HEAD_DIM
SSD_GROUPS = 2
SSD_STATE = 128
SSD_CONV = 4
SSD_CONV_CH = SSD_INNER + 2 * SSD_GROUPS * SSD_STATE
GLA_HEADS = 4
GLA_DK = 128
GLA_DV = 256
GLA_KEY_WIDTH = GLA_HEADS * GLA_DK
GLA_VAL_WIDTH = GLA_HEADS * GLA_DV
GLA_GATE_RANK = 16
GLA_TAU = 16.0
GLA_CHUNK = 32
SSD_CHUNK = 64

LANES = 128
SUBLANES = 8
VMEM_LIMIT = 56 * 1024 * 1024

TM_IN = 256
TM_OUT = 512
CMP_ROWS = 128
CMP_PAGES = 32
SEQ_PER_STEP = 2
SAMPLES_PER_STEP = 4
GLA_SUB = 4

NEG_BIG = -1.0e30
M_INIT = -1.0e29
DEC_PAD = 8

E_XBC, E_Q, E_ZB, E_ZA, E_KV, E_GD, E_END = 0, 1536, 2048, 3072, 3584, 4352, 4480
GD_GATE, GD_DT = 0, 24
O_Q, O_K, O_V, O_R, O_GLR, O_END = 0, 512, 1024, 2048, 3072, 3200


def _cparams(sem):
    return pltpu.CompilerParams(dimension_semantics=sem, vmem_limit_bytes=VMEM_LIMIT)


def _dot(a, b):
    return jnp.dot(a, b, preferred_element_type=F32)


def _dot_nt(a, b):
    return lax.dot_general(a, b, (((1,), (1,)), ((), ())), preferred_element_type=F32)


def _dot_tn(a, b):
    return lax.dot_general(a, b, (((0,), (0,)), ((), ())), preferred_element_type=F32)


def _split3(x):
    hi = x.astype(BF16)
    r1 = x - hi.astype(F32)
    mid = r1.astype(BF16)
    lo = (r1 - mid.astype(F32)).astype(BF16)
    return hi, mid, lo


def _dot3(x, w01):
    hi, mid, lo = _split3(x)
    return _dot(hi, w01) + _dot(mid, w01) + _dot(lo, w01)


def _dot3_tn(x, w01):
    hi, mid, lo = _split3(x)
    return _dot_tn(hi, w01) + _dot_tn(mid, w01) + _dot_tn(lo, w01)


def _dot3_wx(w01, x):
    hi, mid, lo = _split3(x)
    return _dot(w01, hi) + _dot(w01, mid) + _dot(w01, lo)


def _silu(x):
    return x * (1.0 / (1.0 + jnp.exp(-x)))


def _sigmoid(x):
    return 1.0 / (1.0 + jnp.exp(-x))


def _softplus(x):
    return jnp.maximum(x, 0.0) + jnp.log1p(jnp.exp(-jnp.abs(x)))


def _log_sigmoid(x):
    return jnp.minimum(x, 0.0) - jnp.log1p(jnp.exp(-jnp.abs(x)))


def _mod_kernel(c_ref, w_ref, b_ref, o_ref):
    a = _silu(c_ref[...]).astype(BF16)
    o_ref[0] = _dot(a, w_ref[0]) + b_ref[0]


def mod_all(c_all, w_mod_bf, b_mod, tn=1024):
    nl, d, n = w_mod_bf.shape
    m = c_all.shape[0]
    return pl.pallas_call(
        _mod_kernel,
        grid=(nl, n // tn),
        in_specs=[
            pl.BlockSpec((m, d), lambda l, j: (0, 0)),
            pl.BlockSpec((1, d, tn), lambda l, j: (l, 0, j)),
            pl.BlockSpec((1, 1, tn), lambda l, j: (l, 0, j)),
        ],
        out_specs=pl.BlockSpec((1, m, tn), lambda l, j: (l, 0, j)),
        out_shape=jax.ShapeDtypeStruct((nl, m, n), F32),
        compiler_params=_cparams(("arbitrary", "arbitrary")),
        name="mod_all",
    )(c_all, w_mod_bf, b_mod.reshape(nl, 1, n))


def _inproj_kernel(x_ref, nw_ref, sc_ref, sh_ref, w_ref, o_ref, *, n_chunk):
    x = x_ref[...]
    ms = jnp.mean(x * x, axis=-1, keepdims=True)
    y = x * lax.rsqrt(ms + 1e-6) * nw_ref[...]
    h = (y * (1.0 + sc_ref[0]) + sh_ref[0]).astype(BF16)
    n = o_ref.shape[1]
    for n0 in range(0, n, n_chunk):
        n1 = min(n0 + n_chunk, n)
        o_ref[:, n0:n1] = _dot(h, w_ref[:, n0:n1])


def inproj(x, nw, scale, shift, w_bf, tm, rows_per_mod):
    t, d = x.shape
    n = w_bf.shape[1]
    r = scale.shape[1]
    mod_map = lambda i: ((i * tm) // rows_per_mod, 0, 0)
    return pl.pallas_call(
        functools.partial(_inproj_kernel, n_chunk=640),
        grid=(t // tm,),
        in_specs=[
            pl.BlockSpec((tm, d), lambda i: (i, 0)),
            pl.BlockSpec((1, d), lambda i: (0, 0)),
            pl.BlockSpec((1, r, d), mod_map),
            pl.BlockSpec((1, r, d), mod_map),
            pl.BlockSpec((d, n), lambda i: (0, 0)),
        ],
        out_specs=pl.BlockSpec((tm, n), lambda i: (i, 0)),
        out_shape=jax.ShapeDtypeStruct((t, n), F32),
        compiler_params=_cparams(("arbitrary",)),
        name="inproj",
    )(x, nw.reshape(1, d), scale, shift, w_bf)


def _outproj_kernel(*refs, n_in, final_norm):
    a_refs = refs[:n_in]
    w_refs = refs[n_in:2 * n_in]
    x_ref, g_ref, fw_ref, o_ref = refs[2 * n_in:]
    acc = _dot(a_refs[0][...].astype(BF16), w_refs[0][...])
    for a_ref, w_ref in zip(a_refs[1:], w_refs[1:]):
        acc = acc + _dot(a_ref[...].astype(BF16), w_ref[...])
    y = x_ref[...] + g_ref[0] * acc
    if final_norm:
        ms = jnp.mean(y * y, axis=-1, keepdims=True)
        y = y * lax.rsqrt(ms + 1e-6) * fw_ref[...]
    o_ref[...] = y


def outproj(a_list, w_list, x, gate, final_w, tm, rows_per_mod, final_norm):
    t, d = x.shape
    r = gate.shape[1]
    n_in = len(a_list)
    in_specs = [pl.BlockSpec((tm, a.shape[1]), lambda i: (i, 0)) for a in a_list]
    in_specs += [pl.BlockSpec(w.shape, lambda i: (0, 0)) for w in w_list]
    in_specs += [
        pl.BlockSpec((tm, d), lambda i: (i, 0)),
        pl.BlockSpec((1, r, d), lambda i: ((i * tm) // rows_per_mod, 0, 0)),
        pl.BlockSpec((1, d), lambda i: (0, 0)),
    ]
    return pl.pallas_call(
        functools.partial(_outproj_kernel, n_in=n_in, final_norm=final_norm),
        grid=(t // tm,),
        in_specs=in_specs,
        out_specs=pl.BlockSpec((tm, d), lambda i: (i, 0)),
        out_shape=jax.ShapeDtypeStruct((t, d), F32),
        compiler_params=_cparams(("arbitrary",)),
        name="outproj",
    )(*a_list, *w_list, x, gate, final_w.reshape(1, d))


def _rope_tile(t, cos, sin_signed, lane):
    fwd = pltpu.roll(t, LANES - HEAD_DIM // 2, 1)
    bwd = pltpu.roll(t, HEAD_DIM // 2, 1)
    partner = jnp.where((lane % HEAD_DIM) < HEAD_DIM // 2, fwd, bwd)
    return t * cos + partner * sin_signed


def _nsa_prep_kernel(q_ref, kv01_ref, kv23_ref, kv45_ref, cos_ref, sin_ref,
                     rows_ref, win_ref, qrot_ref, *hm_refs, head_major):
    cos = cos_ref[...]
    sin = sin_ref[...]
    lane = lax.broadcasted_iota(jnp.int32, cos.shape, 1)
    rope = lambda t: _rope_tile(t, cos, sin, lane)
    kv01, kv23, kv45 = kv01_ref[...], kv23_ref[...], kv45_ref[...]
    k_slc = rope(kv23[:, :LANES])
    v_slc = kv23[:, LANES:]
    k_win = rope(kv45[:, :LANES])
    v_win = kv45[:, LANES:]
    rows_ref[:, 0:LANES] = rope(kv01[:, :LANES])
    rows_ref[:, LANES:2 * LANES] = kv01[:, LANES:]
    rows_ref[:, 2 * LANES:3 * LANES] = k_slc
    rows_ref[:, 3 * LANES:4 * LANES] = v_slc
    win_ref[:, 0:LANES] = k_win
    win_ref[:, LANES:2 * LANES] = v_win
    q = q_ref[...]
    qr = [rope(q[:, j * LANES:(j + 1) * LANES]) for j in range(NSA_WIDTH // LANES)]
    for j, t in enumerate(qr):
        qrot_ref[:, j * LANES:(j + 1) * LANES] = t
    if head_major:
        qh_ref, ks_ref, vs_ref, kw_ref, vw_ref = hm_refs
        scale = HEAD_DIM ** -0.5
        for h in range(NSA_HEADS):
            lo = (h % 2) * HEAD_DIM
            qh_ref[0, h] = (qr[h // 2][:, lo:lo + HEAD_DIM] * scale).astype(BF16)
        for g in range(NSA_KV_HEADS):
            lo = g * HEAD_DIM
            ks_ref[0, g] = k_slc[:, lo:lo + HEAD_DIM].astype(BF16)
            vs_ref[0, g] = v_slc[:, lo:lo + HEAD_DIM].astype(BF16)
            kw_ref[0, g] = k_win[:, lo:lo + HEAD_DIM].astype(BF16)
            vw_ref[0, g] = v_win[:, lo:lo + HEAD_DIM].astype(BF16)


def nsa_prep(proj, cos_tab, sin_tab, tm, seq, head_major):
    t = proj.shape[0]
    tab_blocks = cos_tab.shape[0] // tm
    sb = seq // tm
    out_shape = [jax.ShapeDtypeStruct((t, 4 * LANES), F32),
                 jax.ShapeDtypeStruct((t, 2 * LANES), F32),
                 jax.ShapeDtypeStruct((t, NSA_WIDTH), F32)]
    out_specs = [pl.BlockSpec((tm, 4 * LANES), lambda i: (i, 0)),
                 pl.BlockSpec((tm, 2 * LANES), lambda i: (i, 0)),
                 pl.BlockSpec((tm, NSA_WIDTH), lambda i: (i, 0))]
    if head_major:
        nb = t // seq
        hm_map = lambda i: (i // sb, 0, i % sb, 0)
        out_shape.append(jax.ShapeDtypeStruct((nb, NSA_HEADS, seq, HEAD_DIM), BF16))
        out_specs.append(pl.BlockSpec((1, NSA_HEADS, tm, HEAD_DIM), hm_map))
        for _ in range(4):
            out_shape.append(jax.ShapeDtypeStruct((nb, NSA_KV_HEADS, seq, HEAD_DIM), BF16))
            out_specs.append(pl.BlockSpec((1, NSA_KV_HEADS, tm, HEAD_DIM), hm_map))
    kvb = E_KV // (2 * LANES)
    return pl.pallas_call(
        functools.partial(_nsa_prep_kernel, head_major=head_major),
        grid=(t // tm,),
        in_specs=[
            pl.BlockSpec((tm, NSA_WIDTH), lambda i: (i, E_Q // NSA_WIDTH)),
            pl.BlockSpec((tm, 2 * LANES), lambda i: (i, kvb)),
            pl.BlockSpec((tm, 2 * LANES), lambda i: (i, kvb + 1)),
            pl.BlockSpec((tm, 2 * LANES), lambda i: (i, kvb + 2)),
            pl.BlockSpec((tm, LANES), lambda i: (i % tab_blocks, 0)),
            pl.BlockSpec((tm, LANES), lambda i: (i % tab_blocks, 0)),
        ],
        out_specs=out_specs,
        out_shape=out_shape,
        compiler_params=_cparams(("arbitrary",)),
        name="nsa_prep",
    )(proj, proj, proj, proj, cos_tab, sin_tab)


def _nsa_compress_kernel(x_ref, pe_ref, w1_ref, w2_ref, o_ref):
    r = x_ref.shape[0]
    row_w = 4 * LANES
    acc = jnp.zeros((r, 4 * CMP_HIDDEN), F32)
    for t in range(CMP_BLOCK):
        xt = x_ref[:, t * row_w:t * row_w + 2 * LANES] + pe_ref[t:t + 1, :]
        acc = acc + _dot(xt.astype(BF16), w1_ref[t])
    o_ref[...] = _dot(_silu(acc).astype(BF16), w2_ref[...])


def nsa_compress(xblk, pe_tab, w1_blk, w2_blk, r):
    nb, kdim = xblk.shape
    return pl.pallas_call(
        _nsa_compress_kernel,
        grid=(nb // r,),
        in_specs=[
            pl.BlockSpec((r, kdim), lambda i: (i, 0)),
            pl.BlockSpec(pe_tab.shape, lambda i: (0, 0)),
            pl.BlockSpec(w1_blk.shape, lambda i: (0, 0, 0)),
            pl.BlockSpec(w2_blk.shape, lambda i: (0, 0)),
        ],
        out_specs=pl.BlockSpec((r, 4 * HEAD_DIM), lambda i: (i, 0)),
        out_shape=jax.ShapeDtypeStruct((nb, 4 * HEAD_DIM), F32),
        compiler_params=_cparams(("arbitrary",)),
        name="nsa_compress",
    )(xblk, pe_tab, w1_blk, w2_blk)


def _nsa_compress_cache_kernel(pt_ref, *refs, n_page):
    del pt_ref
    page_refs = refs[:n_page]
    pe_ref, w1_ref, w2_ref, o_ref, xbuf = refs[n_page:]
    gd = NSA_KV_HEADS * HEAD_DIM
    per_page = PAGE_SIZE // CMP_BLOCK
    for p, x_ref in enumerate(page_refs):
        for c in range(2):
            xbuf[c, p * PAGE_SIZE:(p + 1) * PAGE_SIZE, :] = x_ref[0, c].reshape(gd, PAGE_SIZE).T + pe_ref[c]
    rows = n_page * per_page
    acc = [jnp.zeros((rows, NSA_KV_HEADS * CMP_HIDDEN), F32) for _ in range(2)]
    for t in range(CMP_BLOCK):
        for c in range(2):
            a = xbuf[c, pl.ds(t, rows, stride=CMP_BLOCK), :].astype(BF16)
            acc[c] = acc[c] + _dot(a, w1_ref[c, t])
    for c in range(2):
        o_ref[:, c * gd:(c + 1) * gd] = _dot(_silu(acc[c]).astype(BF16), w2_ref[c])


def nsa_compress_cache(cache_t, layer, pages, pe_tok, w1_bd, w2_bd, n_page):
    per_page = PAGE_SIZE // CMP_BLOCK
    gd = NSA_KV_HEADS * HEAD_DIM
    full = lambda a: pl.BlockSpec(a.shape, lambda i, pt: (0,) * a.ndim)
    page_blk = (None, 1, 2, NSA_KV_HEADS, HEAD_DIM, PAGE_SIZE)
    page_specs = [pl.BlockSpec(page_blk, lambda i, pt, j=j: (layer, pt[i * n_page + j], 0, 0, 0, 0))
                  for j in range(n_page)]
    return pl.pallas_call(
        functools.partial(_nsa_compress_cache_kernel, n_page=n_page),
        grid_spec=pltpu.PrefetchScalarGridSpec(
            num_scalar_prefetch=1,
            grid=(pages.shape[0] // n_page,),
            in_specs=[*page_specs, full(pe_tok), full(w1_bd), full(w2_bd)],
            out_specs=pl.BlockSpec((n_page * per_page, 2 * gd), lambda i, pt: (i, 0)),
            scratch_shapes=[pltpu.VMEM((2, n_page * PAGE_SIZE, gd), F32)],
        ),
        out_shape=jax.ShapeDtypeStruct((pages.shape[0] * per_page, 2 * gd), F32),
        compiler_params=_cparams(("arbitrary",)),
        name="nsa_compress_cache",
    )(pages, *([cache_t] * n_page), pe_tok, w1_bd, w2_bd)


def _masked_softmax3(s, mask):
    s = jnp.where(mask[None], s, -jnp.inf)
    m = jnp.max(s, axis=-1, keepdims=True)
    m = jnp.where(m > -jnp.inf, m, 0.0)
    p = jnp.exp(s - m)
    d = jnp.sum(p, axis=-1, keepdims=True)
    return p / jnp.where(d > 0, d, 1.0)


def _drain(gen):
    try:
        while True:
            next(gen)
    except StopIteration as stop:
        return stop.value


def _round_robin(gens):
    results = [None] * len(gens)
    live = list(range(len(gens)))
    while live:
        for i in list(live):
            try:
                next(gens[i])
            except StopIteration as stop:
                results[i] = stop.value
                live.remove(i)
        yield
    return results


def _run_interleaved(gens):
    return _drain(_round_robin(gens))


def _cmp_branch_and_select(*args, **kwargs):
    return _drain(_cmp_branch_and_select_stages(*args, **kwargs))


def _cmp_branch_and_select_stages(q2, kc, vc, qpos, qpos_row, nq, imp_sc, n_blk, pair_t=None):
    nc = kc.shape[0]
    s = _dot_nt(q2, kc).reshape(NSA_HG, nq, nc)
    yield
    col = lax.broadcasted_iota(jnp.int32, (1, nc), 1)
    cblk = col if pair_t is not None else 2 * (col % LANES) + col // LANES
    cmp_end = (cblk + 1) * CMP_BLOCK - 1
    p = _masked_softmax3(s, cmp_end <= qpos)
    yield
    o_c = _dot(p.reshape(NSA_HG * nq, nc).astype(BF16), vc).reshape(NSA_HG, nq, HEAD_DIM)
    imp = jnp.sum(p, axis=0)
    if pair_t is not None:
        hi, mid, lo = _split3(imp)
        imp = _dot_nt(pair_t, hi) + _dot_nt(pair_t, mid) + _dot_nt(pair_t, lo)
    else:
        imp = (imp[:, :LANES] + imp[:, LANES:]).T
    yield
    j = lax.broadcasted_iota(jnp.int32, (LANES, 1), 0)
    cur = qpos_row // SEL_BLOCK
    forced = jnp.where(j == 0, 1, jnp.where(j == cur, 1, jnp.where(j == cur - 1, 1, 0))) > 0
    causal = j <= cur
    imp = jnp.where(forced, imp + FORCE_BONUS, imp)
    imp = jnp.where(causal, imp, -jnp.inf)
    def count_ahead(i, ri, cnt):
        wins_ties = jnp.where(j > i, 1.0, 0.0)
        return cnt + jnp.where(ri >= imp, jnp.where(ri > imp, 1.0, wins_ties), 0.0)

    rank = jnp.zeros((LANES, nq), F32)
    if isinstance(n_blk, int):
        for i in range(n_blk):
            rank = count_ahead(i, imp[i:i + 1, :], rank)
    else:
        imp_sc[...] = imp
        rank = lax.fori_loop(0, n_blk, lambda i, c: count_ahead(i, imp_sc[pl.ds(i, 1), :], c), rank)
    sel = jnp.where(causal, jnp.where(rank < TOP_N, 1.0, 0.0), 0.0)
    return o_c, sel.T.astype(BF16)


def _flash_init(nq):
    return (jnp.full((NSA_HG, nq, 1), M_INIT, F32), jnp.zeros((NSA_HG, nq, 1), F32),
            jnp.zeros((NSA_HG, nq, HEAD_DIM), F32))


def _mask_bias(allowed):
    return jnp.where(allowed, 0.0, NEG_BIG)


def _flash_update_stages(carry, s, bias, v, v_transposed=False):
    m, l, acc = carry
    s = s + bias[None]
    m_new = jnp.maximum(m, jnp.max(s, axis=-1, keepdims=True))
    alpha = jnp.exp(m - m_new)
    p = jnp.exp(s - m_new)
    l = alpha * l + jnp.sum(p, axis=-1, keepdims=True)
    h, q, n = p.shape
    p2 = p.reshape(h * q, n).astype(v.dtype)
    yield
    pv = (_dot_nt(p2, v) if v_transposed else _dot(p2, v)).reshape(h, q, HEAD_DIM)
    return m_new, l, alpha * acc + pv


def _flash_finish(carry):
    _, l, acc = carry
    return acc / jnp.where(l > 0, l, 1.0)


def _gate_column(sig, lane, col):
    return jnp.sum(jnp.where(lane == col, sig, 0.0), axis=-1, keepdims=True)


def _softmax_rows(s_ref, b_ref, p_ref, m_ref, l_ref, a_ref, n, tq):
    rb = 2 * SUBLANES
    for r0 in range(0, s_ref.shape[0], rb):
        rows = slice(r0, r0 + rb)
        sb = s_ref[rows, :n] + b_ref[r0 % tq:r0 % tq + rb, :n]
        m_old = m_ref[rows, :]
        m_new = jnp.maximum(m_old, jnp.max(sb, axis=-1, keepdims=True))
        p = jnp.exp(sb - jnp.concatenate([m_new] * (n // LANES), axis=1))
        alpha = jnp.exp(m_old - m_new)
        l_ref[rows, :] = alpha * l_ref[rows, :] + jnp.sum(p, axis=-1, keepdims=True)
        m_ref[rows, :] = m_new
        a_ref[rows, :] = alpha
        p_ref[rows, :n] = p.astype(BF16)


def _nsa_attn_kernel(q_ref, kc_ref, vc_ref, ks_ref, vs_ref, kw_ref, vw_ref, gd_ref, za_ref, e_ref,
                     o_ref, s_sc, p_sc, b_sc, m_sc, l_sc, a_sc, acc_sc, imp_sc, *, tq, tk):
    g = pl.program_id(1)
    q0 = pl.program_id(2) * tq
    q2 = q_ref[0].reshape(NSA_HG * tq, HEAD_DIM)
    qpos = q0 + lax.broadcasted_iota(jnp.int32, (tq, 1), 0)
    qpos_row = q0 + lax.broadcasted_iota(jnp.int32, (1, tq), 1)
    n_blk = (q0 + tq - 1) // SEL_BLOCK + 1
    o_c, sel = _cmp_branch_and_select(q2, kc_ref[0, 0], vc_ref[0, 0], qpos, qpos_row, tq, imp_sc, n_blk)

    def reset():
        m_sc[...] = jnp.full(m_sc.shape, M_INIT, F32)
        l_sc[...] = jnp.zeros(l_sc.shape, F32)
        acc_sc[...] = jnp.zeros(acc_sc.shape, F32)

    def softmax_step(k, v, bias, n):
        b_sc[:, :n] = bias

        def head_stages(h):
            rows = pl.ds(h * tq, tq)
            s_sc[rows, :n] = _dot_nt(q2[h * tq:(h + 1) * tq], k)
            yield
            _softmax_rows(s_sc.at[rows], b_sc, p_sc.at[rows], m_sc.at[rows], l_sc.at[rows], a_sc.at[rows], n, tq)
            yield
            acc_sc[rows, :] = a_sc[rows, :HEAD_DIM] * acc_sc[rows, :] + _dot(p_sc[rows, :n], v)

        _run_interleaved([head_stages(h) for h in range(NSA_HG)])

    def finish():
        l = l_sc[:, :HEAD_DIM]
        return acc_sc[...] / jnp.where(l > 0, l, 1.0)

    def sel_tile(t, diagonal):
        k0 = pl.multiple_of(t * tk, tk)
        bias = (_dot(sel, e_ref[:, pl.ds(k0, tk)]) - 1.0) * (-NEG_BIG)
        if diagonal:
            kpos = k0 + lax.broadcasted_iota(jnp.int32, (1, tk), 1)
            bias = jnp.where(kpos <= qpos, bias, NEG_BIG)
        softmax_step(ks_ref[0, 0, pl.ds(k0, tk), :], vs_ref[0, 0, pl.ds(k0, tk), :], bias, tk)

    n_below = q0 // tk
    reset()

    def below(t, carry):
        sel_tile(t, False)
        return carry

    lax.fori_loop(0, n_below, below, 0)
    sel_tile(n_below, True)
    o_s = finish()

    wn = WINDOW + tq
    w0 = pl.multiple_of(jnp.maximum(q0 - WINDOW, 0), tq)
    dist = qpos - (w0 + lax.broadcasted_iota(jnp.int32, (1, wn), 1))
    reset()
    softmax_step(kw_ref[0, 0, pl.ds(w0, wn), :], vw_ref[0, 0, pl.ds(w0, wn), :],
                 _mask_bias(jnp.where(dist >= 0, dist, WINDOW) < WINDOW), wn)
    o_w = finish()


    sig = _sigmoid(gd_ref[...])
    lane = lax.broadcasted_iota(jnp.int32, sig.shape, 1)
    outs = []
    for h in range(NSA_HG):
        head = g * NSA_HG + h
        hrows = slice(h * tq, (h + 1) * tq)
        g_c = _gate_column(sig, lane, GD_GATE + head)
        g_s = _gate_column(sig, lane, GD_GATE + NSA_HEADS + head)
        g_w = _gate_column(sig, lane, GD_GATE + 2 * NSA_HEADS + head)
        outs.append(g_c * o_c[h] + g_s * o_s[hrows] + g_w * o_w[hrows])
    o_ref[...] = jnp.concatenate(outs, axis=-1) * _silu(za_ref[...])


def nsa_attn_prompt(proj, qh, kc, vc, ks, vs, kw, vw, e_mat, tq=128, tk=1024):
    nb, _, seq, _ = qh.shape
    assert seq >= WINDOW + tq and tk % tq == 0 and seq % tk == 0
    rows, wn = NSA_HG * tq, max(WINDOW + tq, tk)
    nq = seq // tq
    gw = NSA_HG * HEAD_DIM
    kv_spec = pl.BlockSpec((1, 1, seq, HEAD_DIM), lambda b, g, i: (b, g, 0, 0))
    c_spec = pl.BlockSpec((1, 1, 2 * LANES, HEAD_DIM), lambda b, g, i: (b, g, 0, 0))
    return pl.pallas_call(
        functools.partial(_nsa_attn_kernel, tq=tq, tk=tk),
        grid=(nb, NSA_KV_HEADS, nq),
        in_specs=[
            pl.BlockSpec((1, NSA_HG, tq, HEAD_DIM), lambda b, g, i: (b, g, i, 0)),
            c_spec, c_spec, kv_spec, kv_spec, kv_spec, kv_spec,
            pl.BlockSpec((tq, LANES), lambda b, g, i: (b * nq + i, E_GD // LANES)),
            pl.BlockSpec((tq, gw), lambda b, g, i: (b * nq + i, E_ZA // gw + g)),
            pl.BlockSpec(e_mat.shape, lambda b, g, i: (0, 0)),
        ],
        out_specs=pl.BlockSpec((tq, gw), lambda b, g, i: (b * nq + i, g)),
        out_shape=jax.ShapeDtypeStruct((nb * seq, NSA_WIDTH), F32),
        scratch_shapes=[
            pltpu.VMEM((rows, wn), F32),
            pltpu.VMEM((rows, wn), BF16),
            pltpu.VMEM((tq, wn), F32),
            pltpu.VMEM((rows, LANES), F32),
            pltpu.VMEM((rows, LANES), F32),
            pltpu.VMEM((rows, LANES), F32),
            pltpu.VMEM((rows, HEAD_DIM), F32),
            pltpu.VMEM((LANES, tq), F32),
        ],
        compiler_params=_cparams(("arbitrary", "arbitrary", "arbitrary")),
        name="nsa_attn_prompt",
    )(qh, kc, vc, ks, vs, kw, vw, proj, proj, e_mat)


def _nsa_dec_kernel(pt_ref, q_ref, cmp_ref, *rest, past_len, n_pages, n_new):
    del pt_ref
    n_samp = q_ref.shape[0]
    ins = rest[n_samp * n_pages:n_samp * n_pages + 7]
    o_ref, wout_ref, imp_sc = rest[-3:]
    pwin_ref, wnew_ref = ins[1], ins[2]
    outs = _run_interleaved([
        _nsa_dec_sample(si, q_ref, cmp_ref, rest[si * n_pages:(si + 1) * n_pages], *ins, None, imp_sc,
                        past_len=past_len)
        for si in range(n_samp)])
    for si in range(n_samp):
        o_ref[si] = outs[si]
        wnew = wnew_ref[si]
        for c in range(2):
            for g in range(NSA_KV_HEADS):
                new_t = wnew[:, c * LANES + g * HEAD_DIM:c * LANES + (g + 1) * HEAD_DIM].T
                wout_ref[si, c, g] = jnp.concatenate([pwin_ref[si, c, g][:, n_new:], new_t[:, :n_new]], axis=1)


def _nsa_dec_sample(si, q_ref, cmp_ref, page_refs, rows_ref, pwin_ref, wnew_ref, gd_ref, za_ref, e_ref, pair_ref,
                    o_ref, imp_sc, *, past_len):
    del o_ref
    nq = DEC_PAD
    scale = HEAD_DIM ** -0.5
    qpos = past_len + lax.broadcasted_iota(jnp.int32, (nq, 1), 0)
    qpos_row = past_len + lax.broadcasted_iota(jnp.int32, (1, nq), 1)
    n_blk = (past_len + nq - 1) // SEL_BLOCK + 1
    q = q_ref[si]
    cmp = cmp_ref[si]

    def q_group(g):
        parts = [q[:, (g * NSA_HG + h) * HEAD_DIM:(g * NSA_HG + h + 1) * HEAD_DIM] for h in range(NSA_HG)]
        return jnp.concatenate(parts, axis=0) * scale

    qg = [q_group(g) for g in range(NSA_KV_HEADS)]

    rows = rows_ref[si]
    wnew = wnew_ref[si]
    win_keep = pwin_ref.shape[-1]
    sig = _sigmoid(gd_ref[si])
    lane = lax.broadcasted_iota(jnp.int32, sig.shape, 1)
    kpos = lax.broadcasted_iota(jnp.int32, (1, past_len), 1)
    npos = past_len + lax.broadcasted_iota(jnp.int32, (1, nq), 1)
    wpos = past_len - win_keep + lax.broadcasted_iota(jnp.int32, (1, win_keep), 1)
    outs = []
    for g in range(NSA_KV_HEADS):
        qb = qg[g].astype(BF16)
        kc = cmp[:, g * HEAD_DIM:(g + 1) * HEAD_DIM].astype(BF16)
        vc = cmp[:, LANES + g * HEAD_DIM:LANES + (g + 1) * HEAD_DIM].astype(BF16)
        o_c, sel = yield from _cmp_branch_and_select_stages(
            qb, kc, vc, qpos, qpos_row, nq, imp_sc.at[si * NSA_KV_HEADS + g], n_blk, pair_ref[...])
        yield
        kt = jnp.concatenate([r[0, 0, g] for r in page_refs], axis=1).astype(BF16)
        vt = jnp.concatenate([r[0, 1, g] for r in page_refs], axis=1).astype(BF16)
        s = _dot(qb, kt).reshape(NSA_HG, nq, past_len)
        blk_on = _dot(sel, e_ref[:, :past_len])
        yield
        bias = _mask_bias(jnp.where(kpos <= qpos, blk_on, 0.0) > 0.5)
        carry = yield from _flash_update_stages(_flash_init(nq), s, bias, vt, v_transposed=True)
        yield
        kn = rows[:, 2 * LANES + g * HEAD_DIM:2 * LANES + (g + 1) * HEAD_DIM]
        vn = rows[:, 3 * LANES + g * HEAD_DIM:3 * LANES + (g + 1) * HEAD_DIM]
        s = _dot_nt(qg[g], kn).reshape(NSA_HG, nq, nq)
        blk_on = _dot(sel, e_ref[:, past_len:past_len + LANES])[:, :nq]
        yield
        bias = _mask_bias(jnp.where(npos <= qpos, blk_on, 0.0) > 0.5)
        o_s = _flash_finish((yield from _flash_update_stages(carry, s, bias, vn)))
        yield
        s = _dot(qb, pwin_ref[si, 0, g].astype(BF16)).reshape(NSA_HG, nq, win_keep)
        yield
        dist = qpos - wpos
        bias = _mask_bias(jnp.where(wpos >= 0, jnp.where(dist >= 0, dist, WINDOW), WINDOW) < WINDOW)
        carry = yield from _flash_update_stages(_flash_init(nq), s, bias, pwin_ref[si, 1, g].astype(BF16),
                                                v_transposed=True)
        yield
        kwn = wnew[:, g * HEAD_DIM:(g + 1) * HEAD_DIM]
        vwn = wnew[:, LANES + g * HEAD_DIM:LANES + (g + 1) * HEAD_DIM]
        s = _dot_nt(qg[g], kwn).reshape(NSA_HG, nq, nq)
        yield
        dist = qpos - npos
        bias = _mask_bias(jnp.where(dist >= 0, dist, WINDOW) < WINDOW)
        o_w = _flash_finish((yield from _flash_update_stages(carry, s, bias, vwn)))
        for h in range(NSA_HG):
            head = g * NSA_HG + h
            g_c = _gate_column(sig, lane, GD_GATE + head)
            g_s = _gate_column(sig, lane, GD_GATE + NSA_HEADS + head)
            g_w = _gate_column(sig, lane, GD_GATE + 2 * NSA_HEADS + head)
            outs.append(g_c * o_c[h] + g_s * o_s[h] + g_w * o_w[h])
        yield
    return jnp.concatenate(outs, axis=-1) * _silu(za_ref[si])


def nsa_attn_decode(page_table, layer, qrot, cmp, cache_t, rows, pwin_t, wnew, proj3, e_mat, past_len, n_samp,
                    n_new, win_stack):
    nb, npg = page_table.shape
    win_keep = pwin_t.shape[-1]
    ncb = cmp.shape[1]
    per_b = lambda blk: pl.BlockSpec(blk, lambda b, pt: (b,) + (0,) * (len(blk) - 1))
    page_blk = (None, 1, 2, NSA_KV_HEADS, HEAD_DIM, PAGE_SIZE)
    page_specs = [pl.BlockSpec(page_blk, lambda b, pt, si=si, j=j: (layer, pt[b * n_samp + si, j], 1, 0, 0, 0))
                  for si in range(n_samp) for j in range(npg)]
    pair_t = (jnp.arange(ncb, dtype=jnp.int32)[None, :] // 2
              == jnp.arange(LANES, dtype=jnp.int32)[:, None]).astype(BF16)
    win_blk = (None, n_samp, 2, NSA_KV_HEADS, HEAD_DIM, win_keep)
    win_spec = pl.BlockSpec(win_blk, lambda b, pt: (layer, b, 0, 0, 0, 0))
    args = [page_table, qrot, cmp, *([cache_t] * (n_samp * npg)), rows, pwin_t, wnew, proj3, proj3, e_mat, pair_t]
    in_specs = [
        per_b((n_samp, DEC_PAD, NSA_WIDTH)),
        per_b((n_samp, ncb, 4 * HEAD_DIM)),
        *page_specs,
        per_b((n_samp, DEC_PAD, 4 * LANES)),
        win_spec,
        per_b((n_samp, DEC_PAD, 2 * LANES)),
        pl.BlockSpec((n_samp, DEC_PAD, LANES), lambda b, pt: (b, 0, E_GD // LANES)),
        pl.BlockSpec((n_samp, DEC_PAD, NSA_WIDTH), lambda b, pt: (b, 0, E_ZA // NSA_WIDTH)),
        pl.BlockSpec(e_mat.shape, lambda b, pt: (0, 0)),
        pl.BlockSpec(pair_t.shape, lambda b, pt: (0, 0)),
    ]
    aliases = {}
    if win_stack is not None:
        aliases = {len(args): 1}
        args.append(win_stack)
        in_specs.append(pl.BlockSpec(memory_space=pl.ANY))
    return pl.pallas_call(
        functools.partial(_nsa_dec_kernel, past_len=past_len, n_pages=npg, n_new=n_new),
        grid_spec=pltpu.PrefetchScalarGridSpec(
            num_scalar_prefetch=1,
            grid=(nb // n_samp,),
            in_specs=in_specs,
            out_specs=[per_b((n_samp, DEC_PAD, NSA_WIDTH)), win_spec],
            scratch_shapes=[pltpu.VMEM((n_samp * NSA_KV_HEADS, LANES, DEC_PAD), F32)],
        ),
        out_shape=[jax.ShapeDtypeStruct((nb, DEC_PAD, NSA_WIDTH), F32),
                   jax.ShapeDtypeStruct(pwin_t.shape, F32)],
        input_output_aliases=aliases,
        compiler_params=_cparams(("arbitrary",)),
        name="nsa_attn_decode",
    )(*args)


def _tri_masks(c):
    row = lax.broadcasted_iota(jnp.int32, (c, c), 0)
    col = lax.broadcasted_iota(jnp.int32, (c, c), 1)
    lower = row >= col
    return lower, jnp.where(lower, 1.0, 0.0).astype(BF16), jnp.where(row <= col, 1.0, 0.0).astype(BF16)


def _ssd_kernel(zb_ref, xbc_ref, gd_ref, prev_ref, h0_ref, cw_ref, cb_ref, dtb_ref, alog_ref, alog16_ref,
                dexp_ref, nw_ref, e16_ref, *rest, chunk, n_valid):
    y_ref, so_ref, h_sc, xbuf = rest[-4:]
    c = pl.program_id(1)
    n_seq = zb_ref.shape[0]
    hp = SSD_HEADS // SSD_GROUPS * SSD_HEAD_DIM
    halo = SUBLANES

    @pl.when(c == 0)
    def _():
        h_sc[...] = h0_ref[...]
        xbuf[:, 0:halo] = prev_ref[...]

    lower, tri, tri_t = _tri_masks(chunk)
    ones8 = jnp.ones((chunk, SUBLANES), BF16)
    lane = lax.broadcasted_iota(jnp.int32, (chunk, LANES), 1)
    low_half = lane < SSD_HEAD_DIM

    def seq_stages(b):
        xbuf[b, halo:halo + chunk] = xbc_ref[b]
        x_all = xbuf[b]
        conv = cb_ref[...]
        for i in range(SSD_CONV):
            shift = SSD_CONV - 1 - i
            tap = x_all if shift == 0 else pltpu.roll(x_all, shift, 0)
            conv = conv + tap[halo:halo + chunk] * cw_ref[i:i + 1, :]
        xbuf[b, 0:halo] = xbuf[b, chunk:chunk + halo]
        act = _silu(conv)
        xs = act[:, :SSD_INNER]
        bm = act[:, SSD_INNER:SSD_INNER + SSD_GROUPS * SSD_STATE]
        cm = act[:, SSD_INNER + SSD_GROUPS * SSD_STATE:]

        dt16 = _softplus(gd_ref[b] + dtb_ref[...])[:, GD_DT:GD_DT + SSD_HEADS]
        if n_valid is not None:
            tok = c * chunk + lax.broadcasted_iota(jnp.int32, (chunk, 1), 0)
            dt16 = jnp.where(tok < n_valid, dt16, 0.0)
        yield
        dt = _dot3(dt16, e16_ref[...])
        da = dt * (-jnp.exp(alog_ref[...]))
        da16 = dt16 * (-jnp.exp(alog16_ref[...]))
        yield
        cum = _dot3_wx(tri, da)
        yield
        cum16 = _dot3_wx(tri, da16)
        cum16_t = _dot3_tn(da16, tri_t)
        yield
        decay_col = jnp.exp(_dot3_tn(da, ones8)[:, 0:1])
        xdt = xs * dt
        cum_last = cum[chunk - 1:chunk, :]
        xw = (xdt * jnp.exp(cum_last - cum)).astype(BF16)
        xdt_b = xdt.astype(BF16)

        y_intra = [None] * (SSD_INNER // LANES)
        y_inter = []
        for g in range(SSD_GROUPS):
            cg = cm[:, g * SSD_STATE:(g + 1) * SSD_STATE].astype(BF16)
            bg = bm[:, g * SSD_STATE:(g + 1) * SSD_STATE].astype(BF16)
            h_g = h_sc[b, g * hp:(g + 1) * hp, :]
            yield
            cb = _dot_nt(cg, bg)
            y_inter.append(_dot_nt(cg, h_g.astype(BF16)))
            for hh in range(SSD_HEADS // SSD_GROUPS):
                h = g * (SSD_HEADS // SSD_GROUPS) + hh
                diff = cum16[:, h:h + 1] - cum16_t[h:h + 1, :]
                lmat = jnp.where(lower, jnp.exp(jnp.where(lower, diff, 0.0)), 0.0)
                m = (cb * lmat).astype(BF16)
                pair = h // 2
                x_pair = xdt_b[:, pair * LANES:(pair + 1) * LANES]
                keep = low_half if h % 2 == 0 else jnp.logical_not(low_half)
                yield
                contrib = _dot(m, jnp.where(keep, x_pair, jnp.zeros_like(x_pair)))
                y_intra[pair] = contrib if y_intra[pair] is None else y_intra[pair] + contrib
            yield
            h_sc[b, g * hp:(g + 1) * hp, :] = (h_g * decay_col[g * hp:(g + 1) * hp, :]
                                               + _dot_tn(xw[:, g * hp:(g + 1) * hp], bg))
        y = jnp.concatenate(y_intra, axis=-1) + jnp.concatenate(y_inter, axis=-1) * jnp.exp(cum)
        y = y + xs * dexp_ref[...]
        y = y * _silu(zb_ref[b])
        ms = jnp.mean(y * y, axis=-1, keepdims=True)
        y_ref[b] = y * lax.rsqrt(ms + 1e-6) * nw_ref[...]

    _run_interleaved([seq_stages(b) for b in range(n_seq)])

    @pl.when(c == pl.num_programs(1) - 1)
    def _():
        so_ref[...] = h_sc[...]


def _stacked_state_io(stack, n_layers, blk, out_layer, n_in):
    shape = jax.ShapeDtypeStruct((n_layers,) + blk[0], F32)
    spec = pl.BlockSpec((None,) + blk[1], lambda b, c: (out_layer, b, 0, 0))
    if stack is None:
        return spec, shape, [], [], {}
    return spec, shape, [stack], [pl.BlockSpec(memory_space=pl.ANY)], {n_in: 1}


def ssd_mixer(proj, prev8, h0, layer, conv_w, conv_b, dtb_row, alog_exp, alog16, d_exp, norm_w, e16, nb, seq, chunk,
              n_seq, n_valid, stack, out_layer, n_layers):
    full = lambda a: pl.BlockSpec(a.shape, lambda b, c: (0,) * a.ndim)
    proj3 = proj.reshape(nb, seq, E_END)
    col_spec = lambda width, off: pl.BlockSpec((n_seq, chunk, width), lambda b, c: (b, c, off // width))
    in_specs = [
        col_spec(SSD_INNER, E_ZB), col_spec(SSD_CONV_CH, E_XBC), col_spec(LANES, E_GD),
        pl.BlockSpec((n_seq, SUBLANES, SSD_CONV_CH), lambda b, c: (b, 0, 0)),
        pl.BlockSpec((None, n_seq, SSD_INNER, SSD_STATE), lambda b, c: (layer, b, 0, 0)),
        full(conv_w), full(conv_b), full(dtb_row), full(alog_exp), full(alog16), full(d_exp), full(norm_w),
        full(e16),
    ]
    st_spec, st_shape, extra, extra_specs, aliases = _stacked_state_io(
        stack, n_layers, ((nb, SSD_INNER, SSD_STATE), (n_seq, SSD_INNER, SSD_STATE)), out_layer, len(in_specs))
    y, st = pl.pallas_call(
        functools.partial(_ssd_kernel, chunk=chunk, n_valid=n_valid),
        grid=(nb // n_seq, seq // chunk),
        in_specs=in_specs + extra_specs,
        out_specs=[pl.BlockSpec((n_seq, chunk, SSD_INNER), lambda b, c: (b, c, 0)), st_spec],
        out_shape=[jax.ShapeDtypeStruct((nb, seq, SSD_INNER), F32), st_shape],
        scratch_shapes=[pltpu.VMEM((n_seq, SSD_INNER, SSD_STATE), F32),
                        pltpu.VMEM((n_seq, SUBLANES + chunk, SSD_CONV_CH), F32)],
        input_output_aliases=aliases,
        compiler_params=_cparams(("arbitrary", "arbitrary")),
        name="ssd_mixer",
    )(proj3, proj3, proj3, prev8, h0, conv_w, conv_b, dtb_row, alog_exp, alog16, d_exp, norm_w, e16, *extra)
    return y.reshape(nb * seq, SSD_INNER), st


def _gla_kernel(q_ref, k_ref, v_ref, r_ref, glr_ref, s0_ref, wg_ref, bg_ref, nw_ref, *rest,
                chunk, n_sub, n_valid):
    o_ref, so_ref, s_sc = rest[-3:]
    c = pl.program_id(1)
    n_seq = q_ref.shape[0]
    span = n_sub * chunk

    @pl.when(c == 0)
    def _():
        s_sc[...] = s0_ref[...]

    row = lax.broadcasted_iota(jnp.int32, (span, span), 0)
    col = lax.broadcasted_iota(jnp.int32, (span, span), 1)
    lower = jnp.where(row // chunk == col // chunk, row - col, -1) >= 0
    tri = jnp.where(lower, 1.0, 0.0).astype(BF16)
    crow = lax.broadcasted_iota(jnp.int32, (span, n_sub * SUBLANES), 0) // chunk
    ccol = lax.broadcasted_iota(jnp.int32, (span, n_sub * SUBLANES), 1) // SUBLANES
    chunk_ones = jnp.where(crow == ccol, 1.0, 0.0).astype(BF16)
    nw = nw_ref[...]

    def head_stages(b, h, qg, kg, kd, v, r, total):
        kcols = slice(h * GLA_DK, (h + 1) * GLA_DK)
        vcols = slice(h * GLA_DV, (h + 1) * GLA_DV)
        att = jnp.where(lower, _dot_nt(qg[:, kcols], kg[:, kcols]), 0.0)
        yield
        o = _dot(att.astype(BF16), v[:, vcols])
        s_h = s_sc[b, kcols, :]
        carried = []
        for u in range(n_sub):
            rows = slice(u * chunk, (u + 1) * chunk)
            yield
            carried.append(_dot(qg[rows, kcols], s_h.astype(BF16)))
            decay = jnp.exp(total[kcols, u * SUBLANES:u * SUBLANES + 1])
            s_h = s_h * decay + _dot_tn(kd[rows, kcols], v[rows, vcols])
        s_sc[b, kcols, :] = s_h
        yield
        o = o + jnp.concatenate(carried, axis=0)
        ms = jnp.mean(o * o, axis=-1, keepdims=True)
        o = o * lax.rsqrt(ms + 1e-6) * nw
        o_ref[b, :, vcols] = o * _silu(r[:, vcols])

    def seq_stages(b):
        lg = _log_sigmoid(_dot(glr_ref[b].astype(BF16), wg_ref[...]) + bg_ref[...]) * (1.0 / GLA_TAU)
        k = k_ref[b]
        if n_valid is not None:
            tok = c * span + lax.broadcasted_iota(jnp.int32, (span, 1), 0)
            lg = jnp.where(tok < n_valid, lg, 0.0)
            k = jnp.where(tok < n_valid, k, 0.0)
        yield
        bcum = _dot3_wx(tri, lg)
        yield
        total = _dot3_tn(lg, chunk_ones)
        qg = (q_ref[b] * (GLA_DK ** -0.5) * jnp.exp(bcum)).astype(BF16)
        kg = (k * jnp.exp(-bcum)).astype(BF16)
        kd = jnp.concatenate(
            [k[u * chunk:(u + 1) * chunk] * jnp.exp(bcum[(u + 1) * chunk - 1:(u + 1) * chunk]
                                                    - bcum[u * chunk:(u + 1) * chunk]) for u in range(n_sub)],
            axis=0).astype(BF16)
        v = v_ref[b].astype(BF16)
        r = r_ref[b]
        yield
        yield from _round_robin([head_stages(b, h, qg, kg, kd, v, r, total) for h in range(GLA_HEADS)])

    _run_interleaved([seq_stages(b) for b in range(n_seq)])

    @pl.when(c == pl.num_programs(1) - 1)
    def _():
        so_ref[...] = s_sc[...]


def gla_mixer(proj, s0, layer, wg_pad, bg, norm_w, nb, seq, chunk, n_sub, n_seq, n_valid, stack, out_layer,
              n_layers):
    span = n_sub * chunk
    full = lambda a: pl.BlockSpec(a.shape, lambda b, c: (0,) * a.ndim)
    srows = GLA_HEADS * GLA_DK
    proj3 = proj.reshape(nb, seq, O_END)
    col_spec = lambda width, off: pl.BlockSpec((n_seq, span, width), lambda b, c: (b, c, off // width))
    in_specs = [
        col_spec(GLA_KEY_WIDTH, O_Q), col_spec(GLA_KEY_WIDTH, O_K), col_spec(GLA_VAL_WIDTH, O_V),
        col_spec(GLA_VAL_WIDTH, O_R), col_spec(LANES, O_GLR),
        pl.BlockSpec((None, n_seq, srows, GLA_DV), lambda b, c: (layer, b, 0, 0)),
        full(wg_pad), full(bg), full(norm_w),
    ]
    st_spec, st_shape, extra, extra_specs, aliases = _stacked_state_io(
        stack, n_layers, ((nb, srows, GLA_DV), (n_seq, srows, GLA_DV)), out_layer, len(in_specs))
    o, st = pl.pallas_call(
        functools.partial(_gla_kernel, chunk=chunk, n_sub=n_sub, n_valid=n_valid),
        grid=(nb // n_seq, seq // span),
        in_specs=in_specs + extra_specs,
        out_specs=[pl.BlockSpec((n_seq, span, GLA_VAL_WIDTH), lambda b, c: (b, c, 0)), st_spec],
        out_shape=[jax.ShapeDtypeStruct((nb, seq, GLA_VAL_WIDTH), F32), st_shape],
        scratch_shapes=[pltpu.VMEM((n_seq, srows, GLA_DV), F32)],
        input_output_aliases=aliases,
        compiler_params=_cparams(("arbitrary", "arbitrary")),
        name="gla_mixer",
    )(proj3, proj3, proj3, proj3, proj3, s0, wg_pad, bg, norm_w, *extra)
    return o.reshape(nb * seq, GLA_VAL_WIDTH), st


def _even_weight(w):
    q, g_a, kv, z_a, z_b, xbc, dt = jnp.split(w, list(np.cumsum(
        [NSA_WIDTH, 3 * NSA_HEADS, 6 * NSA_KV_HEADS * HEAD_DIM, NSA_WIDTH, SSD_INNER, SSD_CONV_CH])), axis=-1)
    pad = jnp.zeros((w.shape[0], E_END - E_GD - 3 * NSA_HEADS - SSD_HEADS), w.dtype)
    return jnp.concatenate([xbc, q, z_b, z_a, kv, g_a, dt, pad], axis=-1).astype(BF16)


def _odd_weight(w):
    q, k, v, glr, r = jnp.split(w, list(np.cumsum(
        [GLA_KEY_WIDTH, GLA_KEY_WIDTH, GLA_VAL_WIDTH, GLA_GATE_RANK])), axis=-1)
    pad = jnp.zeros((w.shape[0], O_END - O_GLR - GLA_GATE_RANK), w.dtype)
    return jnp.concatenate([q, k, v, r, glr, pad], axis=-1).astype(BF16)


def _rope_tables(pos):
    half = HEAD_DIM // 2
    inv = ROPE_THETA ** (-jnp.arange(half, dtype=F32) / half)
    ang = pos.astype(F32)[:, None] * inv[None, :]
    cos, sin = jnp.cos(ang), jnp.sin(ang)
    reps = LANES // HEAD_DIM
    return jnp.tile(jnp.concatenate([cos, cos], -1), (1, reps)), jnp.tile(jnp.concatenate([-sin, sin], -1), (1, reps))


def _compress_weights(pe, w1, w2):
    pe_tab = jnp.concatenate([pe[0], pe[0], pe[1], pe[1]], axis=-1)
    w1r = w1.reshape(2, CMP_BLOCK, HEAD_DIM, CMP_HIDDEN)
    w1_blk = jnp.zeros((CMP_BLOCK, 4 * HEAD_DIM, 4 * CMP_HIDDEN), F32)
    w2_blk = jnp.zeros((4 * CMP_HIDDEN, 4 * HEAD_DIM), F32)
    for part in range(4):
        src = part // 2
        w1_blk = w1_blk.at[:, part * HEAD_DIM:(part + 1) * HEAD_DIM,
                           part * CMP_HIDDEN:(part + 1) * CMP_HIDDEN].set(w1r[src])
        w2_blk = w2_blk.at[part * CMP_HIDDEN:(part + 1) * CMP_HIDDEN,
                           part * HEAD_DIM:(part + 1) * HEAD_DIM].set(w2[src])
    return pe_tab, w1_blk.astype(BF16), w2_blk.astype(BF16)


def _compress_weights_cache(pe, w1, w2):
    w1r = w1.reshape(2, CMP_BLOCK, HEAD_DIM, CMP_HIDDEN)
    w1_bd = jnp.zeros((2, CMP_BLOCK, NSA_KV_HEADS * HEAD_DIM, NSA_KV_HEADS * CMP_HIDDEN), F32)
    w2_bd = jnp.zeros((2, NSA_KV_HEADS * CMP_HIDDEN, NSA_KV_HEADS * HEAD_DIM), F32)
    for g in range(NSA_KV_HEADS):
        w1_bd = w1_bd.at[:, :, g * HEAD_DIM:(g + 1) * HEAD_DIM, g * CMP_HIDDEN:(g + 1) * CMP_HIDDEN].set(w1r)
        w2_bd = w2_bd.at[:, g * CMP_HIDDEN:(g + 1) * CMP_HIDDEN, g * HEAD_DIM:(g + 1) * HEAD_DIM].set(w2)
    pe_tok = jnp.tile(jnp.concatenate([pe] * NSA_KV_HEADS, axis=-1), (1, PAGE_SIZE // CMP_BLOCK, 1))
    return pe_tok, w1_bd.astype(BF16), w2_bd.astype(BF16)


def _cmp_layout(cmp, nb, nc, dtype):
    c = cmp.reshape(nb, nc, 2, NSA_KV_HEADS, HEAD_DIM).transpose(2, 0, 3, 1, 4)
    halves = []
    for par in range(2):
        h = c[:, :, :, par::2]
        halves.append(jnp.pad(h, ((0, 0), (0, 0), (0, 0), (0, LANES - h.shape[3]), (0, 0))))
    c = jnp.concatenate(halves, axis=3).astype(dtype)
    return c[0], c[1]


def _per_step(n, want):
    return want if n % want == 0 else 1


def _sel_expand(n_keys):
    blk = jnp.arange(n_keys, dtype=jnp.int32) // SEL_BLOCK
    return (blk[None, :] == jnp.arange(LANES, dtype=jnp.int32)[:, None]).astype(BF16)


def kernel(x_prompt, x_sample, c_prompt, c_sample, cache_nsa, cache_nsa_win, state_ssd_conv, state_ssd, state_gla, page_table, norm_w, w_mod, b_mod, w_in_even, w_out_even, nsa_cmp_pe, nsa_cmp_w1, nsa_cmp_w2, ssd_conv_w, ssd_conv_b, ssd_dt_bias, ssd_a_log, ssd_d, ssd_norm_w, w_in_odd, gla_w_gate2, gla_b_gate, gla_norm_w, w_out_odd, final_norm_w):
    bp, sp, d = x_prompt.shape
    bs, ss, _ = x_sample.shape
    depth = norm_w.shape[0]
    npg = page_table.shape[1]
    past_len = npg * PAGE_SIZE
    assert ss <= cache_nsa_win.shape[2]
    assert sp % 512 == 0 and sp // SEL_BLOCK <= LANES and sp // CMP_BLOCK <= 2 * LANES
    assert ss <= DEC_PAD and ss < CMP_BLOCK and past_len % SEL_BLOCK == 0 and ss >= SSD_CONV - 1
    assert past_len // SEL_BLOCK + 1 <= LANES and (bs * DEC_PAD) % TM_IN == 0 and (bs * npg) % CMP_PAGES == 0
    tp, td = bp * sp, bs * DEC_PAD
    tm_d = TM_IN
    seq_ps, samp_ps = _per_step(bp, SEQ_PER_STEP), _per_step(bs, SAMPLES_PER_STEP)

    c_all = jnp.concatenate([c_prompt, c_sample], axis=0)
    c_all = jnp.pad(c_all, ((0, -c_all.shape[0] % SUBLANES), (0, 0)))
    mod = mod_all(c_all, w_mod.astype(BF16), b_mod)

    def mods(l):
        shift, scale, gate = jnp.split(mod[l], 3, axis=-1)
        mp = [m[:bp].reshape(bp, 1, d) for m in (shift, scale, gate)]
        ms = [jnp.repeat(m[bp:bp + bs], DEC_PAD, axis=0).reshape(td // tm_d, tm_d, d) for m in (shift, scale, gate)]
        return mp, ms

    xp = x_prompt.reshape(tp, d)
    xs = jnp.pad(x_sample, ((0, 0), (0, DEC_PAD - ss), (0, 0))).reshape(td, d)

    cos_p, sin_p = _rope_tables(jnp.arange(sp, dtype=jnp.int32))
    cos_s, sin_s = _rope_tables(past_len + jnp.arange(DEC_PAD, dtype=jnp.int32))
    cos_s, sin_s = jnp.tile(cos_s, (tm_d // DEC_PAD, 1)), jnp.tile(sin_s, (tm_d // DEC_PAD, 1))
    cache_t = jnp.transpose(cache_nsa, (0, 1, 3, 4, 5, 2))
    win_t = jnp.transpose(cache_nsa_win, (0, 1, 3, 4, 5, 2))
    ssd_h0_s = state_ssd.reshape(state_ssd.shape[0], bs, SSD_INNER, SSD_STATE)
    gla_s0_s = state_gla.reshape(state_gla.shape[0], bs, GLA_HEADS * GLA_DK, GLA_DV)
    e_mat_p = _sel_expand(sp)
    e_mat_s = _sel_expand(past_len + LANES)
    e16 = (jnp.arange(SSD_INNER, dtype=jnp.int32)[None, :] // SSD_HEAD_DIM
           == jnp.arange(SSD_HEADS, dtype=jnp.int32)[:, None]).astype(BF16)

    outs = {k: [] for k in ("kv_p", "kv_s", "win_p", "cv_p", "cv_s")}
    stacks = dict.fromkeys(("win_s", "ss_p", "ss_s", "gl_p", "gl_s"))
    n_even, n_odd = (depth + 1) // 2, depth // 2
    for l in range(depth):
        e = l // 2
        (shift_p, scale_p, gate_p), (shift_s, scale_s, gate_s) = mods(l)
        last = l == depth - 1
        if l % 2 == 0:
            w_in = _even_weight(w_in_even[e])
            w_out = w_out_even[e].astype(BF16)
            w_out_a, w_out_b = w_out[:NSA_WIDTH], w_out[NSA_WIDTH:]
            pe_tab, w1_blk, w2_blk = _compress_weights(nsa_cmp_pe[e], nsa_cmp_w1[e], nsa_cmp_w2[e])
            conv_w, conv_b = ssd_conv_w[e], ssd_conv_b[e].reshape(1, SSD_CONV_CH)
            dtb_row = jnp.zeros((1, LANES), F32).at[0, GD_DT:GD_DT + SSD_HEADS].set(ssd_dt_bias[e])
            alog16 = ssd_a_log[e].reshape(1, SSD_HEADS)
            alog_exp = jnp.repeat(ssd_a_log[e], SSD_HEAD_DIM).reshape(1, SSD_INNER)
            d_exp = jnp.repeat(ssd_d[e], SSD_HEAD_DIM).reshape(1, SSD_INNER)
            ssd_nw = ssd_norm_w[e].reshape(1, SSD_INNER)
            ssd_args = (conv_w, conv_b, dtb_row, alog_exp, alog16, d_exp, ssd_nw, e16)

            proj = inproj(xp, norm_w[l], scale_p, shift_p, w_in, TM_IN, sp)
            rows, win, _, qh, ks, vs, kw, vw = nsa_prep(proj, cos_p, sin_p, TM_IN, sp, True)
            nc = sp // CMP_BLOCK
            cmp = nsa_compress(rows.reshape(tp // CMP_BLOCK, CMP_BLOCK * 4 * LANES), pe_tab, w1_blk, w2_blk,
                               min(CMP_ROWS, tp // CMP_BLOCK))
            kc, vc = _cmp_layout(cmp, bp, nc, BF16)
            o_a = nsa_attn_prompt(proj, qh, kc, vc, ks, vs, kw, vw, e_mat_p)
            y, stacks["ss_p"] = ssd_mixer(proj, jnp.zeros((bp, SUBLANES, SSD_CONV_CH), F32),
                                          jnp.zeros((1, bp, SSD_INNER, SSD_STATE), F32), 0, *ssd_args, bp, sp,
                                          SSD_CHUNK, seq_ps, None, stacks["ss_p"], e, n_even)
            xp = outproj([o_a, y], [w_out_a, w_out_b], xp, gate_p, final_norm_w, TM_OUT, sp, last)
            outs["kv_p"].append(rows.reshape(bp, sp, 4, NSA_KV_HEADS, HEAD_DIM))
            outs["win_p"].append(win.reshape(bp, sp, 2, NSA_KV_HEADS, HEAD_DIM)[:, -min(WINDOW, sp):])
            outs["cv_p"].append(proj.reshape(bp, sp, E_END)[:, -(SSD_CONV - 1):, E_XBC:E_XBC + SSD_CONV_CH])

            proj = inproj(xs, norm_w[l], scale_s, shift_s, w_in, tm_d, tm_d)
            rows, win, qrot = nsa_prep(proj, cos_s, sin_s, tm_d, DEC_PAD, False)
            pe_tok, w1_bd, w2_bd = _compress_weights_cache(nsa_cmp_pe[e], nsa_cmp_w1[e], nsa_cmp_w2[e])
            cmp_s = nsa_compress_cache(cache_t, e, page_table.reshape(-1), pe_tok, w1_bd, w2_bd, CMP_PAGES)
            per_page = PAGE_SIZE // CMP_BLOCK
            o_a, stacks["win_s"] = nsa_attn_decode(
                page_table, e, qrot.reshape(bs, DEC_PAD, NSA_WIDTH),
                cmp_s.reshape(bs, npg * per_page, 4 * HEAD_DIM), cache_t, rows.reshape(bs, DEC_PAD, 4 * LANES),
                win_t, win.reshape(bs, DEC_PAD, 2 * LANES), proj.reshape(bs, DEC_PAD, E_END), e_mat_s, past_len,
                samp_ps, ss, stacks["win_s"])
            o_a = o_a.reshape(td, NSA_WIDTH)
            prev8 = jnp.pad(state_ssd_conv[e], ((0, 0), (SUBLANES - (SSD_CONV - 1), 0), (0, 0)))
            y, stacks["ss_s"] = ssd_mixer(proj, prev8, ssd_h0_s, e, *ssd_args, bs, DEC_PAD, DEC_PAD,
                                          samp_ps, ss, stacks["ss_s"], e, n_even)
            xs = outproj([o_a, y], [w_out_a, w_out_b], xs, gate_s, final_norm_w, tm_d, tm_d, last)
            outs["kv_s"].append(rows.reshape(bs, DEC_PAD, 4, NSA_KV_HEADS, HEAD_DIM)[:, :ss])
            new_xbc = proj.reshape(bs, DEC_PAD, E_END)[:, :ss, E_XBC:E_XBC + SSD_CONV_CH]
            outs["cv_s"].append(jnp.concatenate([state_ssd_conv[e], new_xbc], axis=1)[:, -(SSD_CONV - 1):])
        else:
            w_in = _odd_weight(w_in_odd[e])
            w_out = w_out_odd[e].astype(BF16)
            wg_pad = jnp.pad(gla_w_gate2[e], ((0, LANES - GLA_GATE_RANK), (0, 0))).astype(BF16)
            bg = gla_b_gate[e].reshape(1, GLA_KEY_WIDTH)
            gnw = gla_norm_w[e].reshape(1, GLA_DV)
            srows = GLA_HEADS * GLA_DK
            proj = inproj(xp, norm_w[l], scale_p, shift_p, w_in, TM_IN, sp)
            o, stacks["gl_p"] = gla_mixer(proj, jnp.zeros((1, bp, srows, GLA_DV), F32), 0, wg_pad, bg, gnw, bp, sp,
                                          GLA_CHUNK, GLA_SUB, seq_ps, None, stacks["gl_p"], e, n_odd)
            xp = outproj([o], [w_out], xp, gate_p, final_norm_w, TM_OUT, sp, last)
            proj = inproj(xs, norm_w[l], scale_s, shift_s, w_in, tm_d, tm_d)
            o, stacks["gl_s"] = gla_mixer(proj, gla_s0_s, e, wg_pad, bg, gnw, bs, DEC_PAD, DEC_PAD, 1,
                                          samp_ps, ss, stacks["gl_s"], e, n_odd)
            xs = outproj([o], [w_out], xs, gate_s, final_norm_w, tm_d, tm_d, last)

    y_prompt = xp.reshape(bp, sp, d)
    y_sample = xs.reshape(bs, DEC_PAD, d)[:, :ss]
    st = {k: jnp.stack(v) for k, v in outs.items()}
    st["win_s"] = jnp.transpose(stacks["win_s"], (0, 1, 5, 2, 3, 4))
    st["ss_p"] = stacks["ss_p"].reshape(n_even, bp, SSD_HEADS, SSD_HEAD_DIM, SSD_STATE)
    st["ss_s"] = stacks["ss_s"].reshape(n_even, bs, SSD_HEADS, SSD_HEAD_DIM, SSD_STATE)
    st["gl_p"] = stacks["gl_p"].reshape(n_odd, bp, GLA_HEADS, GLA_DK, GLA_DV)
    st["gl_s"] = stacks["gl_s"].reshape(n_odd, bs, GLA_HEADS, GLA_DK, GLA_DV)
    return (y_prompt, y_sample, st["kv_p"], st["kv_s"], st["win_p"], st["win_s"], st["cv_p"], st["cv_s"],
            st["ss_p"], st["ss_s"], st["gl_p"], st["gl_s"])
```

```python
import functools

import jax
import jax.numpy as jnp
import numpy as np
from jax import lax
from jax.experimental import pallas as pl
from jax.experimental.pallas import tpu as pltpu

F32 = jnp.float32
BF16 = jnp.bfloat16

PAGE_SIZE = 128
NSA_HEADS = 8
NSA_KV_HEADS = 2
HEAD_DIM = 64
NSA_WIDTH = NSA_HEADS * HEAD_DIM
NSA_HG = NSA_HEADS // NSA_KV_HEADS
CMP_BLOCK = 32
CMP_HIDDEN = 2 * HEAD_DIM
SEL_BLOCK = 64
TOP_N = 16
WINDOW = 512
FORCE_BONUS = 1.0e4
ROPE_THETA = 10000.0
SSD_HEADS = 16
SSD_HEAD_DIM = 64
SSD_INNER = SSD_HEADS * SSD_HEAD_DIM
SSD_GROUPS = 2
SSD_STATE = 128
SSD_CONV = 4
SSD_CONV_CH = SSD_INNER + 2 * SSD_GROUPS * SSD_STATE
GLA_HEADS = 4
GLA_DK = 128
GLA_DV = 256
GLA_KEY_WIDTH = GLA_HEADS * GLA_DK
GLA_VAL_WIDTH = GLA_HEADS * GLA_DV
GLA_GATE_RANK = 16
GLA_TAU = 16.0
GLA_CHUNK = 32
SSD_CHUNK = 128

LANES = 128
SUBLANES = 8
VMEM_LIMIT = 56 * 1024 * 1024

TM_IN = 256
TM_OUT = 512
CMP_ROWS = 128
CMP_PAGES = 32
SEQ_PER_STEP = 2
SAMPLES_PER_STEP = 4
GLA_SUB = 4

NEG_BIG = -1.0e30
M_INIT = -1.0e29
DEC_PAD = 8

E_XBC, E_Q, E_ZB, E_ZA, E_KV, E_GD, E_END = 0, 1536, 2048, 3072, 3584, 4352, 4480
GD_GATE, GD_DT = 0, 24
O_Q, O_K, O_V, O_R, O_GLR, O_END = 0, 512, 1024, 2048, 3072, 3200


def _cparams(sem):
    return pltpu.CompilerParams(dimension_semantics=sem, vmem_limit_bytes=VMEM_LIMIT)


def _dot(a, b):
    return jnp.dot(a, b, preferred_element_type=F32)


def _dot_nt(a, b):
    return lax.dot_general(a, b, (((1,), (1,)), ((), ())), preferred_element_type=F32)


def _dot_tn(a, b):
    return lax.dot_general(a, b, (((0,), (0,)), ((), ())), preferred_element_type=F32)


def _split3(x):
    hi = x.astype(BF16)
    r1 = x - hi.astype(F32)
    mid = r1.astype(BF16)
    lo = (r1 - mid.astype(F32)).astype(BF16)
    return hi, mid, lo


def _dot3(x, w01):
    hi, mid, lo = _split3(x)
    return _dot(hi, w01) + _dot(mid, w01) + _dot(lo, w01)


def _dot3_tn(x, w01):
    hi, mid, lo = _split3(x)
    return _dot_tn(hi, w01) + _dot_tn(mid, w01) + _dot_tn(lo, w01)


def _dot3_wx(w01, x):
    hi, mid, lo = _split3(x)
    return _dot(w01, hi) + _dot(w01, mid) + _dot(w01, lo)


def _silu(x):
    return x * (1.0 / (1.0 + jnp.exp(-x)))


def _sigmoid(x):
    return 1.0 / (1.0 + jnp.exp(-x))


def _softplus(x):
    return jnp.maximum(x, 0.0) + jnp.log1p(jnp.exp(-jnp.abs(x)))


def _log_sigmoid(x):
    return jnp.minimum(x, 0.0) - jnp.log1p(jnp.exp(-jnp.abs(x)))


def _mod_kernel(c_ref, w_ref, b_ref, o_ref):
    a = _silu(c_ref[...]).astype(BF16)
    o_ref[0] = _dot(a, w_ref[0]) + b_ref[0]


def mod_all(c_all, w_mod_bf, b_mod, tn=1024):
    nl, d, n = w_mod_bf.shape
    m = c_all.shape[0]
    return pl.pallas_call(
        _mod_kernel,
        grid=(nl, n // tn),
        in_specs=[
            pl.BlockSpec((m, d), lambda l, j: (0, 0)),
            pl.BlockSpec((1, d, tn), lambda l, j: (l, 0, j)),
            pl.BlockSpec((1, 1, tn), lambda l, j: (l, 0, j)),
        ],
        out_specs=pl.BlockSpec((1, m, tn), lambda l, j: (l, 0, j)),
        out_shape=jax.ShapeDtypeStruct((nl, m, n), F32),
        compiler_params=_cparams(("arbitrary", "arbitrary")),
        name="mod_all",
    )(c_all, w_mod_bf, b_mod.reshape(nl, 1, n))


def _inproj_kernel(x_ref, nw_ref, sc_ref, sh_ref, w_ref, o_ref, *, n_chunk):
    x = x_ref[...]
    ms = jnp.mean(x * x, axis=-1, keepdims=True)
    y = x * lax.rsqrt(ms + 1e-6) * nw_ref[...]
    h = (y * (1.0 + sc_ref[0]) + sh_ref[0]).astype(BF16)
    n = o_ref.shape[1]
    for n0 in range(0, n, n_chunk):
        n1 = min(n0 + n_chunk, n)
        o_ref[:, n0:n1] = _dot(h, w_ref[:, n0:n1])


def inproj(x, nw, scale, shift, w_bf, tm, rows_per_mod):
    t, d = x.shape
    n = w_bf.shape[1]
    r = scale.shape[1]
    mod_map = lambda i: ((i * tm) // rows_per_mod, 0, 0)
    return pl.pallas_call(
        functools.partial(_inproj_kernel, n_chunk=640),
        grid=(t // tm,),
        in_specs=[
            pl.BlockSpec((tm, d), lambda i: (i, 0)),
            pl.BlockSpec((1, d), lambda i: (0, 0)),
            pl.BlockSpec((1, r, d), mod_map),
            pl.BlockSpec((1, r, d), mod_map),
            pl.BlockSpec((d, n), lambda i: (0, 0)),
        ],
        out_specs=pl.BlockSpec((tm, n), lambda i: (i, 0)),
        out_shape=jax.ShapeDtypeStruct((t, n), F32),
        compiler_params=_cparams(("arbitrary",)),
        name="inproj",
    )(x, nw.reshape(1, d), scale, shift, w_bf)


def _outproj_kernel(*refs, n_in, final_norm):
    a_refs = refs[:n_in]
    w_refs = refs[n_in:2 * n_in]
    x_ref, g_ref, fw_ref, o_ref = refs[2 * n_in:]
    acc = _dot(a_refs[0][...].astype(BF16), w_refs[0][...])
    for a_ref, w_ref in zip(a_refs[1:], w_refs[1:]):
        acc = acc + _dot(a_ref[...].astype(BF16), w_ref[...])
    y = x_ref[...] + g_ref[0] * acc
    if final_norm:
        ms = jnp.mean(y * y, axis=-1, keepdims=True)
        y = y * lax.rsqrt(ms + 1e-6) * fw_ref[...]
    o_ref[...] = y


def outproj(a_list, w_list, x, gate, final_w, tm, rows_per_mod, final_norm):
    t, d = x.shape
    r = gate.shape[1]
    n_in = len(a_list)
    in_specs = [pl.BlockSpec((tm, a.shape[1]), lambda i: (i, 0)) for a in a_list]
    in_specs += [pl.BlockSpec(w.shape, lambda i: (0, 0)) for w in w_list]
    in_specs += [
        pl.BlockSpec((tm, d), lambda i: (i, 0)),
        pl.BlockSpec((1, r, d), lambda i: ((i * tm) // rows_per_mod, 0, 0)),
        pl.BlockSpec((1, d), lambda i: (0, 0)),
    ]
    return pl.pallas_call(
        functools.partial(_outproj_kernel, n_in=n_in, final_norm=final_norm),
        grid=(t // tm,),
        in_specs=in_specs,
        out_specs=pl.BlockSpec((tm, d), lambda i: (i, 0)),
        out_shape=jax.ShapeDtypeStruct((t, d), F32),
        compiler_params=_cparams(("arbitrary",)),
        name="outproj",
    )(*a_list, *w_list, x, gate, final_w.reshape(1, d))


def _rope_tile(t, cos, sin_signed, lane):
    fwd = pltpu.roll(t, LANES - HEAD_DIM // 2, 1)
    bwd = pltpu.roll(t, HEAD_DIM // 2, 1)
    partner = jnp.where((lane % HEAD_DIM) < HEAD_DIM // 2, fwd, bwd)
    return t * cos + partner * sin_signed


def _nsa_prep_kernel(q_ref, kv01_ref, kv23_ref, kv45_ref, cos_ref, sin_ref,
                     rows_ref, win_ref, qrot_ref, *hm_refs, head_major):
    cos = cos_ref[...]
    sin = sin_ref[...]
    lane = lax.broadcasted_iota(jnp.int32, cos.shape, 1)
    rope = lambda t: _rope_tile(t, cos, sin, lane)
    kv01, kv23, kv45 = kv01_ref[...], kv23_ref[...], kv45_ref[...]
    k_slc = rope(kv23[:, :LANES])
    v_slc = kv23[:, LANES:]
    k_win = rope(kv45[:, :LANES])
    v_win = kv45[:, LANES:]
    rows_ref[:, 0:LANES] = rope(kv01[:, :LANES])
    rows_ref[:, LANES:2 * LANES] = kv01[:, LANES:]
    rows_ref[:, 2 * LANES:3 * LANES] = k_slc
    rows_ref[:, 3 * LANES:4 * LANES] = v_slc
    win_ref[:, 0:LANES] = k_win
    win_ref[:, LANES:2 * LANES] = v_win
    q = q_ref[...]
    qr = [rope(q[:, j * LANES:(j + 1) * LANES]) for j in range(NSA_WIDTH // LANES)]
    for j, t in enumerate(qr):
        qrot_ref[:, j * LANES:(j + 1) * LANES] = t
    if head_major:
        qh_ref, ks_ref, vs_ref, kw_ref, vw_ref = hm_refs
        scale = HEAD_DIM ** -0.5
        for h in range(NSA_HEADS):
            lo = (h % 2) * HEAD_DIM
            qh_ref[0, h] = (qr[h // 2][:, lo:lo + HEAD_DIM] * scale).astype(BF16)
        for g in range(NSA_KV_HEADS):
            lo = g * HEAD_DIM
            ks_ref[0, g] = k_slc[:, lo:lo + HEAD_DIM].astype(BF16)
            vs_ref[0, g] = v_slc[:, lo:lo + HEAD_DIM].astype(BF16)
            kw_ref[0, g] = k_win[:, lo:lo + HEAD_DIM].astype(BF16)
            vw_ref[0, g] = v_win[:, lo:lo + HEAD_DIM].astype(BF16)


def nsa_prep(proj, cos_tab, sin_tab, tm, seq, head_major):
    t = proj.shape[0]
    tab_blocks = cos_tab.shape[0] // tm
    sb = seq // tm
    out_shape = [jax.ShapeDtypeStruct((t, 4 * LANES), F32),
                 jax.ShapeDtypeStruct((t, 2 * LANES), F32),
                 jax.ShapeDtypeStruct((t, NSA_WIDTH), F32)]
    out_specs = [pl.BlockSpec((tm, 4 * LANES), lambda i: (i, 0)),
                 pl.BlockSpec((tm, 2 * LANES), lambda i: (i, 0)),
                 pl.BlockSpec((tm, NSA_WIDTH), lambda i: (i, 0))]
    if head_major:
        nb = t // seq
        hm_map = lambda i: (i // sb, 0, i % sb, 0)
        out_shape.append(jax.ShapeDtypeStruct((nb, NSA_HEADS, seq, HEAD_DIM), BF16))
        out_specs.append(pl.BlockSpec((1, NSA_HEADS, tm, HEAD_DIM), hm_map))
        for _ in range(4):
            out_shape.append(jax.ShapeDtypeStruct((nb, NSA_KV_HEADS, seq, HEAD_DIM), BF16))
            out_specs.append(pl.BlockSpec((1, NSA_KV_HEADS, tm, HEAD_DIM), hm_map))
    kvb = E_KV // (2 * LANES)
    return pl.pallas_call(
        functools.partial(_nsa_prep_kernel, head_major=head_major),
        grid=(t // tm,),
        in_specs=[
            pl.BlockSpec((tm, NSA_WIDTH), lambda i: (i, E_Q // NSA_WIDTH)),
            pl.BlockSpec((tm, 2 * LANES), lambda i: (i, kvb)),
            pl.BlockSpec((tm, 2 * LANES), lambda i: (i, kvb + 1)),
            pl.BlockSpec((tm, 2 * LANES), lambda i: (i, kvb + 2)),
            pl.BlockSpec((tm, LANES), lambda i: (i % tab_blocks, 0)),
            pl.BlockSpec((tm, LANES), lambda i: (i % tab_blocks, 0)),
        ],
        out_specs=out_specs,
        out_shape=out_shape,
        compiler_params=_cparams(("arbitrary",)),
        name="nsa_prep",
    )(proj, proj, proj, proj, cos_tab, sin_tab)


def _nsa_compress_kernel(x_ref, pe_ref, w1_ref, w2_ref, o_ref):
    r = x_ref.shape[0]
    row_w = 4 * LANES
    acc = jnp.zeros((r, 4 * CMP_HIDDEN), F32)
    for t in range(CMP_BLOCK):
        xt = x_ref[:, t * row_w:t * row_w + 2 * LANES] + pe_ref[t:t + 1, :]
        acc = acc + _dot(xt.astype(BF16), w1_ref[t])
    o_ref[...] = _dot(_silu(acc).astype(BF16), w2_ref[...])


def nsa_compress(xblk, pe_tab, w1_blk, w2_blk, r):
    nb, kdim = xblk.shape
    return pl.pallas_call(
        _nsa_compress_kernel,
        grid=(nb // r,),
        in_specs=[
            pl.BlockSpec((r, kdim), lambda i: (i, 0)),
            pl.BlockSpec(pe_tab.shape, lambda i: (0, 0)),
            pl.BlockSpec(w1_blk.shape, lambda i: (0, 0, 0)),
            pl.BlockSpec(w2_blk.shape, lambda i: (0, 0)),
        ],
        out_specs=pl.BlockSpec((r, 4 * HEAD_DIM), lambda i: (i, 0)),
        out_shape=jax.ShapeDtypeStruct((nb, 4 * HEAD_DIM), F32),
        compiler_params=_cparams(("arbitrary",)),
        name="nsa_compress",
    )(xblk, pe_tab, w1_blk, w2_blk)


def _nsa_compress_cache_kernel(pt_ref, *refs, n_page):
    del pt_ref
    page_refs = refs[:n_page]
    pe_ref, w1_ref, w2_ref, o_ref, xbuf = refs[n_page:]
    gd = NSA_KV_HEADS * HEAD_DIM
    per_page = PAGE_SIZE // CMP_BLOCK
    for p, x_ref in enumerate(page_refs):
        for c in range(2):
            xbuf[c, p * PAGE_SIZE:(p + 1) * PAGE_SIZE, :] = x_ref[0, c].reshape(gd, PAGE_SIZE).T + pe_ref[c]
    rows = n_page * per_page
    acc = [jnp.zeros((rows, NSA_KV_HEADS * CMP_HIDDEN), F32) for _ in range(2)]
    for t in range(CMP_BLOCK):
        for c in range(2):
            a = xbuf[c, pl.ds(t, rows, stride=CMP_BLOCK), :].astype(BF16)
            acc[c] = acc[c] + _dot(a, w1_ref[c, t])
    for c in range(2):
        o_ref[:, c * gd:(c + 1) * gd] = _dot(_silu(acc[c]).astype(BF16), w2_ref[c])


def nsa_compress_cache(cache_t, layer, pages, pe_tok, w1_bd, w2_bd, n_page):
    per_page = PAGE_SIZE // CMP_BLOCK
    gd = NSA_KV_HEADS * HEAD_DIM
    full = lambda a: pl.BlockSpec(a.shape, lambda i, pt: (0,) * a.ndim)
    page_blk = (None, 1, 2, NSA_KV_HEADS, HEAD_DIM, PAGE_SIZE)
    page_specs = [pl.BlockSpec(page_blk, lambda i, pt, j=j: (layer, pt[i * n_page + j], 0, 0, 0, 0))
                  for j in range(n_page)]
    return pl.pallas_call(
        functools.partial(_nsa_compress_cache_kernel, n_page=n_page),
        grid_spec=pltpu.PrefetchScalarGridSpec(
            num_scalar_prefetch=1,
            grid=(pages.shape[0] // n_page,),
            in_specs=[*page_specs, full(pe_tok), full(w1_bd), full(w2_bd)],
            out_specs=pl.BlockSpec((n_page * per_page, 2 * gd), lambda i, pt: (i, 0)),
            scratch_shapes=[pltpu.VMEM((2, n_page * PAGE_SIZE, gd), F32)],
        ),
        out_shape=jax.ShapeDtypeStruct((pages.shape[0] * per_page, 2 * gd), F32),
        compiler_params=_cparams(("arbitrary",)),
        name="nsa_compress_cache",
    )(pages, *([cache_t] * n_page), pe_tok, w1_bd, w2_bd)


def _masked_softmax3(s, mask):
    s = jnp.where(mask[None], s, -jnp.inf)
    m = jnp.max(s, axis=-1, keepdims=True)
    m = jnp.where(m > -jnp.inf, m, 0.0)
    p = jnp.exp(s - m)
    d = jnp.sum(p, axis=-1, keepdims=True)
    return p / jnp.where(d > 0, d, 1.0)


def _drain(gen):
    try:
        while True:
            next(gen)
    except StopIteration as stop:
        return stop.value


def _round_robin(gens):
    results = [None] * len(gens)
    live = list(range(len(gens)))
    while live:
        for i in list(live):
            try:
                next(gens[i])
            except StopIteration as stop:
                results[i] = stop.value
                live.remove(i)
        yield
    return results


def _run_interleaved(gens):
    return _drain(_round_robin(gens))


def _cmp_branch_and_select(*args, **kwargs):
    return _drain(_cmp_branch_and_select_stages(*args, **kwargs))


def _cmp_branch_and_select_stages(q2, kc, vc, qpos, qpos_row, nq, imp_sc, n_blk, pair_t=None):
    nc = kc.shape[0]
    s = _dot_nt(q2, kc).reshape(NSA_HG, nq, nc)
    yield
    col = lax.broadcasted_iota(jnp.int32, (1, nc), 1)
    cblk = col if pair_t is not None else 2 * (col % LANES) + col // LANES
    cmp_end = (cblk + 1) * CMP_BLOCK - 1
    p = _masked_softmax3(s, cmp_end <= qpos)
    yield
    o_c = _dot(p.reshape(NSA_HG * nq, nc).astype(BF16), vc).reshape(NSA_HG, nq, HEAD_DIM)
    imp = jnp.sum(p, axis=0)
    if pair_t is not None:
        hi, mid, lo = _split3(imp)
        imp = _dot_nt(pair_t, hi) + _dot_nt(pair_t, mid) + _dot_nt(pair_t, lo)
    else:
        imp = (imp[:, :LANES] + imp[:, LANES:]).T
    yield
    j = lax.broadcasted_iota(jnp.int32, (LANES, 1), 0)
    cur = qpos_row // SEL_BLOCK
    forced = jnp.where(j == 0, 1, jnp.where(j == cur, 1, jnp.where(j == cur - 1, 1, 0))) > 0
    causal = j <= cur
    imp = jnp.where(forced, imp + FORCE_BONUS, imp)
    imp = jnp.where(causal, imp, -jnp.inf)
    def count_ahead(i, ri, cnt):
        wins_ties = jnp.where(j > i, 1.0, 0.0)
        return cnt + jnp.where(ri >= imp, jnp.where(ri > imp, 1.0, wins_ties), 0.0)

    rank = jnp.zeros((LANES, nq), F32)
    if isinstance(n_blk, int):
        for i in range(n_blk):
            rank = count_ahead(i, imp[i:i + 1, :], rank)
    else:
        imp_sc[...] = imp
        rank = lax.fori_loop(0, n_blk, lambda i, c: count_ahead(i, imp_sc[pl.ds(i, 1), :], c), rank)
    sel = jnp.where(causal, jnp.where(rank < TOP_N, 1.0, 0.0), 0.0)
    return o_c, sel.T.astype(BF16)


def _flash_init(nq):
    return (jnp.full((NSA_HG, nq, 1), M_INIT, F32), jnp.zeros((NSA_HG, nq, 1), F32),
            jnp.zeros((NSA_HG, nq, HEAD_DIM), F32))


def _mask_bias(allowed):
    return jnp.where(allowed, 0.0, NEG_BIG)


def _flash_update_stages(carry, s, bias, v, v_transposed=False):
    m, l, acc = carry
    s = s + bias[None]
    m_new = jnp.maximum(m, jnp.max(s, axis=-1, keepdims=True))
    alpha = jnp.exp(m - m_new)
    p = jnp.exp(s - m_new)
    l = alpha * l + jnp.sum(p, axis=-1, keepdims=True)
    h, q, n = p.shape
    p2 = p.reshape(h * q, n).astype(v.dtype)
    yield
    pv = (_dot_nt(p2, v) if v_transposed else _dot(p2, v)).reshape(h, q, HEAD_DIM)
    return m_new, l, alpha * acc + pv


def _flash_finish(carry):
    _, l, acc = carry
    return acc / jnp.where(l > 0, l, 1.0)


def _gate_column(sig, lane, col):
    return jnp.sum(jnp.where(lane == col, sig, 0.0), axis=-1, keepdims=True)


def _softmax_rows(s_ref, b_ref, p_ref, m_ref, l_ref, a_ref, n, tq):
    rb = 2 * SUBLANES
    for r0 in range(0, s_ref.shape[0], rb):
        rows = slice(r0, r0 + rb)
        sb = s_ref[rows, :n] + b_ref[r0 % tq:r0 % tq + rb, :n]
        m_old = m_ref[rows, :]
        m_new = jnp.maximum(m_old, jnp.max(sb, axis=-1, keepdims=True))
        p = jnp.exp(sb - jnp.concatenate([m_new] * (n // LANES), axis=1))
        alpha = jnp.exp(m_old - m_new)
        l_ref[rows, :] = alpha * l_ref[rows, :] + jnp.sum(p, axis=-1, keepdims=True)
        m_ref[rows, :] = m_new
        a_ref[rows, :] = alpha
        p_ref[rows, :n] = p.astype(BF16)


def _nsa_attn_kernel(q_ref, kc_ref, vc_ref, ks_ref, vs_ref, kw_ref, vw_ref, gd_ref, za_ref, e_ref,
                     o_ref, s_sc, p_sc, b_sc, m_sc, l_sc, a_sc, acc_sc, imp_sc, *, tq, tk):
    g = pl.program_id(1)
    q0 = pl.program_id(2) * tq
    q2 = q_ref[0].reshape(NSA_HG * tq, HEAD_DIM)
    qpos = q0 + lax.broadcasted_iota(jnp.int32, (tq, 1), 0)
    qpos_row = q0 + lax.broadcasted_iota(jnp.int32, (1, tq), 1)
    n_blk = (q0 + tq - 1) // SEL_BLOCK + 1
    o_c, sel = _cmp_branch_and_select(q2, kc_ref[0, 0], vc_ref[0, 0], qpos, qpos_row, tq, imp_sc, n_blk)

    def reset():
        m_sc[...] = jnp.full(m_sc.shape, M_INIT, F32)
        l_sc[...] = jnp.zeros(l_sc.shape, F32)
        acc_sc[...] = jnp.zeros(acc_sc.shape, F32)

    def softmax_step(k, v, bias, n):
        b_sc[:, :n] = bias

        def head_stages(h):
            rows = pl.ds(h * tq, tq)
            s_sc[rows, :n] = _dot_nt(q2[h * tq:(h + 1) * tq], k)
            yield
            _softmax_rows(s_sc.at[rows], b_sc, p_sc.at[rows], m_sc.at[rows], l_sc.at[rows], a_sc.at[rows], n, tq)
            yield
            acc_sc[rows, :] = a_sc[rows, :HEAD_DIM] * acc_sc[rows, :] + _dot(p_sc[rows, :n], v)

        _run_interleaved([head_stages(h) for h in range(NSA_HG)])

    def finish():
        l = l_sc[:, :HEAD_DIM]
        return acc_sc[...] / jnp.where(l > 0, l, 1.0)

    def sel_tile(t, diagonal):
        k0 = pl.multiple_of(t * tk, tk)
        bias = (_dot(sel, e_ref[:, pl.ds(k0, tk)]) - 1.0) * (-NEG_BIG)
        if diagonal:
            kpos = k0 + lax.broadcasted_iota(jnp.int32, (1, tk), 1)
            bias = jnp.where(kpos <= qpos, bias, NEG_BIG)
        softmax_step(ks_ref[0, 0, pl.ds(k0, tk), :], vs_ref[0, 0, pl.ds(k0, tk), :], bias, tk)

    n_below = q0 // tk
    reset()

    def below(t, carry):
        sel_tile(t, False)
        return carry

    lax.fori_loop(0, n_below, below, 0)
    sel_tile(n_below, True)
    o_s = finish()

    wn = WINDOW + tq
    w0 = pl.multiple_of(jnp.maximum(q0 - WINDOW, 0), tq)
    dist = qpos - (w0 + lax.broadcasted_iota(jnp.int32, (1, wn), 1))
    reset()
    softmax_step(kw_ref[0, 0, pl.ds(w0, wn), :], vw_ref[0, 0, pl.ds(w0, wn), :],
                 _mask_bias(jnp.where(dist >= 0, dist, WINDOW) < WINDOW), wn)
    o_w = finish()


    sig = _sigmoid(gd_ref[...])
    lane = lax.broadcasted_iota(jnp.int32, sig.shape, 1)
    outs = []
    for h in range(NSA_HG):
        head = g * NSA_HG + h
        hrows = slice(h * tq, (h + 1) * tq)
        g_c = _gate_column(sig, lane, GD_GATE + head)
        g_s = _gate_column(sig, lane, GD_GATE + NSA_HEADS + head)
        g_w = _gate_column(sig, lane, GD_GATE + 2 * NSA_HEADS + head)
        outs.append(g_c * o_c[h] + g_s * o_s[hrows] + g_w * o_w[hrows])
    o_ref[...] = jnp.concatenate(outs, axis=-1) * _silu(za_ref[...])


def nsa_attn_prompt(proj, qh, kc, vc, ks, vs, kw, vw, e_mat, tq=128, tk=1024):
    nb, _, seq, _ = qh.shape
    assert seq >= WINDOW + tq and tk % tq == 0 and seq % tk == 0
    rows, wn = NSA_HG * tq, max(WINDOW + tq, tk)
    nq = seq // tq
    gw = NSA_HG * HEAD_DIM
    kv_spec = pl.BlockSpec((1, 1, seq, HEAD_DIM), lambda b, g, i: (b, g, 0, 0))
    c_spec = pl.BlockSpec((1, 1, 2 * LANES, HEAD_DIM), lambda b, g, i: (b, g, 0, 0))
    return pl.pallas_call(
        functools.partial(_nsa_attn_kernel, tq=tq, tk=tk),
        grid=(nb, NSA_KV_HEADS, nq),
        in_specs=[
            pl.BlockSpec((1, NSA_HG, tq, HEAD_DIM), lambda b, g, i: (b, g, i, 0)),
            c_spec, c_spec, kv_spec, kv_spec, kv_spec, kv_spec,
            pl.BlockSpec((tq, LANES), lambda b, g, i: (b * nq + i, E_GD // LANES)),
            pl.BlockSpec((tq, gw), lambda b, g, i: (b * nq + i, E_ZA // gw + g)),
            pl.BlockSpec(e_mat.shape, lambda b, g, i: (0, 0)),
        ],
        out_specs=pl.BlockSpec((tq, gw), lambda b, g, i: (b * nq + i, g)),
        out_shape=jax.ShapeDtypeStruct((nb * seq, NSA_WIDTH), F32),
        scratch_shapes=[
            pltpu.VMEM((rows, wn), F32),
            pltpu.VMEM((rows, wn), BF16),
            pltpu.VMEM((tq, wn), F32),
            pltpu.VMEM((rows, LANES), F32),
            pltpu.VMEM((rows, LANES), F32),
            pltpu.VMEM((rows, LANES), F32),
            pltpu.VMEM((rows, HEAD_DIM), F32),
            pltpu.VMEM((LANES, tq), F32),
        ],
        compiler_params=_cparams(("arbitrary", "arbitrary", "arbitrary")),
        name="nsa_attn_prompt",
    )(qh, kc, vc, ks, vs, kw, vw, proj, proj, e_mat)


def _nsa_dec_kernel(pt_ref, q_ref, cmp_ref, *rest, past_len, n_pages, n_new):
    del pt_ref
    n_samp = q_ref.shape[0]
    ins = rest[n_samp * n_pages:n_samp * n_pages + 7]
    o_ref, wout_ref, imp_sc = rest[-3:]
    pwin_ref, wnew_ref = ins[1], ins[2]
    outs = _run_interleaved([
        _nsa_dec_sample(si, q_ref, cmp_ref, rest[si * n_pages:(si + 1) * n_pages], *ins, None, imp_sc,
                        past_len=past_len)
        for si in range(n_samp)])
    for si in range(n_samp):
        o_ref[si] = outs[si]
        wnew = wnew_ref[si]
        for c in range(2):
            for g in range(NSA_KV_HEADS):
                new_t = wnew[:, c * LANES + g * HEAD_DIM:c * LANES + (g + 1) * HEAD_DIM].T
                wout_ref[si, c, g] = jnp.concatenate([pwin_ref[si, c, g][:, n_new:], new_t[:, :n_new]], axis=1)


def _nsa_dec_sample(si, q_ref, cmp_ref, page_refs, rows_ref, pwin_ref, wnew_ref, gd_ref, za_ref, e_ref, pair_ref,
                    o_ref, imp_sc, *, past_len):
    del o_ref
    nq = DEC_PAD
    scale = HEAD_DIM ** -0.5
    qpos = past_len + lax.broadcasted_iota(jnp.int32, (nq, 1), 0)
    qpos_row = past_len + lax.broadcasted_iota(jnp.int32, (1, nq), 1)
    n_blk = (past_len + nq - 1) // SEL_BLOCK + 1
    q = q_ref[si]
    cmp = cmp_ref[si]

    def q_group(g):
        parts = [q[:, (g * NSA_HG + h) * HEAD_DIM:(g * NSA_HG + h + 1) * HEAD_DIM] for h in range(NSA_HG)]
        return jnp.concatenate(parts, axis=0) * scale

    qg = [q_group(g) for g in range(NSA_KV_HEADS)]

    rows = rows_ref[si]
    wnew = wnew_ref[si]
    win_keep = pwin_ref.shape[-1]
    sig = _sigmoid(gd_ref[si])
    lane = lax.broadcasted_iota(jnp.int32, sig.shape, 1)
    kpos = lax.broadcasted_iota(jnp.int32, (1, past_len), 1)
    npos = past_len + lax.broadcasted_iota(jnp.int32, (1, nq), 1)
    wpos = past_len - win_keep + lax.broadcasted_iota(jnp.int32, (1, win_keep), 1)
    outs = []
    for g in range(NSA_KV_HEADS):
        qb = qg[g].astype(BF16)
        kc = cmp[:, g * HEAD_DIM:(g + 1) * HEAD_DIM].astype(BF16)
        vc = cmp[:, LANES + g * HEAD_DIM:LANES + (g + 1) * HEAD_DIM].astype(BF16)
        o_c, sel = yield from _cmp_branch_and_select_stages(
            qb, kc, vc, qpos, qpos_row, nq, imp_sc.at[si * NSA_KV_HEADS + g], n_blk, pair_ref[...])
        yield
        kt = jnp.concatenate([r[0, 0, g] for r in page_refs], axis=1).astype(BF16)
        vt = jnp.concatenate([r[0, 1, g] for r in page_refs], axis=1).astype(BF16)
        s = _dot(qb, kt).reshape(NSA_HG, nq, past_len)
        blk_on = _dot(sel, e_ref[:, :past_len])
        yield
        bias = _mask_bias(jnp.where(kpos <= qpos, blk_on, 0.0) > 0.5)
        carry = yield from _flash_update_stages(_flash_init(nq), s, bias, vt, v_transposed=True)
        yield
        kn = rows[:, 2 * LANES + g * HEAD_DIM:2 * LANES + (g + 1) * HEAD_DIM]
        vn = rows[:, 3 * LANES + g * HEAD_DIM:3 * LANES + (g + 1) * HEAD_DIM]
        s = _dot_nt(qg[g], kn).reshape(NSA_HG, nq, nq)
        blk_on = _dot(sel, e_ref[:, past_len:past_len + LANES])[:, :nq]
        yield
        bias = _mask_bias(jnp.where(npos <= qpos, blk_on, 0.0) > 0.5)
        o_s = _flash_finish((yield from _flash_update_stages(carry, s, bias, vn)))
        yield
        s = _dot(qb, pwin_ref[si, 0, g].astype(BF16)).reshape(NSA_HG, nq, win_keep)
        yield
        dist = qpos - wpos
        bias = _mask_bias(jnp.where(wpos >= 0, jnp.where(dist >= 0, dist, WINDOW), WINDOW) < WINDOW)
        carry = yield from _flash_update_stages(_flash_init(nq), s, bias, pwin_ref[si, 1, g].astype(BF16),
                                                v_transposed=True)
        yield
        kwn = wnew[:, g * HEAD_DIM:(g + 1) * HEAD_DIM]
        vwn = wnew[:, LANES + g * HEAD_DIM:LANES + (g + 1) * HEAD_DIM]
        s = _dot_nt(qg[g], kwn).reshape(NSA_HG, nq, nq)
        yield
        dist = qpos - npos
        bias = _mask_bias(jnp.where(dist >= 0, dist, WINDOW) < WINDOW)
        o_w = _flash_finish((yield from _flash_update_stages(carry, s, bias, vwn)))
        for h in range(NSA_HG):
            head = g * NSA_HG + h
            g_c = _gate_column(sig, lane, GD_GATE + head)
            g_s = _gate_column(sig, lane, GD_GATE + NSA_HEADS + head)
            g_w = _gate_column(sig, lane, GD_GATE + 2 * NSA_HEADS + head)
            outs.append(g_c * o_c[h] + g_s * o_s[h] + g_w * o_w[h])
        yield
    return jnp.concatenate(outs, axis=-1) * _silu(za_ref[si])


def nsa_attn_decode(page_table, layer, qrot, cmp, cache_t, rows, pwin_t, wnew, proj3, e_mat, past_len, n_samp,
                    n_new, win_stack):
    nb, npg = page_table.shape
    win_keep = pwin_t.shape[-1]
    ncb = cmp.shape[1]
    per_b = lambda blk: pl.BlockSpec(blk, lambda b, pt: (b,) + (0,) * (len(blk) - 1))
    page_blk = (None, 1, 2, NSA_KV_HEADS, HEAD_DIM, PAGE_SIZE)
    page_specs = [pl.BlockSpec(page_blk, lambda b, pt, si=si, j=j: (layer, pt[b * n_samp + si, j], 1, 0, 0, 0))
                  for si in range(n_samp) for j in range(npg)]
    pair_t = (jnp.arange(ncb, dtype=jnp.int32)[None, :] // 2
              == jnp.arange(LANES, dtype=jnp.int32)[:, None]).astype(BF16)
    win_blk = (None, n_samp, 2, NSA_KV_HEADS, HEAD_DIM, win_keep)
    win_spec = pl.BlockSpec(win_blk, lambda b, pt: (layer, b, 0, 0, 0, 0))
    args = [page_table, qrot, cmp, *([cache_t] * (n_samp * npg)), rows, pwin_t, wnew, proj3, proj3, e_mat, pair_t]
    in_specs = [
        per_b((n_samp, DEC_PAD, NSA_WIDTH)),
        per_b((n_samp, ncb, 4 * HEAD_DIM)),
        *page_specs,
        per_b((n_samp, DEC_PAD, 4 * LANES)),
        win_spec,
        per_b((n_samp, DEC_PAD, 2 * LANES)),
        pl.BlockSpec((n_samp, DEC_PAD, LANES), lambda b, pt: (b, 0, E_GD // LANES)),
        pl.BlockSpec((n_samp, DEC_PAD, NSA_WIDTH), lambda b, pt: (b, 0, E_ZA // NSA_WIDTH)),
        pl.BlockSpec(e_mat.shape, lambda b, pt: (0, 0)),
        pl.BlockSpec(pair_t.shape, lambda b, pt: (0, 0)),
    ]
    aliases = {}
    if win_stack is not None:
        aliases = {len(args): 1}
        args.append(win_stack)
        in_specs.append(pl.BlockSpec(memory_space=pl.ANY))
    return pl.pallas_call(
        functools.partial(_nsa_dec_kernel, past_len=past_len, n_pages=npg, n_new=n_new),
        grid_spec=pltpu.PrefetchScalarGridSpec(
            num_scalar_prefetch=1,
            grid=(nb // n_samp,),
            in_specs=in_specs,
            out_specs=[per_b((n_samp, DEC_PAD, NSA_WIDTH)), win_spec],
            scratch_shapes=[pltpu.VMEM((n_samp * NSA_KV_HEADS, LANES, DEC_PAD), F32)],
        ),
        out_shape=[jax.ShapeDtypeStruct((nb, DEC_PAD, NSA_WIDTH), F32),
                   jax.ShapeDtypeStruct(pwin_t.shape, F32)],
        input_output_aliases=aliases,
        compiler_params=_cparams(("arbitrary",)),
        name="nsa_attn_decode",
    )(*args)


def _tri_masks(c):
    row = lax.broadcasted_iota(jnp.int32, (c, c), 0)
    col = lax.broadcasted_iota(jnp.int32, (c, c), 1)
    lower = row >= col
    return lower, jnp.where(lower, 1.0, 0.0).astype(BF16), jnp.where(row <= col, 1.0, 0.0).astype(BF16)


def _ssd_kernel(zb_ref, xbc_ref, gd_ref, prev_ref, h0_ref, cw_ref, cb_ref, dtb_ref, alog_ref, alog16_ref,
                dexp_ref, nw_ref, e16_ref, *rest, chunk, n_valid):
    y_ref, so_ref, h_sc, xbuf = rest[-4:]
    c = pl.program_id(1)
    n_seq = zb_ref.shape[0]
    hp = SSD_HEADS // SSD_GROUPS * SSD_HEAD_DIM
    halo = SUBLANES

    @pl.when(c == 0)
    def _():
        h_sc[...] = h0_ref[...]
        xbuf[:, 0:halo] = prev_ref[...]

    lower, tri, tri_t = _tri_masks(chunk)
    ones8 = jnp.ones((chunk, SUBLANES), BF16)
    lane = lax.broadcasted_iota(jnp.int32, (chunk, LANES), 1)
    low_half = lane < SSD_HEAD_DIM

    def seq_stages(b):
        xbuf[b, halo:halo + chunk] = xbc_ref[b]
        x_all = xbuf[b]
        conv = cb_ref[...]
        for i in range(SSD_CONV):
            shift = SSD_CONV - 1 - i
            tap = x_all if shift == 0 else pltpu.roll(x_all, shift, 0)
            conv = conv + tap[halo:halo + chunk] * cw_ref[i:i + 1, :]
        xbuf[b, 0:halo] = xbuf[b, chunk:chunk + halo]
        act = _silu(conv)
        xs = act[:, :SSD_INNER]
        bm = act[:, SSD_INNER:SSD_INNER + SSD_GROUPS * SSD_STATE]
        cm = act[:, SSD_INNER + SSD_GROUPS * SSD_STATE:]

        dt16 = _softplus(gd_ref[b] + dtb_ref[...])[:, GD_DT:GD_DT + SSD_HEADS]
        if n_valid is not None:
            tok = c * chunk + lax.broadcasted_iota(jnp.int32, (chunk, 1), 0)
            dt16 = jnp.where(tok < n_valid, dt16, 0.0)
        yield
        dt = _dot3(dt16, e16_ref[...])
        da = dt * (-jnp.exp(alog_ref[...]))
        da16 = dt16 * (-jnp.exp(alog16_ref[...]))
        yield
        cum = _dot3_wx(tri, da)
        yield
        cum16 = _dot3_wx(tri, da16)
        cum16_t = _dot3_tn(da16, tri_t)
        yield
        decay_col = jnp.exp(_dot3_tn(da, ones8)[:, 0:1])
        xdt = xs * dt
        cum_last = cum[chunk - 1:chunk, :]
        xw = (xdt * jnp.exp(cum_last - cum)).astype(BF16)
        xdt_b = xdt.astype(BF16)

        y_intra = [None] * (SSD_INNER // LANES)
        y_inter = []
        for g in range(SSD_GROUPS):
            cg = cm[:, g * SSD_STATE:(g + 1) * SSD_STATE].astype(BF16)
            bg = bm[:, g * SSD_STATE:(g + 1) * SSD_STATE].astype(BF16)
            h_g = h_sc[b, g * hp:(g + 1) * hp, :]
            yield
            cb = _dot_nt(cg, bg)
            y_inter.append(_dot_nt(cg, h_g.astype(BF16)))
            for hh in range(SSD_HEADS // SSD_GROUPS):
                h = g * (SSD_HEADS // SSD_GROUPS) + hh
                diff = cum16[:, h:h + 1] - cum16_t[h:h + 1, :]
                lmat = jnp.where(lower, jnp.exp(jnp.where(lower, diff, 0.0)), 0.0)
                m = (cb * lmat).astype(BF16)
                pair = h // 2
                x_pair = xdt_b[:, pair * LANES:(pair + 1) * LANES]
                keep = low_half if h % 2 == 0 else jnp.logical_not(low_half)
                yield
                contrib = _dot(m, jnp.where(keep, x_pair, jnp.zeros_like(x_pair)))
                y_intra[pair] = contrib if y_intra[pair] is None else y_intra[pair] + contrib
            yield
            h_sc[b, g * hp:(g + 1) * hp, :] = (h_g * decay_col[g * hp:(g + 1) * hp, :]
                                               + _dot_tn(xw[:, g * hp:(g + 1) * hp], bg))
        y = jnp.concatenate(y_intra, axis=-1) + jnp.concatenate(y_inter, axis=-1) * jnp.exp(cum)
        y = y + xs * dexp_ref[...]
        y = y * _silu(zb_ref[b])
        ms = jnp.mean(y * y, axis=-1, keepdims=True)
        y_ref[b] = y * lax.rsqrt(ms + 1e-6) * nw_ref[...]

    _run_interleaved([seq_stages(b) for b in range(n_seq)])

    @pl.when(c == pl.num_programs(1) - 1)
    def _():
        so_ref[...] = h_sc[...]


def _stacked_state_io(stack, n_layers, blk, out_layer, n_in):
    shape = jax.ShapeDtypeStruct((n_layers,) + blk[0], F32)
    spec = pl.BlockSpec((None,) + blk[1], lambda b, c: (out_layer, b, 0, 0))
    if stack is None:
        return spec, shape, [], [], {}
    return spec, shape, [stack], [pl.BlockSpec(memory_space=pl.ANY)], {n_in: 1}


def ssd_mixer(proj, prev8, h0, layer, conv_w, conv_b, dtb_row, alog_exp, alog16, d_exp, norm_w, e16, nb, seq, chunk,
              n_seq, n_valid, stack, out_layer, n_layers):
    full = lambda a: pl.BlockSpec(a.shape, lambda b, c: (0,) * a.ndim)
    proj3 = proj.reshape(nb, seq, E_END)
    col_spec = lambda width, off: pl.BlockSpec((n_seq, chunk, width), lambda b, c: (b, c, off // width))
    in_specs = [
        col_spec(SSD_INNER, E_ZB), col_spec(SSD_CONV_CH, E_XBC), col_spec(LANES, E_GD),
        pl.BlockSpec((n_seq, SUBLANES, SSD_CONV_CH), lambda b, c: (b, 0, 0)),
        pl.BlockSpec((None, n_seq, SSD_INNER, SSD_STATE), lambda b, c: (layer, b, 0, 0)),
        full(conv_w), full(conv_b), full(dtb_row), full(alog_exp), full(alog16), full(d_exp), full(norm_w),
        full(e16),
    ]
    st_spec, st_shape, extra, extra_specs, aliases = _stacked_state_io(
        stack, n_layers, ((nb, SSD_INNER, SSD_STATE), (n_seq, SSD_INNER, SSD_STATE)), out_layer, len(in_specs))
    y, st = pl.pallas_call(
        functools.partial(_ssd_kernel, chunk=chunk, n_valid=n_valid),
        grid=(nb // n_seq, seq // chunk),
        in_specs=in_specs + extra_specs,
        out_specs=[pl.BlockSpec((n_seq, chunk, SSD_INNER), lambda b, c: (b, c, 0)), st_spec],
        out_shape=[jax.ShapeDtypeStruct((nb, seq, SSD_INNER), F32), st_shape],
        scratch_shapes=[pltpu.VMEM((n_seq, SSD_INNER, SSD_STATE), F32),
                        pltpu.VMEM((n_seq, SUBLANES + chunk, SSD_CONV_CH), F32)],
        input_output_aliases=aliases,
        compiler_params=_cparams(("arbitrary", "arbitrary")),
        name="ssd_mixer",
    )(proj3, proj3, proj3, prev8, h0, conv_w, conv_b, dtb_row, alog_exp, alog16, d_exp, norm_w, e16, *extra)
    return y.reshape(nb * seq, SSD_INNER), st


def _gla_kernel(q_ref, k_ref, v_ref, r_ref, glr_ref, s0_ref, wg_ref, bg_ref, nw_ref, *rest,
                chunk, n_sub, n_valid):
    o_ref, so_ref, s_sc = rest[-3:]
    c = pl.program_id(1)
    n_seq = q_ref.shape[0]
    span = n_sub * chunk

    @pl.when(c == 0)
    def _():
        s_sc[...] = s0_ref[...]

    row = lax.broadcasted_iota(jnp.int32, (span, span), 0)
    col = lax.broadcasted_iota(jnp.int32, (span, span), 1)
    lower = jnp.where(row // chunk == col // chunk, row - col, -1) >= 0
    tri = jnp.where(lower, 1.0, 0.0).astype(BF16)
    crow = lax.broadcasted_iota(jnp.int32, (span, n_sub * SUBLANES), 0) // chunk
    ccol = lax.broadcasted_iota(jnp.int32, (span, n_sub * SUBLANES), 1) // SUBLANES
    chunk_ones = jnp.where(crow == ccol, 1.0, 0.0).astype(BF16)
    nw = nw_ref[...]

    def head_stages(b, h, qg, kg, kd, v, r, total):
        kcols = slice(h * GLA_DK, (h + 1) * GLA_DK)
        vcols = slice(h * GLA_DV, (h + 1) * GLA_DV)
        att = jnp.where(lower, _dot_nt(qg[:, kcols], kg[:, kcols]), 0.0)
        yield
        o = _dot(att.astype(BF16), v[:, vcols])
        s_h = s_sc[b, kcols, :]
        carried = []
        for u in range(n_sub):
            rows = slice(u * chunk, (u + 1) * chunk)
            yield
            carried.append(_dot(qg[rows, kcols], s_h.astype(BF16)))
            decay = jnp.exp(total[kcols, u * SUBLANES:u * SUBLANES + 1])
            s_h = s_h * decay + _dot_tn(kd[rows, kcols], v[rows, vcols])
        s_sc[b, kcols, :] = s_h
        yield
        o = o + jnp.concatenate(carried, axis=0)
        ms = jnp.mean(o * o, axis=-1, keepdims=True)
        o = o * lax.rsqrt(ms + 1e-6) * nw
        o_ref[b, :, vcols] = o * _silu(r[:, vcols])

    def seq_stages(b):
        lg = _log_sigmoid(_dot(glr_ref[b].astype(BF16), wg_ref[...]) + bg_ref[...]) * (1.0 / GLA_TAU)
        k = k_ref[b]
        if n_valid is not None:
            tok = c * span + lax.broadcasted_iota(jnp.int32, (span, 1), 0)
            lg = jnp.where(tok < n_valid, lg, 0.0)
            k = jnp.where(tok < n_valid, k, 0.0)
        yield
        bcum = _dot3_wx(tri, lg)
        yield
        total = _dot3_tn(lg, chunk_ones)
        qg = (q_ref[b] * (GLA_DK ** -0.5) * jnp.exp(bcum)).astype(BF16)
        kg = (k * jnp.exp(-bcum)).astype(BF16)
        kd = jnp.concatenate(
            [k[u * chunk:(u + 1) * chunk] * jnp.exp(bcum[(u + 1) * chunk - 1:(u + 1) * chunk]
                                                    - bcum[u * chunk:(u + 1) * chunk]) for u in range(n_sub)],
            axis=0).astype(BF16)
        v = v_ref[b].astype(BF16)
        r = r_ref[b]
        yield
        yield from _round_robin([head_stages(b, h, qg, kg, kd, v, r, total) for h in range(GLA_HEADS)])

    _run_interleaved([seq_stages(b) for b in range(n_seq)])

    @pl.when(c == pl.num_programs(1) - 1)
    def _():
        so_ref[...] = s_sc[...]


def gla_mixer(proj, s0, layer, wg_pad, bg, norm_w, nb, seq, chunk, n_sub, n_seq, n_valid, stack, out_layer,
              n_layers):
    span = n_sub * chunk
    full = lambda a: pl.BlockSpec(a.shape, lambda b, c: (0,) * a.ndim)
    srows = GLA_HEADS * GLA_DK
    proj3 = proj.reshape(nb, seq, O_END)
    col_spec = lambda width, off: pl.BlockSpec((n_seq, span, width), lambda b, c: (b, c, off // width))
    in_specs = [
        col_spec(GLA_KEY_WIDTH, O_Q), col_spec(GLA_KEY_WIDTH, O_K), col_spec(GLA_VAL_WIDTH, O_V),
        col_spec(GLA_VAL_WIDTH, O_R), col_spec(LANES, O_GLR),
        pl.BlockSpec((None, n_seq, srows, GLA_DV), lambda b, c: (layer, b, 0, 0)),
        full(wg_pad), full(bg), full(norm_w),
    ]
    st_spec, st_shape, extra, extra_specs, aliases = _stacked_state_io(
        stack, n_layers, ((nb, srows, GLA_DV), (n_seq, srows, GLA_DV)), out_layer, len(in_specs))
    o, st = pl.pallas_call(
        functools.partial(_gla_kernel, chunk=chunk, n_sub=n_sub, n_valid=n_valid),
        grid=(nb // n_seq, seq // span),
        in_specs=in_specs + extra_specs,
        out_specs=[pl.BlockSpec((n_seq, span, GLA_VAL_WIDTH), lambda b, c: (b, c, 0)), st_spec],
        out_shape=[jax.ShapeDtypeStruct((nb, seq, GLA_VAL_WIDTH), F32), st_shape],
        scratch_shapes=[pltpu.VMEM((n_seq, srows, GLA_DV), F32)],
        input_output_aliases=aliases,
        compiler_params=_cparams(("arbitrary", "arbitrary")),
        name="gla_mixer",
    )(proj3, proj3, proj3, proj3, proj3, s0, wg_pad, bg, norm_w, *extra)
    return o.reshape(nb * seq, GLA_VAL_WIDTH), st


def _even_weight(w):
    q, g_a, kv, z_a, z_b, xbc, dt = jnp.split(w, list(np.cumsum(
        [NSA_WIDTH, 3 * NSA_HEADS, 6 * NSA_KV_HEADS * HEAD_DIM, NSA_WIDTH, SSD_INNER, SSD_CONV_CH])), axis=-1)
    pad = jnp.zeros((w.shape[0], E_END - E_GD - 3 * NSA_HEADS - SSD_HEADS), w.dtype)
    return jnp.concatenate([xbc, q, z_b, z_a, kv, g_a, dt, pad], axis=-1).astype(BF16)


def _odd_weight(w):
    q, k, v, glr, r = jnp.split(w, list(np.cumsum(
        [GLA_KEY_WIDTH, GLA_KEY_WIDTH, GLA_VAL_WIDTH, GLA_GATE_RANK])), axis=-1)
    pad = jnp.zeros((w.shape[0], O_END - O_GLR - GLA_GATE_RANK), w.dtype)
    return jnp.concatenate([q, k, v, r, glr, pad], axis=-1).astype(BF16)


def _rope_tables(pos):
    half = HEAD_DIM // 2
    inv = ROPE_THETA ** (-jnp.arange(half, dtype=F32) / half)
    ang = pos.astype(F32)[:, None] * inv[None, :]
    cos, sin = jnp.cos(ang), jnp.sin(ang)
    reps = LANES // HEAD_DIM
    return jnp.tile(jnp.concatenate([cos, cos], -1), (1, reps)), jnp.tile(jnp.concatenate([-sin, sin], -1), (1, reps))


def _compress_weights(pe, w1, w2):
    pe_tab = jnp.concatenate([pe[0], pe[0], pe[1], pe[1]], axis=-1)
    w1r = w1.reshape(2, CMP_BLOCK, HEAD_DIM, CMP_HIDDEN)
    w1_blk = jnp.zeros((CMP_BLOCK, 4 * HEAD_DIM, 4 * CMP_HIDDEN), F32)
    w2_blk = jnp.zeros((4 * CMP_HIDDEN, 4 * HEAD_DIM), F32)
    for part in range(4):
        src = part // 2
        w1_blk = w1_blk.at[:, part * HEAD_DIM:(part + 1) * HEAD_DIM,
                           part * CMP_HIDDEN:(part + 1) * CMP_HIDDEN].set(w1r[src])
        w2_blk = w2_blk.at[part * CMP_HIDDEN:(part + 1) * CMP_HIDDEN,
                           part * HEAD_DIM:(part + 1) * HEAD_DIM].set(w2[src])
    return pe_tab, w1_blk.astype(BF16), w2_blk.astype(BF16)


def _compress_weights_cache(pe, w1, w2):
    w1r = w1.reshape(2, CMP_BLOCK, HEAD_DIM, CMP_HIDDEN)
    w1_bd = jnp.zeros((2, CMP_BLOCK, NSA_KV_HEADS * HEAD_DIM, NSA_KV_HEADS * CMP_HIDDEN), F32)
    w2_bd = jnp.zeros((2, NSA_KV_HEADS * CMP_HIDDEN, NSA_KV_HEADS * HEAD_DIM), F32)
    for g in range(NSA_KV_HEADS):
        w1_bd = w1_bd.at[:, :, g * HEAD_DIM:(g + 1) * HEAD_DIM, g * CMP_HIDDEN:(g + 1) * CMP_HIDDEN].set(w1r)
        w2_bd = w2_bd.at[:, g * CMP_HIDDEN:(g + 1) * CMP_HIDDEN, g * HEAD_DIM:(g + 1) * HEAD_DIM].set(w2)
    pe_tok = jnp.tile(jnp.concatenate([pe] * NSA_KV_HEADS, axis=-1), (1, PAGE_SIZE // CMP_BLOCK, 1))
    return pe_tok, w1_bd.astype(BF16), w2_bd.astype(BF16)


def _cmp_layout(cmp, nb, nc, dtype):
    c = cmp.reshape(nb, nc, 2, NSA_KV_HEADS, HEAD_DIM).transpose(2, 0, 3, 1, 4)
    halves = []
    for par in range(2):
        h = c[:, :, :, par::2]
        halves.append(jnp.pad(h, ((0, 0), (0, 0), (0, 0), (0, LANES - h.shape[3]), (0, 0))))
    c = jnp.concatenate(halves, axis=3).astype(dtype)
    return c[0], c[1]


def _per_step(n, want):
    return want if n % want == 0 else 1


def _sel_expand(n_keys):
    blk = jnp.arange(n_keys, dtype=jnp.int32) // SEL_BLOCK
    return (blk[None, :] == jnp.arange(LANES, dtype=jnp.int32)[:, None]).astype(BF16)


def kernel(x_prompt, x_sample, c_prompt, c_sample, cache_nsa, cache_nsa_win, state_ssd_conv, state_ssd, state_gla, page_table, norm_w, w_mod, b_mod, w_in_even, w_out_even, nsa_cmp_pe, nsa_cmp_w1, nsa_cmp_w2, ssd_conv_w, ssd_conv_b, ssd_dt_bias, ssd_a_log, ssd_d, ssd_norm_w, w_in_odd, gla_w_gate2, gla_b_gate, gla_norm_w, w_out_odd, final_norm_w):
    bp, sp, d = x_prompt.shape
    bs, ss, _ = x_sample.shape
    depth = norm_w.shape[0]
    npg = page_table.shape[1]
    past_len = npg * PAGE_SIZE
    assert ss <= cache_nsa_win.shape[2]
    assert sp % 512 == 0 and sp // SEL_BLOCK <= LANES and sp // CMP_BLOCK <= 2 * LANES
    assert ss <= DEC_PAD and ss < CMP_BLOCK and past_len % SEL_BLOCK == 0 and ss >= SSD_CONV - 1
    assert past_len // SEL_BLOCK + 1 <= LANES and (bs * DEC_PAD) % TM_IN == 0 and (bs * npg) % CMP_PAGES == 0
    tp, td = bp * sp, bs * DEC_PAD
    tm_d = TM_IN
    seq_ps, samp_ps = _per_step(bp, SEQ_PER_STEP), _per_step(bs, SAMPLES_PER_STEP)

    c_all = jnp.concatenate([c_prompt, c_sample], axis=0)
    c_all = jnp.pad(c_all, ((0, -c_all.shape[0] % SUBLANES), (0, 0)))
    mod = mod_all(c_all, w_mod.astype(BF16), b_mod)

    def mods(l):
        shift, scale, gate = jnp.split(mod[l], 3, axis=-1)
        mp = [m[:bp].reshape(bp, 1, d) for m in (shift, scale, gate)]
        ms = [jnp.repeat(m[bp:bp + bs], DEC_PAD, axis=0).reshape(td // tm_d, tm_d, d) for m in (shift, scale, gate)]
        return mp, ms

    xp = x_prompt.reshape(tp, d)
    xs = jnp.pad(x_sample, ((0, 0), (0, DEC_PAD - ss), (0, 0))).reshape(td, d)

    cos_p, sin_p = _rope_tables(jnp.arange(sp, dtype=jnp.int32))
    cos_s, sin_s = _rope_tables(past_len + jnp.arange(DEC_PAD, dtype=jnp.int32))
    cos_s, sin_s = jnp.tile(cos_s, (tm_d // DEC_PAD, 1)), jnp.tile(sin_s, (tm_d // DEC_PAD, 1))
    cache_t = jnp.transpose(cache_nsa, (0, 1, 3, 4, 5, 2))
    win_t = jnp.transpose(cache_nsa_win, (0, 1, 3, 4, 5, 2))
    ssd_h0_s = state_ssd.reshape(state_ssd.shape[0], bs, SSD_INNER, SSD_STATE)
    gla_s0_s = state_gla.reshape(state_gla.shape[0], bs, GLA_HEADS * GLA_DK, GLA_DV)
    e_mat_p = _sel_expand(sp)
    e_mat_s = _sel_expand(past_len + LANES)
    e16 = (jnp.arange(SSD_INNER, dtype=jnp.int32)[None, :] // SSD_HEAD_DIM
           == jnp.arange(SSD_HEADS, dtype=jnp.int32)[:, None]).astype(BF16)

    outs = {k: [] for k in ("kv_p", "kv_s", "win_p", "cv_p", "cv_s")}
    stacks = dict.fromkeys(("win_s", "ss_p", "ss_s", "gl_p", "gl_s"))
    n_even, n_odd = (depth + 1) // 2, depth // 2
    for l in range(depth):
        e = l // 2
        (shift_p, scale_p, gate_p), (shift_s, scale_s, gate_s) = mods(l)
        last = l == depth - 1
        if l % 2 == 0:
            w_in = _even_weight(w_in_even[e])
            w_out = w_out_even[e].astype(BF16)
            w_out_a, w_out_b = w_out[:NSA_WIDTH], w_out[NSA_WIDTH:]
            pe_tab, w1_blk, w2_blk = _compress_weights(nsa_cmp_pe[e], nsa_cmp_w1[e], nsa_cmp_w2[e])
            conv_w, conv_b = ssd_conv_w[e], ssd_conv_b[e].reshape(1, SSD_CONV_CH)
            dtb_row = jnp.zeros((1, LANES), F32).at[0, GD_DT:GD_DT + SSD_HEADS].set(ssd_dt_bias[e])
            alog16 = ssd_a_log[e].reshape(1, SSD_HEADS)
            alog_exp = jnp.repeat(ssd_a_log[e], SSD_HEAD_DIM).reshape(1, SSD_INNER)
            d_exp = jnp.repeat(ssd_d[e], SSD_HEAD_DIM).reshape(1, SSD_INNER)
            ssd_nw = ssd_norm_w[e].reshape(1, SSD_INNER)
            ssd_args = (conv_w, conv_b, dtb_row, alog_exp, alog16, d_exp, ssd_nw, e16)

            proj = inproj(xp, norm_w[l], scale_p, shift_p, w_in, TM_IN, sp)
            rows, win, _, qh, ks, vs, kw, vw = nsa_prep(proj, cos_p, sin_p, TM_IN, sp, True)
            nc = sp // CMP_BLOCK
            cmp = nsa_compress(rows.reshape(tp // CMP_BLOCK, CMP_BLOCK * 4 * LANES), pe_tab, w1_blk, w2_blk,
                               min(CMP_ROWS, tp // CMP_BLOCK))
            kc, vc = _cmp_layout(cmp, bp, nc, BF16)
            o_a = nsa_attn_prompt(proj, qh, kc, vc, ks, vs, kw, vw, e_mat_p)
            y, stacks["ss_p"] = ssd_mixer(proj, jnp.zeros((bp, SUBLANES, SSD_CONV_CH), F32),
                                          jnp.zeros((1, bp, SSD_INNER, SSD_STATE), F32), 0, *ssd_args, bp, sp,
                                          SSD_CHUNK, seq_ps, None, stacks["ss_p"], e, n_even)
            xp = outproj([o_a, y], [w_out_a, w_out_b], xp, gate_p, final_norm_w, TM_OUT, sp, last)
            outs["kv_p"].append(rows.reshape(bp, sp, 4, NSA_KV_HEADS, HEAD_DIM))
            outs["win_p"].append(win.reshape(bp, sp, 2, NSA_KV_HEADS, HEAD_DIM)[:, -min(WINDOW, sp):])
            outs["cv_p"].append(proj.reshape(bp, sp, E_END)[:, -(SSD_CONV - 1):, E_XBC:E_XBC + SSD_CONV_CH])

            proj = inproj(xs, norm_w[l], scale_s, shift_s, w_in, tm_d, tm_d)
            rows, win, qrot = nsa_prep(proj, cos_s, sin_s, tm_d, DEC_PAD, False)
            pe_tok, w1_bd, w2_bd = _compress_weights_cache(nsa_cmp_pe[e], nsa_cmp_w1[e], nsa_cmp_w2[e])
            cmp_s = nsa_compress_cache(cache_t, e, page_table.reshape(-1), pe_tok, w1_bd, w2_bd, CMP_PAGES)
            per_page = PAGE_SIZE // CMP_BLOCK
            o_a, stacks["win_s"] = nsa_attn_decode(
                page_table, e, qrot.reshape(bs, DEC_PAD, NSA_WIDTH),
                cmp_s.reshape(bs, npg * per_page, 4 * HEAD_DIM), cache_t, rows.reshape(bs, DEC_PAD, 4 * LANES),
                win_t, win.reshape(bs, DEC_PAD, 2 * LANES), proj.reshape(bs, DEC_PAD, E_END), e_mat_s, past_len,
                samp_ps, ss, stacks["win_s"])
            o_a = o_a.reshape(td, NSA_WIDTH)
            prev8 = jnp.pad(state_ssd_conv[e], ((0, 0), (SUBLANES - (SSD_CONV - 1), 0), (0, 0)))
            y, stacks["ss_s"] = ssd_mixer(proj, prev8, ssd_h0_s, e, *ssd_args, bs, DEC_PAD, DEC_PAD,
                                          samp_ps, ss, stacks["ss_s"], e, n_even)
            xs = outproj([o_a, y], [w_out_a, w_out_b], xs, gate_s, final_norm_w, tm_d, tm_d, last)
            outs["kv_s"].append(rows.reshape(bs, DEC_PAD, 4, NSA_KV_HEADS, HEAD_DIM)[:, :ss])
            new_xbc = proj.reshape(bs, DEC_PAD, E_END)[:, :ss, E_XBC:E_XBC + SSD_CONV_CH]
            outs["cv_s"].append(jnp.concatenate([state_ssd_conv[e], new_xbc], axis=1)[:, -(SSD_CONV - 1):])
        else:
            w_in = _odd_weight(w_in_odd[e])
            w_out = w_out_odd[e].astype(BF16)
            wg_pad = jnp.pad(gla_w_gate2[e], ((0, LANES - GLA_GATE_RANK), (0, 0))).astype(BF16)
            bg = gla_b_gate[e].reshape(1, GLA_KEY_WIDTH)
            gnw = gla_norm_w[e].reshape(1, GLA_DV)
            srows = GLA_HEADS * GLA_DK
            proj = inproj(xp, norm_w[l], scale_p, shift_p, w_in, TM_IN, sp)
            o, stacks["gl_p"] = gla_mixer(proj, jnp.zeros((1, bp, srows, GLA_DV), F32), 0, wg_pad, bg, gnw, bp, sp,
                                          GLA_CHUNK, GLA_SUB, seq_ps, None, stacks["gl_p"], e, n_odd)
            xp = outproj([o], [w_out], xp, gate_p, final_norm_w, TM_OUT, sp, last)
            proj = inproj(xs, norm_w[l], scale_s, shift_s, w_in, tm_d, tm_d)
            o, stacks["gl_s"] = gla_mixer(proj, gla_s0_s, e, wg_pad, bg, gnw, bs, DEC_PAD, DEC_PAD, 1,
                                          samp_ps, ss, stacks["gl_s"], e, n_odd)
            xs = outproj([o], [w_out], xs, gate_s, final_norm_w, tm_d, tm_d, last)

    y_prompt = xp.reshape(bp, sp, d)
    y_sample = xs.reshape(bs, DEC_PAD, d)[:, :ss]
    st = {k: jnp.stack(v) for k, v in outs.items()}
    st["win_s"] = jnp.transpose(stacks["win_s"], (0, 1, 5, 2, 3, 4))
    st["ss_p"] = stacks["ss_p"].reshape(n_even, bp, SSD_HEADS, SSD_HEAD_DIM, SSD_STATE)
    st["ss_s"] = stacks["ss_s"].reshape(n_even, bs, SSD_HEADS, SSD_HEAD_DIM, SSD_STATE)
    st["gl_p"] = stacks["gl_p"].reshape(n_odd, bp, GLA_HEADS, GLA_DK, GLA_DV)
    st["gl_s"] = stacks["gl_s"].reshape(n_odd, bs, GLA_HEADS, GLA_DK, GLA_DV)
    return (y_prompt, y_sample, st["kv_p"], st["kv_s"], st["win_p"], st["win_s"], st["cv_p"], st["cv_s"],
            st["ss_p"], st["ss_s"], st["gl_p"], st["gl_s"])
```

```python
import functools

import jax
import jax.numpy as jnp
import numpy as np
from jax import lax
from jax.experimental import pallas as pl
from jax.experimental.pallas import tpu as pltpu

F32 = jnp.float32
BF16 = jnp.bfloat16

PAGE_SIZE = 128
NSA_HEADS = 8
NSA_KV_HEADS = 2
HEAD_DIM = 64
NSA_WIDTH = NSA_HEADS * HEAD_DIM
NSA_HG = NSA_HEADS // NSA_KV_HEADS
CMP_BLOCK = 32
CMP_HIDDEN = 2 * HEAD_DIM
SEL_BLOCK = 64
TOP_N = 16
WINDOW = 512
FORCE_BONUS = 1.0e4
ROPE_THETA = 10000.0
SSD_HEADS = 16
SSD_HEAD_DIM = 64
SSD_INNER = SSD_HEADS * SSD_HEAD_DIM
SSD_GROUPS = 2
SSD_STATE = 128
SSD_CONV = 4
SSD_CONV_CH = SSD_INNER + 2 * SSD_GROUPS * SSD_STATE
GLA_HEADS = 4
GLA_DK = 128
GLA_DV = 256
GLA_KEY_WIDTH = GLA_HEADS * GLA_DK
GLA_VAL_WIDTH = GLA_HEADS * GLA_DV
GLA_GATE_RANK = 16
GLA_TAU = 16.0
GLA_CHUNK = 32
SSD_CHUNK = 128

LANES = 128
SUBLANES = 8
VMEM_LIMIT = 56 * 1024 * 1024

TM_IN = 256
TM_OUT = 512
CMP_ROWS = 128
CMP_PAGES = 32
SEQ_PER_STEP = 2
SAMPLES_PER_STEP = 4
GLA_SUB = 4

NEG_BIG = -1.0e30
M_INIT = -1.0e29
DEC_PAD = 8

E_XBC, E_Q, E_ZB, E_ZA, E_KV, E_GD, E_END = 0, 1536, 2048, 3072, 3584, 4352, 4480
GD_GATE, GD_DT = 0, 24
O_Q, O_K, O_V, O_R, O_GLR, O_END = 0, 512, 1024, 2048, 3072, 3200


def _cparams(sem):
    return pltpu.CompilerParams(dimension_semantics=sem, vmem_limit_bytes=VMEM_LIMIT)


def _dot(a, b):
    return jnp.dot(a, b, preferred_element_type=F32)


def _dot_nt(a, b):
    return lax.dot_general(a, b, (((1,), (1,)), ((), ())), preferred_element_type=F32)


def _dot_tn(a, b):
    return lax.dot_general(a, b, (((0,), (0,)), ((), ())), preferred_element_type=F32)


def _split3(x):
    hi = x.astype(BF16)
    r1 = x - hi.astype(F32)
    mid = r1.astype(BF16)
    lo = (r1 - mid.astype(F32)).astype(BF16)
    return hi, mid, lo


def _dot3(x, w01):
    hi, mid, lo = _split3(x)
    return _dot(hi, w01) + _dot(mid, w01) + _dot(lo, w01)


def _dot3_tn(x, w01):
    hi, mid, lo = _split3(x)
    return _dot_tn(hi, w01) + _dot_tn(mid, w01) + _dot_tn(lo, w01)


def _dot3_wx(w01, x):
    hi, mid, lo = _split3(x)
    return _dot(w01, hi) + _dot(w01, mid) + _dot(w01, lo)


def _silu(x):
    return x * (1.0 / (1.0 + jnp.exp(-x)))


def _sigmoid(x):
    return 1.0 / (1.0 + jnp.exp(-x))


def _softplus(x):
    return jnp.maximum(x, 0.0) + jnp.log1p(jnp.exp(-jnp.abs(x)))


def _log_sigmoid(x):
    return jnp.minimum(x, 0.0) - jnp.log1p(jnp.exp(-jnp.abs(x)))


def _mod_kernel(c_ref, w_ref, b_ref, o_ref):
    a = _silu(c_ref[...]).astype(BF16)
    o_ref[0] = _dot(a, w_ref[0]) + b_ref[0]


def mod_all(c_all, w_mod_bf, b_mod, tn=1024):
    nl, d, n = w_mod_bf.shape
    m = c_all.shape[0]
    return pl.pallas_call(
        _mod_kernel,
        grid=(nl, n // tn),
        in_specs=[
            pl.BlockSpec((m, d), lambda l, j: (0, 0)),
            pl.BlockSpec((1, d, tn), lambda l, j: (l, 0, j)),
            pl.BlockSpec((1, 1, tn), lambda l, j: (l, 0, j)),
        ],
        out_specs=pl.BlockSpec((1, m, tn), lambda l, j: (l, 0, j)),
        out_shape=jax.ShapeDtypeStruct((nl, m, n), F32),
        compiler_params=_cparams(("arbitrary", "arbitrary")),
        name="mod_all",
    )(c_all, w_mod_bf, b_mod.reshape(nl, 1, n))


def _inproj_kernel(x_ref, nw_ref, sc_ref, sh_ref, w_ref, o_ref, *, n_chunk):
    x = x_ref[...]
    ms = jnp.mean(x * x, axis=-1, keepdims=True)
    y = x * lax.rsqrt(ms + 1e-6) * nw_ref[...]
    h = (y * (1.0 + sc_ref[0]) + sh_ref[0]).astype(BF16)
    n = o_ref.shape[1]
    for n0 in range(0, n, n_chunk):
        n1 = min(n0 + n_chunk, n)
        o_ref[:, n0:n1] = _dot(h, w_ref[:, n0:n1])


def inproj(x, nw, scale, shift, w_bf, tm, rows_per_mod):
    t, d = x.shape
    n = w_bf.shape[1]
    r = scale.shape[1]
    mod_map = lambda i: ((i * tm) // rows_per_mod, 0, 0)
    return pl.pallas_call(
        functools.partial(_inproj_kernel, n_chunk=640),
        grid=(t // tm,),
        in_specs=[
            pl.BlockSpec((tm, d), lambda i: (i, 0)),
            pl.BlockSpec((1, d), lambda i: (0, 0)),
            pl.BlockSpec((1, r, d), mod_map),
            pl.BlockSpec((1, r, d), mod_map),
            pl.BlockSpec((d, n), lambda i: (0, 0)),
        ],
        out_specs=pl.BlockSpec((tm, n), lambda i: (i, 0)),
        out_shape=jax.ShapeDtypeStruct((t, n), F32),
        compiler_params=_cparams(("arbitrary",)),
        name="inproj",
    )(x, nw.reshape(1, d), scale, shift, w_bf)


def _outproj_kernel(*refs, n_in, final_norm):
    a_refs = refs[:n_in]
    w_refs = refs[n_in:2 * n_in]
    x_ref, g_ref, fw_ref, o_ref = refs[2 * n_in:]
    acc = _dot(a_refs[0][...].astype(BF16), w_refs[0][...])
    for a_ref, w_ref in zip(a_refs[1:], w_refs[1:]):
        acc = acc + _dot(a_ref[...].astype(BF16), w_ref[...])
    y = x_ref[...] + g_ref[0] * acc
    if final_norm:
        ms = jnp.mean(y * y, axis=-1, keepdims=True)
        y = y * lax.rsqrt(ms + 1e-6) * fw_ref[...]
    o_ref[...] = y


def outproj(a_list, w_list, x, gate, final_w, tm, rows_per_mod, final_norm):
    t, d = x.shape
    r = gate.shape[1]
    n_in = len(a_list)
    in_specs = [pl.BlockSpec((tm, a.shape[1]), lambda i: (i, 0)) for a in a_list]
    in_specs += [pl.BlockSpec(w.shape, lambda i: (0, 0)) for w in w_list]
    in_specs += [
        pl.BlockSpec((tm, d), lambda i: (i, 0)),
        pl.BlockSpec((1, r, d), lambda i: ((i * tm) // rows_per_mod, 0, 0)),
        pl.BlockSpec((1, d), lambda i: (0, 0)),
    ]
    return pl.pallas_call(
        functools.partial(_outproj_kernel, n_in=n_in, final_norm=final_norm),
        grid=(t // tm,),
        in_specs=in_specs,
        out_specs=pl.BlockSpec((tm, d), lambda i: (i, 0)),
        out_shape=jax.ShapeDtypeStruct((t, d), F32),
        compiler_params=_cparams(("arbitrary",)),
        name="outproj",
    )(*a_list, *w_list, x, gate, final_w.reshape(1, d))


def _rope_tile(t, cos, sin_signed, lane):
    fwd = pltpu.roll(t, LANES - HEAD_DIM // 2, 1)
    bwd = pltpu.roll(t, HEAD_DIM // 2, 1)
    partner = jnp.where((lane % HEAD_DIM) < HEAD_DIM // 2, fwd, bwd)
    return t * cos + partner * sin_signed


def _nsa_prep_kernel(q_ref, kv01_ref, kv23_ref, kv45_ref, cos_ref, sin_ref,
                     rows_ref, win_ref, qrot_ref, *hm_refs, head_major):
    cos = cos_ref[...]
    sin = sin_ref[...]
    lane = lax.broadcasted_iota(jnp.int32, cos.shape, 1)
    rope = lambda t: _rope_tile(t, cos, sin, lane)
    kv01, kv23, kv45 = kv01_ref[...], kv23_ref[...], kv45_ref[...]
    k_slc = rope(kv23[:, :LANES])
    v_slc = kv23[:, LANES:]
    k_win = rope(kv45[:, :LANES])
    v_win = kv45[:, LANES:]
    rows_ref[:, 0:LANES] = rope(kv01[:, :LANES])
    rows_ref[:, LANES:2 * LANES] = kv01[:, LANES:]
    rows_ref[:, 2 * LANES:3 * LANES] = k_slc
    rows_ref[:, 3 * LANES:4 * LANES] = v_slc
    win_ref[:, 0:LANES] = k_win
    win_ref[:, LANES:2 * LANES] = v_win
    q = q_ref[...]
    qr = [rope(q[:, j * LANES:(j + 1) * LANES]) for j in range(NSA_WIDTH // LANES)]
    for j, t in enumerate(qr):
        qrot_ref[:, j * LANES:(j + 1) * LANES] = t
    if head_major:
        qh_ref, ks_ref, vs_ref, kw_ref, vw_ref = hm_refs
        scale = HEAD_DIM ** -0.5
        for h in range(NSA_HEADS):
            lo = (h % 2) * HEAD_DIM
            qh_ref[0, h] = (qr[h // 2][:, lo:lo + HEAD_DIM] * scale).astype(BF16)
        for g in range(NSA_KV_HEADS):
            lo = g * HEAD_DIM
            ks_ref[0, g] = k_slc[:, lo:lo + HEAD_DIM].astype(BF16)
            vs_ref[0, g] = v_slc[:, lo:lo + HEAD_DIM].astype(BF16)
            kw_ref[0, g] = k_win[:, lo:lo + HEAD_DIM].astype(BF16)
            vw_ref[0, g] = v_win[:, lo:lo + HEAD_DIM].astype(BF16)


def nsa_prep(proj, cos_tab, sin_tab, tm, seq, head_major):
    t = proj.shape[0]
    tab_blocks = cos_tab.shape[0] // tm
    sb = seq // tm
    out_shape = [jax.ShapeDtypeStruct((t, 4 * LANES), F32),
                 jax.ShapeDtypeStruct((t, 2 * LANES), F32),
                 jax.ShapeDtypeStruct((t, NSA_WIDTH), F32)]
    out_specs = [pl.BlockSpec((tm, 4 * LANES), lambda i: (i, 0)),
                 pl.BlockSpec((tm, 2 * LANES), lambda i: (i, 0)),
                 pl.BlockSpec((tm, NSA_WIDTH), lambda i: (i, 0))]
    if head_major:
        nb = t // seq
        hm_map = lambda i: (i // sb, 0, i % sb, 0)
        out_shape.append(jax.ShapeDtypeStruct((nb, NSA_HEADS, seq, HEAD_DIM), BF16))
        out_specs.append(pl.BlockSpec((1, NSA_HEADS, tm, HEAD_DIM), hm_map))
        for _ in range(4):
            out_shape.append(jax.ShapeDtypeStruct((nb, NSA_KV_HEADS, seq, HEAD_DIM), BF16))
            out_specs.append(pl.BlockSpec((1, NSA_KV_HEADS, tm, HEAD_DIM), hm_map))
    kvb = E_KV // (2 * LANES)
    return pl.pallas_call(
        functools.partial(_nsa_prep_kernel, head_major=head_major),
        grid=(t // tm,),
        in_specs=[
            pl.BlockSpec((tm, NSA_WIDTH), lambda i: (i, E_Q // NSA_WIDTH)),
            pl.BlockSpec((tm, 2 * LANES), lambda i: (i, kvb)),
            pl.BlockSpec((tm, 2 * LANES), lambda i: (i, kvb + 1)),
            pl.BlockSpec((tm, 2 * LANES), lambda i: (i, kvb + 2)),
            pl.BlockSpec((tm, LANES), lambda i: (i % tab_blocks, 0)),
            pl.BlockSpec((tm, LANES), lambda i: (i % tab_blocks, 0)),
        ],
        out_specs=out_specs,
        out_shape=out_shape,
        compiler_params=_cparams(("arbitrary",)),
        name="nsa_prep",
    )(proj, proj, proj, proj, cos_tab, sin_tab)


def _nsa_compress_kernel(x_ref, pe_ref, w1_ref, w2_ref, o_ref):
    r = x_ref.shape[0]
    row_w = 4 * LANES
    acc = jnp.zeros((r, 4 * CMP_HIDDEN), F32)
    for t in range(CMP_BLOCK):
        xt = x_ref[:, t * row_w:t * row_w + 2 * LANES] + pe_ref[t:t + 1, :]
        acc = acc + _dot(xt.astype(BF16), w1_ref[t])
    o_ref[...] = _dot(_silu(acc).astype(BF16), w2_ref[...])


def nsa_compress(xblk, pe_tab, w1_blk, w2_blk, r):
    nb, kdim = xblk.shape
    return pl.pallas_call(
        _nsa_compress_kernel,
        grid=(nb // r,),
        in_specs=[
            pl.BlockSpec((r, kdim), lambda i: (i, 0)),
            pl.BlockSpec(pe_tab.shape, lambda i: (0, 0)),
            pl.BlockSpec(w1_blk.shape, lambda i: (0, 0, 0)),
            pl.BlockSpec(w2_blk.shape, lambda i: (0, 0)),
        ],
        out_specs=pl.BlockSpec((r, 4 * HEAD_DIM), lambda i: (i, 0)),
        out_shape=jax.ShapeDtypeStruct((nb, 4 * HEAD_DIM), F32),
        compiler_params=_cparams(("arbitrary",)),
        name="nsa_compress",
    )(xblk, pe_tab, w1_blk, w2_blk)


def _nsa_compress_cache_kernel(pt_ref, *refs, n_page):
    del pt_ref
    page_refs = refs[:n_page]
    pe_ref, w1_ref, w2_ref, o_ref, xbuf = refs[n_page:]
    gd = NSA_KV_HEADS * HEAD_DIM
    per_page = PAGE_SIZE // CMP_BLOCK
    for p, x_ref in enumerate(page_refs):
        for c in range(2):
            xbuf[c, p * PAGE_SIZE:(p + 1) * PAGE_SIZE, :] = x_ref[0, c].reshape(gd, PAGE_SIZE).T + pe_ref[c]
    rows = n_page * per_page
    acc = [jnp.zeros((rows, NSA_KV_HEADS * CMP_HIDDEN), F32) for _ in range(2)]
    for t in range(CMP_BLOCK):
        for c in range(2):
            a = xbuf[c, pl.ds(t, rows, stride=CMP_BLOCK), :].astype(BF16)
            acc[c] = acc[c] + _dot(a, w1_ref[c, t])
    for c in range(2):
        o_ref[:, c * gd:(c + 1) * gd] = _dot(_silu(acc[c]).astype(BF16), w2_ref[c])


def nsa_compress_cache(cache_t, layer, pages, pe_tok, w1_bd, w2_bd, n_page):
    per_page = PAGE_SIZE // CMP_BLOCK
    gd = NSA_KV_HEADS * HEAD_DIM
    full = lambda a: pl.BlockSpec(a.shape, lambda i, pt: (0,) * a.ndim)
    page_blk = (None, 1, 2, NSA_KV_HEADS, HEAD_DIM, PAGE_SIZE)
    page_specs = [pl.BlockSpec(page_blk, lambda i, pt, j=j: (layer, pt[i * n_page + j], 0, 0, 0, 0))
                  for j in range(n_page)]
    return pl.pallas_call(
        functools.partial(_nsa_compress_cache_kernel, n_page=n_page),
        grid_spec=pltpu.PrefetchScalarGridSpec(
            num_scalar_prefetch=1,
            grid=(pages.shape[0] // n_page,),
            in_specs=[*page_specs, full(pe_tok), full(w1_bd), full(w2_bd)],
            out_specs=pl.BlockSpec((n_page * per_page, 2 * gd), lambda i, pt: (i, 0)),
            scratch_shapes=[pltpu.VMEM((2, n_page * PAGE_SIZE, gd), F32)],
        ),
        out_shape=jax.ShapeDtypeStruct((pages.shape[0] * per_page, 2 * gd), F32),
        compiler_params=_cparams(("arbitrary",)),
        name="nsa_compress_cache",
    )(pages, *([cache_t] * n_page), pe_tok, w1_bd, w2_bd)


def _masked_softmax3(s, mask):
    s = jnp.where(mask[None], s, -jnp.inf)
    m = jnp.max(s, axis=-1, keepdims=True)
    m = jnp.where(m > -jnp.inf, m, 0.0)
    p = jnp.exp(s - m)
    d = jnp.sum(p, axis=-1, keepdims=True)
    return p / jnp.where(d > 0, d, 1.0)


def _drain(gen):
    try:
        while True:
            next(gen)
    except StopIteration as stop:
        return stop.value


def _round_robin(gens):
    results = [None] * len(gens)
    live = list(range(len(gens)))
    while live:
        for i in list(live):
            try:
                next(gens[i])
            except StopIteration as stop:
                results[i] = stop.value
                live.remove(i)
        yield
    return results


def _run_interleaved(gens):
    return _drain(_round_robin(gens))


def _cmp_branch_and_select(*args, **kwargs):
    return _drain(_cmp_branch_and_select_stages(*args, **kwargs))


def _cmp_branch_and_select_stages(q2, kc, vc, qpos, qpos_row, nq, imp_sc, n_blk, pair_t=None):
    nc = kc.shape[0]
    s = _dot_nt(q2, kc).reshape(NSA_HG, nq, nc)
    yield
    col = lax.broadcasted_iota(jnp.int32, (1, nc), 1)
    cblk = col if pair_t is not None else 2 * (col % LANES) + col // LANES
    cmp_end = (cblk + 1) * CMP_BLOCK - 1
    p = _masked_softmax3(s, cmp_end <= qpos)
    yield
    o_c = _dot(p.reshape(NSA_HG * nq, nc).astype(BF16), vc).reshape(NSA_HG, nq, HEAD_DIM)
    imp = jnp.sum(p, axis=0)
    if pair_t is not None:
        hi, mid, lo = _split3(imp)
        imp = _dot_nt(pair_t, hi) + _dot_nt(pair_t, mid) + _dot_nt(pair_t, lo)
    else:
        imp = (imp[:, :LANES] + imp[:, LANES:]).T
    yield
    j = lax.broadcasted_iota(jnp.int32, (LANES, 1), 0)
    cur = qpos_row // SEL_BLOCK
    forced = jnp.where(j == 0, 1, jnp.where(j == cur, 1, jnp.where(j == cur - 1, 1, 0))) > 0
    causal = j <= cur
    imp = jnp.where(forced, imp + FORCE_BONUS, imp)
    imp = jnp.where(causal, imp, -jnp.inf)
    def count_ahead(i, ri, cnt):
        wins_ties = jnp.where(j > i, 1.0, 0.0)
        return cnt + jnp.where(ri >= imp, jnp.where(ri > imp, 1.0, wins_ties), 0.0)

    rank = jnp.zeros((LANES, nq), F32)
    if isinstance(n_blk, int):
        for i in range(n_blk):
            rank = count_ahead(i, imp[i:i + 1, :], rank)
    else:
        imp_sc[...] = imp
        half = LANES // 2

        def rank_rows(n_rows):
            imp_r, j_r = imp[:n_rows], j[:n_rows]

            def body(i, cnt):
                ri = imp_sc[pl.ds(i, 1), :]
                wins_ties = jnp.where(j_r > i, 1.0, 0.0)
                return cnt + jnp.where(ri >= imp_r, jnp.where(ri > imp_r, 1.0, wins_ties), 0.0)

            return lax.fori_loop(0, n_blk, body, jnp.zeros((n_rows, nq), F32))

        rank = lax.cond(n_blk <= half,
                        lambda: jnp.concatenate([rank_rows(half), jnp.zeros((LANES - half, nq), F32)], axis=0),
                        lambda: rank_rows(LANES))
    sel = jnp.where(causal, jnp.where(rank < TOP_N, 1.0, 0.0), 0.0)
    return o_c, sel.T.astype(BF16)


def _flash_init(nq):
    return (jnp.full((NSA_HG, nq, 1), M_INIT, F32), jnp.zeros((NSA_HG, nq, 1), F32),
            jnp.zeros((NSA_HG, nq, HEAD_DIM), F32))


def _mask_bias(allowed):
    return jnp.where(allowed, 0.0, NEG_BIG)


def _flash_update_stages(carry, s, bias, v, v_transposed=False):
    m, l, acc = carry
    s = s + bias[None]
    m_new = jnp.maximum(m, jnp.max(s, axis=-1, keepdims=True))
    alpha = jnp.exp(m - m_new)
    p = jnp.exp(s - m_new)
    l = alpha * l + jnp.sum(p, axis=-1, keepdims=True)
    h, q, n = p.shape
    p2 = p.reshape(h * q, n).astype(v.dtype)
    yield
    pv = (_dot_nt(p2, v) if v_transposed else _dot(p2, v)).reshape(h, q, HEAD_DIM)
    return m_new, l, alpha * acc + pv


def _flash_finish(carry):
    _, l, acc = carry
    return acc / jnp.where(l > 0, l, 1.0)


def _gate_column(sig, lane, col):
    return jnp.sum(jnp.where(lane == col, sig, 0.0), axis=-1, keepdims=True)


def _softmax_rows(s_ref, b_ref, p_ref, m_ref, l_ref, a_ref, n, tq):
    rb = 2 * SUBLANES
    for r0 in range(0, s_ref.shape[0], rb):
        rows = slice(r0, r0 + rb)
        sb = s_ref[rows, :n] + b_ref[r0 % tq:r0 % tq + rb, :n]
        m_old = m_ref[rows, :]
        m_new = jnp.maximum(m_old, jnp.max(sb, axis=-1, keepdims=True))
        p = jnp.exp(sb - jnp.concatenate([m_new] * (n // LANES), axis=1))
        alpha = jnp.exp(m_old - m_new)
        l_ref[rows, :] = alpha * l_ref[rows, :] + jnp.sum(p, axis=-1, keepdims=True)
        m_ref[rows, :] = m_new
        a_ref[rows, :] = alpha
        p_ref[rows, :n] = p.astype(BF16)


def _nsa_attn_kernel(q_ref, kc_ref, vc_ref, ks_ref, vs_ref, kw_ref, vw_ref, gd_ref, za_ref, e_ref,
                     o_ref, s_sc, p_sc, b_sc, m_sc, l_sc, a_sc, acc_sc, imp_sc, *, tq, tk):
    g = pl.program_id(1)
    q0 = pl.program_id(2) * tq
    q2 = q_ref[0].reshape(NSA_HG * tq, HEAD_DIM)
    qpos = q0 + lax.broadcasted_iota(jnp.int32, (tq, 1), 0)
    qpos_row = q0 + lax.broadcasted_iota(jnp.int32, (1, tq), 1)
    n_blk = (q0 + tq - 1) // SEL_BLOCK + 1
    o_c, sel = _cmp_branch_and_select(q2, kc_ref[0, 0], vc_ref[0, 0], qpos, qpos_row, tq, imp_sc, n_blk)

    def reset():
        m_sc[...] = jnp.full(m_sc.shape, M_INIT, F32)
        l_sc[...] = jnp.zeros(l_sc.shape, F32)
        acc_sc[...] = jnp.zeros(acc_sc.shape, F32)

    def softmax_step(k, v, bias, n):
        b_sc[:, :n] = bias

        def head_stages(h):
            rows = pl.ds(h * tq, tq)
            s_sc[rows, :n] = _dot_nt(q2[h * tq:(h + 1) * tq], k)
            yield
            _softmax_rows(s_sc.at[rows], b_sc, p_sc.at[rows], m_sc.at[rows], l_sc.at[rows], a_sc.at[rows], n, tq)
            yield
            acc_sc[rows, :] = a_sc[rows, :HEAD_DIM] * acc_sc[rows, :] + _dot(p_sc[rows, :n], v)

        _run_interleaved([head_stages(h) for h in range(NSA_HG)])

    def finish():
        l = l_sc[:, :HEAD_DIM]
        return acc_sc[...] / jnp.where(l > 0, l, 1.0)

    def sel_tile(t, diagonal):
        k0 = pl.multiple_of(t * tk, tk)
        bias = (_dot(sel, e_ref[:, pl.ds(k0, tk)]) - 1.0) * (-NEG_BIG)
        if diagonal:
            kpos = k0 + lax.broadcasted_iota(jnp.int32, (1, tk), 1)
            bias = jnp.where(kpos <= qpos, bias, NEG_BIG)
        softmax_step(ks_ref[0, 0, pl.ds(k0, tk), :], vs_ref[0, 0, pl.ds(k0, tk), :], bias, tk)

    n_below = q0 // tk
    reset()

    def below(t, carry):
        sel_tile(t, False)
        return carry

    lax.fori_loop(0, n_below, below, 0)
    sel_tile(n_below, True)
    o_s = finish()

    wn = WINDOW + tq
    w0 = pl.multiple_of(jnp.maximum(q0 - WINDOW, 0), tq)
    dist = qpos - (w0 + lax.broadcasted_iota(jnp.int32, (1, wn), 1))
    reset()
    softmax_step(kw_ref[0, 0, pl.ds(w0, wn), :], vw_ref[0, 0, pl.ds(w0, wn), :],
                 _mask_bias(jnp.where(dist >= 0, dist, WINDOW) < WINDOW), wn)
    o_w = finish()


    sig = _sigmoid(gd_ref[...])
    lane = lax.broadcasted_iota(jnp.int32, sig.shape, 1)
    outs = []
    for h in range(NSA_HG):
        head = g * NSA_HG + h
        hrows = slice(h * tq, (h + 1) * tq)
        g_c = _gate_column(sig, lane, GD_GATE + head)
        g_s = _gate_column(sig, lane, GD_GATE + NSA_HEADS + head)
        g_w = _gate_column(sig, lane, GD_GATE + 2 * NSA_HEADS + head)
        outs.append(g_c * o_c[h] + g_s * o_s[hrows] + g_w * o_w[hrows])
    o_ref[...] = jnp.concatenate(outs, axis=-1) * _silu(za_ref[...])


def nsa_attn_prompt(proj, qh, kc, vc, ks, vs, kw, vw, e_mat, tq=128, tk=1024):
    nb, _, seq, _ = qh.shape
    assert seq >= WINDOW + tq and tk % tq == 0 and seq % tk == 0
    rows, wn = NSA_HG * tq, max(WINDOW + tq, tk)
    nq = seq // tq
    gw = NSA_HG * HEAD_DIM
    kv_spec = pl.BlockSpec((1, 1, seq, HEAD_DIM), lambda b, g, i: (b, g, 0, 0))
    c_spec = pl.BlockSpec((1, 1, 2 * LANES, HEAD_DIM), lambda b, g, i: (b, g, 0, 0))
    return pl.pallas_call(
        functools.partial(_nsa_attn_kernel, tq=tq, tk=tk),
        grid=(nb, NSA_KV_HEADS, nq),
        in_specs=[
            pl.BlockSpec((1, NSA_HG, tq, HEAD_DIM), lambda b, g, i: (b, g, i, 0)),
            c_spec, c_spec, kv_spec, kv_spec, kv_spec, kv_spec,
            pl.BlockSpec((tq, LANES), lambda b, g, i: (b * nq + i, E_GD // LANES)),
            pl.BlockSpec((tq, gw), lambda b, g, i: (b * nq + i, E_ZA // gw + g)),
            pl.BlockSpec(e_mat.shape, lambda b, g, i: (0, 0)),
        ],
        out_specs=pl.BlockSpec((tq, gw), lambda b, g, i: (b * nq + i, g)),
        out_shape=jax.ShapeDtypeStruct((nb * seq, NSA_WIDTH), F32),
        scratch_shapes=[
            pltpu.VMEM((rows, wn), F32),
            pltpu.VMEM((rows, wn), BF16),
            pltpu.VMEM((tq, wn), F32),
            pltpu.VMEM((rows, LANES), F32),
            pltpu.VMEM((rows, LANES), F32),
            pltpu.VMEM((rows, LANES), F32),
            pltpu.VMEM((rows, HEAD_DIM), F32),
            pltpu.VMEM((LANES, tq), F32),
        ],
        compiler_params=_cparams(("arbitrary", "arbitrary", "arbitrary")),
        name="nsa_attn_prompt",
    )(qh, kc, vc, ks, vs, kw, vw, proj, proj, e_mat)


def _nsa_dec_kernel(pt_ref, q_ref, cmp_ref, *rest, past_len, n_pages, n_new):
    del pt_ref
    n_samp = q_ref.shape[0]
    ins = rest[n_samp * n_pages:n_samp * n_pages + 7]
    o_ref, wout_ref, imp_sc = rest[-3:]
    pwin_ref, wnew_ref = ins[1], ins[2]
    outs = _run_interleaved([
        _nsa_dec_sample(si, q_ref, cmp_ref, rest[si * n_pages:(si + 1) * n_pages], *ins, None, imp_sc,
                        past_len=past_len)
        for si in range(n_samp)])
    for si in range(n_samp):
        o_ref[si] = outs[si]
        wnew = wnew_ref[si]
        for c in range(2):
            for g in range(NSA_KV_HEADS):
                new_t = wnew[:, c * LANES + g * HEAD_DIM:c * LANES + (g + 1) * HEAD_DIM].T
                wout_ref[si, c, g] = jnp.concatenate([pwin_ref[si, c, g][:, n_new:], new_t[:, :n_new]], axis=1)


def _nsa_dec_sample(si, q_ref, cmp_ref, page_refs, rows_ref, pwin_ref, wnew_ref, gd_ref, za_ref, e_ref, pair_ref,
                    o_ref, imp_sc, *, past_len):
    del o_ref
    nq = DEC_PAD
    scale = HEAD_DIM ** -0.5
    qpos = past_len + lax.broadcasted_iota(jnp.int32, (nq, 1), 0)
    qpos_row = past_len + lax.broadcasted_iota(jnp.int32, (1, nq), 1)
    n_blk = (past_len + nq - 1) // SEL_BLOCK + 1
    q = q_ref[si]
    cmp = cmp_ref[si]

    def q_group(g):
        parts = [q[:, (g * NSA_HG + h) * HEAD_DIM:(g * NSA_HG + h + 1) * HEAD_DIM] for h in range(NSA_HG)]
        return jnp.concatenate(parts, axis=0) * scale

    qg = [q_group(g) for g in range(NSA_KV_HEADS)]

    rows = rows_ref[si]
    wnew = wnew_ref[si]
    win_keep = pwin_ref.shape[-1]
    sig = _sigmoid(gd_ref[si])
    lane = lax.broadcasted_iota(jnp.int32, sig.shape, 1)
    kpos = lax.broadcasted_iota(jnp.int32, (1, past_len), 1)
    npos = past_len + lax.broadcasted_iota(jnp.int32, (1, nq), 1)
    wpos = past_len - win_keep + lax.broadcasted_iota(jnp.int32, (1, win_keep), 1)
    outs = []
    for g in range(NSA_KV_HEADS):
        qb = qg[g].astype(BF16)
        kc = cmp[:, g * HEAD_DIM:(g + 1) * HEAD_DIM].astype(BF16)
        vc = cmp[:, LANES + g * HEAD_DIM:LANES + (g + 1) * HEAD_DIM].astype(BF16)
        o_c, sel = yield from _cmp_branch_and_select_stages(
            qb, kc, vc, qpos, qpos_row, nq, imp_sc.at[si * NSA_KV_HEADS + g], n_blk, pair_ref[...])
        yield
        kt = jnp.concatenate([r[0, 0, g] for r in page_refs], axis=1).astype(BF16)
        vt = jnp.concatenate([r[0, 1, g] for r in page_refs], axis=1).astype(BF16)
        s = _dot(qb, kt).reshape(NSA_HG, nq, past_len)
        blk_on = _dot(sel, e_ref[:, :past_len])
        yield
        bias = _mask_bias(jnp.where(kpos <= qpos, blk_on, 0.0) > 0.5)
        carry = yield from _flash_update_stages(_flash_init(nq), s, bias, vt, v_transposed=True)
        yield
        kn = rows[:, 2 * LANES + g * HEAD_DIM:2 * LANES + (g + 1) * HEAD_DIM]
        vn = rows[:, 3 * LANES + g * HEAD_DIM:3 * LANES + (g + 1) * HEAD_DIM]
        s = _dot_nt(qg[g], kn).reshape(NSA_HG, nq, nq)
        blk_on = _dot(sel, e_ref[:, past_len:past_len + LANES])[:, :nq]
        yield
        bias = _mask_bias(jnp.where(npos <= qpos, blk_on, 0.0) > 0.5)
        o_s = _flash_finish((yield from _flash_update_stages(carry, s, bias, vn)))
        yield
        s = _dot(qb, pwin_ref[si, 0, g].astype(BF16)).reshape(NSA_HG, nq, win_keep)
        yield
        dist = qpos - wpos
        bias = _mask_bias(jnp.where(wpos >= 0, jnp.where(dist >= 0, dist, WINDOW), WINDOW) < WINDOW)
        carry = yield from _flash_update_stages(_flash_init(nq), s, bias, pwin_ref[si, 1, g].astype(BF16),
                                                v_transposed=True)
        yield
        kwn = wnew[:, g * HEAD_DIM:(g + 1) * HEAD_DIM]
        vwn = wnew[:, LANES + g * HEAD_DIM:LANES + (g + 1) * HEAD_DIM]
        s = _dot_nt(qg[g], kwn).reshape(NSA_HG, nq, nq)
        yield
        dist = qpos - npos
        bias = _mask_bias(jnp.where(dist >= 0, dist, WINDOW) < WINDOW)
        o_w = _flash_finish((yield from _flash_update_stages(carry, s, bias, vwn)))
        for h in range(NSA_HG):
            head = g * NSA_HG + h
            g_c = _gate_column(sig, lane, GD_GATE + head)
            g_s = _gate_column(sig, lane, GD_GATE + NSA_HEADS + head)
            g_w = _gate_column(sig, lane, GD_GATE + 2 * NSA_HEADS + head)
            outs.append(g_c * o_c[h] + g_s * o_s[h] + g_w * o_w[h])
        yield
    return jnp.concatenate(outs, axis=-1) * _silu(za_ref[si])


def nsa_attn_decode(page_table, layer, qrot, cmp, cache_t, rows, pwin_t, wnew, proj3, e_mat, past_len, n_samp,
                    n_new, win_stack):
    nb, npg = page_table.shape
    win_keep = pwin_t.shape[-1]
    ncb = cmp.shape[1]
    per_b = lambda blk: pl.BlockSpec(blk, lambda b, pt: (b,) + (0,) * (len(blk) - 1))
    page_blk = (None, 1, 2, NSA_KV_HEADS, HEAD_DIM, PAGE_SIZE)
    page_specs = [pl.BlockSpec(page_blk, lambda b, pt, si=si, j=j: (layer, pt[b * n_samp + si, j], 1, 0, 0, 0))
                  for si in range(n_samp) for j in range(npg)]
    pair_t = (jnp.arange(ncb, dtype=jnp.int32)[None, :] // 2
              == jnp.arange(LANES, dtype=jnp.int32)[:, None]).astype(BF16)
    win_blk = (None, n_samp, 2, NSA_KV_HEADS, HEAD_DIM, win_keep)
    win_spec = pl.BlockSpec(win_blk, lambda b, pt: (layer, b, 0, 0, 0, 0))
    args = [page_table, qrot, cmp, *([cache_t] * (n_samp * npg)), rows, pwin_t, wnew, proj3, proj3, e_mat, pair_t]
    in_specs = [
        per_b((n_samp, DEC_PAD, NSA_WIDTH)),
        per_b((n_samp, ncb, 4 * HEAD_DIM)),
        *page_specs,
        per_b((n_samp, DEC_PAD, 4 * LANES)),
        win_spec,
        per_b((n_samp, DEC_PAD, 2 * LANES)),
        pl.BlockSpec((n_samp, DEC_PAD, LANES), lambda b, pt: (b, 0, E_GD // LANES)),
        pl.BlockSpec((n_samp, DEC_PAD, NSA_WIDTH), lambda b, pt: (b, 0, E_ZA // NSA_WIDTH)),
        pl.BlockSpec(e_mat.shape, lambda b, pt: (0, 0)),
        pl.BlockSpec(pair_t.shape, lambda b, pt: (0, 0)),
    ]
    aliases = {}
    if win_stack is not None:
        aliases = {len(args): 1}
        args.append(win_stack)
        in_specs.append(pl.BlockSpec(memory_space=pl.ANY))
    return pl.pallas_call(
        functools.partial(_nsa_dec_kernel, past_len=past_len, n_pages=npg, n_new=n_new),
        grid_spec=pltpu.PrefetchScalarGridSpec(
            num_scalar_prefetch=1,
            grid=(nb // n_samp,),
            in_specs=in_specs,
            out_specs=[per_b((n_samp, DEC_PAD, NSA_WIDTH)), win_spec],
            scratch_shapes=[pltpu.VMEM((n_samp * NSA_KV_HEADS, LANES, DEC_PAD), F32)],
        ),
        out_shape=[jax.ShapeDtypeStruct((nb, DEC_PAD, NSA_WIDTH), F32),
                   jax.ShapeDtypeStruct(pwin_t.shape, F32)],
        input_output_aliases=aliases,
        compiler_params=_cparams(("arbitrary",)),
        name="nsa_attn_decode",
    )(*args)


def _tri_masks(c):
    row = lax.broadcasted_iota(jnp.int32, (c, c), 0)
    col = lax.broadcasted_iota(jnp.int32, (c, c), 1)
    lower = row >= col
    return lower, jnp.where(lower, 1.0, 0.0).astype(BF16), jnp.where(row <= col, 1.0, 0.0).astype(BF16)


def _ssd_kernel(zb_ref, xbc_ref, gd_ref, prev_ref, h0_ref, cw_ref, cb_ref, dtb_ref, alog_ref, alog16_ref,
                dexp_ref, nw_ref, e16_ref, *rest, chunk, n_valid):
    y_ref, so_ref, h_sc, xbuf = rest[-4:]
    c = pl.program_id(1)
    n_seq = zb_ref.shape[0]
    hp = SSD_HEADS // SSD_GROUPS * SSD_HEAD_DIM
    halo = SUBLANES

    @pl.when(c == 0)
    def _():
        h_sc[...] = h0_ref[...]
        xbuf[:, 0:halo] = prev_ref[...]

    lower, tri, tri_t = _tri_masks(chunk)
    ones8 = jnp.ones((chunk, SUBLANES), BF16)
    lane = lax.broadcasted_iota(jnp.int32, (chunk, LANES), 1)
    low_half = lane < SSD_HEAD_DIM

    def seq_stages(b):
        xbuf[b, halo:halo + chunk] = xbc_ref[b]
        x_all = xbuf[b]
        conv = cb_ref[...]
        for i in range(SSD_CONV):
            shift = SSD_CONV - 1 - i
            tap = x_all if shift == 0 else pltpu.roll(x_all, shift, 0)
            conv = conv + tap[halo:halo + chunk] * cw_ref[i:i + 1, :]
        xbuf[b, 0:halo] = xbuf[b, chunk:chunk + halo]
        act = _silu(conv)
        xs = act[:, :SSD_INNER]
        bm = act[:, SSD_INNER:SSD_INNER + SSD_GROUPS * SSD_STATE]
        cm = act[:, SSD_INNER + SSD_GROUPS * SSD_STATE:]

        dt16 = _softplus(gd_ref[b] + dtb_ref[...])[:, GD_DT:GD_DT + SSD_HEADS]
        if n_valid is not None:
            tok = c * chunk + lax.broadcasted_iota(jnp.int32, (chunk, 1), 0)
            dt16 = jnp.where(tok < n_valid, dt16, 0.0)
        yield
        dt = _dot3(dt16, e16_ref[...])
        da = dt * (-jnp.exp(alog_ref[...]))
        da16 = dt16 * (-jnp.exp(alog16_ref[...]))
        yield
        cum = _dot3_wx(tri, da)
        yield
        cum16 = _dot3_wx(tri, da16)
        cum16_t = _dot3_tn(da16, tri_t)
        yield
        decay_col = jnp.exp(_dot3_tn(da, ones8)[:, 0:1])
        xdt = xs * dt
        cum_last = cum[chunk - 1:chunk, :]
        xw = (xdt * jnp.exp(cum_last - cum)).astype(BF16)
        xdt_b = xdt.astype(BF16)

        y_intra = [None] * (SSD_INNER // LANES)
        y_inter = []
        for g in range(SSD_GROUPS):
            cg = cm[:, g * SSD_STATE:(g + 1) * SSD_STATE].astype(BF16)
            bg = bm[:, g * SSD_STATE:(g + 1) * SSD_STATE].astype(BF16)
            h_g = h_sc[b, g * hp:(g + 1) * hp, :]
            yield
            cb = _dot_nt(cg, bg)
            y_inter.append(_dot_nt(cg, h_g.astype(BF16)))
            for hh in range(SSD_HEADS // SSD_GROUPS):
                h = g * (SSD_HEADS // SSD_GROUPS) + hh
                diff = cum16[:, h:h + 1] - cum16_t[h:h + 1, :]
                lmat = jnp.where(lower, jnp.exp(jnp.where(lower, diff, 0.0)), 0.0)
                m = (cb * lmat).astype(BF16)
                pair = h // 2
                x_pair = xdt_b[:, pair * LANES:(pair + 1) * LANES]
                keep = low_half if h % 2 == 0 else jnp.logical_not(low_half)
                yield
                contrib = _dot(m, jnp.where(keep, x_pair, jnp.zeros_like(x_pair)))
                y_intra[pair] = contrib if y_intra[pair] is None else y_intra[pair] + contrib
            yield
            h_sc[b, g * hp:(g + 1) * hp, :] = (h_g * decay_col[g * hp:(g + 1) * hp, :]
                                               + _dot_tn(xw[:, g * hp:(g + 1) * hp], bg))
        y = jnp.concatenate(y_intra, axis=-1) + jnp.concatenate(y_inter, axis=-1) * jnp.exp(cum)
        y = y + xs * dexp_ref[...]
        y = y * _silu(zb_ref[b])
        ms = jnp.mean(y * y, axis=-1, keepdims=True)
        y_ref[b] = y * lax.rsqrt(ms + 1e-6) * nw_ref[...]

    _run_interleaved([seq_stages(b) for b in range(n_seq)])

    @pl.when(c == pl.num_programs(1) - 1)
    def _():
        so_ref[...] = h_sc[...]


def _stacked_state_io(stack, n_layers, blk, out_layer, n_in):
    shape = jax.ShapeDtypeStruct((n_layers,) + blk[0], F32)
    spec = pl.BlockSpec((None,) + blk[1], lambda b, c: (out_layer, b, 0, 0))
    if stack is None:
        return spec, shape, [], [], {}
    return spec, shape, [stack], [pl.BlockSpec(memory_space=pl.ANY)], {n_in: 1}


def ssd_mixer(proj, prev8, h0, layer, conv_w, conv_b, dtb_row, alog_exp, alog16, d_exp, norm_w, e16, nb, seq, chunk,
              n_seq, n_valid, stack, out_layer, n_layers):
    full = lambda a: pl.BlockSpec(a.shape, lambda b, c: (0,) * a.ndim)
    proj3 = proj.reshape(nb, seq, E_END)
    col_spec = lambda width, off: pl.BlockSpec((n_seq, chunk, width), lambda b, c: (b, c, off // width))
    in_specs = [
        col_spec(SSD_INNER, E_ZB), col_spec(SSD_CONV_CH, E_XBC), col_spec(LANES, E_GD),
        pl.BlockSpec((n_seq, SUBLANES, SSD_CONV_CH), lambda b, c: (b, 0, 0)),
        pl.BlockSpec((None, n_seq, SSD_INNER, SSD_STATE), lambda b, c: (layer, b, 0, 0)),
        full(conv_w), full(conv_b), full(dtb_row), full(alog_exp), full(alog16), full(d_exp), full(norm_w),
        full(e16),
    ]
    st_spec, st_shape, extra, extra_specs, aliases = _stacked_state_io(
        stack, n_layers, ((nb, SSD_INNER, SSD_STATE), (n_seq, SSD_INNER, SSD_STATE)), out_layer, len(in_specs))
    y, st = pl.pallas_call(
        functools.partial(_ssd_kernel, chunk=chunk, n_valid=n_valid),
        grid=(nb // n_seq, seq // chunk),
        in_specs=in_specs + extra_specs,
        out_specs=[pl.BlockSpec((n_seq, chunk, SSD_INNER), lambda b, c: (b, c, 0)), st_spec],
        out_shape=[jax.ShapeDtypeStruct((nb, seq, SSD_INNER), F32), st_shape],
        scratch_shapes=[pltpu.VMEM((n_seq, SSD_INNER, SSD_STATE), F32),
                        pltpu.VMEM((n_seq, SUBLANES + chunk, SSD_CONV_CH), F32)],
        input_output_aliases=aliases,
        compiler_params=_cparams(("arbitrary", "arbitrary")),
        name="ssd_mixer",
    )(proj3, proj3, proj3, prev8, h0, conv_w, conv_b, dtb_row, alog_exp, alog16, d_exp, norm_w, e16, *extra)
    return y.reshape(nb * seq, SSD_INNER), st


def _gla_kernel(q_ref, k_ref, v_ref, r_ref, glr_ref, s0_ref, wg_ref, bg_ref, nw_ref, *rest,
                chunk, n_sub, n_valid):
    o_ref, so_ref, s_sc = rest[-3:]
    c = pl.program_id(1)
    n_seq = q_ref.shape[0]
    span = n_sub * chunk

    @pl.when(c == 0)
    def _():
        s_sc[...] = s0_ref[...]

    row = lax.broadcasted_iota(jnp.int32, (span, span), 0)
    col = lax.broadcasted_iota(jnp.int32, (span, span), 1)
    lower = jnp.where(row // chunk == col // chunk, row - col, -1) >= 0
    tri = jnp.where(lower, 1.0, 0.0).astype(BF16)
    crow = lax.broadcasted_iota(jnp.int32, (span, n_sub * SUBLANES), 0) // chunk
    ccol = lax.broadcasted_iota(jnp.int32, (span, n_sub * SUBLANES), 1) // SUBLANES
    chunk_ones = jnp.where(crow == ccol, 1.0, 0.0).astype(BF16)
    nw = nw_ref[...]

    def head_stages(b, h, qg, kg, kd, v, r, total):
        kcols = slice(h * GLA_DK, (h + 1) * GLA_DK)
        vcols = slice(h * GLA_DV, (h + 1) * GLA_DV)
        att = jnp.where(lower, _dot_nt(qg[:, kcols], kg[:, kcols]), 0.0)
        yield
        o = _dot(att.astype(BF16), v[:, vcols])
        s_h = s_sc[b, kcols, :]
        carried = []
        for u in range(n_sub):
            rows = slice(u * chunk, (u + 1) * chunk)
            yield
            carried.append(_dot(qg[rows, kcols], s_h.astype(BF16)))
            decay = jnp.exp(total[kcols, u * SUBLANES:u * SUBLANES + 1])
            s_h = s_h * decay + _dot_tn(kd[rows, kcols], v[rows, vcols])
        s_sc[b, kcols, :] = s_h
        yield
        o = o + jnp.concatenate(carried, axis=0)
        ms = jnp.mean(o * o, axis=-1, keepdims=True)
        o = o * lax.rsqrt(ms + 1e-6) * nw
        o_ref[b, :, vcols] = o * _silu(r[:, vcols])

    def seq_stages(b):
        lg = _log_sigmoid(_dot(glr_ref[b].astype(BF16), wg_ref[...]) + bg_ref[...]) * (1.0 / GLA_TAU)
        k = k_ref[b]
        if n_valid is not None:
            tok = c * span + lax.broadcasted_iota(jnp.int32, (span, 1), 0)
            lg = jnp.where(tok < n_valid, lg, 0.0)
            k = jnp.where(tok < n_valid, k, 0.0)
        yield
        bcum = _dot3_wx(tri, lg)
        yield
        total = _dot3_tn(lg, chunk_ones)
        qg = (q_ref[b] * (GLA_DK ** -0.5) * jnp.exp(bcum)).astype(BF16)
        kg = (k * jnp.exp(-bcum)).astype(BF16)
        kd = jnp.concatenate(
            [k[u * chunk:(u + 1) * chunk] * jnp.exp(bcum[(u + 1) * chunk - 1:(u + 1) * chunk]
                                                    - bcum[u * chunk:(u + 1) * chunk]) for u in range(n_sub)],
            axis=0).astype(BF16)
        v = v_ref[b].astype(BF16)
        r = r_ref[b]
        yield
        yield from _round_robin([head_stages(b, h, qg, kg, kd, v, r, total) for h in range(GLA_HEADS)])

    _run_interleaved([seq_stages(b) for b in range(n_seq)])

    @pl.when(c == pl.num_programs(1) - 1)
    def _():
        so_ref[...] = s_sc[...]


def gla_mixer(proj, s0, layer, wg_pad, bg, norm_w, nb, seq, chunk, n_sub, n_seq, n_valid, stack, out_layer,
              n_layers):
    span = n_sub * chunk
    full = lambda a: pl.BlockSpec(a.shape, lambda b, c: (0,) * a.ndim)
    srows = GLA_HEADS * GLA_DK
    proj3 = proj.reshape(nb, seq, O_END)
    col_spec = lambda width, off: pl.BlockSpec((n_seq, span, width), lambda b, c: (b, c, off // width))
    in_specs = [
        col_spec(GLA_KEY_WIDTH, O_Q), col_spec(GLA_KEY_WIDTH, O_K), col_spec(GLA_VAL_WIDTH, O_V),
        col_spec(GLA_VAL_WIDTH, O_R), col_spec(LANES, O_GLR),
        pl.BlockSpec((None, n_seq, srows, GLA_DV), lambda b, c: (layer, b, 0, 0)),
        full(wg_pad), full(bg), full(norm_w),
    ]
    st_spec, st_shape, extra, extra_specs, aliases = _stacked_state_io(
        stack, n_layers, ((nb, srows, GLA_DV), (n_seq, srows, GLA_DV)), out_layer, len(in_specs))
    o, st = pl.pallas_call(
        functools.partial(_gla_kernel, chunk=chunk, n_sub=n_sub, n_valid=n_valid),
        grid=(nb // n_seq, seq // span),
        in_specs=in_specs + extra_specs,
        out_specs=[pl.BlockSpec((n_seq, span, GLA_VAL_WIDTH), lambda b, c: (b, c, 0)), st_spec],
        out_shape=[jax.ShapeDtypeStruct((nb, seq, GLA_VAL_WIDTH), F32), st_shape],
        scratch_shapes=[pltpu.VMEM((n_seq, srows, GLA_DV), F32)],
        input_output_aliases=aliases,
        compiler_params=_cparams(("arbitrary", "arbitrary")),
        name="gla_mixer",
    )(proj3, proj3, proj3, proj3, proj3, s0, wg_pad, bg, norm_w, *extra)
    return o.reshape(nb * seq, GLA_VAL_WIDTH), st


def _even_weight(w):
    q, g_a, kv, z_a, z_b, xbc, dt = jnp.split(w, list(np.cumsum(
        [NSA_WIDTH, 3 * NSA_HEADS, 6 * NSA_KV_HEADS * HEAD_DIM, NSA_WIDTH, SSD_INNER, SSD_CONV_CH])), axis=-1)
    pad = jnp.zeros((w.shape[0], E_END - E_GD - 3 * NSA_HEADS - SSD_HEADS), w.dtype)
    return jnp.concatenate([xbc, q, z_b, z_a, kv, g_a, dt, pad], axis=-1).astype(BF16)


def _odd_weight(w):
    q, k, v, glr, r = jnp.split(w, list(np.cumsum(
        [GLA_KEY_WIDTH, GLA_KEY_WIDTH, GLA_VAL_WIDTH, GLA_GATE_RANK])), axis=-1)
    pad = jnp.zeros((w.shape[0], O_END - O_GLR - GLA_GATE_RANK), w.dtype)
    return jnp.concatenate([q, k, v, r, glr, pad], axis=-1).astype(BF16)


def _rope_tables(pos):
    half = HEAD_DIM // 2
    inv = ROPE_THETA ** (-jnp.arange(half, dtype=F32) / half)
    ang = pos.astype(F32)[:, None] * inv[None, :]
    cos, sin = jnp.cos(ang), jnp.sin(ang)
    reps = LANES // HEAD_DIM
    return jnp.tile(jnp.concatenate([cos, cos], -1), (1, reps)), jnp.tile(jnp.concatenate([-sin, sin], -1), (1, reps))


def _compress_weights(pe, w1, w2):
    pe_tab = jnp.concatenate([pe[0], pe[0], pe[1], pe[1]], axis=-1)
    w1r = w1.reshape(2, CMP_BLOCK, HEAD_DIM, CMP_HIDDEN)
    w1_blk = jnp.zeros((CMP_BLOCK, 4 * HEAD_DIM, 4 * CMP_HIDDEN), F32)
    w2_blk = jnp.zeros((4 * CMP_HIDDEN, 4 * HEAD_DIM), F32)
    for part in range(4):
        src = part // 2
        w1_blk = w1_blk.at[:, part * HEAD_DIM:(part + 1) * HEAD_DIM,
                           part * CMP_HIDDEN:(part + 1) * CMP_HIDDEN].set(w1r[src])
        w2_blk = w2_blk.at[part * CMP_HIDDEN:(part + 1) * CMP_HIDDEN,
                           part * HEAD_DIM:(part + 1) * HEAD_DIM].set(w2[src])
    return pe_tab, w1_blk.astype(BF16), w2_blk.astype(BF16)


def _compress_weights_cache(pe, w1, w2):
    w1r = w1.reshape(2, CMP_BLOCK, HEAD_DIM, CMP_HIDDEN)
    w1_bd = jnp.zeros((2, CMP_BLOCK, NSA_KV_HEADS * HEAD_DIM, NSA_KV_HEADS * CMP_HIDDEN), F32)
    w2_bd = jnp.zeros((2, NSA_KV_HEADS * CMP_HIDDEN, NSA_KV_HEADS * HEAD_DIM), F32)
    for g in range(NSA_KV_HEADS):
        w1_bd = w1_bd.at[:, :, g * HEAD_DIM:(g + 1) * HEAD_DIM, g * CMP_HIDDEN:(g + 1) * CMP_HIDDEN].set(w1r)
        w2_bd = w2_bd.at[:, g * CMP_HIDDEN:(g + 1) * CMP_HIDDEN, g * HEAD_DIM:(g + 1) * HEAD_DIM].set(w2)
    pe_tok = jnp.tile(jnp.concatenate([pe] * NSA_KV_HEADS, axis=-1), (1, PAGE_SIZE // CMP_BLOCK, 1))
    return pe_tok, w1_bd.astype(BF16), w2_bd.astype(BF16)


def _cmp_layout(cmp, nb, nc, dtype):
    c = cmp.reshape(nb, nc, 2, NSA_KV_HEADS, HEAD_DIM).transpose(2, 0, 3, 1, 4)
    halves = []
    for par in range(2):
        h = c[:, :, :, par::2]
        halves.append(jnp.pad(h, ((0, 0), (0, 0), (0, 0), (0, LANES - h.shape[3]), (0, 0))))
    c = jnp.concatenate(halves, axis=3).astype(dtype)
    return c[0], c[1]


def _per_step(n, want):
    return want if n % want == 0 else 1


def _sel_expand(n_keys):
    blk = jnp.arange(n_keys, dtype=jnp.int32) // SEL_BLOCK
    return (blk[None, :] == jnp.arange(LANES, dtype=jnp.int32)[:, None]).astype(BF16)


def kernel(x_prompt, x_sample, c_prompt, c_sample, cache_nsa, cache_nsa_win, state_ssd_conv, state_ssd, state_gla, page_table, norm_w, w_mod, b_mod, w_in_even, w_out_even, nsa_cmp_pe, nsa_cmp_w1, nsa_cmp_w2, ssd_conv_w, ssd_conv_b, ssd_dt_bias, ssd_a_log, ssd_d, ssd_norm_w, w_in_odd, gla_w_gate2, gla_b_gate, gla_norm_w, w_out_odd, final_norm_w):
    bp, sp, d = x_prompt.shape
    bs, ss, _ = x_sample.shape
    depth = norm_w.shape[0]
    npg = page_table.shape[1]
    past_len = npg * PAGE_SIZE
    assert ss <= cache_nsa_win.shape[2]
    assert sp % 512 == 0 and sp // SEL_BLOCK <= LANES and sp // CMP_BLOCK <= 2 * LANES
    assert ss <= DEC_PAD and ss < CMP_BLOCK and past_len % SEL_BLOCK == 0 and ss >= SSD_CONV - 1
    assert past_len // SEL_BLOCK + 1 <= LANES and (bs * DEC_PAD) % TM_IN == 0 and (bs * npg) % CMP_PAGES == 0
    tp, td = bp * sp, bs * DEC_PAD
    tm_d = TM_IN
    seq_ps, samp_ps = _per_step(bp, SEQ_PER_STEP), _per_step(bs, SAMPLES_PER_STEP)

    c_all = jnp.concatenate([c_prompt, c_sample], axis=0)
    c_all = jnp.pad(c_all, ((0, -c_all.shape[0] % SUBLANES), (0, 0)))
    mod = mod_all(c_all, w_mod.astype(BF16), b_mod)

    def mods(l):
        shift, scale, gate = jnp.split(mod[l], 3, axis=-1)
        mp = [m[:bp].reshape(bp, 1, d) for m in (shift, scale, gate)]
        ms = [jnp.repeat(m[bp:bp + bs], DEC_PAD, axis=0).reshape(td // tm_d, tm_d, d) for m in (shift, scale, gate)]
        return mp, ms

    xp = x_prompt.reshape(tp, d)
    xs = jnp.pad(x_sample, ((0, 0), (0, DEC_PAD - ss), (0, 0))).reshape(td, d)

    cos_p, sin_p = _rope_tables(jnp.arange(sp, dtype=jnp.int32))
    cos_s, sin_s = _rope_tables(past_len + jnp.arange(DEC_PAD, dtype=jnp.int32))
    cos_s, sin_s = jnp.tile(cos_s, (tm_d // DEC_PAD, 1)), jnp.tile(sin_s, (tm_d // DEC_PAD, 1))
    cache_t = jnp.transpose(cache_nsa, (0, 1, 3, 4, 5, 2))
    win_t = jnp.transpose(cache_nsa_win, (0, 1, 3, 4, 5, 2))
    ssd_h0_s = state_ssd.reshape(state_ssd.shape[0], bs, SSD_INNER, SSD_STATE)
    gla_s0_s = state_gla.reshape(state_gla.shape[0], bs, GLA_HEADS * GLA_DK, GLA_DV)
    e_mat_p = _sel_expand(sp)
    e_mat_s = _sel_expand(past_len + LANES)
    e16 = (jnp.arange(SSD_INNER, dtype=jnp.int32)[None, :] // SSD_HEAD_DIM
           == jnp.arange(SSD_HEADS, dtype=jnp.int32)[:, None]).astype(BF16)

    outs = {k: [] for k in ("kv_p", "kv_s", "win_p", "cv_p", "cv_s")}
    stacks = dict.fromkeys(("win_s", "ss_p", "ss_s", "gl_p", "gl_s"))
    n_even, n_odd = (depth + 1) // 2, depth // 2
    for l in range(depth):
        e = l // 2
        (shift_p, scale_p, gate_p), (shift_s, scale_s, gate_s) = mods(l)
        last = l == depth - 1
        if l % 2 == 0:
            w_in = _even_weight(w_in_even[e])
            w_out = w_out_even[e].astype(BF16)
            w_out_a, w_out_b = w_out[:NSA_WIDTH], w_out[NSA_WIDTH:]
            pe_tab, w1_blk, w2_blk = _compress_weights(nsa_cmp_pe[e], nsa_cmp_w1[e], nsa_cmp_w2[e])
            conv_w, conv_b = ssd_conv_w[e], ssd_conv_b[e].reshape(1, SSD_CONV_CH)
            dtb_row = jnp.zeros((1, LANES), F32).at[0, GD_DT:GD_DT + SSD_HEADS].set(ssd_dt_bias[e])
            alog16 = ssd_a_log[e].reshape(1, SSD_HEADS)
            alog_exp = jnp.repeat(ssd_a_log[e], SSD_HEAD_DIM).reshape(1, SSD_INNER)
            d_exp = jnp.repeat(ssd_d[e], SSD_HEAD_DIM).reshape(1, SSD_INNER)
            ssd_nw = ssd_norm_w[e].reshape(1, SSD_INNER)
            ssd_args = (conv_w, conv_b, dtb_row, alog_exp, alog16, d_exp, ssd_nw, e16)

            proj = inproj(xp, norm_w[l], scale_p, shift_p, w_in, TM_IN, sp)
            rows, win, _, qh, ks, vs, kw, vw = nsa_prep(proj, cos_p, sin_p, TM_IN, sp, True)
            nc = sp // CMP_BLOCK
            cmp = nsa_compress(rows.reshape(tp // CMP_BLOCK, CMP_BLOCK * 4 * LANES), pe_tab, w1_blk, w2_blk,
                               min(CMP_ROWS, tp // CMP_BLOCK))
            kc, vc = _cmp_layout(cmp, bp, nc, BF16)
            o_a = nsa_attn_prompt(proj, qh, kc, vc, ks, vs, kw, vw, e_mat_p)
            y, stacks["ss_p"] = ssd_mixer(proj, jnp.zeros((bp, SUBLANES, SSD_CONV_CH), F32),
                                          jnp.zeros((1, bp, SSD_INNER, SSD_STATE), F32), 0, *ssd_args, bp, sp,
                                          SSD_CHUNK, seq_ps, None, stacks["ss_p"], e, n_even)
            xp = outproj([o_a, y], [w_out_a, w_out_b], xp, gate_p, final_norm_w, TM_OUT, sp, last)
            outs["kv_p"].append(rows.reshape(bp, sp, 4, NSA_KV_HEADS, HEAD_DIM))
            outs["win_p"].append(win.reshape(bp, sp, 2, NSA_KV_HEADS, HEAD_DIM)[:, -min(WINDOW, sp):])
            outs["cv_p"].append(proj.reshape(bp, sp, E_END)[:, -(SSD_CONV - 1):, E_XBC:E_XBC + SSD_CONV_CH])

            proj = inproj(xs, norm_w[l], scale_s, shift_s, w_in, tm_d, tm_d)
            rows, win, qrot = nsa_prep(proj, cos_s, sin_s, tm_d, DEC_PAD, False)
            pe_tok, w1_bd, w2_bd = _compress_weights_cache(nsa_cmp_pe[e], nsa_cmp_w1[e], nsa_cmp_w2[e])
            cmp_s = nsa_compress_cache(cache_t, e, page_table.reshape(-1), pe_tok, w1_bd, w2_bd, CMP_PAGES)
            per_page = PAGE_SIZE // CMP_BLOCK
            o_a, stacks["win_s"] = nsa_attn_decode(
                page_table, e, qrot.reshape(bs, DEC_PAD, NSA_WIDTH),
                cmp_s.reshape(bs, npg * per_page, 4 * HEAD_DIM), cache_t, rows.reshape(bs, DEC_PAD, 4 * LANES),
                win_t, win.reshape(bs, DEC_PAD, 2 * LANES), proj.reshape(bs, DEC_PAD, E_END), e_mat_s, past_len,
                samp_ps, ss, stacks["win_s"])
            o_a = o_a.reshape(td, NSA_WIDTH)
            prev8 = jnp.pad(state_ssd_conv[e], ((0, 0), (SUBLANES - (SSD_CONV - 1), 0), (0, 0)))
            y, stacks["ss_s"] = ssd_mixer(proj, prev8, ssd_h0_s, e, *ssd_args, bs, DEC_PAD, DEC_PAD,
                                          samp_ps, ss, stacks["ss_s"], e, n_even)
            xs = outproj([o_a, y], [w_out_a, w_out_b], xs, gate_s, final_norm_w, tm_d, tm_d, last)
            outs["kv_s"].append(rows.reshape(bs, DEC_PAD, 4, NSA_KV_HEADS, HEAD_DIM)[:, :ss])
            new_xbc = proj.reshape(bs, DEC_PAD, E_END)[:, :ss, E_XBC:E_XBC + SSD_CONV_CH]
            outs["cv_s"].append(jnp.concatenate([state_ssd_conv[e], new_xbc], axis=1)[:, -(SSD_CONV - 1):])
        else:
            w_in = _odd_weight(w_in_odd[e])
            w_out = w_out_odd[e].astype(BF16)
            wg_pad = jnp.pad(gla_w_gate2[e], ((0, LANES - GLA_GATE_RANK), (0, 0))).astype(BF16)
            bg = gla_b_gate[e].reshape(1, GLA_KEY_WIDTH)
            gnw = gla_norm_w[e].reshape(1, GLA_DV)
            srows = GLA_HEADS * GLA_DK
            proj = inproj(xp, norm_w[l], scale_p, shift_p, w_in, TM_IN, sp)
            o, stacks["gl_p"] = gla_mixer(proj, jnp.zeros((1, bp, srows, GLA_DV), F32), 0, wg_pad, bg, gnw, bp, sp,
                                          GLA_CHUNK, GLA_SUB, seq_ps, None, stacks["gl_p"], e, n_odd)
            xp = outproj([o], [w_out], xp, gate_p, final_norm_w, TM_OUT, sp, last)
            proj = inproj(xs, norm_w[l], scale_s, shift_s, w_in, tm_d, tm_d)
            o, stacks["gl_s"] = gla_mixer(proj, gla_s0_s, e, wg_pad, bg, gnw, bs, DEC_PAD, DEC_PAD, 1,
                                          samp_ps, ss, stacks["gl_s"], e, n_odd)
            xs = outproj([o], [w_out], xs, gate_s, final_norm_w, tm_d, tm_d, last)

    y_prompt = xp.reshape(bp, sp, d)
    y_sample = xs.reshape(bs, DEC_PAD, d)[:, :ss]
    st = {k: jnp.stack(v) for k, v in outs.items()}
    st["win_s"] = jnp.transpose(stacks["win_s"], (0, 1, 5, 2, 3, 4))
    st["ss_p"] = stacks["ss_p"].reshape(n_even, bp, SSD_HEADS, SSD_HEAD_DIM, SSD_STATE)
    st["ss_s"] = stacks["ss_s"].reshape(n_even, bs, SSD_HEADS, SSD_HEAD_DIM, SSD_STATE)
    st["gl_p"] = stacks["gl_p"].reshape(n_odd, bp, GLA_HEADS, GLA_DK, GLA_DV)
    st["gl_s"] = stacks["gl_s"].reshape(n_odd, bs, GLA_HEADS, GLA_DK, GLA_DV)
    return (y_prompt, y_sample, st["kv_p"], st["kv_s"], st["win_p"], st["win_s"], st["cv_p"], st["cv_s"],
            st["ss_p"], st["ss_s"], st["gl_p"], st["gl_s"])
```
